```python
import jax, jax.numpy as jnp
from jax import lax
import numpy as np

D_MODEL = 2048
BATCH = 8
SEQ = 8192
DEPTH = 4

GRID_W = 64
CTX_LEN = 256
D_CONV = D_MODEL // 2
D_ATTN = D_MODEL - D_CONV
HEAD_DIM = 64
N_HEADS = D_ATTN // HEAD_DIM
N_KV_HEADS = 4
GQA_GROUP = N_HEADS // N_KV_HEADS
WINDOW = 128
BLOCK = 128
CONV_WIDTH = 3
D_FF = 4 * D_MODEL
ROPE_THETA = 10000.0
ROPE_AXIS_DIM = HEAD_DIM // 2
EPS = 1e-6
N_MOD = 6
KV_START = 3 * D_CONV + D_ATTN
D_IN_PROJ = KV_START + 2 * N_KV_HEADS * HEAD_DIM
SCALE = HEAD_DIM ** -0.5
NEG_INF = -1e30

kernel_name = "hybrid_conv_swa_dit_block"


def rmsnorm(x, g):
    xf = x.astype(jnp.float32)
    y = xf * lax.rsqrt(jnp.mean(xf * xf, axis=-1, keepdims=True) + EPS)
    return (y * g.astype(jnp.float32)).astype(x.dtype)


def modulate(h, shift, scale):
    return h * (1 + scale) + shift


def short_conv(u, w, b):
    n = u.shape[1]
    up = jnp.pad(u, ((0, 0), (1, 1), (0, 0)))
    return up[:, 0:n] * w[0] + up[:, 1:n + 1] * w[1] + up[:, 2:n + 2] * w[2] + b


def gated_conv_mixer(p_conv, w, b):
    bg, cg, h = jnp.split(p_conv, 3, axis=-1)
    return bg * short_conv(cg * h, w, b)


def rope_tables(n_tokens, dtype):
    rows = n_tokens // GRID_W
    row_pos = jnp.repeat(jnp.arange(rows, dtype=jnp.float32), GRID_W)
    col_pos = jnp.tile(jnp.arange(GRID_W, dtype=jnp.float32), rows)
    inv = ROPE_THETA ** (-jnp.arange(0, ROPE_AXIS_DIM, 2, dtype=jnp.float32) / ROPE_AXIS_DIM)
    ang_r = row_pos[:, None] * inv[None, :]
    ang_c = col_pos[:, None] * inv[None, :]
    return (jnp.cos(ang_r)[:, None, :].astype(dtype), jnp.sin(ang_r)[:, None, :].astype(dtype),
            jnp.cos(ang_c)[:, None, :].astype(dtype), jnp.sin(ang_c)[:, None, :].astype(dtype))


def rotate(x, cos, sin):
    x1, x2 = jnp.split(x, 2, axis=-1)
    return jnp.concatenate([x1 * cos - x2 * sin, x2 * cos + x1 * sin], axis=-1)


def rope_2d(x, tabs):
    cr, sr, cc, sc = tabs
    xr, xc = jnp.split(x, 2, axis=-1)
    return jnp.concatenate([rotate(xr, cr, sr), rotate(xc, cc, sc)], axis=-1)


def band_mask(nb, n_tokens):
    n = jnp.arange(nb)[:, None, None]
    r = jnp.arange(BLOCK)[None, :, None]
    j = jnp.arange(3 * BLOCK)[None, None, :]
    q_pos = n * BLOCK + r
    k_pos = (n - 1) * BLOCK + j
    return (jnp.abs(k_pos - q_pos) <= WINDOW) & (k_pos >= 0) & (k_pos < n_tokens)


def latent_window_attention(q, k, v, kc, vc, sink):
    bsz, n_tok = q.shape[0], q.shape[1]
    n_ctx = kc.shape[1]
    nb = n_tok // BLOCK
    qb = q.reshape(bsz, nb, BLOCK, N_KV_HEADS, GQA_GROUP, HEAD_DIM)

    def band(t):
        tb = t.reshape(bsz, nb, BLOCK, N_KV_HEADS, HEAD_DIM)
        tb = jnp.pad(tb, ((0, 0), (1, 1), (0, 0), (0, 0), (0, 0)))
        return jnp.concatenate([tb[:, :-2], tb[:, 1:-1], tb[:, 2:]], axis=2)

    kw, vw = band(k), band(v)
    s_loc = jnp.einsum('bnqhgd,bnkhd->bnhgqk', qb, kw).astype(jnp.float32) * SCALE
    s_loc = jnp.where(band_mask(nb, n_tok)[None, :, None, None], s_loc, NEG_INF)
    s_ctx = jnp.einsum('bnqhgd,bchd->bnhgqc', qb, kc).astype(jnp.float32) * SCALE
    snk = jnp.broadcast_to(sink.astype(jnp.float32).reshape(1, 1, N_KV_HEADS, GQA_GROUP, 1, 1),
                           s_loc.shape[:-1] + (1,))
    p = jax.nn.softmax(jnp.concatenate([s_loc, s_ctx, snk], axis=-1), axis=-1).astype(v.dtype)
    nk = 3 * BLOCK
    o = (jnp.einsum('bnhgqk,bnkhd->bnqhgd', p[..., :nk], vw)
         + jnp.einsum('bnhgqc,bchd->bnqhgd', p[..., nk:nk + n_ctx], vc))
    return o.reshape(bsz, n_tok, D_ATTN)


def context_attention(qc, kc, vc, sink):
    bsz, n_ctx = qc.shape[0], qc.shape[1]
    qg = qc.reshape(bsz, n_ctx, N_KV_HEADS, GQA_GROUP, HEAD_DIM)
    s = jnp.einsum('blhgd,bchd->bhglc', qg, kc).astype(jnp.float32) * SCALE
    snk = jnp.broadcast_to(sink.astype(jnp.float32).reshape(1, N_KV_HEADS, GQA_GROUP, 1, 1),
                           s.shape[:-1] + (1,))
    p = jax.nn.softmax(jnp.concatenate([s, snk], axis=-1), axis=-1).astype(vc.dtype)
    o = jnp.einsum('bhglc,bchd->blhgd', p[..., :n_ctx], vc)
    return o.reshape(bsz, n_ctx, D_ATTN)


def mixer_merge(conv_out, attn_out, g_oc, g_oa, w_out):
    return jnp.concatenate([rmsnorm(conv_out, g_oc), rmsnorm(attn_out, g_oa)], axis=-1) @ w_out


def sq_relu_mlp(h, w1, w2):
    return jnp.square(jax.nn.relu(h @ w1)) @ w2


def _fwd_setup_inputs(seed: int = 0) -> dict:
    key = jax.random.key(seed)
    ks = jax.random.split(key, 18)

    def nrm(k, shape, s):
        return jax.random.normal(k, shape, jnp.float32) * s

    return {
        "x": nrm(ks[0], (BATCH, SEQ, D_MODEL), 1.0),
        "c": nrm(ks[1], (BATCH, D_MODEL), 1.0),
        "ctx": nrm(ks[2], (BATCH, CTX_LEN, D_MODEL), 1.0),
        "c_ctx": nrm(ks[3], (D_MODEL,), 1.0),
        "w_ada": nrm(ks[4], (DEPTH, D_MODEL, N_MOD * D_MODEL), 0.5 * D_MODEL ** -0.5),
        "b_ada": nrm(ks[5], (DEPTH, N_MOD * D_MODEL), 0.02),
        "g_norm1": 1.0 + nrm(ks[6], (DEPTH, D_MODEL), 0.02),
        "g_norm2": 1.0 + nrm(ks[7], (DEPTH, D_MODEL), 0.02),
        "w_in": nrm(ks[8], (DEPTH, D_MODEL, D_IN_PROJ), D_MODEL ** -0.5),
        "conv_w": nrm(ks[9], (DEPTH, CONV_WIDTH, D_CONV), CONV_WIDTH ** -0.5),
        "conv_b": nrm(ks[10], (DEPTH, D_CONV), 0.02),
        "sink": nrm(ks[11], (DEPTH, N_HEADS), 0.5),
        "g_out_conv": 1.0 + nrm(ks[12], (DEPTH, D_CONV), 0.02),
        "g_out_attn": 1.0 + nrm(ks[13], (DEPTH, D_ATTN), 0.02),
        "w_out": nrm(ks[14], (DEPTH, D_MODEL, D_MODEL), D_MODEL ** -0.5),
        "w_mlp1": nrm(ks[15], (DEPTH, D_MODEL, D_FF), D_MODEL ** -0.5),
        "w_mlp2": nrm(ks[16], (DEPTH, D_FF, D_MODEL), D_FF ** -0.5),
        "g_final": 1.0 + nrm(ks[17], (D_MODEL,), 0.02),
    }


def _fwd_reference(x, c, ctx, c_ctx, w_ada, b_ada, g_norm1, g_norm2, w_in, conv_w, conv_b, sink,
              g_out_conv, g_out_attn, w_out, w_mlp1, w_mlp2, g_final):
    bsz, n_tok, _ = x.shape
    n_ctx = ctx.shape[1]
    tabs = rope_tables(n_tok, x.dtype)
    sc = jax.nn.silu(c)
    scc = jax.nn.silu(c_ctx)
    for i in range(DEPTH):
        last = i == DEPTH - 1
        m = jnp.split((sc @ w_ada[i] + b_ada[i])[:, None, :], N_MOD, axis=-1)
        mc = jnp.split(scc @ w_ada[i] + b_ada[i], N_MOD, axis=-1)

        h = modulate(rmsnorm(x, g_norm1[i]), m[0], m[1])
        hc = modulate(rmsnorm(ctx, g_norm1[i]), mc[0], mc[1])
        p = h @ w_in[i]
        p_conv = p[..., :3 * D_CONV]
        q = p[..., 3 * D_CONV:KV_START].reshape(bsz, n_tok, N_HEADS, HEAD_DIM)
        k, v = jnp.split(p[..., KV_START:], 2, axis=-1)
        q = rope_2d(q, tabs)
        k = rope_2d(k.reshape(bsz, n_tok, N_KV_HEADS, HEAD_DIM), tabs)
        v = v.reshape(bsz, n_tok, N_KV_HEADS, HEAD_DIM)
        kc, vc = jnp.split(hc @ w_in[i][:, KV_START:], 2, axis=-1)
        kc = kc.reshape(bsz, n_ctx, N_KV_HEADS, HEAD_DIM)
        vc = vc.reshape(bsz, n_ctx, N_KV_HEADS, HEAD_DIM)

        conv_out = gated_conv_mixer(p_conv, conv_w[i], conv_b[i])
        attn_out = latent_window_attention(q, k, v, kc, vc, sink[i])
        x = x + m[2] * mixer_merge(conv_out, attn_out, g_out_conv[i], g_out_attn[i], w_out[i])

        if not last:
            pc = hc @ w_in[i][:, :KV_START]
            ctx_conv = gated_conv_mixer(pc[..., :3 * D_CONV], conv_w[i], conv_b[i])
            qc = pc[..., 3 * D_CONV:].reshape(bsz, n_ctx, N_HEADS, HEAD_DIM)
            ctx_attn = context_attention(qc, kc, vc, sink[i])
            ctx = ctx + mc[2] * mixer_merge(ctx_conv, ctx_attn, g_out_conv[i], g_out_attn[i], w_out[i])

        x = x + m[5] * sq_relu_mlp(modulate(rmsnorm(x, g_norm2[i]), m[3], m[4]), w_mlp1[i], w_mlp2[i])
        if not last:
            ctx = ctx + mc[5] * sq_relu_mlp(modulate(rmsnorm(ctx, g_norm2[i]), mc[3], mc[4]),
                                            w_mlp1[i], w_mlp2[i])
    return rmsnorm(x, g_final)


import jax as _jax
import jax.numpy as _jnp

TWIN_FORMAT = 'train_step'
FWD_PARAMS = ['x', 'c', 'ctx', 'c_ctx', 'w_ada', 'b_ada', 'g_norm1', 'g_norm2', 'w_in', 'conv_w', 'conv_b', 'sink', 'g_out_conv', 'g_out_attn', 'w_out', 'w_mlp1', 'w_mlp2', 'g_final']
TWIN_WEIGHTS = ['c_ctx', 'w_ada', 'b_ada', 'g_norm1', 'g_norm2', 'w_in', 'conv_w', 'conv_b', 'sink', 'g_out_conv', 'g_out_attn', 'w_out', 'w_mlp1', 'w_mlp2', 'g_final']
TWIN_DIFF_INPUT = 'x'
TWIN_INPUTS = ['x', 'c', 'ctx', 'c_ctx', 'w_ada', 'b_ada', 'g_norm1', 'g_norm2', 'w_in', 'conv_w', 'conv_b', 'sink', 'g_out_conv', 'g_out_attn', 'w_out', 'w_mlp1', 'w_mlp2', 'g_final', 'loss_target', 'm_c_ctx', 'm_w_ada', 'm_b_ada', 'm_g_norm1', 'm_g_norm2', 'm_w_in', 'm_conv_w', 'm_conv_b', 'm_sink', 'm_g_out_conv', 'm_g_out_attn', 'm_w_out', 'm_w_mlp1', 'm_w_mlp2', 'm_g_final', 'v_c_ctx', 'v_w_ada', 'v_b_ada', 'v_g_norm1', 'v_g_norm2', 'v_w_in', 'v_conv_w', 'v_conv_b', 'v_sink', 'v_g_out_conv', 'v_g_out_attn', 'v_w_out', 'v_w_mlp1', 'v_w_mlp2', 'v_g_final']
TWIN_OUTPUTS = ['loss', 'grad_x', 'grad_c_ctx', 'grad_w_ada', 'grad_b_ada', 'grad_g_norm1', 'grad_g_norm2', 'grad_w_in', 'grad_conv_w', 'grad_conv_b', 'grad_sink', 'grad_g_out_conv', 'grad_g_out_attn', 'grad_w_out', 'grad_w_mlp1', 'grad_w_mlp2', 'grad_g_final', 'delta_c_ctx', 'delta_w_ada', 'delta_b_ada', 'delta_g_norm1', 'delta_g_norm2', 'delta_w_in', 'delta_conv_w', 'delta_conv_b', 'delta_sink', 'delta_g_out_conv', 'delta_g_out_attn', 'delta_w_out', 'delta_w_mlp1', 'delta_w_mlp2', 'delta_g_final', 'new_m_c_ctx', 'new_m_w_ada', 'new_m_b_ada', 'new_m_g_norm1', 'new_m_g_norm2', 'new_m_w_in', 'new_m_conv_w', 'new_m_conv_b', 'new_m_sink', 'new_m_g_out_conv', 'new_m_g_out_attn', 'new_m_w_out', 'new_m_w_mlp1', 'new_m_w_mlp2', 'new_m_g_final', 'new_v_c_ctx', 'new_v_w_ada', 'new_v_b_ada', 'new_v_g_norm1', 'new_v_g_norm2', 'new_v_w_in', 'new_v_conv_w', 'new_v_conv_b', 'new_v_sink', 'new_v_g_out_conv', 'new_v_g_out_attn', 'new_v_w_out', 'new_v_w_mlp1', 'new_v_w_mlp2', 'new_v_g_final']
TWIN_LEAF_KINDS = {'loss': 'loss', 'grad_x': 'grad_x', 'grad_c_ctx': 'grad_w', 'grad_w_ada': 'grad_w', 'grad_b_ada': 'grad_w', 'grad_g_norm1': 'grad_w', 'grad_g_norm2': 'grad_w', 'grad_w_in': 'grad_w', 'grad_conv_w': 'grad_w', 'grad_conv_b': 'grad_w', 'grad_sink': 'grad_w', 'grad_g_out_conv': 'grad_w', 'grad_g_out_attn': 'grad_w', 'grad_w_out': 'grad_w', 'grad_w_mlp1': 'grad_w', 'grad_w_mlp2': 'grad_w', 'grad_g_final': 'grad_w', 'delta_c_ctx': 'delta_w', 'delta_w_ada': 'delta_w', 'delta_b_ada': 'delta_w', 'delta_g_norm1': 'delta_w', 'delta_g_norm2': 'delta_w', 'delta_w_in': 'delta_w', 'delta_conv_w': 'delta_w', 'delta_conv_b': 'delta_w', 'delta_sink': 'delta_w', 'delta_g_out_conv': 'delta_w', 'delta_g_out_attn': 'delta_w', 'delta_w_out': 'delta_w', 'delta_w_mlp1': 'delta_w', 'delta_w_mlp2': 'delta_w', 'delta_g_final': 'delta_w', 'new_m_c_ctx': 'new_m', 'new_m_w_ada': 'new_m', 'new_m_b_ada': 'new_m', 'new_m_g_norm1': 'new_m', 'new_m_g_norm2': 'new_m', 'new_m_w_in': 'new_m', 'new_m_conv_w': 'new_m', 'new_m_conv_b': 'new_m', 'new_m_sink': 'new_m', 'new_m_g_out_conv': 'new_m', 'new_m_g_out_attn': 'new_m', 'new_m_w_out': 'new_m', 'new_m_w_mlp1': 'new_m', 'new_m_w_mlp2': 'new_m', 'new_m_g_final': 'new_m', 'new_v_c_ctx': 'new_v', 'new_v_w_ada': 'new_v', 'new_v_b_ada': 'new_v', 'new_v_g_norm1': 'new_v', 'new_v_g_norm2': 'new_v', 'new_v_w_in': 'new_v', 'new_v_conv_w': 'new_v', 'new_v_conv_b': 'new_v', 'new_v_sink': 'new_v', 'new_v_g_out_conv': 'new_v', 'new_v_g_out_attn': 'new_v', 'new_v_w_out': 'new_v', 'new_v_w_mlp1': 'new_v', 'new_v_w_mlp2': 'new_v', 'new_v_g_final': 'new_v'}


def _forward(args):
    return _fwd_reference(*[args[k] for k in FWD_PARAMS])


def _output_shape():
    def fwd():
        inp = _fwd_setup_inputs(0)
        return _fwd_reference(*[inp[k] for k in FWD_PARAMS])
    out = _jax.eval_shape(fwd)
    return out.shape, out.dtype

N_MICROBATCH = 1
ADAM_LR = 0.001
ADAM_B1 = 0.9
ADAM_B2 = 0.999
ADAM_EPS = 1e-08
ADAM_WD = 0.01
ADAM_STEP = 10
PER_EXAMPLE_BATCH_AXIS = {'x': 0, 'c': 0, 'ctx': 0, 'loss_target': 0}
SHARED_INPUTS = []
_WEIGHT_DTYPES = {'c_ctx': _jnp.float32, 'w_ada': _jnp.float32, 'b_ada': _jnp.float32, 'g_norm1': _jnp.float32, 'g_norm2': _jnp.float32, 'w_in': _jnp.float32, 'conv_w': _jnp.float32, 'conv_b': _jnp.float32, 'sink': _jnp.float32, 'g_out_conv': _jnp.float32, 'g_out_attn': _jnp.float32, 'w_out': _jnp.float32, 'w_mlp1': _jnp.float32, 'w_mlp2': _jnp.float32, 'g_final': _jnp.float32}
MOMENT_SCALE = {'c_ctx': 4.523967e-02, 'w_ada': 6.347592e-02, 'b_ada': 1.165453e-01, 'g_norm1': 4.452137e-02, 'g_norm2': 5.059192e-02, 'w_in': 3.577617e-02, 'conv_w': 3.444161e-02, 'conv_b': 3.073699e-02, 'sink': 5.495905e-04, 'g_out_conv': 3.333736e-02, 'g_out_attn': 4.602689e-02, 'w_out': 4.061846e-02, 'w_mlp1': 2.633192e-02, 'w_mlp2': 5.043190e-02, 'g_final': 3.231317e+01}


def _to_microbatches(a, axis):
    t = _jnp.moveaxis(a, axis, 0)
    t = t.reshape((N_MICROBATCH, t.shape[0] // N_MICROBATCH) + t.shape[1:])
    return _jnp.moveaxis(t, 1, axis + 1)


def setup_inputs(seed: int = 0) -> dict:
    inp = _fwd_setup_inputs(seed)
    key = _jax.random.fold_in(_jax.random.key(seed), 7919)
    shape, _ = _output_shape()
    out = dict(inp)
    out["loss_target"] = _jax.random.normal(_jax.random.fold_in(key, 0), shape, _jnp.float32)
    for i, name in enumerate(TWIN_WEIGHTS):
        w = inp[name].astype(_jnp.float32)
        if MOMENT_SCALE is None:
            s = _jnp.sqrt(_jnp.mean(_jnp.square(w)) + 1e-30)
        else:
            s = MOMENT_SCALE[name]
        km, kv = _jax.random.split(_jax.random.fold_in(key, i + 1))
        out[name] = w
        out["m_" + name] = s * _jax.random.normal(km, w.shape, _jnp.float32)
        out["v_" + name] = (s * s) * _jax.random.uniform(kv, w.shape, _jnp.float32, 0.5, 1.5)
    if N_MICROBATCH > 1:
        for name, axis in PER_EXAMPLE_BATCH_AXIS.items():
            out[name] = _to_microbatches(out[name], axis)
    return {'x': out['x'], 'c': out['c'], 'ctx': out['ctx'], 'c_ctx': out['c_ctx'], 'w_ada': out['w_ada'], 'b_ada': out['b_ada'], 'g_norm1': out['g_norm1'], 'g_norm2': out['g_norm2'], 'w_in': out['w_in'], 'conv_w': out['conv_w'], 'conv_b': out['conv_b'], 'sink': out['sink'], 'g_out_conv': out['g_out_conv'], 'g_out_attn': out['g_out_attn'], 'w_out': out['w_out'], 'w_mlp1': out['w_mlp1'], 'w_mlp2': out['w_mlp2'], 'g_final': out['g_final'], 'loss_target': out['loss_target'], 'm_c_ctx': out['m_c_ctx'], 'm_w_ada': out['m_w_ada'], 'm_b_ada': out['m_b_ada'], 'm_g_norm1': out['m_g_norm1'], 'm_g_norm2': out['m_g_norm2'], 'm_w_in': out['m_w_in'], 'm_conv_w': out['m_conv_w'], 'm_conv_b': out['m_conv_b'], 'm_sink': out['m_sink'], 'm_g_out_conv': out['m_g_out_conv'], 'm_g_out_attn': out['m_g_out_attn'], 'm_w_out': out['m_w_out'], 'm_w_mlp1': out['m_w_mlp1'], 'm_w_mlp2': out['m_w_mlp2'], 'm_g_final': out['m_g_final'], 'v_c_ctx': out['v_c_ctx'], 'v_w_ada': out['v_w_ada'], 'v_b_ada': out['v_b_ada'], 'v_g_norm1': out['v_g_norm1'], 'v_g_norm2': out['v_g_norm2'], 'v_w_in': out['v_w_in'], 'v_conv_w': out['v_conv_w'], 'v_conv_b': out['v_conv_b'], 'v_sink': out['v_sink'], 'v_g_out_conv': out['v_g_out_conv'], 'v_g_out_attn': out['v_g_out_attn'], 'v_w_out': out['v_w_out'], 'v_w_mlp1': out['v_w_mlp1'], 'v_w_mlp2': out['v_w_mlp2'], 'v_g_final': out['v_g_final']}


def _loss(weights, diff, rest, loss_target):
    with _jax.named_scope("forward"):
        args = {**rest, TWIN_DIFF_INPUT: diff, **{k: w.astype(_WEIGHT_DTYPES[k]) for k, w in weights.items()}}
        y = _forward(args)
    with _jax.named_scope("loss_head"):
        err = _jnp.square(y.astype(_jnp.float32) - loss_target)
        return 0.5 * _jnp.sum(_jnp.mean(err, axis=-1)) if err.ndim else 0.5 * err


def _adamw(w, g, m, v):
    m = ADAM_B1 * m + (1.0 - ADAM_B1) * g
    v = ADAM_B2 * v + (1.0 - ADAM_B2) * _jnp.square(g)
    m_hat = m / (1.0 - ADAM_B1 ** ADAM_STEP)
    v_hat = v / (1.0 - ADAM_B2 ** ADAM_STEP)
    delta = -ADAM_LR * (m_hat / (_jnp.sqrt(v_hat) + ADAM_EPS) + ADAM_WD * w)
    return delta, m, v


def reference(x, c, ctx, c_ctx, w_ada, b_ada, g_norm1, g_norm2, w_in, conv_w, conv_b, sink, g_out_conv, g_out_attn, w_out, w_mlp1, w_mlp2, g_final, loss_target, m_c_ctx, m_w_ada, m_b_ada, m_g_norm1, m_g_norm2, m_w_in, m_conv_w, m_conv_b, m_sink, m_g_out_conv, m_g_out_attn, m_w_out, m_w_mlp1, m_w_mlp2, m_g_final, v_c_ctx, v_w_ada, v_b_ada, v_g_norm1, v_g_norm2, v_w_in, v_conv_w, v_conv_b, v_sink, v_g_out_conv, v_g_out_attn, v_w_out, v_w_mlp1, v_w_mlp2, v_g_final):
    given = dict(x=x, c=c, ctx=ctx, c_ctx=c_ctx, w_ada=w_ada, b_ada=b_ada, g_norm1=g_norm1, g_norm2=g_norm2, w_in=w_in, conv_w=conv_w, conv_b=conv_b, sink=sink, g_out_conv=g_out_conv, g_out_attn=g_out_attn, w_out=w_out, w_mlp1=w_mlp1, w_mlp2=w_mlp2, g_final=g_final, loss_target=loss_target, m_c_ctx=m_c_ctx, m_w_ada=m_w_ada, m_b_ada=m_b_ada, m_g_norm1=m_g_norm1, m_g_norm2=m_g_norm2, m_w_in=m_w_in, m_conv_w=m_conv_w, m_conv_b=m_conv_b, m_sink=m_sink, m_g_out_conv=m_g_out_conv, m_g_out_attn=m_g_out_attn, m_w_out=m_w_out, m_w_mlp1=m_w_mlp1, m_w_mlp2=m_w_mlp2, m_g_final=m_g_final, v_c_ctx=v_c_ctx, v_w_ada=v_w_ada, v_b_ada=v_b_ada, v_g_norm1=v_g_norm1, v_g_norm2=v_g_norm2, v_w_in=v_w_in, v_conv_w=v_conv_w, v_conv_b=v_conv_b, v_sink=v_sink, v_g_out_conv=v_g_out_conv, v_g_out_attn=v_g_out_attn, v_w_out=v_w_out, v_w_mlp1=v_w_mlp1, v_w_mlp2=v_w_mlp2, v_g_final=v_g_final)
    weights = {n: given[n] for n in TWIN_WEIGHTS}
    shared = {n: given[n] for n in SHARED_INPUTS}
    per_example = {n: given[n] for n in ['x', 'c', 'ctx']}
    grad_fn = _jax.value_and_grad(_loss, argnums=(0, 1))

    def one_microbatch(ex, loss_target):
        ex = dict(ex)
        diff = ex.pop(TWIN_DIFF_INPUT)
        return grad_fn(weights, diff, {**shared, **ex}, loss_target)

    if N_MICROBATCH == 1:
        loss, (grad_w, grad_x) = one_microbatch(per_example, given["loss_target"])
    else:
        def body(carry, xs):
            loss_sum, grad_sum = carry
            l_k, (gw_k, gx_k) = one_microbatch(xs[0], xs[1])
            with _jax.named_scope("update"):
                return (loss_sum + l_k, _jax.tree.map(_jnp.add, grad_sum, gw_k)), gx_k

        init = (_jnp.zeros((), _jnp.float32), _jax.tree.map(_jnp.zeros_like, weights))
        (loss, grad_w), grad_x = _jax.lax.scan(body, init, (per_example, given["loss_target"]))
    with _jax.named_scope("update"):
        delta_w, new_m, new_v = {}, {}, {}
        for n in TWIN_WEIGHTS:
            delta_w[n], new_m[n], new_v[n] = _adamw(weights[n], grad_w[n], given["m_" + n], given["v_" + n])
    return (loss, grad_x, *[grad_w[n] for n in TWIN_WEIGHTS], *[delta_w[n] for n in TWIN_WEIGHTS],
            *[new_m[n] for n in TWIN_WEIGHTS], *[new_v[n] for n in TWIN_WEIGHTS])
```

```python
import functools

import jax
import jax.numpy as jnp
from jax import lax
from jax.experimental import pallas as pl
from jax.experimental.pallas import tpu as pltpu

HEAD_DIM = 64
N_KV_HEADS = 4
BLOCK = 128
GRID_W = 64
ROPE_THETA = 10000.0
EPS = 1e-6
N_MOD = 6
SCALE = HEAD_DIM ** -0.5
NEG_INF = -1e30
ADAM_LR = 0.001
ADAM_B1 = 0.9
ADAM_B2 = 0.999
ADAM_EPS = 1e-08
ADAM_WD = 0.01
ADAM_STEP = 10
N_DEV = 8
N_CHIP = 4
VMEM_LIMIT_BYTES = 48 * 1024 * 1024
MESH = pl.DeviceIdType.MESH
BF16 = jnp.bfloat16
F32 = jnp.float32
ANY = pl.BlockSpec(memory_space=pl.ANY)


def _pick(dim, prefs):
    for p in prefs:
        if p <= dim and dim % p == 0:
            return p
    return dim


def _params(sem):
    return pltpu.CompilerParams(dimension_semantics=sem, vmem_limit_bytes=VMEM_LIMIT_BYTES)


def _row_ids(i, tr):
    return i * tr + lax.broadcasted_iota(jnp.int32, (tr, 1), 0)


def _colsum(v):
    return jnp.sum(v, axis=0, keepdims=True)


def _rowmean(v):
    return jnp.mean(v, axis=1, keepdims=True)


def _all_gather(xs, name):
    na = len(xs)

    def body(*refs):
        x_refs, o_refs = refs[:na], refs[na:2 * na]
        send_sems, recv_sems, local_sems = refs[2 * na:]
        x, y, c = lax.axis_index("x"), lax.axis_index("y"), lax.axis_index("c")
        me, sibling = (x, y, c), (x, y, 1 - c)
        chips = [(1 - x, y), (x, 1 - y), (1 - x, 1 - y)]

        def slot(a, px, py, pc):
            return o_refs[a].at[4 * px + 2 * py + pc]

        def copy(a, k, block, to, src=None):
            return pltpu.make_async_remote_copy(
                src_ref=slot(a, *block) if src is None else src, dst_ref=slot(a, *block),
                send_sem=send_sems.at[a, k], recv_sem=recv_sems.at[a, k], device_id=to, device_id_type=MESH)

        mine = [pltpu.make_async_copy(x_refs[a], slot(a, *me), local_sems.at[a]) for a in range(na)]
        for cp in mine:
            cp.start()
        first = []
        for a in range(na):
            first.append(copy(a, 0, me, sibling, src=x_refs[a]))
            first += [copy(a, 1 + j, me, (*chip, c), src=x_refs[a]) for j, chip in enumerate(chips)]
        for cp in first:
            cp.start()
        passed = []
        for j, chip in enumerate(chips):
            for a in range(na):
                copy(a, 1 + j, (*chip, c), me).wait_recv()
                fwd = copy(a, 4 + j, (*chip, c), sibling)
                fwd.start()
                passed.append(fwd)
        for a in range(na):
            copy(a, 0, sibling, me).wait_recv()
            for j, chip in enumerate(chips):
                copy(a, 4 + j, (*chip, 1 - c), me).wait_recv()
        for cp in first + passed:
            cp.wait_send()
        for cp in mine:
            cp.wait()

    outs = pl.pallas_call(
        body, name=name,
        out_shape=tuple(jax.ShapeDtypeStruct((N_DEV,) + x.shape, x.dtype) for x in xs),
        in_specs=[ANY] * na, out_specs=tuple([ANY] * na),
        scratch_shapes=[pltpu.SemaphoreType.DMA((na, 7)), pltpu.SemaphoreType.DMA((na, 7)),
                        pltpu.SemaphoreType.DMA((na,))],
    )(*xs)
    return list(outs)


def _xchg_sibling(srcs, name):
    na = len(srcs)

    def body(*refs):
        s_refs, o_refs = refs[:na], refs[na:2 * na]
        send_sems, recv_sems = refs[2 * na:]
        x, y, c = lax.axis_index("x"), lax.axis_index("y"), lax.axis_index("c")
        sibling = (x, y, 1 - c)

        def copy(a, k):
            return pltpu.make_async_remote_copy(
                src_ref=s_refs[a].at[2 * k + (1 - c)], dst_ref=o_refs[a].at[k],
                send_sem=send_sems.at[a, k], recv_sem=recv_sems.at[a, k], device_id=sibling, device_id_type=MESH)

        cps = [copy(a, k) for a in range(na) for k in range(N_CHIP)]
        for cp in cps:
            cp.start()
        for cp in cps:
            cp.wait_recv()
        for cp in cps:
            cp.wait_send()

    outs = pl.pallas_call(
        body, name=name,
        out_shape=tuple(jax.ShapeDtypeStruct((N_CHIP,) + s.shape[1:], s.dtype) for s in srcs),
        in_specs=[ANY] * na, out_specs=tuple([ANY] * na),
        scratch_shapes=[pltpu.SemaphoreType.DMA((na, N_CHIP)), pltpu.SemaphoreType.DMA((na, N_CHIP))],
    )(*srcs)
    return list(outs)


def _xchg_chips(srcs, name):
    na = len(srcs)

    def body(*refs):
        s_refs, o_refs = refs[:na], refs[na:2 * na]
        send_sems, recv_sems, local_sems = refs[2 * na:]
        x, y, c = lax.axis_index("x"), lax.axis_index("y"), lax.axis_index("c")
        my_chip = 2 * x + y
        chips = [(1 - x, y), (x, 1 - y), (1 - x, 1 - y)]

        def copy(a, j):
            px, py = chips[j]
            return pltpu.make_async_remote_copy(
                src_ref=s_refs[a].at[2 * px + py], dst_ref=o_refs[a].at[my_chip],
                send_sem=send_sems.at[a, j], recv_sem=recv_sems.at[a, j], device_id=(px, py, c), device_id_type=MESH)

        def arrival(a, j):
            px, py = chips[j]
            return pltpu.make_async_remote_copy(
                src_ref=s_refs[a].at[2 * px + py], dst_ref=o_refs[a].at[2 * px + py],
                send_sem=send_sems.at[a, j], recv_sem=recv_sems.at[a, j], device_id=(px, py, c), device_id_type=MESH)

        mine = [pltpu.make_async_copy(s_refs[a].at[my_chip], o_refs[a].at[my_chip], local_sems.at[a]) for a in range(na)]
        for cp in mine:
            cp.start()
        cps = [copy(a, j) for a in range(na) for j in range(3)]
        for cp in cps:
            cp.start()
        for a in range(na):
            for j in range(3):
                arrival(a, j).wait_recv()
        for cp in cps:
            cp.wait_send()
        for cp in mine:
            cp.wait()

    outs = pl.pallas_call(
        body, name=name,
        out_shape=tuple(jax.ShapeDtypeStruct(s.shape, s.dtype) for s in srcs),
        in_specs=[ANY] * na, out_specs=tuple([ANY] * na),
        scratch_shapes=[pltpu.SemaphoreType.DMA((na, 3)), pltpu.SemaphoreType.DMA((na, 3)),
                        pltpu.SemaphoreType.DMA((na,))],
    )(*srcs)
    return list(outs)


def _to_bf16(x2d, name):
    r, c = x2d.shape
    tr = _pick(r, (512, 256, 128, 64, 32, 16))

    def body(x_ref, o_ref):
        o_ref[...] = x_ref[...].astype(BF16)

    return pl.pallas_call(
        body, name=name, out_shape=jax.ShapeDtypeStruct((r, c), BF16), grid=(r // tr,),
        in_specs=[pl.BlockSpec((tr, c), lambda i: (i, 0))], out_specs=pl.BlockSpec((tr, c), lambda i: (i, 0)),
        compiler_params=_params(("parallel",)),
    )(x2d)


def _pair_add(own8, got4, name):
    _, r, c = own8.shape
    tr = _pick(r, (512, 256, 128, 64, 32, 16))
    core = lax.axis_index("c").astype(jnp.int32).reshape(1)

    def body(c_ref, a_ref, b_ref, o_ref):
        o_ref[...] = (a_ref[...] + b_ref[...]).astype(BF16)

    return pl.pallas_call(
        body, name=name, out_shape=jax.ShapeDtypeStruct((N_CHIP, r, c), BF16),
        grid_spec=pltpu.PrefetchScalarGridSpec(
            num_scalar_prefetch=1, grid=(N_CHIP, r // tr),
            in_specs=[pl.BlockSpec((None, tr, c), lambda k, i, cr: (2 * k + cr[0], i, 0)),
                      pl.BlockSpec((None, tr, c), lambda k, i, cr: (k, i, 0))],
            out_specs=pl.BlockSpec((None, tr, c), lambda k, i, cr: (k, i, 0))),
        compiler_params=_params(("parallel", "parallel")),
    )(core, own8, got4)


def _adamw(w, m, v, parts, name, premul=None):
    r, c = w.shape
    n_parts = parts.shape[0]
    tr = _pick(r, (256, 128, 64, 32, 16, 8))
    tc = _pick(c, (1024, 512, 256, 128))
    has_pre = premul is not None

    def body(*refs):
        if has_pre:
            w_ref, m_ref, v_ref, p_ref, q_ref, g_ref, d_ref, nm_ref, nv_ref = refs
        else:
            w_ref, m_ref, v_ref, p_ref, g_ref, d_ref, nm_ref, nv_ref = refs
        g = p_ref[0].astype(F32)
        for k in range(1, n_parts):
            g = g + p_ref[k].astype(F32)
        if has_pre:
            g = g * q_ref[...]
        wv = w_ref[...]
        nm = ADAM_B1 * m_ref[...] + (1.0 - ADAM_B1) * g
        nv = ADAM_B2 * v_ref[...] + (1.0 - ADAM_B2) * (g * g)
        m_hat = nm / (1.0 - ADAM_B1 ** ADAM_STEP)
        v_hat = nv / (1.0 - ADAM_B2 ** ADAM_STEP)
        g_ref[...] = g
        d_ref[...] = -ADAM_LR * (m_hat / (jnp.sqrt(v_hat) + ADAM_EPS) + ADAM_WD * wv)
        nm_ref[...] = nm
        nv_ref[...] = nv

    tile = pl.BlockSpec((tr, tc), lambda i, j: (i, j))
    ins = [w, m, v, parts] + ([premul] if has_pre else [])
    in_specs = [tile, tile, tile, pl.BlockSpec((n_parts, tr, tc), lambda i, j: (0, i, j))] + ([tile] if has_pre else [])
    sh = jax.ShapeDtypeStruct((r, c), F32)
    return pl.pallas_call(
        body, name=name, out_shape=(sh, sh, sh, sh), grid=(r // tr, c // tc),
        in_specs=in_specs, out_specs=(tile, tile, tile, tile),
        compiler_params=_params(("parallel", "parallel")),
    )(*ins)


def _matmul(a, b, *, dims, shape, tiles, b_spec=None, out_specs=None, out_shapes=None, out_dtypes=(F32,),
            epilogue=None, extras=(), alias=None, name):
    m_dim, n_dim, k_dim = shape
    tm, tn, tk = tiles
    assert m_dim % tm == 0 and n_dim % tn == 0 and k_dim % tk == 0, (name, shape, tiles)
    nk = k_dim // tk
    n_extra = len(extras)
    n_alias = 0 if alias is None else 1
    n_out = len(out_dtypes)
    if dims == "tn":
        a_spec = pl.BlockSpec((tk, tm), lambda i, j, k: (k, i))
        contract = (((0,), (0,)), ((), ()))
    else:
        a_spec = pl.BlockSpec((tm, tk), lambda i, j, k: (i, k))
        contract = (((1,), (1,)), ((), ())) if dims == "nt" else (((1,), (0,)), ((), ()))
    if b_spec is None:
        b_spec = (pl.BlockSpec((tn, tk), lambda i, j, k: (j, k)) if dims == "nt"
                  else pl.BlockSpec((tk, tn), lambda i, j, k: (k, j)))
    if out_specs is None:
        out_specs = tuple(pl.BlockSpec((tm, tn), lambda i, j, k: (i, j)) for _ in range(n_out))
    if out_shapes is None:
        out_shapes = tuple(jax.ShapeDtypeStruct((m_dim, n_dim), d) for d in out_dtypes)

    def body(*refs):
        a_ref, b_ref = refs[0], refs[1]
        extra_refs = refs[2:2 + n_extra]
        out_refs = refs[2 + n_extra + n_alias:2 + n_extra + n_alias + n_out]
        acc_ref = refs[-1]
        k = pl.program_id(2)

        @pl.when(k == 0)
        def _():
            acc_ref[...] = jnp.zeros_like(acc_ref)

        acc_ref[...] += lax.dot_general(a_ref[...].astype(BF16), b_ref[...].astype(BF16), contract,
                                        preferred_element_type=F32)

        @pl.when(k == nk - 1)
        def _():
            acc = acc_ref[...]
            if epilogue is None:
                out_refs[0][...] = acc.astype(out_refs[0].dtype)
            else:
                epilogue(acc, pl.program_id(0), pl.program_id(1), extra_refs, out_refs)

    ins = [a, b] + [e[0] for e in extras]
    in_specs = [a_spec, b_spec] + [e[1] for e in extras]
    io_alias = {}
    if alias is not None:
        ins.append(alias)
        in_specs.append(ANY)
        io_alias = {len(ins) - 1: 0}
    outs = pl.pallas_call(
        body, name=name, out_shape=tuple(out_shapes), grid=(m_dim // tm, n_dim // tn, nk),
        in_specs=in_specs, out_specs=tuple(out_specs), scratch_shapes=[pltpu.VMEM((tm, tn), F32)],
        input_output_aliases=io_alias,
        compiler_params=_params(("parallel", "parallel", "arbitrary")),
    )(*ins)
    return outs if n_out > 1 else outs[0]


def _w_spec(kind, layer, per, dims, tn, tk):
    if kind == "nat":
        if dims == "nn":
            return pl.BlockSpec((None, tk, tn), lambda i, j, k: (layer, k, j))
        return pl.BlockSpec((None, tn, tk), lambda i, j, k: (layer, j, k))
    if kind == "rows":
        if dims == "nn":
            q = per // tk
            return pl.BlockSpec((None, None, tk, tn), lambda i, j, k: (k // q, layer, k % q, j))
        q = per // tn
        return pl.BlockSpec((None, None, tn, tk), lambda i, j, k: (j // q, layer, j % q, k))
    if dims == "nn":
        q = per // tn
        return pl.BlockSpec((None, None, tk, tn), lambda i, j, k: (j // q, layer, k, j % q))
    q = per // tk
    return pl.BlockSpec((None, None, tn, tk), lambda i, j, k: (k // q, layer, j, k % q))


def _g_spec(kind, layer, per, tm, tn):
    if kind == "nat":
        return pl.BlockSpec((None, tm, tn), lambda i, j, k: (layer, i, j))
    if kind == "rows":
        q = per // tm
        return pl.BlockSpec((None, None, tm, tn), lambda i, j, k: (i // q, layer, i % q, j))
    q = per // tn
    return pl.BlockSpec((None, None, tm, tn), lambda i, j, k: (j // q, layer, i, j % q))


TOK = (768, 256, 128)
FEAT = (1024, 512, 256, 128)
KDIM = (512, 256, 128)


def _sel(is_ctx, ref):
    return jnp.where(is_ctx, ref[1:2, :], ref[0:1, :])


def _norm_mod(t, gain, mod8, shift_k, n_ctx, name):
    n, d = t.shape
    tr = _pick(n, (256, 128))

    def body(t_ref, g_ref, sh_ref, sc_ref, o_ref):
        x = t_ref[...]
        r = lax.rsqrt(_rowmean(x * x) + EPS)
        y = (x * r) * g_ref[...]
        is_ctx = _row_ids(pl.program_id(0), tr) < n_ctx
        o_ref[...] = (y * (1.0 + _sel(is_ctx, sc_ref)) + _sel(is_ctx, sh_ref)).astype(BF16)

    return pl.pallas_call(
        body, name=name, out_shape=jax.ShapeDtypeStruct((n, d), BF16), grid=(n // tr,),
        in_specs=[pl.BlockSpec((tr, d), lambda i: (i, 0)), pl.BlockSpec((1, d), lambda i: (0, 0)),
                  pl.BlockSpec((8, d), lambda i: (0, shift_k)), pl.BlockSpec((8, d), lambda i: (0, shift_k + 1))],
        out_specs=pl.BlockSpec((tr, d), lambda i: (i, 0)),
        compiler_params=_params(("parallel",)),
    )(t, gain, mod8, mod8)


def _norm_mod_bwd(dh, t, d_res, gain, mod8, shift_k, n_ctx, name):
    n, d = t.shape
    tr = _pick(n, (256, 128))

    def body(dh_ref, t_ref, dr_ref, g_ref, sc_ref, dt_ref, dss_ref, dg_ref):
        i = pl.program_id(0)

        @pl.when(i == 0)
        def _():
            dss_ref[...] = jnp.zeros_like(dss_ref)
            dg_ref[...] = jnp.zeros_like(dg_ref)

        x = t_ref[...]
        r = lax.rsqrt(_rowmean(x * x) + EPS)
        xn = x * r
        g = g_ref[...]
        y = xn * g
        dhv = dh_ref[...]
        is_ctx = _row_ids(i, tr) < n_ctx
        zero = jnp.zeros_like(dhv)
        dhy = dhv * y
        dss_ref[0:1, 0:d] += _colsum(jnp.where(is_ctx, zero, dhv))
        dss_ref[1:2, 0:d] += _colsum(jnp.where(is_ctx, dhv, zero))
        dss_ref[0:1, d:2 * d] += _colsum(jnp.where(is_ctx, zero, dhy))
        dss_ref[1:2, d:2 * d] += _colsum(jnp.where(is_ctx, dhy, zero))
        dy = dhv * (1.0 + _sel(is_ctx, sc_ref))
        dg_ref[0:1, :] += _colsum(dy * xn)
        dxn = dy * g
        dt_ref[...] = dr_ref[...] + r * (dxn - xn * _rowmean(dxn * xn))

    row = pl.BlockSpec((tr, d), lambda i: (i, 0))
    return pl.pallas_call(
        body, name=name,
        out_shape=(jax.ShapeDtypeStruct((n, d), F32), jax.ShapeDtypeStruct((8, 2 * d), F32), jax.ShapeDtypeStruct((8, d), F32)),
        grid=(n // tr,),
        in_specs=[row, row, row, pl.BlockSpec((1, d), lambda i: (0, 0)), pl.BlockSpec((8, d), lambda i: (0, shift_k + 1))],
        out_specs=(row, pl.BlockSpec((8, 2 * d), lambda i: (0, 0)), pl.BlockSpec((8, d), lambda i: (0, 0))),
        compiler_params=_params(("arbitrary",)),
    )(dh, t, d_res, gain, mod8)


def _gate_bwd(d_t, branch, mod8, gate_k, n_ctx, name):
    n, d = d_t.shape
    tr = _pick(n, (256, 128))

    def body(dt_ref, o_ref, gt_ref, dob_ref, dgate_ref):
        i = pl.program_id(0)

        @pl.when(i == 0)
        def _():
            dgate_ref[...] = jnp.zeros_like(dgate_ref)

        dv = dt_ref[...]
        is_ctx = _row_ids(i, tr) < n_ctx
        dob_ref[...] = (dv * _sel(is_ctx, gt_ref)).astype(BF16)
        prod = dv * o_ref[...].astype(F32)
        zero = jnp.zeros_like(prod)
        dgate_ref[0:1, :] += _colsum(jnp.where(is_ctx, zero, prod))
        dgate_ref[1:2, :] += _colsum(jnp.where(is_ctx, prod, zero))

    row = pl.BlockSpec((tr, d), lambda i: (i, 0))
    return pl.pallas_call(
        body, name=name, out_shape=(jax.ShapeDtypeStruct((n, d), BF16), jax.ShapeDtypeStruct((8, d), F32)),
        grid=(n // tr,),
        in_specs=[row, row, pl.BlockSpec((8, d), lambda i: (0, gate_k))],
        out_specs=(row, pl.BlockSpec((8, d), lambda i: (0, 0))),
        compiler_params=_params(("arbitrary",)),
    )(d_t, branch, mod8)


def _swap16(v):
    w = v.shape[1]
    lane = lax.broadcasted_iota(jnp.int32, v.shape, 1)
    return jnp.where((lane % 32) < 16, pltpu.roll(v, w - 16, 1), pltpu.roll(v, 16, 1))


def _rope(v, cs, sn, sign):
    reps = v.shape[1] // 128
    c = jnp.tile(cs, (1, reps)) if reps > 1 else cs
    s = jnp.tile(sn, (1, reps)) if reps > 1 else sn
    return v * c + sign * (_swap16(v) * s)


def _rope_split(p, cs, sn, dc, hd, kd, name):
    n, d_in = p.shape
    tr = _pick(n, (256, 128))
    q0 = 3 * dc

    def body(p_ref, cs_ref, sn_ref, o_ref):
        cs_v, sn_v = cs_ref[...], sn_ref[...]
        q = _rope(p_ref[:, q0:q0 + hd], cs_v, sn_v, 1.0) * SCALE
        k = _rope(p_ref[:, q0 + hd:q0 + hd + kd], cs_v, sn_v, 1.0)
        v = p_ref[:, q0 + hd + kd:q0 + hd + 2 * kd]
        o_ref[...] = jnp.concatenate([q, k, v], axis=1).astype(BF16)

    return pl.pallas_call(
        body, name=name, out_shape=jax.ShapeDtypeStruct((n, hd + 2 * kd), BF16), grid=(n // tr,),
        in_specs=[pl.BlockSpec((tr, d_in), lambda i: (i, 0)), pl.BlockSpec((tr, 128), lambda i: (i, 0)),
                  pl.BlockSpec((tr, 128), lambda i: (i, 0))],
        out_specs=pl.BlockSpec((tr, hd + 2 * kd), lambda i: (i, 0)),
        compiler_params=_params(("parallel",)),
    )(p, cs, sn)


def _attn_specs(nb, n_ctx, hd, kd):
    kci = hd // kd
    specs = [pl.BlockSpec((BLOCK, hd), lambda b: (b, 0)),
             pl.BlockSpec((n_ctx, kd), lambda b: (0, kci)), pl.BlockSpec((n_ctx, kd), lambda b: (0, kci + 1))]
    for col in (kci, kci + 1):
        specs.append(pl.BlockSpec((BLOCK, kd), lambda b, col=col: (jnp.maximum(b - 1, 0), col)))
        specs.append(pl.BlockSpec((BLOCK, kd), lambda b, col=col: (b, col)))
        specs.append(pl.BlockSpec((BLOCK, kd), lambda b, col=col: (jnp.minimum(b + 1, nb - 1), col)))
    return specs


def _band_valid(b, group, n_ctx, n):
    q_pos = b * BLOCK + lax.broadcasted_iota(jnp.int32, (group * BLOCK, 1), 0) % BLOCK
    k_pos = (b - 1) * BLOCK + lax.broadcasted_iota(jnp.int32, (1, 3 * BLOCK), 1)
    return (jnp.abs(k_pos - q_pos) <= BLOCK) & (k_pos >= n_ctx) & (k_pos < n) & (q_pos >= n_ctx)


NT = (((1,), (1,)), ((), ()))
NN = (((1,), (0,)), ((), ()))
TN = (((0,), (0,)), ((), ()))


def _dot(a, b, dn):
    return lax.dot_general(a, b, dn, preferred_element_type=F32)


def _softmax_parts(qg, kc, kl, valid, snk):
    s_c = _dot(qg, kc, NT)
    s_l = jnp.where(valid, _dot(qg, kl, NT), NEG_INF)
    m = jnp.maximum(jnp.maximum(jnp.max(s_c, axis=1, keepdims=True), jnp.max(s_l, axis=1, keepdims=True)), snk)
    e_c, e_l, e_s = jnp.exp(s_c - m), jnp.exp(s_l - m), jnp.exp(snk - m)
    den = jnp.sum(e_c, axis=1, keepdims=True) + jnp.sum(e_l, axis=1, keepdims=True) + e_s
    return e_c / den, e_l / den, e_s / den


def _attention(qkv, sink, n_ctx, hd, kd, name):
    n = qkv.shape[0]
    nb = n // BLOCK
    n_kv = kd // HEAD_DIM
    group = hd // kd

    def body(q_ref, kc_ref, vc_ref, k0, k1, k2, v0, v1, v2, sink_ref, o_ref):
        b = pl.program_id(0)
        valid = _band_valid(b, group, n_ctx, n)
        heads = []
        for h in range(n_kv):
            hs = slice(h * HEAD_DIM, (h + 1) * HEAD_DIM)
            qg = jnp.concatenate([q_ref[:, (h * group + j) * HEAD_DIM:(h * group + j + 1) * HEAD_DIM] for j in range(group)], axis=0)
            kl = jnp.concatenate([k0[:, hs], k1[:, hs], k2[:, hs]], axis=0)
            vl = jnp.concatenate([v0[:, hs], v1[:, hs], v2[:, hs]], axis=0)
            snk = jnp.concatenate([jnp.full((BLOCK, 1), sink_ref[h * group + j], F32) for j in range(group)], axis=0)
            p_c, p_l, _ = _softmax_parts(qg, kc_ref[:, hs], kl, valid, snk)
            o = _dot(p_c.astype(BF16), vc_ref[:, hs], NN) + _dot(p_l.astype(BF16), vl, NN)
            heads += [o[j * BLOCK:(j + 1) * BLOCK, :] for j in range(group)]
        o_ref[...] = jnp.concatenate(heads, axis=1)

    return pl.pallas_call(
        body, name=name, out_shape=jax.ShapeDtypeStruct((n, hd), F32), grid=(nb,),
        in_specs=_attn_specs(nb, n_ctx, hd, kd) + [pl.BlockSpec(memory_space=pltpu.SMEM)],
        out_specs=pl.BlockSpec((BLOCK, hd), lambda b: (b, 0)),
        compiler_params=_params(("parallel",)),
    )(qkv, qkv, qkv, qkv, qkv, qkv, qkv, qkv, qkv, sink)


def _attention_bwd(qkv, sink, ao, d_mg, cpar, n_ctx, hd, kd, name):
    n = qkv.shape[0]
    nb = n // BLOCK
    n_kv = kd // HEAD_DIM
    group = hd // kd
    n_heads = n_kv * group

    def body(q_ref, kc_ref, vc_ref, k0, k1, k2, v0, v1, v2, sink_ref, ao_ref, dmg_ref, cp_ref,
             dq_ref, part_ref, dctx_ref, dsink_ref, dgain_ref):
        b = pl.program_id(0)

        @pl.when(b == 0)
        def _():
            dctx_ref[...] = jnp.zeros_like(dctx_ref)
            dsink_ref[...] = jnp.zeros_like(dsink_ref)
            dgain_ref[...] = jnp.zeros_like(dgain_ref)

        ao_v = ao_ref[...]
        ra = lax.rsqrt(_rowmean(ao_v * ao_v) + EPS)
        an = ao_v * ra
        dmg = dmg_ref[...]
        dgain_ref[0:1, :] += _colsum(dmg * an)
        d_an = dmg * cp_ref[5:6, :]
        d_ao = (ra * (d_an - an * _rowmean(d_an * an))).astype(BF16)

        valid = _band_valid(b, group, n_ctx, n)
        lane = lax.broadcasted_iota(jnp.int32, (1, 128), 1)
        dsink_row = jnp.zeros((1, 128), F32)
        dq_heads, dkc, dvc, dkl, dvl = [], [], [], [], []
        for h in range(n_kv):
            hs = slice(h * HEAD_DIM, (h + 1) * HEAD_DIM)
            cols = [slice((h * group + j) * HEAD_DIM, (h * group + j + 1) * HEAD_DIM) for j in range(group)]
            qg = jnp.concatenate([q_ref[:, cs] for cs in cols], axis=0)
            dog = jnp.concatenate([d_ao[:, cs] for cs in cols], axis=0)
            kc, vc = kc_ref[:, hs], vc_ref[:, hs]
            kl = jnp.concatenate([k0[:, hs], k1[:, hs], k2[:, hs]], axis=0)
            vl = jnp.concatenate([v0[:, hs], v1[:, hs], v2[:, hs]], axis=0)
            snk = jnp.concatenate([jnp.full((BLOCK, 1), sink_ref[h * group + j], F32) for j in range(group)], axis=0)
            p_c, p_l, p_s = _softmax_parts(qg, kc, kl, valid, snk)
            dp_c = _dot(dog, vc, NT)
            dp_l = _dot(dog, vl, NT)
            delta = jnp.sum(p_c * dp_c, axis=1, keepdims=True) + jnp.sum(p_l * dp_l, axis=1, keepdims=True)
            ds_c = (p_c * (dp_c - delta)).astype(BF16)
            ds_l = (p_l * (dp_l - delta)).astype(BF16)
            psd = p_s * delta
            for j in range(group):
                val = -jnp.sum(psd[j * BLOCK:(j + 1) * BLOCK, :], axis=0, keepdims=True)
                dsink_row = dsink_row + jnp.where(lane == h * group + j, val, 0.0)
            dq = (_dot(ds_c, kc, NN) + _dot(ds_l, kl, NN)) * SCALE
            dq_heads += [dq[j * BLOCK:(j + 1) * BLOCK, :] for j in range(group)]
            dkc.append(_dot(ds_c, qg, TN))
            dvc.append(_dot(p_c.astype(BF16), dog, TN))
            dkl.append(_dot(ds_l, qg, TN))
            dvl.append(_dot(p_l.astype(BF16), dog, TN))
        dq_ref[...] = jnp.concatenate(dq_heads, axis=1)
        dctx_ref[...] += jnp.concatenate(dkc + dvc, axis=1)
        loc = jnp.concatenate(dkl + dvl, axis=1)
        for j in range(3):
            part_ref[j] = loc[j * BLOCK:(j + 1) * BLOCK, :]
        dsink_ref[0:1, :] += dsink_row

    assert n_heads <= 128
    out_shape = (jax.ShapeDtypeStruct((n, hd), F32), jax.ShapeDtypeStruct((nb, 3, BLOCK, 2 * kd), F32),
                 jax.ShapeDtypeStruct((n_ctx, 2 * kd), F32), jax.ShapeDtypeStruct((8, 128), F32),
                 jax.ShapeDtypeStruct((8, hd), F32))
    return pl.pallas_call(
        body, name=name, out_shape=out_shape, grid=(nb,),
        in_specs=_attn_specs(nb, n_ctx, hd, kd) + [
            pl.BlockSpec(memory_space=pltpu.SMEM), pl.BlockSpec((BLOCK, hd), lambda b: (b, 0)),
            pl.BlockSpec((BLOCK, hd), lambda b: (b, 1)), pl.BlockSpec((8, hd), lambda b: (0, 0))],
        out_specs=(pl.BlockSpec((BLOCK, hd), lambda b: (b, 0)),
                   pl.BlockSpec((None, 3, BLOCK, 2 * kd), lambda b: (b, 0, 0, 0)),
                   pl.BlockSpec((n_ctx, 2 * kd), lambda b: (0, 0)), pl.BlockSpec((8, 128), lambda b: (0, 0)),
                   pl.BlockSpec((8, hd), lambda b: (0, 0))),
        compiler_params=_params(("arbitrary",)),
    )(qkv, qkv, qkv, qkv, qkv, qkv, qkv, qkv, qkv, sink, ao, d_mg, cpar)


def _halo_specs(tr, n, width, col=0):
    q = tr // 8
    return [pl.BlockSpec((8, width), lambda i: (jnp.maximum(i * q - 1, 0), col)),
            pl.BlockSpec((8, width), lambda i: (jnp.minimum((i + 1) * q, n // 8 - 1), col))]


def _mix_fwd(p, ao, cpar, n_ctx, dc, name):
    n, d_in = p.shape
    tr = _pick(n, (256, 128))

    def body(p_ref, pp_ref, pn_ref, ao_ref, cp_ref, o_ref):
        i = pl.program_id(0)
        bg = p_ref[:, 0:dc]
        u = p_ref[:, dc:2 * dc] * p_ref[:, 2 * dc:3 * dc]
        u_before = pp_ref[7:8, dc:2 * dc] * pp_ref[7:8, 2 * dc:3 * dc]
        u_after = pn_ref[0:1, dc:2 * dc] * pn_ref[0:1, 2 * dc:3 * dc]
        loc = lax.broadcasted_iota(jnp.int32, (tr, 1), 0)
        gid = i * tr + loc
        has_prev = (gid != 0) & (gid != n_ctx)
        has_next = (gid != n_ctx - 1) & (gid != n - 1)
        u_m1 = jnp.where(has_prev, jnp.where(loc == 0, u_before, pltpu.roll(u, 1, 0)), 0.0)
        u_p1 = jnp.where(has_next, jnp.where(loc == tr - 1, u_after, pltpu.roll(u, tr - 1, 0)), 0.0)
        cv = u_m1 * cp_ref[0:1, :] + u * cp_ref[1:2, :] + u_p1 * cp_ref[2:3, :] + cp_ref[3:4, :]
        co = bg * cv
        nc = (co * lax.rsqrt(_rowmean(co * co) + EPS)) * cp_ref[4:5, :]
        ao_v = ao_ref[...]
        na = (ao_v * lax.rsqrt(_rowmean(ao_v * ao_v) + EPS)) * cp_ref[5:6, :]
        o_ref[...] = jnp.concatenate([nc, na], axis=1).astype(BF16)

    return pl.pallas_call(
        body, name=name, out_shape=jax.ShapeDtypeStruct((n, 2 * dc), BF16), grid=(n // tr,),
        in_specs=[pl.BlockSpec((tr, d_in), lambda i: (i, 0))] + _halo_specs(tr, n, d_in)
        + [pl.BlockSpec((tr, dc), lambda i: (i, 0)), pl.BlockSpec((8, dc), lambda i: (0, 0))],
        out_specs=pl.BlockSpec((tr, 2 * dc), lambda i: (i, 0)),
        compiler_params=_params(("parallel",)),
    )(p, p, p, ao, cpar)


def _mix_bwd(d_mg, p, cpar, d_q, parts, d_ctx, cs, sn, n_ctx, dc, hd, kd, name):
    n, d_in = p.shape
    tr = BLOCK
    nb = n // tr
    ext = tr + 16
    n_ctx_blocks = n_ctx // BLOCK

    def body(dm_ref, dmp_ref, dmn_ref, p_ref, pp_ref, pn_ref, cp_ref, dq_ref, pa_ref, pb_ref, pc_ref, dctx_ref,
             cs_ref, sn_ref, dp_ref, acc_ref):
        i = pl.program_id(0)

        @pl.when(i == 0)
        def _():
            acc_ref[...] = jnp.zeros_like(acc_ref)

        def cat(before, here, after):
            return jnp.concatenate([before, here, after], axis=0)

        bg = cat(pp_ref[:, 0:dc], p_ref[:, 0:dc], pn_ref[:, 0:dc])
        cg = cat(pp_ref[:, dc:2 * dc], p_ref[:, dc:2 * dc], pn_ref[:, dc:2 * dc])
        hh = cat(pp_ref[:, 2 * dc:3 * dc], p_ref[:, 2 * dc:3 * dc], pn_ref[:, 2 * dc:3 * dc])
        dme = cat(dmp_ref[...], dm_ref[...], dmn_ref[...])
        gid = i * tr - 8 + lax.broadcasted_iota(jnp.int32, (ext, 1), 0)
        inside = (gid >= 0) & (gid < n)
        has_prev = inside & (gid != 0) & (gid != n_ctx)
        has_next = inside & (gid != n_ctx - 1) & (gid != n - 1)
        w0, w1, w2, bias, gain = cp_ref[0:1, :], cp_ref[1:2, :], cp_ref[2:3, :], cp_ref[3:4, :], cp_ref[4:5, :]
        u = jnp.where(inside, cg * hh, 0.0)
        u_m1 = jnp.where(has_prev, pltpu.roll(u, 1, 0), 0.0)
        u_p1 = jnp.where(has_next, pltpu.roll(u, ext - 1, 0), 0.0)
        cv = u_m1 * w0 + u * w1 + u_p1 * w2 + bias
        co = bg * cv
        rc = lax.rsqrt(_rowmean(co * co) + EPS)
        cn = co * rc
        d_cn = dme * gain
        d_co = rc * (d_cn - cn * _rowmean(d_cn * cn))
        d_cv = jnp.where(inside, d_co * bg, 0.0)
        d_bg = d_co * cv
        d_cv_p1 = jnp.where(has_next, pltpu.roll(d_cv, ext - 1, 0), 0.0)
        d_cv_m1 = jnp.where(has_prev, pltpu.roll(d_cv, 1, 0), 0.0)
        d_u = d_cv_p1 * w0 + d_cv * w1 + d_cv_m1 * w2
        mid = slice(8, 8 + tr)
        acc_ref[0:1, :] += _colsum((d_cv * u_m1)[mid])
        acc_ref[1:2, :] += _colsum((d_cv * u)[mid])
        acc_ref[2:3, :] += _colsum((d_cv * u_p1)[mid])
        acc_ref[3:4, :] += _colsum(d_cv[mid])
        acc_ref[4:5, :] += _colsum((dme * cn)[mid])

        d_kv = (jnp.where(i >= 1, pa_ref[...], 0.0) + pb_ref[...] + jnp.where(i + 1 < nb, pc_ref[...], 0.0))
        ctx_rows = dctx_ref[pl.ds(pl.multiple_of(jnp.minimum(i, n_ctx_blocks - 1) * BLOCK, BLOCK), BLOCK), :]
        d_kv = d_kv + jnp.where(i < n_ctx_blocks, ctx_rows, 0.0)
        cs_v, sn_v = cs_ref[...], sn_ref[...]
        d_qu = _rope(dq_ref[...], cs_v, sn_v, -1.0)
        d_ku = _rope(d_kv[:, 0:kd], cs_v, sn_v, -1.0)
        dp_ref[...] = jnp.concatenate(
            [d_bg[mid], (d_u * hh)[mid], (d_u * cg)[mid], d_qu, d_ku, d_kv[:, kd:2 * kd]], axis=1).astype(BF16)

    part = lambda sel, which: pl.BlockSpec((None, None, BLOCK, 2 * kd), lambda i: (sel(i), which, 0, 0))
    return pl.pallas_call(
        body, name=name, out_shape=(jax.ShapeDtypeStruct((n, d_in), BF16), jax.ShapeDtypeStruct((8, dc), F32)),
        grid=(nb,),
        in_specs=[pl.BlockSpec((tr, dc), lambda i: (i, 0))] + _halo_specs(tr, n, dc)
        + [pl.BlockSpec((tr, d_in), lambda i: (i, 0))] + _halo_specs(tr, n, d_in)
        + [pl.BlockSpec((8, dc), lambda i: (0, 0)), pl.BlockSpec((tr, hd), lambda i: (i, 0)),
           part(lambda i: jnp.maximum(i - 1, 0), 2), part(lambda i: i, 1), part(lambda i: jnp.minimum(i + 1, nb - 1), 0),
           pl.BlockSpec((n_ctx, 2 * kd), lambda i: (0, 0)),
           pl.BlockSpec((tr, 128), lambda i: (i, 0)), pl.BlockSpec((tr, 128), lambda i: (i, 0))],
        out_specs=(pl.BlockSpec((tr, d_in), lambda i: (i, 0)), pl.BlockSpec((8, dc), lambda i: (0, 0))),
        compiler_params=_params(("arbitrary",)),
    )(d_mg, d_mg, d_mg, p, p, p, cpar, d_q, parts, parts, parts, d_ctx, cs, sn)


def _loss_bwd(t, gain, target, n_ctx, name):
    n, d = t.shape
    tr = _pick(n_ctx, (256, 128))
    first = n_ctx // tr

    def body(t_ref, g_ref, y_ref, dt_ref, loss_ref, dg_ref):
        i = pl.program_id(0)

        @pl.when(i == 0)
        def _():
            loss_ref[...] = jnp.zeros_like(loss_ref)
            dg_ref[...] = jnp.zeros_like(dg_ref)

        @pl.when(i < first)
        def _():
            dt_ref[...] = jnp.zeros_like(dt_ref)

        @pl.when(i >= first)
        def _():
            x = t_ref[...]
            g = g_ref[...]
            r = lax.rsqrt(_rowmean(x * x) + EPS)
            xn = x * r
            err = xn * g - y_ref[...]
            loss_ref[...] += 0.5 * _colsum(_rowmean(err * err))
            dy = err * (1.0 / d)
            dg_ref[0:1, :] += _colsum(dy * xn)
            dxn = dy * g
            dt_ref[...] = r * (dxn - xn * _rowmean(dxn * xn))

    row = pl.BlockSpec((tr, d), lambda i: (i, 0))
    return pl.pallas_call(
        body, name=name,
        out_shape=(jax.ShapeDtypeStruct((n, d), F32), jax.ShapeDtypeStruct((8, 128), F32), jax.ShapeDtypeStruct((8, d), F32)),
        grid=(n // tr,),
        in_specs=[row, pl.BlockSpec((1, d), lambda i: (0, 0)), pl.BlockSpec((tr, d), lambda i: (jnp.maximum(i - first, 0), 0))],
        out_specs=(row, pl.BlockSpec((8, 128), lambda i: (0, 0)), pl.BlockSpec((8, d), lambda i: (0, 0))),
        compiler_params=_params(("arbitrary",)),
    )(t, gain, target)


def _silu16(c16, name):
    def body(c_ref, o_ref):
        v = c_ref[...]
        o_ref[...] = (v * jax.nn.sigmoid(v)).astype(BF16)

    return pl.pallas_call(body, name=name, out_shape=jax.ShapeDtypeStruct(c16.shape, BF16))(c16)


def _cctx_grad(parts, c_ctx, name):
    def body(p_ref, c_ref, o_ref):
        g = _colsum(p_ref[...])
        v = c_ref[...]
        s = jax.nn.sigmoid(v)
        o_ref[...] = g * (s * (1.0 + v * (1.0 - s)))

    return pl.pallas_call(body, name=name, out_shape=jax.ShapeDtypeStruct(c_ctx.shape, F32))(parts, c_ctx)


def _rope_tables(n_ctx, n_tok):
    half = HEAD_DIM // 4
    inv = ROPE_THETA ** (-jnp.arange(0, HEAD_DIM // 2, 2, dtype=F32) / (HEAD_DIM // 2))
    rows = n_tok // GRID_W
    row_pos = jnp.repeat(jnp.arange(rows, dtype=F32), GRID_W)
    col_pos = jnp.tile(jnp.arange(GRID_W, dtype=F32), rows)
    ang_r, ang_c = row_pos[:, None] * inv[None, :], col_pos[:, None] * inv[None, :]
    cos = jnp.concatenate([jnp.cos(ang_r), jnp.cos(ang_r), jnp.cos(ang_c), jnp.cos(ang_c)], axis=1)
    sin = jnp.concatenate([-jnp.sin(ang_r), jnp.sin(ang_r), -jnp.sin(ang_c), jnp.sin(ang_c)], axis=1)
    assert cos.shape[1] == 4 * half == HEAD_DIM
    cos = jnp.concatenate([jnp.ones((n_ctx, HEAD_DIM), F32), cos], axis=0)
    sin = jnp.concatenate([jnp.zeros((n_ctx, HEAD_DIM), F32), sin], axis=0)
    return jnp.tile(cos, (1, 2)), jnp.tile(sin, (1, 2))


def kernel(x, c, ctx, c_ctx, w_ada, b_ada, g_norm1, g_norm2, w_in, conv_w, conv_b, sink, g_out_conv, g_out_attn, w_out, w_mlp1, w_mlp2, g_final, loss_target, m_c_ctx, m_w_ada, m_b_ada, m_g_norm1, m_g_norm2, m_w_in, m_conv_w, m_conv_b, m_sink, m_g_out_conv, m_g_out_attn, m_w_out, m_w_mlp1, m_w_mlp2, m_g_final, v_c_ctx, v_w_ada, v_b_ada, v_g_norm1, v_g_norm2, v_w_in, v_conv_w, v_conv_b, v_sink, v_g_out_conv, v_g_out_attn, v_w_out, v_w_mlp1, v_w_mlp2, v_g_final):
    n_lat, d = x.shape[1], x.shape[2]
    n_ctx = ctx.shape[1]
    n = n_ctx + n_lat
    depth = w_in.shape[0]
    dc = d // 2
    hd, kd = dc, N_KV_HEADS * HEAD_DIM
    n_heads = hd // HEAD_DIM
    d_in = 3 * dc + hd + 2 * kd
    cin, c_ada, r_out, c_ff, r_ff = w_in.shape[2], w_ada.shape[2], w_out.shape[1], w_mlp1.shape[2], w_mlp2.shape[1]
    d_ff = N_DEV * c_ff
    cw = conv_w.shape[2]
    assert d_in == N_DEV * cin and n_ctx % BLOCK == 0 and n_lat % BLOCK == 0 and hd % kd == 0
    dev = 4 * lax.axis_index("x") + 2 * lax.axis_index("y") + lax.axis_index("c")

    c_all, conv_w_all = _all_gather([c, conv_w], "gather_cond")
    conv_w_full = jnp.transpose(conv_w_all, (1, 2, 0, 3)).reshape(depth, 3, dc)
    c16 = jnp.concatenate([c_all.reshape(N_DEV, d), jnp.broadcast_to(c_ctx[None, :], (8, d))], axis=0)
    sc16 = _silu16(c16, "silu_cond")

    b_ada_loc = lax.dynamic_index_in_dim(b_ada.reshape(depth, N_DEV, c_ada), dev, axis=1, keepdims=False)
    tn_ada = _pick(c_ada, FEAT)

    def add_bias(acc, i, j, extra, outs):
        outs[0][...] = acc + extra[0][...]

    mod_loc = []
    for l in range(depth):
        mod_loc.append(_matmul(
            sc16, w_ada, dims="nn", shape=(16, c_ada, d), tiles=(16, tn_ada, _pick(d, KDIM)),
            b_spec=_w_spec("nat", l, None, "nn", tn_ada, _pick(d, KDIM)), epilogue=add_bias,
            extras=[(b_ada_loc[l][None, :], pl.BlockSpec((1, tn_ada), lambda i, j, k: (0, j)))], name=f"ada_fwd{l}"))
    (mod_all,) = _all_gather([jnp.stack(mod_loc)], "gather_mod")
    mod_full = jnp.transpose(mod_all, (1, 2, 0, 3)).reshape(depth, 16, N_MOD * d)
    mod_mine = lax.dynamic_index_in_dim(mod_full, dev, axis=1, keepdims=True)
    mod8 = jnp.concatenate([mod_mine, mod_full[:, 8:9], jnp.zeros((depth, 6, N_MOD * d), F32)], axis=1)

    w_in_b = _to_bf16(w_in.reshape(depth * d, cin), "cast_w_in").reshape(depth, d, cin)
    w_out_b = _to_bf16(w_out.reshape(depth * r_out, d), "cast_w_out").reshape(depth, r_out, d)
    w1_b = _to_bf16(w_mlp1.reshape(depth * d, c_ff), "cast_w_mlp1").reshape(depth, d, c_ff)
    w2_b = _to_bf16(w_mlp2.reshape(depth * r_ff, d), "cast_w_mlp2").reshape(depth, r_ff, d)
    g_in, g_out, g_w1, g_w2 = _all_gather([w_in_b, w_out_b, w1_b, w2_b], "gather_weights")
    w_in_full = jnp.transpose(g_in, (1, 2, 0, 3)).reshape(depth, d, d_in)

    t = jnp.concatenate([ctx[0], x[0]], axis=0)
    cs, sn = _rope_tables(n_ctx, n_lat)
    tm = _pick(n, TOK)
    tk_d = _pick(d, KDIM)

    def resid_epilogue(gate_k):
        def epi(acc, i, j, extra, outs):
            is_ctx = _row_ids(i, tm) < n_ctx
            outs[0][...] = extra[0][...] + _sel(is_ctx, extra[1]) * acc
            outs[1][...] = acc.astype(BF16)
        return epi

    def sq_relu_epilogue(acc, i, j, extra, outs):
        outs[0][...] = acc.astype(BF16)
        rl = jnp.maximum(acc, 0.0)
        outs[1][...] = (rl * rl).astype(BF16)

    def d_sq_relu_epilogue(acc, i, j, extra, outs):
        outs[0][...] = (acc * (2.0 * jnp.maximum(extra[0][...].astype(F32), 0.0))).astype(BF16)

    saved = []
    for l in range(depth):
        cpar = jnp.concatenate([conv_w_full[l], conv_b[l][None], g_out_conv[l][None], g_out_attn[l][None],
                                jnp.zeros((2, dc), F32)], axis=0)
        h = _norm_mod(t, g_norm1[l][None], mod8[l], 0, n_ctx, f"norm1_{l}")
        tn = _pick(d_in, (768, 512, 256, 128))
        p = _matmul(h, w_in_full, dims="nn", shape=(n, d_in, d), tiles=(tm, tn, tk_d),
                    b_spec=_w_spec("nat", l, None, "nn", tn, tk_d), name=f"in_proj{l}")
        qkv = _rope_split(p, cs, sn, dc, hd, kd, f"rope{l}")
        ao = _attention(qkv, sink[l], n_ctx, hd, kd, f"attn{l}")
        mg = _mix_fwd(p, ao, cpar, n_ctx, dc, f"mix{l}")
        tn = _pick(d, FEAT)
        tk = _pick(r_out, KDIM)
        t2, z = _matmul(
            mg, g_out, dims="nn", shape=(n, d, d), tiles=(tm, tn, tk), b_spec=_w_spec("rows", l, r_out, "nn", tn, tk),
            out_dtypes=(F32, BF16), epilogue=resid_epilogue(2),
            extras=[(t, pl.BlockSpec((tm, tn), lambda i, j, k: (i, j))),
                    (mod8[l], pl.BlockSpec((8, tn), lambda i, j, k, tn=tn: (0, 2 * (d // tn) + j)))], name=f"out_proj{l}")
        h2 = _norm_mod(t2, g_norm2[l][None], mod8[l], 3, n_ctx, f"norm2_{l}")
        tn = _pick(c_ff, FEAT)
        a, s = _matmul(h2, g_w1, dims="nn", shape=(n, d_ff, d), tiles=(tm, tn, tk_d),
                       b_spec=_w_spec("cols", l, c_ff, "nn", tn, tk_d), out_dtypes=(BF16, BF16),
                       epilogue=sq_relu_epilogue, name=f"mlp_up{l}")
        tn = _pick(d, FEAT)
        tk = _pick(r_ff, KDIM)
        t3, o = _matmul(
            s, g_w2, dims="nn", shape=(n, d, d_ff), tiles=(tm, tn, tk), b_spec=_w_spec("rows", l, r_ff, "nn", tn, tk),
            out_dtypes=(F32, BF16), epilogue=resid_epilogue(5),
            extras=[(t2, pl.BlockSpec((tm, tn), lambda i, j, k: (i, j))),
                    (mod8[l], pl.BlockSpec((8, tn), lambda i, j, k, tn=tn: (0, 5 * (d // tn) + j)))], name=f"mlp_down{l}")
        saved.append((t, h, p, qkv, ao, mg, z, t2, h2, a, s, o, cpar))
        t = t3

    d_t, loss_tile, dg_final = _loss_bwd(t, g_final[None], loss_target[0], n_ctx, "loss")
    loss = lax.psum(loss_tile[0, 0], ("x", "y", "c"))

    gb_in = lax.empty((depth, d, d_in), F32)
    gb_out = lax.empty((N_DEV, depth, r_out, d), F32)
    gb_w1 = lax.empty((N_DEV, depth, d, c_ff), F32)
    gb_w2 = lax.empty((N_DEV, depth, r_ff, d), F32)
    tkn = _pick(n, TOK)
    small = [None] * depth
    for l in reversed(range(depth)):
        t_in, h, p, qkv, ao, mg, z, t2, h2, a, s, o, cpar = saved[l]
        dob, dgate2 = _gate_bwd(d_t, o, mod8[l], 5, n_ctx, f"gate2_bwd{l}")
        tm_g = _pick(r_ff, FEAT)
        tn = _pick(d, FEAT)
        gb_w2 = _matmul(s, dob, dims="tn", shape=(d_ff, d, n), tiles=(tm_g, tn, tkn),
                        out_specs=(_g_spec("rows", l, r_ff, tm_g, tn),), out_shapes=(jax.ShapeDtypeStruct(gb_w2.shape, F32),),
                        alias=gb_w2, name=f"mlp_down_dw{l}")
        tn = _pick(r_ff, FEAT)
        da = _matmul(dob, g_w2, dims="nt", shape=(n, d_ff, d), tiles=(tm, tn, tk_d),
                     b_spec=_w_spec("rows", l, r_ff, "nt", tn, tk_d), out_dtypes=(BF16,), epilogue=d_sq_relu_epilogue,
                     extras=[(a, pl.BlockSpec((tm, tn), lambda i, j, k: (i, j)))], name=f"mlp_down_dx{l}")
        tm_g = _pick(d, FEAT)
        tn = _pick(c_ff, FEAT)
        gb_w1 = _matmul(h2, da, dims="tn", shape=(d, d_ff, n), tiles=(tm_g, tn, tkn),
                        out_specs=(_g_spec("cols", l, c_ff, tm_g, tn),), out_shapes=(jax.ShapeDtypeStruct(gb_w1.shape, F32),),
                        alias=gb_w1, name=f"mlp_up_dw{l}")
        tn = _pick(d, FEAT)
        tk = _pick(c_ff, KDIM)
        dh2 = _matmul(da, g_w1, dims="nt", shape=(n, d, d_ff), tiles=(tm, tn, tk),
                      b_spec=_w_spec("cols", l, c_ff, "nt", tn, tk), name=f"mlp_up_dx{l}")
        d_t2, dss2, dgn2 = _norm_mod_bwd(dh2, t2, d_t, g_norm2[l][None], mod8[l], 3, n_ctx, f"norm2_bwd{l}")
        dzb, dgate1 = _gate_bwd(d_t2, z, mod8[l], 2, n_ctx, f"gate1_bwd{l}")
        tm_g = _pick(r_out, FEAT)
        tn = _pick(d, FEAT)
        gb_out = _matmul(mg, dzb, dims="tn", shape=(d, d, n), tiles=(tm_g, tn, tkn),
                         out_specs=(_g_spec("rows", l, r_out, tm_g, tn),), out_shapes=(jax.ShapeDtypeStruct(gb_out.shape, F32),),
                         alias=gb_out, name=f"out_proj_dw{l}")
        tn = _pick(r_out, FEAT)
        d_mg = _matmul(dzb, g_out, dims="nt", shape=(n, d, d), tiles=(tm, tn, tk_d),
                       b_spec=_w_spec("rows", l, r_out, "nt", tn, tk_d), name=f"out_proj_dx{l}")
        d_q, parts, d_kv_ctx, d_sink, d_goa = _attention_bwd(qkv, sink[l], ao, d_mg, cpar, n_ctx, hd, kd, f"attn_bwd{l}")
        d_p, conv_acc = _mix_bwd(d_mg, p, cpar, d_q, parts, d_kv_ctx, cs, sn, n_ctx, dc, hd, kd, f"mix_bwd{l}")
        tm_g = _pick(d, FEAT)
        tn = _pick(d_in, (768, 512, 256, 128))
        gb_in = _matmul(h, d_p, dims="tn", shape=(d, d_in, n), tiles=(tm_g, tn, tkn),
                        out_specs=(_g_spec("nat", l, None, tm_g, tn),), out_shapes=(jax.ShapeDtypeStruct(gb_in.shape, F32),),
                        alias=gb_in, name=f"in_proj_dw{l}")
        tn = _pick(d, FEAT)
        tk = _pick(d_in, (768, 512, 256, 128))
        dh = _matmul(d_p, w_in_full, dims="nt", shape=(n, d, d_in), tiles=(tm, tn, tk),
                     b_spec=_w_spec("nat", l, None, "nt", tn, tk), name=f"in_proj_dx{l}")
        d_t, dss1, dgn1 = _norm_mod_bwd(dh, t_in, d_t2, g_norm1[l][None], mod8[l], 0, n_ctx, f"norm1_bwd{l}")
        d_mod2 = jnp.concatenate([dss1[0:2], dgate1[0:2], dss2[0:2], dgate2[0:2]], axis=1)
        small[l] = (d_mod2, dgn1[0], dgn2[0], conv_acc, d_sink[0, 0:n_heads], d_goa[0])
    grad_x = d_t[n_ctx:][None]

    def pack(l):
        d_mod2, dgn1, dgn2, conv_acc, d_sink, d_goa = small[l]
        row0 = [d_mod2[0], dgn1, dgn2, conv_acc[3], d_sink, conv_acc[4], d_goa, conv_acc[0:3].reshape(-1)]
        row1 = [d_mod2[1]] + [jnp.zeros_like(v) for v in row0[1:]]
        return jnp.stack([jnp.concatenate(row0), jnp.concatenate(row1)])
    per_layer = N_MOD * d + 2 * d + dc + n_heads + 2 * dc + 3 * dc
    packed = jnp.concatenate([pack(l) for l in range(depth)] +
                             [jnp.stack([dg_final[0], jnp.zeros((d,), F32)])], axis=1)
    f_tot = depth * per_layer + d
    f_pad = -f_tot % 1024
    packed = jnp.pad(packed, ((0, 0), (0, f_pad)))
    (small_all,) = _all_gather([packed], "gather_small")
    small_parts = small_all.reshape(2 * N_DEV, 1, f_tot + f_pad)

    def section(arr, l, off, size):
        return lax.slice_in_dim(arr, l * per_layer + off, l * per_layer + off + size, axis=-1)

    offs = {}
    o_ = 0
    for nm_, sz in (("mod", N_MOD * d), ("gn1", d), ("gn2", d), ("cb", dc), ("sink", n_heads), ("goc", dc), ("goa", dc), ("cw", 3 * dc)):
        offs[nm_] = (o_, sz)
        o_ += sz

    def packw(b_ada_, gn1_, gn2_, cb_, sk_, goc_, goa_, gf_):
        rows = []
        for l in range(depth):
            rows += [b_ada_[l], gn1_[l], gn2_[l], cb_[l], sk_[l], goc_[l], goa_[l], jnp.zeros((3 * dc,), F32)]
        return jnp.pad(jnp.concatenate(rows + [gf_]), (0, f_pad))[None]
    pw = packw(b_ada, g_norm1, g_norm2, conv_b, sink, g_out_conv, g_out_attn, g_final)
    pm = packw(m_b_ada, m_g_norm1, m_g_norm2, m_conv_b, m_sink, m_g_out_conv, m_g_out_attn, m_g_final)
    pv = packw(v_b_ada, v_g_norm1, v_g_norm2, v_conv_b, v_sink, v_g_out_conv, v_g_out_attn, v_g_final)
    sg, sd, sm, sv = _adamw(pw, pm, pv, small_parts, "adamw_small")

    def unpack(arr):
        arr = arr[0]
        out = {}
        for nm_ in ("mod", "gn1", "gn2", "cb", "sink", "goc", "goa", "cw"):
            off, size = offs[nm_]
            out[nm_] = jnp.stack([section(arr, l, off, size) for l in range(depth)])
        out["gf"] = arr[depth * per_layer:depth * per_layer + d]
        return out
    ug, ud, um, uv = unpack(sg), unpack(sd), unpack(sm), unpack(sv)

    cw_grad_full = ug["cw"].reshape(depth, 3, N_DEV, cw)
    cw_grad = lax.dynamic_index_in_dim(cw_grad_full, dev, axis=2, keepdims=False).reshape(1, depth * 3, cw)
    cwg, cwd, cwm, cwv = _adamw(conv_w.reshape(depth * 3, cw), m_conv_w.reshape(depth * 3, cw),
                                v_conv_w.reshape(depth * 3, cw), cw_grad, "adamw_conv_w")
    cw_shape = conv_w.shape

    mod_rows = small_all[:, :, :depth * per_layer].reshape(N_DEV, 2, depth, per_layer)[:, :, :, :N_MOD * d]
    dm16 = jnp.concatenate([mod_rows[:, 0], mod_rows[:, 1]], axis=0)
    dm16 = jnp.transpose(dm16, (1, 0, 2)).reshape(depth, 16, N_DEV, c_ada)
    dm16_loc = lax.dynamic_index_in_dim(dm16, dev, axis=2, keepdims=False)
    gb_ada = lax.empty((depth, d, c_ada), F32)
    dsc_parts = []
    tm_g = _pick(d, FEAT)
    for l in range(depth):
        gb_ada = _matmul(sc16, dm16_loc[l], dims="tn", shape=(d, c_ada, 16), tiles=(tm_g, tn_ada, 16),
                         out_specs=(_g_spec("nat", l, None, tm_g, tn_ada),), out_shapes=(jax.ShapeDtypeStruct(gb_ada.shape, F32),),
                         alias=gb_ada, name=f"ada_dw{l}")
        tn = _pick(d, FEAT)
        tk = _pick(c_ada, KDIM)
        dsc_parts.append(_matmul(dm16_loc[l], w_ada, dims="nt", shape=(16, d, c_ada), tiles=(16, tn, tk),
                                 b_spec=_w_spec("nat", l, None, "nt", tn, tk), name=f"ada_dx{l}"))
    (dsc_all,) = _all_gather([jnp.stack(dsc_parts)[:, 8:16]], "gather_dcond")
    g_cctx = _cctx_grad(dsc_all.reshape(N_DEV * depth * 8, d), c_ctx[None], "c_ctx_grad")
    ccg, ccd, ccm, ccv = _adamw(c_ctx[None], m_c_ctx[None], v_c_ctx[None], g_cctx[None], "adamw_c_ctx")
    adg, add, adm, adv = _adamw(w_ada.reshape(depth * d, c_ada), m_w_ada.reshape(depth * d, c_ada),
                                v_w_ada.reshape(depth * d, c_ada), gb_ada.reshape(1, depth * d, c_ada), "adamw_w_ada")

    gb_in8 = jnp.transpose(gb_in.reshape(depth, d, N_DEV, cin), (2, 0, 1, 3))
    own = [gb_in8.reshape(N_DEV, depth * d, cin), gb_out.reshape(N_DEV, depth * r_out, d),
           gb_w1.reshape(N_DEV, depth * d, c_ff), gb_w2.reshape(N_DEV, depth * r_ff, d)]
    got = _xchg_sibling(own, "rs_sibling")
    pair = [_pair_add(o8, g4, f"rs_pair_add{k}") for k, (o8, g4) in enumerate(zip(own, got))]
    parts4 = _xchg_chips(pair, "rs_chips")
    big = []
    for k, (w_, m_, v_) in enumerate(((w_in, m_w_in, v_w_in), (w_out, m_w_out, v_w_out),
                                      (w_mlp1, m_w_mlp1, v_w_mlp1), (w_mlp2, m_w_mlp2, v_w_mlp2))):
        r2, c2 = w_.shape[0] * w_.shape[1], w_.shape[2]
        res = _adamw(w_.reshape(r2, c2), m_.reshape(r2, c2), v_.reshape(r2, c2), parts4[k], f"adamw_big{k}")
        big.append([a_.reshape(w_.shape) for a_ in res])

    def leaf(i):
        return (
            (ccg, ccd, ccm, ccv)[i][0], (adg, add, adm, adv)[i].reshape(w_ada.shape),
            (ug, ud, um, uv)[i]["mod"], (ug, ud, um, uv)[i]["gn1"], (ug, ud, um, uv)[i]["gn2"], big[0][i],
            (cwg, cwd, cwm, cwv)[i].reshape(cw_shape), (ug, ud, um, uv)[i]["cb"], (ug, ud, um, uv)[i]["sink"],
            (ug, ud, um, uv)[i]["goc"], (ug, ud, um, uv)[i]["goa"], big[1][i], big[2][i], big[3][i], (ug, ud, um, uv)[i]["gf"])

    return (loss, grad_x) + leaf(0) + leaf(1) + leaf(2) + leaf(3)
```

```python
import functools

import jax
import jax.numpy as jnp
from jax import lax
from jax.experimental import pallas as pl
from jax.experimental.pallas import tpu as pltpu

HEAD_DIM = 64
N_KV_HEADS = 4
BLOCK = 128
GRID_W = 64
ROPE_THETA = 10000.0
EPS = 1e-6
N_MOD = 6
SCALE = HEAD_DIM ** -0.5
NEG_INF = -1e30
ADAM_LR = 0.001
ADAM_B1 = 0.9
ADAM_B2 = 0.999
ADAM_EPS = 1e-08
ADAM_WD = 0.01
ADAM_STEP = 10
N_DEV = 8
N_CHIP = 4
VMEM_LIMIT_BYTES = 48 * 1024 * 1024
MESH = pl.DeviceIdType.MESH
BF16 = jnp.bfloat16
F32 = jnp.float32
ANY = pl.BlockSpec(memory_space=pl.ANY)


def _pick(dim, prefs):
    for p in prefs:
        if p <= dim and dim % p == 0:
            return p
    return dim


def _params(sem):
    return pltpu.CompilerParams(dimension_semantics=sem, vmem_limit_bytes=VMEM_LIMIT_BYTES)


def _row_ids(i, tr):
    return i * tr + lax.broadcasted_iota(jnp.int32, (tr, 1), 0)


def _colsum(v):
    return jnp.sum(v, axis=0, keepdims=True)


def _rowmean(v):
    return jnp.mean(v, axis=1, keepdims=True)


def _all_gather(xs, name, layered=False):
    na = len(xs)

    def body(*refs):
        x_refs, o_refs = refs[:na], refs[na:2 * na]
        send_sems, recv_sems, local_sems = refs[2 * na:]
        x, y, c = lax.axis_index("x"), lax.axis_index("y"), lax.axis_index("c")
        me, sibling = (x, y, c), (x, y, 1 - c)
        chips = [(1 - x, y), (x, 1 - y), (1 - x, 1 - y)]

        def slot(a, px, py, pc):
            idx = 4 * px + 2 * py + pc
            return o_refs[a].at[:, idx] if layered else o_refs[a].at[idx]

        def copy(a, k, block, to, src=None):
            return pltpu.make_async_remote_copy(
                src_ref=slot(a, *block) if src is None else src, dst_ref=slot(a, *block),
                send_sem=send_sems.at[a, k], recv_sem=recv_sems.at[a, k], device_id=to, device_id_type=MESH)

        mine = [pltpu.make_async_copy(x_refs[a], slot(a, *me), local_sems.at[a]) for a in range(na)]
        for cp in mine:
            cp.start()
        first = []
        for a in range(na):
            first.append(copy(a, 0, me, sibling, src=x_refs[a]))
            first += [copy(a, 1 + j, me, (*chip, c), src=x_refs[a]) for j, chip in enumerate(chips)]
        for cp in first:
            cp.start()
        passed = []
        for j, chip in enumerate(chips):
            for a in range(na):
                copy(a, 1 + j, (*chip, c), me).wait_recv()
                fwd = copy(a, 4 + j, (*chip, c), sibling)
                fwd.start()
                passed.append(fwd)
        for a in range(na):
            copy(a, 0, sibling, me).wait_recv()
            for j, chip in enumerate(chips):
                copy(a, 4 + j, (*chip, 1 - c), me).wait_recv()
        for cp in first + passed:
            cp.wait_send()
        for cp in mine:
            cp.wait()

    outs = pl.pallas_call(
        body, name=name,
        out_shape=tuple(jax.ShapeDtypeStruct(
            (x.shape[0], N_DEV) + x.shape[1:] if layered else (N_DEV,) + x.shape, x.dtype) for x in xs),
        in_specs=[ANY] * na, out_specs=tuple([ANY] * na),
        scratch_shapes=[pltpu.SemaphoreType.DMA((na, 7)), pltpu.SemaphoreType.DMA((na, 7)),
                        pltpu.SemaphoreType.DMA((na,))],
    )(*xs)
    return list(outs)


def _xchg_sibling(srcs, name):
    na = len(srcs)

    def body(*refs):
        s_refs, o_refs = refs[:na], refs[na:2 * na]
        send_sems, recv_sems = refs[2 * na:]
        x, y, c = lax.axis_index("x"), lax.axis_index("y"), lax.axis_index("c")
        sibling = (x, y, 1 - c)

        def copy(a, k):
            return pltpu.make_async_remote_copy(
                src_ref=s_refs[a].at[:, 2 * k + (1 - c)], dst_ref=o_refs[a].at[k],
                send_sem=send_sems.at[a, k], recv_sem=recv_sems.at[a, k], device_id=sibling, device_id_type=MESH)

        cps = [copy(a, k) for a in range(na) for k in range(N_CHIP)]
        for cp in cps:
            cp.start()
        for cp in cps:
            cp.wait_recv()
        for cp in cps:
            cp.wait_send()

    outs = pl.pallas_call(
        body, name=name,
        out_shape=tuple(jax.ShapeDtypeStruct((N_CHIP, s.shape[0]) + s.shape[2:], s.dtype) for s in srcs),
        in_specs=[ANY] * na, out_specs=tuple([ANY] * na),
        scratch_shapes=[pltpu.SemaphoreType.DMA((na, N_CHIP)), pltpu.SemaphoreType.DMA((na, N_CHIP))],
    )(*srcs)
    return list(outs)


def _xchg_chips(srcs, name):
    na = len(srcs)

    def body(*refs):
        s_refs, o_refs = refs[:na], refs[na:2 * na]
        send_sems, recv_sems, local_sems = refs[2 * na:]
        x, y, c = lax.axis_index("x"), lax.axis_index("y"), lax.axis_index("c")
        my_chip = 2 * x + y
        chips = [(1 - x, y), (x, 1 - y), (1 - x, 1 - y)]

        def copy(a, j):
            px, py = chips[j]
            return pltpu.make_async_remote_copy(
                src_ref=s_refs[a].at[2 * px + py], dst_ref=o_refs[a].at[my_chip],
                send_sem=send_sems.at[a, j], recv_sem=recv_sems.at[a, j], device_id=(px, py, c), device_id_type=MESH)

        def arrival(a, j):
            px, py = chips[j]
            return pltpu.make_async_remote_copy(
                src_ref=s_refs[a].at[2 * px + py], dst_ref=o_refs[a].at[2 * px + py],
                send_sem=send_sems.at[a, j], recv_sem=recv_sems.at[a, j], device_id=(px, py, c), device_id_type=MESH)

        mine = [pltpu.make_async_copy(s_refs[a].at[my_chip], o_refs[a].at[my_chip], local_sems.at[a]) for a in range(na)]
        for cp in mine:
            cp.start()
        cps = [copy(a, j) for a in range(na) for j in range(3)]
        for cp in cps:
            cp.start()
        for a in range(na):
            for j in range(3):
                arrival(a, j).wait_recv()
        for cp in cps:
            cp.wait_send()
        for cp in mine:
            cp.wait()

    outs = pl.pallas_call(
        body, name=name,
        out_shape=tuple(jax.ShapeDtypeStruct(s.shape, s.dtype) for s in srcs),
        in_specs=[ANY] * na, out_specs=tuple([ANY] * na),
        scratch_shapes=[pltpu.SemaphoreType.DMA((na, 3)), pltpu.SemaphoreType.DMA((na, 3)),
                        pltpu.SemaphoreType.DMA((na,))],
    )(*srcs)
    return list(outs)


def _to_bf16(x2d, name):
    r, c = x2d.shape
    tr = _pick(r, (512, 256, 128, 64, 32, 16))

    def body(x_ref, o_ref):
        o_ref[...] = x_ref[...].astype(BF16)

    return pl.pallas_call(
        body, name=name, out_shape=jax.ShapeDtypeStruct((r, c), BF16), grid=(r // tr,),
        in_specs=[pl.BlockSpec((tr, c), lambda i: (i, 0))], out_specs=pl.BlockSpec((tr, c), lambda i: (i, 0)),
        compiler_params=_params(("parallel",)),
    )(x2d)


def _pair_add(own8, got4, name):
    depth, _, r, c = own8.shape
    tr = _pick(r, (512, 256, 128, 64, 32, 16))
    core = lax.axis_index("c").astype(jnp.int32).reshape(1)

    def body(c_ref, a_ref, b_ref, o_ref):
        o_ref[...] = (a_ref[...] + b_ref[...]).astype(BF16)

    return pl.pallas_call(
        body, name=name, out_shape=jax.ShapeDtypeStruct((N_CHIP, depth, r, c), BF16),
        grid_spec=pltpu.PrefetchScalarGridSpec(
            num_scalar_prefetch=1, grid=(N_CHIP, depth, r // tr),
            in_specs=[pl.BlockSpec((None, None, tr, c), lambda k, l, i, cr: (l, 2 * k + cr[0], i, 0)),
                      pl.BlockSpec((None, None, tr, c), lambda k, l, i, cr: (k, l, i, 0))],
            out_specs=pl.BlockSpec((None, None, tr, c), lambda k, l, i, cr: (k, l, i, 0))),
        compiler_params=_params(("parallel", "parallel", "parallel")),
    )(core, own8, got4)


def _adamw(w, m, v, parts, name, premul=None):
    r, c = w.shape
    n_parts = parts.shape[0]
    tr = _pick(r, (256, 128, 64, 32, 16, 8))
    tc = _pick(c, (1024, 512, 256, 128))
    has_pre = premul is not None

    def body(*refs):
        if has_pre:
            w_ref, m_ref, v_ref, p_ref, q_ref, g_ref, d_ref, nm_ref, nv_ref = refs
        else:
            w_ref, m_ref, v_ref, p_ref, g_ref, d_ref, nm_ref, nv_ref = refs
        g = p_ref[0].astype(F32)
        for k in range(1, n_parts):
            g = g + p_ref[k].astype(F32)
        if has_pre:
            g = g * q_ref[...]
        wv = w_ref[...]
        nm = ADAM_B1 * m_ref[...] + (1.0 - ADAM_B1) * g
        nv = ADAM_B2 * v_ref[...] + (1.0 - ADAM_B2) * (g * g)
        m_hat = nm / (1.0 - ADAM_B1 ** ADAM_STEP)
        v_hat = nv / (1.0 - ADAM_B2 ** ADAM_STEP)
        g_ref[...] = g
        d_ref[...] = -ADAM_LR * (m_hat / (jnp.sqrt(v_hat) + ADAM_EPS) + ADAM_WD * wv)
        nm_ref[...] = nm
        nv_ref[...] = nv

    tile = pl.BlockSpec((tr, tc), lambda i, j: (i, j))
    ins = [w, m, v, parts] + ([premul] if has_pre else [])
    in_specs = [tile, tile, tile, pl.BlockSpec((n_parts, tr, tc), lambda i, j: (0, i, j))] + ([tile] if has_pre else [])
    sh = jax.ShapeDtypeStruct((r, c), F32)
    return pl.pallas_call(
        body, name=name, out_shape=(sh, sh, sh, sh), grid=(r // tr, c // tc),
        in_specs=in_specs, out_specs=(tile, tile, tile, tile),
        compiler_params=_params(("parallel", "parallel")),
    )(*ins)


def _matmul(a, b, *, dims, shape, tiles, b_spec=None, out_specs=None, out_shapes=None, out_dtypes=(F32,),
            epilogue=None, extras=(), alias=None, name):
    m_dim, n_dim, k_dim = shape
    tm, tn, tk = tiles
    assert m_dim % tm == 0 and n_dim % tn == 0 and k_dim % tk == 0, (name, shape, tiles)
    nk = k_dim // tk
    n_extra = len(extras)
    n_alias = 0 if alias is None else 1
    n_out = len(out_dtypes)
    if dims == "tn":
        a_spec = pl.BlockSpec((tk, tm), lambda i, j, k: (k, i))
        contract = (((0,), (0,)), ((), ()))
    else:
        a_spec = pl.BlockSpec((tm, tk), lambda i, j, k: (i, k))
        contract = (((1,), (1,)), ((), ())) if dims == "nt" else (((1,), (0,)), ((), ()))
    if b_spec is None:
        b_spec = (pl.BlockSpec((tn, tk), lambda i, j, k: (j, k)) if dims == "nt"
                  else pl.BlockSpec((tk, tn), lambda i, j, k: (k, j)))
    if out_specs is None:
        out_specs = tuple(pl.BlockSpec((tm, tn), lambda i, j, k: (i, j)) for _ in range(n_out))
    if out_shapes is None:
        out_shapes = tuple(jax.ShapeDtypeStruct((m_dim, n_dim), d) for d in out_dtypes)

    def body(*refs):
        a_ref, b_ref = refs[0], refs[1]
        extra_refs = refs[2:2 + n_extra]
        out_refs = refs[2 + n_extra + n_alias:2 + n_extra + n_alias + n_out]

        def finish(acc):
            if epilogue is None:
                out_refs[0][...] = acc.astype(out_refs[0].dtype)
            else:
                epilogue(acc, pl.program_id(0), pl.program_id(1), extra_refs, out_refs)

        prod = lax.dot_general(a_ref[...].astype(BF16), b_ref[...].astype(BF16), contract, preferred_element_type=F32)
        if nk == 1:
            finish(prod)
            return
        acc_ref = refs[-1]
        k = pl.program_id(2)

        @pl.when(k == 0)
        def _():
            acc_ref[...] = prod

        @pl.when((k > 0) & (k < nk - 1))
        def _():
            acc_ref[...] += prod

        @pl.when(k == nk - 1)
        def _():
            finish(acc_ref[...] + prod)

    ins = [a, b] + [e[0] for e in extras]
    in_specs = [a_spec, b_spec] + [e[1] for e in extras]
    io_alias = {}
    if alias is not None:
        ins.append(alias)
        in_specs.append(ANY)
        io_alias = {len(ins) - 1: 0}
    outs = pl.pallas_call(
        body, name=name, out_shape=tuple(out_shapes), grid=(m_dim // tm, n_dim // tn, nk),
        in_specs=in_specs, out_specs=tuple(out_specs), scratch_shapes=[pltpu.VMEM((tm, tn), F32)] if nk > 1 else [],
        input_output_aliases=io_alias,
        compiler_params=_params(("parallel", "parallel", "arbitrary")),
    )(*ins)
    return outs if n_out > 1 else outs[0]


def _w_spec(kind, layer, per, dims, tn, tk):
    if kind == "nat":
        if dims == "nn":
            return pl.BlockSpec((None, tk, tn), lambda i, j, k: (layer, k, j))
        return pl.BlockSpec((None, tn, tk), lambda i, j, k: (layer, j, k))
    if dims == "nn":
        q = per // tn
        return pl.BlockSpec((None, None, tk, tn), lambda i, j, k: (layer, j // q, k, j % q))
    q = per // tk
    return pl.BlockSpec((None, None, tn, tk), lambda i, j, k: (layer, k // q, j, k % q))


def _g_spec(kind, layer, per, tm, tn):
    if kind == "nat":
        return pl.BlockSpec((None, tm, tn), lambda i, j, k: (layer, i, j))
    q = per // tn
    return pl.BlockSpec((None, None, tm, tn), lambda i, j, k: (layer, j // q, i, j % q))


TOK = (1408, 768, 256, 128)
FEAT = (1024, 512, 256, 128)
KDIM = (2048, 1536, 1024, 512, 256, 128)


def _sel(is_ctx, ref):
    return jnp.where(is_ctx, ref[1:2, :], ref[0:1, :])


def _norm_mod(t, gain, mod8, shift_k, n_ctx, name):
    n, d = t.shape
    tr = _pick(n, (256, 128))

    def body(t_ref, g_ref, sh_ref, sc_ref, o_ref):
        x = t_ref[...]
        r = lax.rsqrt(_rowmean(x * x) + EPS)
        y = (x * r) * g_ref[...]
        is_ctx = _row_ids(pl.program_id(0), tr) < n_ctx
        o_ref[...] = (y * (1.0 + _sel(is_ctx, sc_ref)) + _sel(is_ctx, sh_ref)).astype(BF16)

    return pl.pallas_call(
        body, name=name, out_shape=jax.ShapeDtypeStruct((n, d), BF16), grid=(n // tr,),
        in_specs=[pl.BlockSpec((tr, d), lambda i: (i, 0)), pl.BlockSpec((1, d), lambda i: (0, 0)),
                  pl.BlockSpec((8, d), lambda i: (0, shift_k)), pl.BlockSpec((8, d), lambda i: (0, shift_k + 1))],
        out_specs=pl.BlockSpec((tr, d), lambda i: (i, 0)),
        compiler_params=_params(("parallel",)),
    )(t, gain, mod8, mod8)


def _norm_mod_bwd(dh, t, d_res, gain, mod8, shift_k, n_ctx, name):
    n, d = t.shape
    tr = _pick(n, (256, 128))

    def body(dh_ref, t_ref, dr_ref, g_ref, sc_ref, dt_ref, dss_ref, dg_ref):
        i = pl.program_id(0)

        @pl.when(i == 0)
        def _():
            dss_ref[...] = jnp.zeros_like(dss_ref)
            dg_ref[...] = jnp.zeros_like(dg_ref)

        x = t_ref[...]
        r = lax.rsqrt(_rowmean(x * x) + EPS)
        xn = x * r
        g = g_ref[...]
        y = xn * g
        dhv = dh_ref[...]
        is_ctx = _row_ids(i, tr) < n_ctx
        zero = jnp.zeros_like(dhv)
        dhy = dhv * y
        dss_ref[0:1, 0:d] += _colsum(jnp.where(is_ctx, zero, dhv))
        dss_ref[1:2, 0:d] += _colsum(jnp.where(is_ctx, dhv, zero))
        dss_ref[0:1, d:2 * d] += _colsum(jnp.where(is_ctx, zero, dhy))
        dss_ref[1:2, d:2 * d] += _colsum(jnp.where(is_ctx, dhy, zero))
        dy = dhv * (1.0 + _sel(is_ctx, sc_ref))
        dg_ref[0:1, :] += _colsum(dy * xn)
        dxn = dy * g
        dt_ref[...] = dr_ref[...] + r * (dxn - xn * _rowmean(dxn * xn))

    row = pl.BlockSpec((tr, d), lambda i: (i, 0))
    return pl.pallas_call(
        body, name=name,
        out_shape=(jax.ShapeDtypeStruct((n, d), F32), jax.ShapeDtypeStruct((8, 2 * d), F32), jax.ShapeDtypeStruct((8, d), F32)),
        grid=(n // tr,),
        in_specs=[row, row, row, pl.BlockSpec((1, d), lambda i: (0, 0)), pl.BlockSpec((8, d), lambda i: (0, shift_k + 1))],
        out_specs=(row, pl.BlockSpec((8, 2 * d), lambda i: (0, 0)), pl.BlockSpec((8, d), lambda i: (0, 0))),
        compiler_params=_params(("arbitrary",)),
    )(dh, t, d_res, gain, mod8)


def _gate_bwd(d_t, branch, mod8, gate_k, n_ctx, name):
    n, d = d_t.shape
    tr = _pick(n, (256, 128))

    def body(dt_ref, o_ref, gt_ref, dob_ref, dgate_ref):
        i = pl.program_id(0)

        @pl.when(i == 0)
        def _():
            dgate_ref[...] = jnp.zeros_like(dgate_ref)

        dv = dt_ref[...]
        is_ctx = _row_ids(i, tr) < n_ctx
        dob_ref[...] = (dv * _sel(is_ctx, gt_ref)).astype(BF16)
        prod = dv * o_ref[...].astype(F32)
        zero = jnp.zeros_like(prod)
        dgate_ref[0:1, :] += _colsum(jnp.where(is_ctx, zero, prod))
        dgate_ref[1:2, :] += _colsum(jnp.where(is_ctx, prod, zero))

    row = pl.BlockSpec((tr, d), lambda i: (i, 0))
    return pl.pallas_call(
        body, name=name, out_shape=(jax.ShapeDtypeStruct((n, d), BF16), jax.ShapeDtypeStruct((8, d), F32)),
        grid=(n // tr,),
        in_specs=[row, row, pl.BlockSpec((8, d), lambda i: (0, gate_k))],
        out_specs=(row, pl.BlockSpec((8, d), lambda i: (0, 0))),
        compiler_params=_params(("arbitrary",)),
    )(d_t, branch, mod8)


def _swap16(v):
    w = v.shape[1]
    lane = lax.broadcasted_iota(jnp.int32, v.shape, 1)
    return jnp.where((lane % 32) < 16, pltpu.roll(v, w - 16, 1), pltpu.roll(v, 16, 1))


def _rope(v, cs, sn, sign):
    reps = v.shape[1] // 128
    c = jnp.tile(cs, (1, reps)) if reps > 1 else cs
    s = jnp.tile(sn, (1, reps)) if reps > 1 else sn
    return v * c + sign * (_swap16(v) * s)


def _rope_split(p, cs, sn, dc, hd, kd, name):
    n, d_in = p.shape
    tr = _pick(n, (256, 128))
    q0 = 3 * dc

    def body(p_ref, cs_ref, sn_ref, o_ref):
        cs_v, sn_v = cs_ref[...], sn_ref[...]
        q = _rope(p_ref[:, q0:q0 + hd], cs_v, sn_v, 1.0) * SCALE
        k = _rope(p_ref[:, q0 + hd:q0 + hd + kd], cs_v, sn_v, 1.0)
        v = p_ref[:, q0 + hd + kd:q0 + hd + 2 * kd]
        o_ref[...] = jnp.concatenate([q, k, v], axis=1).astype(BF16)

    return pl.pallas_call(
        body, name=name, out_shape=jax.ShapeDtypeStruct((n, hd + 2 * kd), BF16), grid=(n // tr,),
        in_specs=[pl.BlockSpec((tr, d_in), lambda i: (i, 0)), pl.BlockSpec((tr, 128), lambda i: (i, 0)),
                  pl.BlockSpec((tr, 128), lambda i: (i, 0))],
        out_specs=pl.BlockSpec((tr, hd + 2 * kd), lambda i: (i, 0)),
        compiler_params=_params(("parallel",)),
    )(p, cs, sn)


def _attn_specs(nb, n_ctx, hd, kd):
    kci = hd // kd
    specs = [pl.BlockSpec((BLOCK, hd), lambda b: (b, 0)),
             pl.BlockSpec((n_ctx, kd), lambda b: (0, kci)), pl.BlockSpec((n_ctx, kd), lambda b: (0, kci + 1))]
    for col in (kci, kci + 1):
        specs.append(pl.BlockSpec((BLOCK, kd), lambda b, col=col: (jnp.maximum(b - 1, 0), col)))
        specs.append(pl.BlockSpec((BLOCK, kd), lambda b, col=col: (b, col)))
        specs.append(pl.BlockSpec((BLOCK, kd), lambda b, col=col: (jnp.minimum(b + 1, nb - 1), col)))
    return specs


def _band_valid(b, group, n_ctx, n):
    q_pos = b * BLOCK + lax.broadcasted_iota(jnp.int32, (group * BLOCK, 1), 0) % BLOCK
    k_pos = (b - 1) * BLOCK + lax.broadcasted_iota(jnp.int32, (1, 3 * BLOCK), 1)
    return (jnp.abs(k_pos - q_pos) <= BLOCK) & (k_pos >= n_ctx) & (k_pos < n) & (q_pos >= n_ctx)


NT = (((1,), (1,)), ((), ()))
NN = (((1,), (0,)), ((), ()))
TN = (((0,), (0,)), ((), ()))


def _dot(a, b, dn):
    return lax.dot_general(a, b, dn, preferred_element_type=F32)


def _softmax_parts(qg, kc, kl, valid, snk):
    s_c = _dot(qg, kc, NT)
    s_l = jnp.where(valid, _dot(qg, kl, NT), NEG_INF)
    m = jnp.maximum(jnp.maximum(jnp.max(s_c, axis=1, keepdims=True), jnp.max(s_l, axis=1, keepdims=True)), snk)
    e_c, e_l, e_s = jnp.exp(s_c - m), jnp.exp(s_l - m), jnp.exp(snk - m)
    den = jnp.sum(e_c, axis=1, keepdims=True) + jnp.sum(e_l, axis=1, keepdims=True) + e_s
    return e_c / den, e_l / den, e_s / den


def _attention(qkv, sink, n_ctx, hd, kd, name):
    n = qkv.shape[0]
    nb = n // BLOCK
    n_kv = kd // HEAD_DIM
    group = hd // kd

    def body(q_ref, kc_ref, vc_ref, k0, k1, k2, v0, v1, v2, sink_ref, o_ref):
        b = pl.program_id(0)
        valid = _band_valid(b, group, n_ctx, n)
        heads = []
        for h in range(n_kv):
            hs = slice(h * HEAD_DIM, (h + 1) * HEAD_DIM)
            qg = jnp.concatenate([q_ref[:, (h * group + j) * HEAD_DIM:(h * group + j + 1) * HEAD_DIM] for j in range(group)], axis=0)
            kl = jnp.concatenate([k0[:, hs], k1[:, hs], k2[:, hs]], axis=0)
            vl = jnp.concatenate([v0[:, hs], v1[:, hs], v2[:, hs]], axis=0)
            snk = jnp.concatenate([jnp.full((BLOCK, 1), sink_ref[h * group + j], F32) for j in range(group)], axis=0)
            p_c, p_l, _ = _softmax_parts(qg, kc_ref[:, hs], kl, valid, snk)
            o = _dot(p_c.astype(BF16), vc_ref[:, hs], NN) + _dot(p_l.astype(BF16), vl, NN)
            heads += [o[j * BLOCK:(j + 1) * BLOCK, :] for j in range(group)]
        o_ref[...] = jnp.concatenate(heads, axis=1)

    return pl.pallas_call(
        body, name=name, out_shape=jax.ShapeDtypeStruct((n, hd), F32), grid=(nb,),
        in_specs=_attn_specs(nb, n_ctx, hd, kd) + [pl.BlockSpec(memory_space=pltpu.SMEM)],
        out_specs=pl.BlockSpec((BLOCK, hd), lambda b: (b, 0)),
        compiler_params=_params(("parallel",)),
    )(qkv, qkv, qkv, qkv, qkv, qkv, qkv, qkv, qkv, sink)


def _attention_bwd(qkv, sink, ao, d_mg, cpar, n_ctx, hd, kd, name):
    n = qkv.shape[0]
    nb = n // BLOCK
    n_kv = kd // HEAD_DIM
    group = hd // kd
    n_heads = n_kv * group

    def body(q_ref, kc_ref, vc_ref, k0, k1, k2, v0, v1, v2, sink_ref, ao_ref, dmg_ref, cp_ref,
             dq_ref, part_ref, dctx_ref, dsink_ref, dgain_ref):
        b = pl.program_id(0)

        @pl.when(b == 0)
        def _():
            dctx_ref[...] = jnp.zeros_like(dctx_ref)
            dsink_ref[...] = jnp.zeros_like(dsink_ref)
            dgain_ref[...] = jnp.zeros_like(dgain_ref)

        ao_v = ao_ref[...]
        ra = lax.rsqrt(_rowmean(ao_v * ao_v) + EPS)
        an = ao_v * ra
        dmg = dmg_ref[...]
        dgain_ref[0:1, :] += _colsum(dmg * an)
        d_an = dmg * cp_ref[5:6, :]
        d_ao = (ra * (d_an - an * _rowmean(d_an * an))).astype(BF16)

        valid = _band_valid(b, group, n_ctx, n)
        lane = lax.broadcasted_iota(jnp.int32, (1, 128), 1)
        dsink_row = jnp.zeros((1, 128), F32)
        dq_heads, dkc, dvc, dkl, dvl = [], [], [], [], []
        for h in range(n_kv):
            hs = slice(h * HEAD_DIM, (h + 1) * HEAD_DIM)
            cols = [slice((h * group + j) * HEAD_DIM, (h * group + j + 1) * HEAD_DIM) for j in range(group)]
            qg = jnp.concatenate([q_ref[:, cs] for cs in cols], axis=0)
            dog = jnp.concatenate([d_ao[:, cs] for cs in cols], axis=0)
            kc, vc = kc_ref[:, hs], vc_ref[:, hs]
            kl = jnp.concatenate([k0[:, hs], k1[:, hs], k2[:, hs]], axis=0)
            vl = jnp.concatenate([v0[:, hs], v1[:, hs], v2[:, hs]], axis=0)
            snk = jnp.concatenate([jnp.full((BLOCK, 1), sink_ref[h * group + j], F32) for j in range(group)], axis=0)
            p_c, p_l, p_s = _softmax_parts(qg, kc, kl, valid, snk)
            dp_c = _dot(dog, vc, NT)
            dp_l = _dot(dog, vl, NT)
            delta = jnp.sum(p_c * dp_c, axis=1, keepdims=True) + jnp.sum(p_l * dp_l, axis=1, keepdims=True)
            ds_c = (p_c * (dp_c - delta)).astype(BF16)
            ds_l = (p_l * (dp_l - delta)).astype(BF16)
            psd = p_s * delta
            for j in range(group):
                val = -jnp.sum(psd[j * BLOCK:(j + 1) * BLOCK, :], axis=0, keepdims=True)
                dsink_row = dsink_row + jnp.where(lane == h * group + j, val, 0.0)
            dq = (_dot(ds_c, kc, NN) + _dot(ds_l, kl, NN)) * SCALE
            dq_heads += [dq[j * BLOCK:(j + 1) * BLOCK, :] for j in range(group)]
            dkc.append(_dot(ds_c, qg, TN))
            dvc.append(_dot(p_c.astype(BF16), dog, TN))
            dkl.append(_dot(ds_l, qg, TN))
            dvl.append(_dot(p_l.astype(BF16), dog, TN))
        dq_ref[...] = jnp.concatenate(dq_heads, axis=1)
        dctx_ref[...] += jnp.concatenate(dkc + dvc, axis=1)
        loc = jnp.concatenate(dkl + dvl, axis=1)
        for j in range(3):
            part_ref[j] = loc[j * BLOCK:(j + 1) * BLOCK, :]
        dsink_ref[0:1, :] += dsink_row

    assert n_heads <= 128
    out_shape = (jax.ShapeDtypeStruct((n, hd), F32), jax.ShapeDtypeStruct((nb, 3, BLOCK, 2 * kd), F32),
                 jax.ShapeDtypeStruct((n_ctx, 2 * kd), F32), jax.ShapeDtypeStruct((8, 128), F32),
                 jax.ShapeDtypeStruct((8, hd), F32))
    return pl.pallas_call(
        body, name=name, out_shape=out_shape, grid=(nb,),
        in_specs=_attn_specs(nb, n_ctx, hd, kd) + [
            pl.BlockSpec(memory_space=pltpu.SMEM), pl.BlockSpec((BLOCK, hd), lambda b: (b, 0)),
            pl.BlockSpec((BLOCK, hd), lambda b: (b, 1)), pl.BlockSpec((8, hd), lambda b: (0, 0))],
        out_specs=(pl.BlockSpec((BLOCK, hd), lambda b: (b, 0)),
                   pl.BlockSpec((None, 3, BLOCK, 2 * kd), lambda b: (b, 0, 0, 0)),
                   pl.BlockSpec((n_ctx, 2 * kd), lambda b: (0, 0)), pl.BlockSpec((8, 128), lambda b: (0, 0)),
                   pl.BlockSpec((8, hd), lambda b: (0, 0))),
        compiler_params=_params(("arbitrary",)),
    )(qkv, qkv, qkv, qkv, qkv, qkv, qkv, qkv, qkv, sink, ao, d_mg, cpar)


def _halo_specs(tr, n, width, col=0):
    q = tr // 8
    return [pl.BlockSpec((8, width), lambda i: (jnp.maximum(i * q - 1, 0), col)),
            pl.BlockSpec((8, width), lambda i: (jnp.minimum((i + 1) * q, n // 8 - 1), col))]


def _mix_fwd(p, ao, cpar, n_ctx, dc, name):
    n, d_in = p.shape
    tr = _pick(n, (256, 128))

    def body(p_ref, pp_ref, pn_ref, ao_ref, cp_ref, o_ref):
        i = pl.program_id(0)
        bg = p_ref[:, 0:dc]
        u = p_ref[:, dc:2 * dc] * p_ref[:, 2 * dc:3 * dc]
        u_before = pp_ref[7:8, dc:2 * dc] * pp_ref[7:8, 2 * dc:3 * dc]
        u_after = pn_ref[0:1, dc:2 * dc] * pn_ref[0:1, 2 * dc:3 * dc]
        loc = lax.broadcasted_iota(jnp.int32, (tr, 1), 0)
        gid = i * tr + loc
        has_prev = (gid != 0) & (gid != n_ctx)
        has_next = (gid != n_ctx - 1) & (gid != n - 1)
        u_m1 = jnp.where(has_prev, jnp.where(loc == 0, u_before, pltpu.roll(u, 1, 0)), 0.0)
        u_p1 = jnp.where(has_next, jnp.where(loc == tr - 1, u_after, pltpu.roll(u, tr - 1, 0)), 0.0)
        cv = u_m1 * cp_ref[0:1, :] + u * cp_ref[1:2, :] + u_p1 * cp_ref[2:3, :] + cp_ref[3:4, :]
        co = bg * cv
        nc = (co * lax.rsqrt(_rowmean(co * co) + EPS)) * cp_ref[4:5, :]
        ao_v = ao_ref[...]
        na = (ao_v * lax.rsqrt(_rowmean(ao_v * ao_v) + EPS)) * cp_ref[5:6, :]
        o_ref[...] = jnp.concatenate([nc, na], axis=1).astype(BF16)

    return pl.pallas_call(
        body, name=name, out_shape=jax.ShapeDtypeStruct((n, 2 * dc), BF16), grid=(n // tr,),
        in_specs=[pl.BlockSpec((tr, d_in), lambda i: (i, 0))] + _halo_specs(tr, n, d_in)
        + [pl.BlockSpec((tr, dc), lambda i: (i, 0)), pl.BlockSpec((8, dc), lambda i: (0, 0))],
        out_specs=pl.BlockSpec((tr, 2 * dc), lambda i: (i, 0)),
        compiler_params=_params(("parallel",)),
    )(p, p, p, ao, cpar)


def _mix_bwd(d_mg, p, cpar, d_q, parts, d_ctx, cs, sn, n_ctx, dc, hd, kd, name):
    n, d_in = p.shape
    tr = BLOCK
    nb = n // tr
    ext = tr + 16
    n_ctx_blocks = n_ctx // BLOCK

    def body(dm_ref, dmp_ref, dmn_ref, p_ref, pp_ref, pn_ref, cp_ref, dq_ref, pa_ref, pb_ref, pc_ref, dctx_ref,
             cs_ref, sn_ref, dp_ref, acc_ref):
        i = pl.program_id(0)

        @pl.when(i == 0)
        def _():
            acc_ref[...] = jnp.zeros_like(acc_ref)

        def cat(before, here, after):
            return jnp.concatenate([before, here, after], axis=0)

        bg = cat(pp_ref[:, 0:dc], p_ref[:, 0:dc], pn_ref[:, 0:dc])
        cg = cat(pp_ref[:, dc:2 * dc], p_ref[:, dc:2 * dc], pn_ref[:, dc:2 * dc])
        hh = cat(pp_ref[:, 2 * dc:3 * dc], p_ref[:, 2 * dc:3 * dc], pn_ref[:, 2 * dc:3 * dc])
        dme = cat(dmp_ref[...], dm_ref[...], dmn_ref[...])
        gid = i * tr - 8 + lax.broadcasted_iota(jnp.int32, (ext, 1), 0)
        inside = (gid >= 0) & (gid < n)
        has_prev = inside & (gid != 0) & (gid != n_ctx)
        has_next = inside & (gid != n_ctx - 1) & (gid != n - 1)
        w0, w1, w2, bias, gain = cp_ref[0:1, :], cp_ref[1:2, :], cp_ref[2:3, :], cp_ref[3:4, :], cp_ref[4:5, :]
        u = jnp.where(inside, cg * hh, 0.0)
        u_m1 = jnp.where(has_prev, pltpu.roll(u, 1, 0), 0.0)
        u_p1 = jnp.where(has_next, pltpu.roll(u, ext - 1, 0), 0.0)
        cv = u_m1 * w0 + u * w1 + u_p1 * w2 + bias
        co = bg * cv
        rc = lax.rsqrt(_rowmean(co * co) + EPS)
        cn = co * rc
        d_cn = dme * gain
        d_co = rc * (d_cn - cn * _rowmean(d_cn * cn))
        d_cv = jnp.where(inside, d_co * bg, 0.0)
        d_bg = d_co * cv
        d_cv_p1 = jnp.where(has_next, pltpu.roll(d_cv, ext - 1, 0), 0.0)
        d_cv_m1 = jnp.where(has_prev, pltpu.roll(d_cv, 1, 0), 0.0)
        d_u = d_cv_p1 * w0 + d_cv * w1 + d_cv_m1 * w2
        mid = slice(8, 8 + tr)
        acc_ref[0:1, :] += _colsum((d_cv * u_m1)[mid])
        acc_ref[1:2, :] += _colsum((d_cv * u)[mid])
        acc_ref[2:3, :] += _colsum((d_cv * u_p1)[mid])
        acc_ref[3:4, :] += _colsum(d_cv[mid])
        acc_ref[4:5, :] += _colsum((dme * cn)[mid])

        d_kv = (jnp.where(i >= 1, pa_ref[...], 0.0) + pb_ref[...] + jnp.where(i + 1 < nb, pc_ref[...], 0.0))
        ctx_rows = dctx_ref[pl.ds(pl.multiple_of(jnp.minimum(i, n_ctx_blocks - 1) * BLOCK, BLOCK), BLOCK), :]
        d_kv = d_kv + jnp.where(i < n_ctx_blocks, ctx_rows, 0.0)
        cs_v, sn_v = cs_ref[...], sn_ref[...]
        d_qu = _rope(dq_ref[...], cs_v, sn_v, -1.0)
        d_ku = _rope(d_kv[:, 0:kd], cs_v, sn_v, -1.0)
        dp_ref[...] = jnp.concatenate(
            [d_bg[mid], (d_u * hh)[mid], (d_u * cg)[mid], d_qu, d_ku, d_kv[:, kd:2 * kd]], axis=1).astype(BF16)

    part = lambda sel, which: pl.BlockSpec((None, None, BLOCK, 2 * kd), lambda i: (sel(i), which, 0, 0))
    return pl.pallas_call(
        body, name=name, out_shape=(jax.ShapeDtypeStruct((n, d_in), BF16), jax.ShapeDtypeStruct((8, dc), F32)),
        grid=(nb,),
        in_specs=[pl.BlockSpec((tr, dc), lambda i: (i, 0))] + _halo_specs(tr, n, dc)
        + [pl.BlockSpec((tr, d_in), lambda i: (i, 0))] + _halo_specs(tr, n, d_in)
        + [pl.BlockSpec((8, dc), lambda i: (0, 0)), pl.BlockSpec((tr, hd), lambda i: (i, 0)),
           part(lambda i: jnp.maximum(i - 1, 0), 2), part(lambda i: i, 1), part(lambda i: jnp.minimum(i + 1, nb - 1), 0),
           pl.BlockSpec((n_ctx, 2 * kd), lambda i: (0, 0)),
           pl.BlockSpec((tr, 128), lambda i: (i, 0)), pl.BlockSpec((tr, 128), lambda i: (i, 0))],
        out_specs=(pl.BlockSpec((tr, d_in), lambda i: (i, 0)), pl.BlockSpec((8, dc), lambda i: (0, 0))),
        compiler_params=_params(("arbitrary",)),
    )(d_mg, d_mg, d_mg, p, p, p, cpar, d_q, parts, parts, parts, d_ctx, cs, sn)


def _loss_bwd(t, gain, target, n_ctx, name):
    n, d = t.shape
    tr = _pick(n_ctx, (256, 128))
    first = n_ctx // tr

    def body(t_ref, g_ref, y_ref, dt_ref, loss_ref, dg_ref):
        i = pl.program_id(0)

        @pl.when(i == 0)
        def _():
            loss_ref[...] = jnp.zeros_like(loss_ref)
            dg_ref[...] = jnp.zeros_like(dg_ref)

        @pl.when(i < first)
        def _():
            dt_ref[...] = jnp.zeros_like(dt_ref)

        @pl.when(i >= first)
        def _():
            x = t_ref[...]
            g = g_ref[...]
            r = lax.rsqrt(_rowmean(x * x) + EPS)
            xn = x * r
            err = xn * g - y_ref[...]
            loss_ref[...] += 0.5 * _colsum(_rowmean(err * err))
            dy = err * (1.0 / d)
            dg_ref[0:1, :] += _colsum(dy * xn)
            dxn = dy * g
            dt_ref[...] = r * (dxn - xn * _rowmean(dxn * xn))

    row = pl.BlockSpec((tr, d), lambda i: (i, 0))
    return pl.pallas_call(
        body, name=name,
        out_shape=(jax.ShapeDtypeStruct((n, d), F32), jax.ShapeDtypeStruct((8, 128), F32), jax.ShapeDtypeStruct((8, d), F32)),
        grid=(n // tr,),
        in_specs=[row, pl.BlockSpec((1, d), lambda i: (0, 0)), pl.BlockSpec((tr, d), lambda i: (jnp.maximum(i - first, 0), 0))],
        out_specs=(row, pl.BlockSpec((8, 128), lambda i: (0, 0)), pl.BlockSpec((8, d), lambda i: (0, 0))),
        compiler_params=_params(("arbitrary",)),
    )(t, gain, target)


def _silu16(c16, name):
    def body(c_ref, o_ref):
        v = c_ref[...]
        o_ref[...] = (v * jax.nn.sigmoid(v)).astype(BF16)

    return pl.pallas_call(body, name=name, out_shape=jax.ShapeDtypeStruct(c16.shape, BF16))(c16)


def _cctx_grad(parts, c_ctx, name):
    def body(p_ref, c_ref, o_ref):
        g = _colsum(p_ref[...])
        v = c_ref[...]
        s = jax.nn.sigmoid(v)
        o_ref[...] = g * (s * (1.0 + v * (1.0 - s)))

    return pl.pallas_call(body, name=name, out_shape=jax.ShapeDtypeStruct(c_ctx.shape, F32))(parts, c_ctx)


def _rope_tables(n_ctx, n_tok):
    half = HEAD_DIM // 4
    inv = ROPE_THETA ** (-jnp.arange(0, HEAD_DIM // 2, 2, dtype=F32) / (HEAD_DIM // 2))
    rows = n_tok // GRID_W
    row_pos = jnp.repeat(jnp.arange(rows, dtype=F32), GRID_W)
    col_pos = jnp.tile(jnp.arange(GRID_W, dtype=F32), rows)
    ang_r, ang_c = row_pos[:, None] * inv[None, :], col_pos[:, None] * inv[None, :]
    cos = jnp.concatenate([jnp.cos(ang_r), jnp.cos(ang_r), jnp.cos(ang_c), jnp.cos(ang_c)], axis=1)
    sin = jnp.concatenate([-jnp.sin(ang_r), jnp.sin(ang_r), -jnp.sin(ang_c), jnp.sin(ang_c)], axis=1)
    assert cos.shape[1] == 4 * half == HEAD_DIM
    cos = jnp.concatenate([jnp.ones((n_ctx, HEAD_DIM), F32), cos], axis=0)
    sin = jnp.concatenate([jnp.zeros((n_ctx, HEAD_DIM), F32), sin], axis=0)
    return jnp.tile(cos, (1, 2)), jnp.tile(sin, (1, 2))


def kernel(x, c, ctx, c_ctx, w_ada, b_ada, g_norm1, g_norm2, w_in, conv_w, conv_b, sink, g_out_conv, g_out_attn, w_out, w_mlp1, w_mlp2, g_final, loss_target, m_c_ctx, m_w_ada, m_b_ada, m_g_norm1, m_g_norm2, m_w_in, m_conv_w, m_conv_b, m_sink, m_g_out_conv, m_g_out_attn, m_w_out, m_w_mlp1, m_w_mlp2, m_g_final, v_c_ctx, v_w_ada, v_b_ada, v_g_norm1, v_g_norm2, v_w_in, v_conv_w, v_conv_b, v_sink, v_g_out_conv, v_g_out_attn, v_w_out, v_w_mlp1, v_w_mlp2, v_g_final):
    n_lat, d = x.shape[1], x.shape[2]
    n_ctx = ctx.shape[1]
    n = n_ctx + n_lat
    depth = w_in.shape[0]
    dc = d // 2
    hd, kd = dc, N_KV_HEADS * HEAD_DIM
    n_heads = hd // HEAD_DIM
    d_in = 3 * dc + hd + 2 * kd
    cin, c_ada, r_out, c_ff, r_ff = w_in.shape[2], w_ada.shape[2], w_out.shape[1], w_mlp1.shape[2], w_mlp2.shape[1]
    d_ff = N_DEV * c_ff
    cw = conv_w.shape[2]
    assert d_in == N_DEV * cin and n_ctx % BLOCK == 0 and n_lat % BLOCK == 0 and hd % kd == 0
    dev = 4 * lax.axis_index("x") + 2 * lax.axis_index("y") + lax.axis_index("c")

    c_all, conv_w_all = _all_gather([c, conv_w], "gather_cond")
    conv_w_full = jnp.transpose(conv_w_all, (1, 2, 0, 3)).reshape(depth, 3, dc)
    c16 = jnp.concatenate([c_all.reshape(N_DEV, d), jnp.broadcast_to(c_ctx[None, :], (8, d))], axis=0)
    sc16 = _silu16(c16, "silu_cond")

    b_ada_loc = lax.dynamic_index_in_dim(b_ada.reshape(depth, N_DEV, c_ada), dev, axis=1, keepdims=False)
    tn_ada = _pick(c_ada, FEAT)

    def add_bias(acc, i, j, extra, outs):
        outs[0][...] = acc + extra[0][...]

    mod_loc = []
    for l in range(depth):
        mod_loc.append(_matmul(
            sc16, w_ada, dims="nn", shape=(16, c_ada, d), tiles=(16, tn_ada, _pick(d, KDIM)),
            b_spec=_w_spec("nat", l, None, "nn", tn_ada, _pick(d, KDIM)), epilogue=add_bias,
            extras=[(b_ada_loc[l][None, :], pl.BlockSpec((1, tn_ada), lambda i, j, k: (0, j)))], name=f"ada_fwd{l}"))
    (mod_all,) = _all_gather([jnp.stack(mod_loc)], "gather_mod")
    mod_full = jnp.transpose(mod_all, (1, 2, 0, 3)).reshape(depth, 16, N_MOD * d)
    mod_mine = lax.dynamic_index_in_dim(mod_full, dev, axis=1, keepdims=True)
    mod8 = jnp.concatenate([mod_mine, mod_full[:, 8:9], jnp.zeros((depth, 6, N_MOD * d), F32)], axis=1)

    w_in_b = _to_bf16(w_in.reshape(depth * d, cin), "cast_w_in").reshape(depth, d, cin)
    w_out_b = _to_bf16(w_out.reshape(depth * r_out, d), "cast_w_out").reshape(depth, r_out, d)
    w1_b = _to_bf16(w_mlp1.reshape(depth * d, c_ff), "cast_w_mlp1").reshape(depth, d, c_ff)
    w2_b = _to_bf16(w_mlp2.reshape(depth * r_ff, d), "cast_w_mlp2").reshape(depth, r_ff, d)
    g_in, g_out, g_w1, g_w2 = _all_gather([w_in_b, w_out_b, w1_b, w2_b], "gather_weights", layered=True)
    w_in_full = jnp.transpose(g_in, (0, 2, 1, 3)).reshape(depth, d, d_in)
    w_out_full = g_out.reshape(depth, d, d)
    w2_full = g_w2.reshape(depth, d_ff, d)

    t = jnp.concatenate([ctx[0], x[0]], axis=0)
    cs, sn = _rope_tables(n_ctx, n_lat)
    tm = _pick(n, TOK)
    tk_d = _pick(d, KDIM)
    tm_res = _pick(n, (1056, 768, 256, 128))
    tn_in = _pick(d_in, (768, 512, 256, 128))

    def resid_epilogue(tile_rows):
        def epi(acc, i, j, extra, outs):
            is_ctx = _row_ids(i, tile_rows) < n_ctx
            outs[0][...] = extra[0][...] + _sel(is_ctx, extra[1]) * acc
            outs[1][...] = acc.astype(BF16)
        return epi

    def sq_relu_epilogue(acc, i, j, extra, outs):
        outs[0][...] = acc.astype(BF16)
        rl = jnp.maximum(acc, 0.0)
        outs[1][...] = (rl * rl).astype(BF16)

    def d_sq_relu_epilogue(acc, i, j, extra, outs):
        outs[0][...] = (acc * (2.0 * jnp.maximum(extra[0][...].astype(F32), 0.0))).astype(BF16)

    saved = []
    for l in range(depth):
        cpar = jnp.concatenate([conv_w_full[l], conv_b[l][None], g_out_conv[l][None], g_out_attn[l][None],
                                jnp.zeros((2, dc), F32)], axis=0)
        h = _norm_mod(t, g_norm1[l][None], mod8[l], 0, n_ctx, f"norm1_{l}")
        p = _matmul(h, w_in_full, dims="nn", shape=(n, d_in, d), tiles=(tm, tn_in, tk_d),
                    b_spec=_w_spec("nat", l, None, "nn", tn_in, tk_d), name=f"in_proj{l}")
        qkv = _rope_split(p, cs, sn, dc, hd, kd, f"rope{l}")
        ao = _attention(qkv, sink[l], n_ctx, hd, kd, f"attn{l}")
        mg = _mix_fwd(p, ao, cpar, n_ctx, dc, f"mix{l}")
        tn = _pick(d, (512, 256, 128))
        t2, z = _matmul(
            mg, w_out_full, dims="nn", shape=(n, d, d), tiles=(tm, tn, tk_d), b_spec=_w_spec("nat", l, None, "nn", tn, tk_d),
            out_dtypes=(F32, BF16), epilogue=resid_epilogue(tm),
            extras=[(t, pl.BlockSpec((tm, tn), lambda i, j, k: (i, j))),
                    (mod8[l], pl.BlockSpec((8, tn), lambda i, j, k, tn=tn: (0, 2 * (d // tn) + j)))], name=f"out_proj{l}")
        h2 = _norm_mod(t2, g_norm2[l][None], mod8[l], 3, n_ctx, f"norm2_{l}")
        tn = _pick(c_ff, FEAT)
        a, s = _matmul(h2, g_w1, dims="nn", shape=(n, d_ff, d), tiles=(tm, tn, tk_d),
                       b_spec=_w_spec("cols", l, c_ff, "nn", tn, tk_d), out_dtypes=(BF16, BF16),
                       epilogue=sq_relu_epilogue, name=f"mlp_up{l}")
        tn = _pick(d, FEAT)
        tk = _pick(d_ff, (1024, 512, 256, 128))
        t3, o = _matmul(
            s, w2_full, dims="nn", shape=(n, d, d_ff), tiles=(tm_res, tn, tk), b_spec=_w_spec("nat", l, None, "nn", tn, tk),
            out_dtypes=(F32, BF16), epilogue=resid_epilogue(tm_res),
            extras=[(t2, pl.BlockSpec((tm_res, tn), lambda i, j, k: (i, j))),
                    (mod8[l], pl.BlockSpec((8, tn), lambda i, j, k, tn=tn: (0, 5 * (d // tn) + j)))], name=f"mlp_down{l}")
        saved.append((t, h, p, qkv, ao, mg, z, t2, h2, a, s, o, cpar))
        t = t3

    d_t, loss_tile, dg_final = _loss_bwd(t, g_final[None], loss_target[0], n_ctx, "loss")
    loss = lax.psum(loss_tile[0, 0], ("x", "y", "c"))

    gb_in = lax.empty((depth, d, d_in), F32)
    gb_out = lax.empty((depth, d, d), F32)
    gb_w1 = lax.empty((depth, N_DEV, d, c_ff), F32)
    gb_w2 = lax.empty((depth, d_ff, d), F32)
    tkn = _pick(n, TOK)
    small = [None] * depth
    for l in reversed(range(depth)):
        t_in, h, p, qkv, ao, mg, z, t2, h2, a, s, o, cpar = saved[l]
        dob, dgate2 = _gate_bwd(d_t, o, mod8[l], 5, n_ctx, f"gate2_bwd{l}")
        tm_g = _pick(d_ff, FEAT)
        tn = _pick(d, FEAT)
        gb_w2 = _matmul(s, dob, dims="tn", shape=(d_ff, d, n), tiles=(tm_g, tn, tkn),
                        out_specs=(_g_spec("nat", l, None, tm_g, tn),), out_shapes=(jax.ShapeDtypeStruct(gb_w2.shape, F32),),
                        alias=gb_w2, name=f"mlp_down_dw{l}")
        tn = _pick(d_ff, FEAT)
        da = _matmul(dob, w2_full, dims="nt", shape=(n, d_ff, d), tiles=(tm, tn, tk_d),
                     b_spec=_w_spec("nat", l, None, "nt", tn, tk_d), out_dtypes=(BF16,), epilogue=d_sq_relu_epilogue,
                     extras=[(a, pl.BlockSpec((tm, tn), lambda i, j, k: (i, j)))], name=f"mlp_down_dx{l}")
        tm_g = _pick(d, FEAT)
        tn = _pick(c_ff, FEAT)
        gb_w1 = _matmul(h2, da, dims="tn", shape=(d, d_ff, n), tiles=(tm_g, tn, tkn),
                        out_specs=(_g_spec("cols", l, c_ff, tm_g, tn),), out_shapes=(jax.ShapeDtypeStruct(gb_w1.shape, F32),),
                        alias=gb_w1, name=f"mlp_up_dw{l}")
        tn = _pick(d, FEAT)
        tk = _pick(c_ff, (1024, 512, 256, 128))
        dh2 = _matmul(da, g_w1, dims="nt", shape=(n, d, d_ff), tiles=(tm, tn, tk),
                      b_spec=_w_spec("cols", l, c_ff, "nt", tn, tk), name=f"mlp_up_dx{l}")
        d_t2, dss2, dgn2 = _norm_mod_bwd(dh2, t2, d_t, g_norm2[l][None], mod8[l], 3, n_ctx, f"norm2_bwd{l}")
        dzb, dgate1 = _gate_bwd(d_t2, z, mod8[l], 2, n_ctx, f"gate1_bwd{l}")
        tm_g = _pick(d, FEAT)
        tn = _pick(d, FEAT)
        gb_out = _matmul(mg, dzb, dims="tn", shape=(d, d, n), tiles=(tm_g, tn, tkn),
                         out_specs=(_g_spec("nat", l, None, tm_g, tn),), out_shapes=(jax.ShapeDtypeStruct(gb_out.shape, F32),),
                         alias=gb_out, name=f"out_proj_dw{l}")
        tn = _pick(d, FEAT)
        d_mg = _matmul(dzb, w_out_full, dims="nt", shape=(n, d, d), tiles=(tm, tn, tk_d),
                       b_spec=_w_spec("nat", l, None, "nt", tn, tk_d), name=f"out_proj_dx{l}")
        d_q, parts, d_kv_ctx, d_sink, d_goa = _attention_bwd(qkv, sink[l], ao, d_mg, cpar, n_ctx, hd, kd, f"attn_bwd{l}")
        d_p, conv_acc = _mix_bwd(d_mg, p, cpar, d_q, parts, d_kv_ctx, cs, sn, n_ctx, dc, hd, kd, f"mix_bwd{l}")
        tm_g = _pick(d, FEAT)
        tn = _pick(d_in, (1536, 768, 512, 256, 128))
        gb_in = _matmul(h, d_p, dims="tn", shape=(d, d_in, n), tiles=(tm_g, tn, tkn),
                        out_specs=(_g_spec("nat", l, None, tm_g, tn),), out_shapes=(jax.ShapeDtypeStruct(gb_in.shape, F32),),
                        alias=gb_in, name=f"in_proj_dw{l}")
        tn = _pick(d, FEAT)
        tk = _pick(d_in, (1536, 768, 512, 256, 128))
        dh = _matmul(d_p, w_in_full, dims="nt", shape=(n, d, d_in), tiles=(tm, tn, tk),
                     b_spec=_w_spec("nat", l, None, "nt", tn, tk), name=f"in_proj_dx{l}")
        d_t, dss1, dgn1 = _norm_mod_bwd(dh, t_in, d_t2, g_norm1[l][None], mod8[l], 0, n_ctx, f"norm1_bwd{l}")
        d_mod2 = jnp.concatenate([dss1[0:2], dgate1[0:2], dss2[0:2], dgate2[0:2]], axis=1)
        small[l] = (d_mod2, dgn1[0], dgn2[0], conv_acc, d_sink[0, 0:n_heads], d_goa[0])
    grad_x = d_t[n_ctx:][None]

    def pack(l):
        d_mod2, dgn1, dgn2, conv_acc, d_sink, d_goa = small[l]
        row0 = [d_mod2[0], dgn1, dgn2, conv_acc[3], d_sink, conv_acc[4], d_goa, conv_acc[0:3].reshape(-1)]
        row1 = [d_mod2[1]] + [jnp.zeros_like(v) for v in row0[1:]]
        return jnp.stack([jnp.concatenate(row0), jnp.concatenate(row1)])
    per_layer = N_MOD * d + 2 * d + dc + n_heads + 2 * dc + 3 * dc
    packed = jnp.concatenate([pack(l) for l in range(depth)] +
                             [jnp.stack([dg_final[0], jnp.zeros((d,), F32)])], axis=1)
    f_tot = depth * per_layer + d
    f_pad = -f_tot % 1024
    packed = jnp.pad(packed, ((0, 0), (0, f_pad)))
    (small_all,) = _all_gather([packed], "gather_small")
    small_parts = small_all.reshape(2 * N_DEV, 1, f_tot + f_pad)

    def section(arr, l, off, size):
        return lax.slice_in_dim(arr, l * per_layer + off, l * per_layer + off + size, axis=-1)

    offs = {}
    o_ = 0
    for nm_, sz in (("mod", N_MOD * d), ("gn1", d), ("gn2", d), ("cb", dc), ("sink", n_heads), ("goc", dc), ("goa", dc), ("cw", 3 * dc)):
        offs[nm_] = (o_, sz)
        o_ += sz

    def packw(b_ada_, gn1_, gn2_, cb_, sk_, goc_, goa_, gf_):
        rows = []
        for l in range(depth):
            rows += [b_ada_[l], gn1_[l], gn2_[l], cb_[l], sk_[l], goc_[l], goa_[l], jnp.zeros((3 * dc,), F32)]
        return jnp.pad(jnp.concatenate(rows + [gf_]), (0, f_pad))[None]
    pw = packw(b_ada, g_norm1, g_norm2, conv_b, sink, g_out_conv, g_out_attn, g_final)
    pm = packw(m_b_ada, m_g_norm1, m_g_norm2, m_conv_b, m_sink, m_g_out_conv, m_g_out_attn, m_g_final)
    pv = packw(v_b_ada, v_g_norm1, v_g_norm2, v_conv_b, v_sink, v_g_out_conv, v_g_out_attn, v_g_final)
    sg, sd, sm, sv = _adamw(pw, pm, pv, small_parts, "adamw_small")

    def unpack(arr):
        arr = arr[0]
        out = {}
        for nm_ in ("mod", "gn1", "gn2", "cb", "sink", "goc", "goa", "cw"):
            off, size = offs[nm_]
            out[nm_] = jnp.stack([section(arr, l, off, size) for l in range(depth)])
        out["gf"] = arr[depth * per_layer:depth * per_layer + d]
        return out
    ug, ud, um, uv = unpack(sg), unpack(sd), unpack(sm), unpack(sv)

    cw_grad_full = ug["cw"].reshape(depth, 3, N_DEV, cw)
    cw_grad = lax.dynamic_index_in_dim(cw_grad_full, dev, axis=2, keepdims=False).reshape(1, depth * 3, cw)
    cwg, cwd, cwm, cwv = _adamw(conv_w.reshape(depth * 3, cw), m_conv_w.reshape(depth * 3, cw),
                                v_conv_w.reshape(depth * 3, cw), cw_grad, "adamw_conv_w")
    cw_shape = conv_w.shape

    mod_rows = small_all[:, :, :depth * per_layer].reshape(N_DEV, 2, depth, per_layer)[:, :, :, :N_MOD * d]
    dm16 = jnp.concatenate([mod_rows[:, 0], mod_rows[:, 1]], axis=0)
    dm16 = jnp.transpose(dm16, (1, 0, 2)).reshape(depth, 16, N_DEV, c_ada)
    dm16_loc = lax.dynamic_index_in_dim(dm16, dev, axis=2, keepdims=False)
    gb_ada = lax.empty((depth, d, c_ada), F32)
    dsc_parts = []
    tm_g = _pick(d, FEAT)
    for l in range(depth):
        gb_ada = _matmul(sc16, dm16_loc[l], dims="tn", shape=(d, c_ada, 16), tiles=(tm_g, tn_ada, 16),
                         out_specs=(_g_spec("nat", l, None, tm_g, tn_ada),), out_shapes=(jax.ShapeDtypeStruct(gb_ada.shape, F32),),
                         alias=gb_ada, name=f"ada_dw{l}")
        tn = _pick(d, FEAT)
        tk = _pick(c_ada, KDIM)
        dsc_parts.append(_matmul(dm16_loc[l], w_ada, dims="nt", shape=(16, d, c_ada), tiles=(16, tn, tk),
                                 b_spec=_w_spec("nat", l, None, "nt", tn, tk), name=f"ada_dx{l}"))
    (dsc_all,) = _all_gather([jnp.stack(dsc_parts)[:, 8:16]], "gather_dcond")
    g_cctx = _cctx_grad(dsc_all.reshape(N_DEV * depth * 8, d), c_ctx[None], "c_ctx_grad")
    ccg, ccd, ccm, ccv = _adamw(c_ctx[None], m_c_ctx[None], v_c_ctx[None], g_cctx[None], "adamw_c_ctx")
    adg, add, adm, adv = _adamw(w_ada.reshape(depth * d, c_ada), m_w_ada.reshape(depth * d, c_ada),
                                v_w_ada.reshape(depth * d, c_ada), gb_ada.reshape(1, depth * d, c_ada), "adamw_w_ada")

    own = [jnp.transpose(gb_in.reshape(depth, d, N_DEV, cin), (0, 2, 1, 3)), gb_out.reshape(depth, N_DEV, r_out, d),
           gb_w1, gb_w2.reshape(depth, N_DEV, r_ff, d)]
    got = _xchg_sibling(own, "rs_sibling")
    pair = [_pair_add(o8, g4, f"rs_pair_add{k}") for k, (o8, g4) in enumerate(zip(own, got))]
    parts4 = _xchg_chips(pair, "rs_chips")
    big = []
    for k, (w_, m_, v_) in enumerate(((w_in, m_w_in, v_w_in), (w_out, m_w_out, v_w_out),
                                      (w_mlp1, m_w_mlp1, v_w_mlp1), (w_mlp2, m_w_mlp2, v_w_mlp2))):
        r2, c2 = w_.shape[0] * w_.shape[1], w_.shape[2]
        res = _adamw(w_.reshape(r2, c2), m_.reshape(r2, c2), v_.reshape(r2, c2), parts4[k].reshape(N_CHIP, r2, c2),
                     f"adamw_big{k}")
        big.append([a_.reshape(w_.shape) for a_ in res])

    def leaf(i):
        return (
            (ccg, ccd, ccm, ccv)[i][0], (adg, add, adm, adv)[i].reshape(w_ada.shape),
            (ug, ud, um, uv)[i]["mod"], (ug, ud, um, uv)[i]["gn1"], (ug, ud, um, uv)[i]["gn2"], big[0][i],
            (cwg, cwd, cwm, cwv)[i].reshape(cw_shape), (ug, ud, um, uv)[i]["cb"], (ug, ud, um, uv)[i]["sink"],
            (ug, ud, um, uv)[i]["goc"], (ug, ud, um, uv)[i]["goa"], big[1][i], big[2][i], big[3][i], (ug, ud, um, uv)[i]["gf"])

    return (loss, grad_x) + leaf(0) + leaf(1) + leaf(2) + leaf(3)
```

```python
import functools

import jax
import jax.numpy as jnp
from jax import lax
from jax.experimental import pallas as pl
from jax.experimental.pallas import tpu as pltpu

HEAD_DIM = 64
N_KV_HEADS = 4
BLOCK = 128
GRID_W = 64
ROPE_THETA = 10000.0
EPS = 1e-6
N_MOD = 6
SCALE = HEAD_DIM ** -0.5
NEG_INF = -1e30
ADAM_LR = 0.001
ADAM_B1 = 0.9
ADAM_B2 = 0.999
ADAM_EPS = 1e-08
ADAM_WD = 0.01
ADAM_STEP = 10
N_DEV = 8
N_CHIP = 4
VMEM_LIMIT_BYTES = 48 * 1024 * 1024
MESH = pl.DeviceIdType.MESH
BF16 = jnp.bfloat16
F32 = jnp.float32
ANY = pl.BlockSpec(memory_space=pl.ANY)


def _pick(dim, prefs):
    for p in prefs:
        if p <= dim and dim % p == 0:
            return p
    return dim


def _params(sem):
    return pltpu.CompilerParams(dimension_semantics=sem, vmem_limit_bytes=VMEM_LIMIT_BYTES)


def _row_ids(i, tr):
    return i * tr + lax.broadcasted_iota(jnp.int32, (tr, 1), 0)


def _colsum(v):
    return jnp.sum(v, axis=0, keepdims=True)


def _rowmean(v):
    return jnp.mean(v, axis=1, keepdims=True)


def _all_gather(xs, name):
    na = len(xs)

    def body(*refs):
        x_refs, o_refs = refs[:na], refs[na:2 * na]
        send_sems, recv_sems, local_sems = refs[2 * na:]
        x, y, c = lax.axis_index("x"), lax.axis_index("y"), lax.axis_index("c")
        me, sibling = (x, y, c), (x, y, 1 - c)
        chips = [(1 - x, y), (x, 1 - y), (1 - x, 1 - y)]

        def slot(a, px, py, pc):
            return o_refs[a].at[4 * px + 2 * py + pc]

        def copy(a, k, block, to, src=None):
            return pltpu.make_async_remote_copy(
                src_ref=slot(a, *block) if src is None else src, dst_ref=slot(a, *block),
                send_sem=send_sems.at[a, k], recv_sem=recv_sems.at[a, k], device_id=to, device_id_type=MESH)

        mine = [pltpu.make_async_copy(x_refs[a], slot(a, *me), local_sems.at[a]) for a in range(na)]
        for cp in mine:
            cp.start()
        first = []
        for a in range(na):
            first.append(copy(a, 0, me, sibling, src=x_refs[a]))
            first += [copy(a, 1 + j, me, (*chip, c), src=x_refs[a]) for j, chip in enumerate(chips)]
        for cp in first:
            cp.start()
        passed = []
        for j, chip in enumerate(chips):
            for a in range(na):
                copy(a, 1 + j, (*chip, c), me).wait_recv()
                fwd = copy(a, 4 + j, (*chip, c), sibling)
                fwd.start()
                passed.append(fwd)
        for a in range(na):
            copy(a, 0, sibling, me).wait_recv()
            for j, chip in enumerate(chips):
                copy(a, 4 + j, (*chip, 1 - c), me).wait_recv()
        for cp in first + passed:
            cp.wait_send()
        for cp in mine:
            cp.wait()

    outs = pl.pallas_call(
        body, name=name,
        out_shape=tuple(jax.ShapeDtypeStruct((N_DEV,) + x.shape, x.dtype) for x in xs),
        in_specs=[ANY] * na, out_specs=tuple([ANY] * na),
        scratch_shapes=[pltpu.SemaphoreType.DMA((na, 7)), pltpu.SemaphoreType.DMA((na, 7)),
                        pltpu.SemaphoreType.DMA((na,))],
    )(*xs)
    return list(outs)


class _Job:
    def __init__(self, ins, outs, alias, sems, copies):
        self.ins, self.outs, self.alias, self.sems, self._copies = ins, outs, alias, sems, copies

    def start(self, in_refs, out_refs, sems):
        local, sends, _ = self._copies(in_refs, out_refs, sems)
        for make in local + sends:
            make().start()

    def wait(self, in_refs, out_refs, sems):
        local, sends, arrivals = self._copies(in_refs, out_refs, sems)
        for make in arrivals:
            make().wait_recv()
        for make in sends:
            make().wait_send()
        for make in local:
            make().wait()


def _other_chips():
    x, y = lax.axis_index("x"), lax.axis_index("y")
    return [(1 - x, y), (x, 1 - y), (1 - x, 1 - y)]


def _remote(src, dst, send_sem, recv_sem, to):
    return functools.partial(pltpu.make_async_remote_copy, src_ref=src, dst_ref=dst, send_sem=send_sem, recv_sem=recv_sem,
                             device_id=to, device_id_type=MESH)


def _local(src, dst, sem):
    return functools.partial(pltpu.make_async_copy, src, dst, sem)


def _job_gather_ici(xs, layer):
    na = len(xs)

    def copies(in_refs, out_refs, sems):
        send, recv, loc = sems
        x, y, c = lax.axis_index("x"), lax.axis_index("y"), lax.axis_index("c")
        me = 4 * x + 2 * y + c
        local, sends, arrivals = [], [], []
        for a in range(na):
            src = in_refs[a].at[layer]
            local.append(_local(src, out_refs[a].at[me], loc.at[a]))
            for j, (px, py) in enumerate(_other_chips()):
                sends.append(_remote(src, out_refs[a].at[me], send.at[a, j], recv.at[a, j], (px, py, c)))
                arrivals.append(_remote(src, out_refs[a].at[4 * px + 2 * py + c], send.at[a, j], recv.at[a, j], (px, py, c)))
        return local, sends, arrivals

    outs = [jax.ShapeDtypeStruct((N_DEV,) + x.shape[1:], x.dtype) for x in xs]
    sems = [pltpu.SemaphoreType.DMA((na, 3)), pltpu.SemaphoreType.DMA((na, 3)), pltpu.SemaphoreType.DMA((na,))]
    return _Job(list(xs), outs, {}, sems, copies)


def _job_gather_d2d(gs):
    na = len(gs)

    def copies(in_refs, out_refs, sems):
        send, recv = sems
        x, y, c = lax.axis_index("x"), lax.axis_index("y"), lax.axis_index("c")
        sends, arrivals = [], []
        for a in range(na):
            for k in range(N_CHIP):
                mine, theirs = 2 * k + c, 2 * k + (1 - c)
                sends.append(_remote(in_refs[a].at[mine], out_refs[a].at[mine], send.at[a, k], recv.at[a, k], (x, y, 1 - c)))
                arrivals.append(_remote(in_refs[a].at[theirs], out_refs[a].at[theirs], send.at[a, k], recv.at[a, k], (x, y, 1 - c)))
        return [], sends, arrivals

    outs = [jax.ShapeDtypeStruct(g.shape, g.dtype) for g in gs]
    sems = [pltpu.SemaphoreType.DMA((na, N_CHIP)), pltpu.SemaphoreType.DMA((na, N_CHIP))]
    return _Job(list(gs), outs, {a: a for a in range(na)}, sems, copies)


def _job_scatter_d2d(gs):
    na = len(gs)

    def copies(in_refs, out_refs, sems):
        send, recv = sems
        x, y, c = lax.axis_index("x"), lax.axis_index("y"), lax.axis_index("c")
        sends, arrivals = [], []
        for a in range(na):
            for k in range(N_CHIP):
                cp = _remote(in_refs[a].at[2 * k + (1 - c)], out_refs[a].at[k], send.at[a, k], recv.at[a, k], (x, y, 1 - c))
                sends.append(cp)
                arrivals.append(cp)
        return [], sends, arrivals

    outs = [jax.ShapeDtypeStruct((N_CHIP,) + g.shape[1:], g.dtype) for g in gs]
    sems = [pltpu.SemaphoreType.DMA((na, N_CHIP)), pltpu.SemaphoreType.DMA((na, N_CHIP))]
    return _Job(list(gs), outs, {}, sems, copies)


def _job_scatter_ici(pairs, bufs, layer):
    na = len(pairs)

    def copies(in_refs, out_refs, sems):
        send, recv, loc = sems
        x, y, c = lax.axis_index("x"), lax.axis_index("y"), lax.axis_index("c")
        my_chip = 2 * x + y
        local, sends, arrivals = [], [], []
        for a in range(na):
            local.append(_local(in_refs[a].at[my_chip], out_refs[a].at[my_chip, layer], loc.at[a]))
            for j, (px, py) in enumerate(_other_chips()):
                src = in_refs[a].at[2 * px + py]
                sends.append(_remote(src, out_refs[a].at[my_chip, layer], send.at[a, j], recv.at[a, j], (px, py, c)))
                arrivals.append(_remote(src, out_refs[a].at[2 * px + py, layer], send.at[a, j], recv.at[a, j], (px, py, c)))
        return local, sends, arrivals

    outs = [jax.ShapeDtypeStruct(b.shape, b.dtype) for b in bufs]
    sems = [pltpu.SemaphoreType.DMA((na, 3)), pltpu.SemaphoreType.DMA((na, 3)), pltpu.SemaphoreType.DMA((na,))]
    return _Job(list(pairs) + list(bufs), outs, {na + a: a for a in range(na)}, sems, copies)


def _split_jobs(jobs, in_refs, out_refs, sem_refs):
    out, i0, o0, s0 = [], 0, 0, 0
    for jb in jobs:
        out.append((jb, in_refs[i0:i0 + len(jb.ins)], out_refs[o0:o0 + len(jb.outs)], sem_refs[s0:s0 + len(jb.sems)]))
        i0, o0, s0 = i0 + len(jb.ins), o0 + len(jb.outs), s0 + len(jb.sems)
    return out


def _comm_call(jobs, name):
    n_in = sum(len(jb.ins) for jb in jobs)
    n_out = sum(len(jb.outs) for jb in jobs)

    def body(*refs):
        parts = _split_jobs(jobs, refs[:n_in], refs[n_in:n_in + n_out], refs[n_in + n_out:])
        for jb, i_r, o_r, s_r in parts:
            jb.start(i_r, o_r, s_r)
        for jb, i_r, o_r, s_r in parts:
            jb.wait(i_r, o_r, s_r)

    io_alias, i0, o0 = {}, 0, 0
    for jb in jobs:
        io_alias.update({i0 + a: o0 + b for a, b in jb.alias.items()})
        i0, o0 = i0 + len(jb.ins), o0 + len(jb.outs)
    outs = pl.pallas_call(
        body, name=name, out_shape=tuple(s for jb in jobs for s in jb.outs),
        in_specs=[ANY] * n_in, out_specs=tuple([ANY] * n_out),
        scratch_shapes=[s for jb in jobs for s in jb.sems], input_output_aliases=io_alias,
    )(*[v for jb in jobs for v in jb.ins])
    res, o0 = [], 0
    for jb in jobs:
        res.append(list(outs[o0:o0 + len(jb.outs)]))
        o0 += len(jb.outs)
    return res


def _to_bf16(x2d, name):
    r, c = x2d.shape
    tr = _pick(r, (512, 256, 128, 64, 32, 16))

    def body(x_ref, o_ref):
        o_ref[...] = x_ref[...].astype(BF16)

    return pl.pallas_call(
        body, name=name, out_shape=jax.ShapeDtypeStruct((r, c), BF16), grid=(r // tr,),
        in_specs=[pl.BlockSpec((tr, c), lambda i: (i, 0))], out_specs=pl.BlockSpec((tr, c), lambda i: (i, 0)),
        compiler_params=_params(("parallel",)),
    )(x2d)


def _pair_add(own8, got4, name):
    _, r, c = own8.shape
    tr = _pick(r, (512, 256, 128, 64, 32, 16))
    core = lax.axis_index("c").astype(jnp.int32).reshape(1)

    def body(c_ref, a_ref, b_ref, o_ref):
        o_ref[...] = (a_ref[...] + b_ref[...]).astype(BF16)

    return pl.pallas_call(
        body, name=name, out_shape=jax.ShapeDtypeStruct((N_CHIP, r, c), BF16),
        grid_spec=pltpu.PrefetchScalarGridSpec(
            num_scalar_prefetch=1, grid=(N_CHIP, r // tr),
            in_specs=[pl.BlockSpec((None, tr, c), lambda k, i, cr: (2 * k + cr[0], i, 0)),
                      pl.BlockSpec((None, tr, c), lambda k, i, cr: (k, i, 0))],
            out_specs=pl.BlockSpec((None, tr, c), lambda k, i, cr: (k, i, 0))),
        compiler_params=_params(("parallel", "parallel")),
    )(core, own8, got4)


def _adamw(w, m, v, parts, name):
    r, c = w.shape
    n_parts = parts.shape[0]
    tr = _pick(r, (256, 128, 64, 32, 16, 8))
    tc = _pick(c, (1024, 512, 256, 128))

    def body(w_ref, m_ref, v_ref, p_ref, g_ref, d_ref, nm_ref, nv_ref):
        g = p_ref[0].astype(F32)
        for k in range(1, n_parts):
            g = g + p_ref[k].astype(F32)
        wv = w_ref[...]
        nm = ADAM_B1 * m_ref[...] + (1.0 - ADAM_B1) * g
        nv = ADAM_B2 * v_ref[...] + (1.0 - ADAM_B2) * (g * g)
        m_hat = nm / (1.0 - ADAM_B1 ** ADAM_STEP)
        v_hat = nv / (1.0 - ADAM_B2 ** ADAM_STEP)
        g_ref[...] = g
        d_ref[...] = -ADAM_LR * (m_hat / (jnp.sqrt(v_hat) + ADAM_EPS) + ADAM_WD * wv)
        nm_ref[...] = nm
        nv_ref[...] = nv

    tile = pl.BlockSpec((tr, tc), lambda i, j: (i, j))
    sh = jax.ShapeDtypeStruct((r, c), F32)
    return pl.pallas_call(
        body, name=name, out_shape=(sh, sh, sh, sh), grid=(r // tr, c // tc),
        in_specs=[tile, tile, tile, pl.BlockSpec((n_parts, tr, tc), lambda i, j: (0, i, j))],
        out_specs=(tile, tile, tile, tile),
        compiler_params=_params(("parallel", "parallel")),
    )(w, m, v, parts)


def _matmul(a, b, *, dims, shape, tiles, b_spec=None, out_specs=None, out_shapes=None, out_dtypes=(F32,),
            epilogue=None, extras=(), alias=None, jobs=(), name):
    m_dim, n_dim, k_dim = shape
    tm, tn, tk = tiles
    assert m_dim % tm == 0 and n_dim % tn == 0 and k_dim % tk == 0, (name, shape, tiles)
    nk = k_dim // tk
    n_extra = len(extras)
    n_alias = 0 if alias is None else 1
    n_out = len(out_dtypes)
    if dims == "tn":
        a_spec = pl.BlockSpec((tk, tm), lambda i, j, k: (k, i))
        contract = (((0,), (0,)), ((), ()))
    else:
        a_spec = pl.BlockSpec((tm, tk), lambda i, j, k: (i, k))
        contract = (((1,), (1,)), ((), ())) if dims == "nt" else (((1,), (0,)), ((), ()))
    if b_spec is None:
        b_spec = (pl.BlockSpec((tn, tk), lambda i, j, k: (j, k)) if dims == "nt"
                  else pl.BlockSpec((tk, tn), lambda i, j, k: (k, j)))
    if out_specs is None:
        out_specs = tuple(pl.BlockSpec((tm, tn), lambda i, j, k: (i, j)) for _ in range(n_out))
    if out_shapes is None:
        out_shapes = tuple(jax.ShapeDtypeStruct((m_dim, n_dim), d) for d in out_dtypes)

    n_local_in = 2 + n_extra + n_alias
    n_job_in = sum(len(jb.ins) for jb in jobs)
    n_job_out = sum(len(jb.outs) for jb in jobs)
    n_acc = 1 if nk > 1 else 0
    grid = (m_dim // tm, n_dim // tn, nk)

    def body(*refs):
        a_ref, b_ref = refs[0], refs[1]
        extra_refs = refs[2:2 + n_extra]
        o0 = n_local_in + n_job_in
        out_refs = refs[o0:o0 + n_out]
        s0 = o0 + n_out + n_job_out
        job_parts = _split_jobs(jobs, refs[n_local_in:o0], refs[o0 + n_out:s0], refs[s0 + n_acc:])
        i, j, k = pl.program_id(0), pl.program_id(1), pl.program_id(2)

        if jobs:
            @pl.when((i == 0) & (j == 0) & (k == 0))
            def _():
                for jb, i_r, o_r, s_r in job_parts:
                    jb.start(i_r, o_r, s_r)

        def finish(acc):
            if epilogue is None:
                out_refs[0][...] = acc.astype(out_refs[0].dtype)
            else:
                epilogue(acc, i, j, extra_refs, out_refs)

        prod = lax.dot_general(a_ref[...].astype(BF16), b_ref[...].astype(BF16), contract, preferred_element_type=F32)
        if nk == 1:
            finish(prod)
        else:
            acc_ref = refs[s0]

            @pl.when(k == 0)
            def _():
                acc_ref[...] = prod

            @pl.when((k > 0) & (k < nk - 1))
            def _():
                acc_ref[...] += prod

            @pl.when(k == nk - 1)
            def _():
                finish(acc_ref[...] + prod)

        if jobs:
            @pl.when((i == grid[0] - 1) & (j == grid[1] - 1) & (k == nk - 1))
            def _():
                for jb, i_r, o_r, s_r in job_parts:
                    jb.wait(i_r, o_r, s_r)

    ins = [a, b] + [e[0] for e in extras]
    in_specs = [a_spec, b_spec] + [e[1] for e in extras]
    io_alias = {}
    if alias is not None:
        ins.append(alias)
        in_specs.append(ANY)
        io_alias = {len(ins) - 1: 0}
    all_out_shapes, all_out_specs = list(out_shapes), list(out_specs)
    for jb in jobs:
        io_alias.update({len(ins) + a_: len(all_out_shapes) + b_ for a_, b_ in jb.alias.items()})
        ins += jb.ins
        in_specs += [ANY] * len(jb.ins)
        all_out_shapes += jb.outs
        all_out_specs += [ANY] * len(jb.outs)
    scratch = ([pltpu.VMEM((tm, tn), F32)] if nk > 1 else []) + [s for jb in jobs for s in jb.sems]
    outs = pl.pallas_call(
        body, name=name, out_shape=tuple(all_out_shapes), grid=grid,
        in_specs=in_specs, out_specs=tuple(all_out_specs), scratch_shapes=scratch,
        input_output_aliases=io_alias,
        compiler_params=_params(("arbitrary",) * 3 if jobs else ("parallel", "parallel", "arbitrary")),
    )(*ins)
    main = outs[0] if n_out == 1 else tuple(outs[:n_out])
    if not jobs:
        return main
    job_outs, o0 = [], n_out
    for jb in jobs:
        job_outs.append(list(outs[o0:o0 + len(jb.outs)]))
        o0 += len(jb.outs)
    return main, job_outs


def _lead(layer, block, index):
    if layer is None:
        return pl.BlockSpec(block, index)
    return pl.BlockSpec((None,) + block, lambda i, j, k: (layer,) + index(i, j, k))


def _w_spec(kind, layer, per, dims, tn, tk):
    if kind == "nat":
        if dims == "nn":
            return _lead(layer, (tk, tn), lambda i, j, k: (k, j))
        return _lead(layer, (tn, tk), lambda i, j, k: (j, k))
    if dims == "nn":
        q = per // tn
        return _lead(layer, (None, tk, tn), lambda i, j, k: (j // q, k, j % q))
    q = per // tk
    return _lead(layer, (None, tn, tk), lambda i, j, k: (k // q, j, k % q))


def _g_spec(kind, layer, per, tm, tn):
    if kind == "nat":
        return _lead(layer, (tm, tn), lambda i, j, k: (i, j))
    q = per // tn
    return _lead(layer, (None, tm, tn), lambda i, j, k: (j // q, i, j % q))


TOK = (1408, 768, 256, 128)
FEAT = (1024, 512, 256, 128)
KDIM = (2048, 1536, 1024, 512, 256, 128)


def _sel(is_ctx, ref):
    return jnp.where(is_ctx, ref[1:2, :], ref[0:1, :])


def _norm_mod(t, gain, mod8, shift_k, n_ctx, name):
    n, d = t.shape
    tr = _pick(n, (256, 128))

    def body(t_ref, g_ref, sh_ref, sc_ref, o_ref):
        x = t_ref[...]
        r = lax.rsqrt(_rowmean(x * x) + EPS)
        y = (x * r) * g_ref[...]
        is_ctx = _row_ids(pl.program_id(0), tr) < n_ctx
        o_ref[...] = (y * (1.0 + _sel(is_ctx, sc_ref)) + _sel(is_ctx, sh_ref)).astype(BF16)

    return pl.pallas_call(
        body, name=name, out_shape=jax.ShapeDtypeStruct((n, d), BF16), grid=(n // tr,),
        in_specs=[pl.BlockSpec((tr, d), lambda i: (i, 0)), pl.BlockSpec((1, d), lambda i: (0, 0)),
                  pl.BlockSpec((8, d), lambda i: (0, shift_k)), pl.BlockSpec((8, d), lambda i: (0, shift_k + 1))],
        out_specs=pl.BlockSpec((tr, d), lambda i: (i, 0)),
        compiler_params=_params(("parallel",)),
    )(t, gain, mod8, mod8)


def _norm_mod_bwd(dh, t, d_res, gain, mod8, shift_k, n_ctx, name):
    n, d = t.shape
    tr = _pick(n, (256, 128))

    def body(dh_ref, t_ref, dr_ref, g_ref, sc_ref, dt_ref, dss_ref, dg_ref):
        i = pl.program_id(0)

        @pl.when(i == 0)
        def _():
            dss_ref[...] = jnp.zeros_like(dss_ref)
            dg_ref[...] = jnp.zeros_like(dg_ref)

        x = t_ref[...]
        r = lax.rsqrt(_rowmean(x * x) + EPS)
        xn = x * r
        g = g_ref[...]
        y = xn * g
        dhv = dh_ref[...]
        is_ctx = _row_ids(i, tr) < n_ctx
        zero = jnp.zeros_like(dhv)
        dhy = dhv * y
        dss_ref[0:1, 0:d] += _colsum(jnp.where(is_ctx, zero, dhv))
        dss_ref[1:2, 0:d] += _colsum(jnp.where(is_ctx, dhv, zero))
        dss_ref[0:1, d:2 * d] += _colsum(jnp.where(is_ctx, zero, dhy))
        dss_ref[1:2, d:2 * d] += _colsum(jnp.where(is_ctx, dhy, zero))
        dy = dhv * (1.0 + _sel(is_ctx, sc_ref))
        dg_ref[0:1, :] += _colsum(dy * xn)
        dxn = dy * g
        dt_ref[...] = dr_ref[...] + r * (dxn - xn * _rowmean(dxn * xn))

    row = pl.BlockSpec((tr, d), lambda i: (i, 0))
    return pl.pallas_call(
        body, name=name,
        out_shape=(jax.ShapeDtypeStruct((n, d), F32), jax.ShapeDtypeStruct((8, 2 * d), F32), jax.ShapeDtypeStruct((8, d), F32)),
        grid=(n // tr,),
        in_specs=[row, row, row, pl.BlockSpec((1, d), lambda i: (0, 0)), pl.BlockSpec((8, d), lambda i: (0, shift_k + 1))],
        out_specs=(row, pl.BlockSpec((8, 2 * d), lambda i: (0, 0)), pl.BlockSpec((8, d), lambda i: (0, 0))),
        compiler_params=_params(("arbitrary",)),
    )(dh, t, d_res, gain, mod8)


def _gate_bwd(d_t, branch, mod8, gate_k, n_ctx, name):
    n, d = d_t.shape
    tr = _pick(n, (256, 128))

    def body(dt_ref, o_ref, gt_ref, dob_ref, dgate_ref):
        i = pl.program_id(0)

        @pl.when(i == 0)
        def _():
            dgate_ref[...] = jnp.zeros_like(dgate_ref)

        dv = dt_ref[...]
        is_ctx = _row_ids(i, tr) < n_ctx
        dob_ref[...] = (dv * _sel(is_ctx, gt_ref)).astype(BF16)
        prod = dv * o_ref[...].astype(F32)
        zero = jnp.zeros_like(prod)
        dgate_ref[0:1, :] += _colsum(jnp.where(is_ctx, zero, prod))
        dgate_ref[1:2, :] += _colsum(jnp.where(is_ctx, prod, zero))

    row = pl.BlockSpec((tr, d), lambda i: (i, 0))
    return pl.pallas_call(
        body, name=name, out_shape=(jax.ShapeDtypeStruct((n, d), BF16), jax.ShapeDtypeStruct((8, d), F32)),
        grid=(n // tr,),
        in_specs=[row, row, pl.BlockSpec((8, d), lambda i: (0, gate_k))],
        out_specs=(row, pl.BlockSpec((8, d), lambda i: (0, 0))),
        compiler_params=_params(("arbitrary",)),
    )(d_t, branch, mod8)


def _swap16(v):
    w = v.shape[1]
    lane = lax.broadcasted_iota(jnp.int32, v.shape, 1)
    return jnp.where((lane % 32) < 16, pltpu.roll(v, w - 16, 1), pltpu.roll(v, 16, 1))


def _rope(v, cs, sn, sign):
    reps = v.shape[1] // 128
    c = jnp.tile(cs, (1, reps)) if reps > 1 else cs
    s = jnp.tile(sn, (1, reps)) if reps > 1 else sn
    return v * c + sign * (_swap16(v) * s)


def _rope_split(p, cs, sn, dc, hd, kd, name):
    n, d_in = p.shape
    tr = _pick(n, (256, 128))
    q0 = 3 * dc

    def body(p_ref, cs_ref, sn_ref, o_ref):
        cs_v, sn_v = cs_ref[...], sn_ref[...]
        q = _rope(p_ref[:, q0:q0 + hd], cs_v, sn_v, 1.0) * SCALE
        k = _rope(p_ref[:, q0 + hd:q0 + hd + kd], cs_v, sn_v, 1.0)
        v = p_ref[:, q0 + hd + kd:q0 + hd + 2 * kd]
        o_ref[...] = jnp.concatenate([q, k, v], axis=1).astype(BF16)

    return pl.pallas_call(
        body, name=name, out_shape=jax.ShapeDtypeStruct((n, hd + 2 * kd), BF16), grid=(n // tr,),
        in_specs=[pl.BlockSpec((tr, d_in), lambda i: (i, 0)), pl.BlockSpec((tr, 128), lambda i: (i, 0)),
                  pl.BlockSpec((tr, 128), lambda i: (i, 0))],
        out_specs=pl.BlockSpec((tr, hd + 2 * kd), lambda i: (i, 0)),
        compiler_params=_params(("parallel",)),
    )(p, cs, sn)


def _attn_specs(nb, n_ctx, hd, kd):
    kci = hd // kd
    specs = [pl.BlockSpec((BLOCK, hd), lambda b: (b, 0)),
             pl.BlockSpec((n_ctx, kd), lambda b: (0, kci)), pl.BlockSpec((n_ctx, kd), lambda b: (0, kci + 1))]
    for col in (kci, kci + 1):
        specs.append(pl.BlockSpec((BLOCK, kd), lambda b, col=col: (jnp.maximum(b - 1, 0), col)))
        specs.append(pl.BlockSpec((BLOCK, kd), lambda b, col=col: (b, col)))
        specs.append(pl.BlockSpec((BLOCK, kd), lambda b, col=col: (jnp.minimum(b + 1, nb - 1), col)))
    return specs


def _band_valid(b, group, n_ctx, n):
    q_pos = b * BLOCK + lax.broadcasted_iota(jnp.int32, (group * BLOCK, 1), 0) % BLOCK
    k_pos = (b - 1) * BLOCK + lax.broadcasted_iota(jnp.int32, (1, 3 * BLOCK), 1)
    return (jnp.abs(k_pos - q_pos) <= BLOCK) & (k_pos >= n_ctx) & (k_pos < n) & (q_pos >= n_ctx)


NT = (((1,), (1,)), ((), ()))
NN = (((1,), (0,)), ((), ()))
TN = (((0,), (0,)), ((), ()))


def _dot(a, b, dn):
    return lax.dot_general(a, b, dn, preferred_element_type=F32)


def _softmax_parts(qg, kc, kl, valid, snk):
    s_c = _dot(qg, kc, NT)
    s_l = jnp.where(valid, _dot(qg, kl, NT), NEG_INF)
    m = jnp.maximum(jnp.maximum(jnp.max(s_c, axis=1, keepdims=True), jnp.max(s_l, axis=1, keepdims=True)), snk)
    e_c, e_l, e_s = jnp.exp(s_c - m), jnp.exp(s_l - m), jnp.exp(snk - m)
    den = jnp.sum(e_c, axis=1, keepdims=True) + jnp.sum(e_l, axis=1, keepdims=True) + e_s
    return e_c / den, e_l / den, e_s / den


def _attention(qkv, sink, n_ctx, hd, kd, name):
    n = qkv.shape[0]
    nb = n // BLOCK
    n_kv = kd // HEAD_DIM
    group = hd // kd

    def body(q_ref, kc_ref, vc_ref, k0, k1, k2, v0, v1, v2, sink_ref, o_ref):
        b = pl.program_id(0)
        valid = _band_valid(b, group, n_ctx, n)
        heads = []
        for h in range(n_kv):
            hs = slice(h * HEAD_DIM, (h + 1) * HEAD_DIM)
            qg = jnp.concatenate([q_ref[:, (h * group + j) * HEAD_DIM:(h * group + j + 1) * HEAD_DIM] for j in range(group)], axis=0)
            kl = jnp.concatenate([k0[:, hs], k1[:, hs], k2[:, hs]], axis=0)
            vl = jnp.concatenate([v0[:, hs], v1[:, hs], v2[:, hs]], axis=0)
            snk = jnp.concatenate([jnp.full((BLOCK, 1), sink_ref[h * group + j], F32) for j in range(group)], axis=0)
            p_c, p_l, _ = _softmax_parts(qg, kc_ref[:, hs], kl, valid, snk)
            o = _dot(p_c.astype(BF16), vc_ref[:, hs], NN) + _dot(p_l.astype(BF16), vl, NN)
            heads += [o[j * BLOCK:(j + 1) * BLOCK, :] for j in range(group)]
        o_ref[...] = jnp.concatenate(heads, axis=1)

    return pl.pallas_call(
        body, name=name, out_shape=jax.ShapeDtypeStruct((n, hd), F32), grid=(nb,),
        in_specs=_attn_specs(nb, n_ctx, hd, kd) + [pl.BlockSpec(memory_space=pltpu.SMEM)],
        out_specs=pl.BlockSpec((BLOCK, hd), lambda b: (b, 0)),
        compiler_params=_params(("parallel",)),
    )(qkv, qkv, qkv, qkv, qkv, qkv, qkv, qkv, qkv, sink)


def _attention_bwd(qkv, sink, ao, d_mg, cpar, n_ctx, hd, kd, name):
    n = qkv.shape[0]
    nb = n // BLOCK
    n_kv = kd // HEAD_DIM
    group = hd // kd
    n_heads = n_kv * group

    def body(q_ref, kc_ref, vc_ref, k0, k1, k2, v0, v1, v2, sink_ref, ao_ref, dmg_ref, cp_ref,
             dq_ref, part_ref, dctx_ref, dsink_ref, dgain_ref):
        b = pl.program_id(0)

        @pl.when(b == 0)
        def _():
            dctx_ref[...] = jnp.zeros_like(dctx_ref)
            dsink_ref[...] = jnp.zeros_like(dsink_ref)
            dgain_ref[...] = jnp.zeros_like(dgain_ref)

        ao_v = ao_ref[...]
        ra = lax.rsqrt(_rowmean(ao_v * ao_v) + EPS)
        an = ao_v * ra
        dmg = dmg_ref[...]
        dgain_ref[0:1, :] += _colsum(dmg * an)
        d_an = dmg * cp_ref[5:6, :]
        d_ao = (ra * (d_an - an * _rowmean(d_an * an))).astype(BF16)

        valid = _band_valid(b, group, n_ctx, n)
        lane = lax.broadcasted_iota(jnp.int32, (1, 128), 1)
        dsink_row = jnp.zeros((1, 128), F32)
        dq_heads, dkc, dvc, dkl, dvl = [], [], [], [], []
        for h in range(n_kv):
            hs = slice(h * HEAD_DIM, (h + 1) * HEAD_DIM)
            cols = [slice((h * group + j) * HEAD_DIM, (h * group + j + 1) * HEAD_DIM) for j in range(group)]
            qg = jnp.concatenate([q_ref[:, cs] for cs in cols], axis=0)
            dog = jnp.concatenate([d_ao[:, cs] for cs in cols], axis=0)
            kc, vc = kc_ref[:, hs], vc_ref[:, hs]
            kl = jnp.concatenate([k0[:, hs], k1[:, hs], k2[:, hs]], axis=0)
            vl = jnp.concatenate([v0[:, hs], v1[:, hs], v2[:, hs]], axis=0)
            snk = jnp.concatenate([jnp.full((BLOCK, 1), sink_ref[h * group + j], F32) for j in range(group)], axis=0)
            p_c, p_l, p_s = _softmax_parts(qg, kc, kl, valid, snk)
            dp_c = _dot(dog, vc, NT)
            dp_l = _dot(dog, vl, NT)
            delta = jnp.sum(p_c * dp_c, axis=1, keepdims=True) + jnp.sum(p_l * dp_l, axis=1, keepdims=True)
            ds_c = (p_c * (dp_c - delta)).astype(BF16)
            ds_l = (p_l * (dp_l - delta)).astype(BF16)
            psd = p_s * delta
            for j in range(group):
                val = -jnp.sum(psd[j * BLOCK:(j + 1) * BLOCK, :], axis=0, keepdims=True)
                dsink_row = dsink_row + jnp.where(lane == h * group + j, val, 0.0)
            dq = (_dot(ds_c, kc, NN) + _dot(ds_l, kl, NN)) * SCALE
            dq_heads += [dq[j * BLOCK:(j + 1) * BLOCK, :] for j in range(group)]
            dkc.append(_dot(ds_c, qg, TN))
            dvc.append(_dot(p_c.astype(BF16), dog, TN))
            dkl.append(_dot(ds_l, qg, TN))
            dvl.append(_dot(p_l.astype(BF16), dog, TN))
        dq_ref[...] = jnp.concatenate(dq_heads, axis=1)
        dctx_ref[...] += jnp.concatenate(dkc + dvc, axis=1)
        loc = jnp.concatenate(dkl + dvl, axis=1)
        for j in range(3):
            part_ref[j] = loc[j * BLOCK:(j + 1) * BLOCK, :]
        dsink_ref[0:1, :] += dsink_row

    assert n_heads <= 128
    out_shape = (jax.ShapeDtypeStruct((n, hd), F32), jax.ShapeDtypeStruct((nb, 3, BLOCK, 2 * kd), F32),
                 jax.ShapeDtypeStruct((n_ctx, 2 * kd), F32), jax.ShapeDtypeStruct((8, 128), F32),
                 jax.ShapeDtypeStruct((8, hd), F32))
    return pl.pallas_call(
        body, name=name, out_shape=out_shape, grid=(nb,),
        in_specs=_attn_specs(nb, n_ctx, hd, kd) + [
            pl.BlockSpec(memory_space=pltpu.SMEM), pl.BlockSpec((BLOCK, hd), lambda b: (b, 0)),
            pl.BlockSpec((BLOCK, hd), lambda b: (b, 1)), pl.BlockSpec((8, hd), lambda b: (0, 0))],
        out_specs=(pl.BlockSpec((BLOCK, hd), lambda b: (b, 0)),
                   pl.BlockSpec((None, 3, BLOCK, 2 * kd), lambda b: (b, 0, 0, 0)),
                   pl.BlockSpec((n_ctx, 2 * kd), lambda b: (0, 0)), pl.BlockSpec((8, 128), lambda b: (0, 0)),
                   pl.BlockSpec((8, hd), lambda b: (0, 0))),
        compiler_params=_params(("arbitrary",)),
    )(qkv, qkv, qkv, qkv, qkv, qkv, qkv, qkv, qkv, sink, ao, d_mg, cpar)


def _halo_specs(tr, n, width, col=0):
    q = tr // 8
    return [pl.BlockSpec((8, width), lambda i: (jnp.maximum(i * q - 1, 0), col)),
            pl.BlockSpec((8, width), lambda i: (jnp.minimum((i + 1) * q, n // 8 - 1), col))]


def _mix_fwd(p, ao, cpar, n_ctx, dc, name):
    n, d_in = p.shape
    tr = _pick(n, (256, 128))

    def body(p_ref, pp_ref, pn_ref, ao_ref, cp_ref, o_ref):
        i = pl.program_id(0)
        bg = p_ref[:, 0:dc]
        u = p_ref[:, dc:2 * dc] * p_ref[:, 2 * dc:3 * dc]
        u_before = pp_ref[7:8, dc:2 * dc] * pp_ref[7:8, 2 * dc:3 * dc]
        u_after = pn_ref[0:1, dc:2 * dc] * pn_ref[0:1, 2 * dc:3 * dc]
        loc = lax.broadcasted_iota(jnp.int32, (tr, 1), 0)
        gid = i * tr + loc
        has_prev = (gid != 0) & (gid != n_ctx)
        has_next = (gid != n_ctx - 1) & (gid != n - 1)
        u_m1 = jnp.where(has_prev, jnp.where(loc == 0, u_before, pltpu.roll(u, 1, 0)), 0.0)
        u_p1 = jnp.where(has_next, jnp.where(loc == tr - 1, u_after, pltpu.roll(u, tr - 1, 0)), 0.0)
        cv = u_m1 * cp_ref[0:1, :] + u * cp_ref[1:2, :] + u_p1 * cp_ref[2:3, :] + cp_ref[3:4, :]
        co = bg * cv
        nc = (co * lax.rsqrt(_rowmean(co * co) + EPS)) * cp_ref[4:5, :]
        ao_v = ao_ref[...]
        na = (ao_v * lax.rsqrt(_rowmean(ao_v * ao_v) + EPS)) * cp_ref[5:6, :]
        o_ref[...] = jnp.concatenate([nc, na], axis=1).astype(BF16)

    return pl.pallas_call(
        body, name=name, out_shape=jax.ShapeDtypeStruct((n, 2 * dc), BF16), grid=(n // tr,),
        in_specs=[pl.BlockSpec((tr, d_in), lambda i: (i, 0))] + _halo_specs(tr, n, d_in)
        + [pl.BlockSpec((tr, dc), lambda i: (i, 0)), pl.BlockSpec((8, dc), lambda i: (0, 0))],
        out_specs=pl.BlockSpec((tr, 2 * dc), lambda i: (i, 0)),
        compiler_params=_params(("parallel",)),
    )(p, p, p, ao, cpar)


def _mix_bwd(d_mg, p, cpar, d_q, parts, d_ctx, cs, sn, n_ctx, dc, hd, kd, name):
    n, d_in = p.shape
    tr = BLOCK
    nb = n // tr
    ext = tr + 16
    n_ctx_blocks = n_ctx // BLOCK

    def body(dm_ref, dmp_ref, dmn_ref, p_ref, pp_ref, pn_ref, cp_ref, dq_ref, pa_ref, pb_ref, pc_ref, dctx_ref,
             cs_ref, sn_ref, dp_ref, acc_ref):
        i = pl.program_id(0)

        @pl.when(i == 0)
        def _():
            acc_ref[...] = jnp.zeros_like(acc_ref)

        def cat(before, here, after):
            return jnp.concatenate([before, here, after], axis=0)

        bg = cat(pp_ref[:, 0:dc], p_ref[:, 0:dc], pn_ref[:, 0:dc])
        cg = cat(pp_ref[:, dc:2 * dc], p_ref[:, dc:2 * dc], pn_ref[:, dc:2 * dc])
        hh = cat(pp_ref[:, 2 * dc:3 * dc], p_ref[:, 2 * dc:3 * dc], pn_ref[:, 2 * dc:3 * dc])
        dme = cat(dmp_ref[...], dm_ref[...], dmn_ref[...])
        gid = i * tr - 8 + lax.broadcasted_iota(jnp.int32, (ext, 1), 0)
        inside = (gid >= 0) & (gid < n)
        has_prev = inside & (gid != 0) & (gid != n_ctx)
        has_next = inside & (gid != n_ctx - 1) & (gid != n - 1)
        w0, w1, w2, bias, gain = cp_ref[0:1, :], cp_ref[1:2, :], cp_ref[2:3, :], cp_ref[3:4, :], cp_ref[4:5, :]
        u = jnp.where(inside, cg * hh, 0.0)
        u_m1 = jnp.where(has_prev, pltpu.roll(u, 1, 0), 0.0)
        u_p1 = jnp.where(has_next, pltpu.roll(u, ext - 1, 0), 0.0)
        cv = u_m1 * w0 + u * w1 + u_p1 * w2 + bias
        co = bg * cv
        rc = lax.rsqrt(_rowmean(co * co) + EPS)
        cn = co * rc
        d_cn = dme * gain
        d_co = rc * (d_cn - cn * _rowmean(d_cn * cn))
        d_cv = jnp.where(inside, d_co * bg, 0.0)
        d_bg = d_co * cv
        d_cv_p1 = jnp.where(has_next, pltpu.roll(d_cv, ext - 1, 0), 0.0)
        d_cv_m1 = jnp.where(has_prev, pltpu.roll(d_cv, 1, 0), 0.0)
        d_u = d_cv_p1 * w0 + d_cv * w1 + d_cv_m1 * w2
        mid = slice(8, 8 + tr)
        acc_ref[0:1, :] += _colsum((d_cv * u_m1)[mid])
        acc_ref[1:2, :] += _colsum((d_cv * u)[mid])
        acc_ref[2:3, :] += _colsum((d_cv * u_p1)[mid])
        acc_ref[3:4, :] += _colsum(d_cv[mid])
        acc_ref[4:5, :] += _colsum((dme * cn)[mid])

        d_kv = (jnp.where(i >= 1, pa_ref[...], 0.0) + pb_ref[...] + jnp.where(i + 1 < nb, pc_ref[...], 0.0))
        ctx_rows = dctx_ref[pl.ds(pl.multiple_of(jnp.minimum(i, n_ctx_blocks - 1) * BLOCK, BLOCK), BLOCK), :]
        d_kv = d_kv + jnp.where(i < n_ctx_blocks, ctx_rows, 0.0)
        cs_v, sn_v = cs_ref[...], sn_ref[...]
        d_qu = _rope(dq_ref[...], cs_v, sn_v, -1.0)
        d_ku = _rope(d_kv[:, 0:kd], cs_v, sn_v, -1.0)
        dp_ref[...] = jnp.concatenate(
            [d_bg[mid], (d_u * hh)[mid], (d_u * cg)[mid], d_qu, d_ku, d_kv[:, kd:2 * kd]], axis=1).astype(BF16)

    part = lambda sel, which: pl.BlockSpec((None, None, BLOCK, 2 * kd), lambda i: (sel(i), which, 0, 0))
    return pl.pallas_call(
        body, name=name, out_shape=(jax.ShapeDtypeStruct((n, d_in), BF16), jax.ShapeDtypeStruct((8, dc), F32)),
        grid=(nb,),
        in_specs=[pl.BlockSpec((tr, dc), lambda i: (i, 0))] + _halo_specs(tr, n, dc)
        + [pl.BlockSpec((tr, d_in), lambda i: (i, 0))] + _halo_specs(tr, n, d_in)
        + [pl.BlockSpec((8, dc), lambda i: (0, 0)), pl.BlockSpec((tr, hd), lambda i: (i, 0)),
           part(lambda i: jnp.maximum(i - 1, 0), 2), part(lambda i: i, 1), part(lambda i: jnp.minimum(i + 1, nb - 1), 0),
           pl.BlockSpec((n_ctx, 2 * kd), lambda i: (0, 0)),
           pl.BlockSpec((tr, 128), lambda i: (i, 0)), pl.BlockSpec((tr, 128), lambda i: (i, 0))],
        out_specs=(pl.BlockSpec((tr, d_in), lambda i: (i, 0)), pl.BlockSpec((8, dc), lambda i: (0, 0))),
        compiler_params=_params(("arbitrary",)),
    )(d_mg, d_mg, d_mg, p, p, p, cpar, d_q, parts, parts, parts, d_ctx, cs, sn)


def _loss_bwd(t, gain, target, n_ctx, name):
    n, d = t.shape
    tr = _pick(n_ctx, (256, 128))
    first = n_ctx // tr

    def body(t_ref, g_ref, y_ref, dt_ref, loss_ref, dg_ref):
        i = pl.program_id(0)

        @pl.when(i == 0)
        def _():
            loss_ref[...] = jnp.zeros_like(loss_ref)
            dg_ref[...] = jnp.zeros_like(dg_ref)

        @pl.when(i < first)
        def _():
            dt_ref[...] = jnp.zeros_like(dt_ref)

        @pl.when(i >= first)
        def _():
            x = t_ref[...]
            g = g_ref[...]
            r = lax.rsqrt(_rowmean(x * x) + EPS)
            xn = x * r
            err = xn * g - y_ref[...]
            loss_ref[...] += 0.5 * _colsum(_rowmean(err * err))
            dy = err * (1.0 / d)
            dg_ref[0:1, :] += _colsum(dy * xn)
            dxn = dy * g
            dt_ref[...] = r * (dxn - xn * _rowmean(dxn * xn))

    row = pl.BlockSpec((tr, d), lambda i: (i, 0))
    return pl.pallas_call(
        body, name=name,
        out_shape=(jax.ShapeDtypeStruct((n, d), F32), jax.ShapeDtypeStruct((8, 128), F32), jax.ShapeDtypeStruct((8, d), F32)),
        grid=(n // tr,),
        in_specs=[row, pl.BlockSpec((1, d), lambda i: (0, 0)), pl.BlockSpec((tr, d), lambda i: (jnp.maximum(i - first, 0), 0))],
        out_specs=(row, pl.BlockSpec((8, 128), lambda i: (0, 0)), pl.BlockSpec((8, d), lambda i: (0, 0))),
        compiler_params=_params(("arbitrary",)),
    )(t, gain, target)


def _silu16(c16, name):
    def body(c_ref, o_ref):
        v = c_ref[...]
        o_ref[...] = (v * jax.nn.sigmoid(v)).astype(BF16)

    return pl.pallas_call(body, name=name, out_shape=jax.ShapeDtypeStruct(c16.shape, BF16))(c16)


def _cctx_grad(parts, c_ctx, name):
    def body(p_ref, c_ref, o_ref):
        g = _colsum(p_ref[...])
        v = c_ref[...]
        s = jax.nn.sigmoid(v)
        o_ref[...] = g * (s * (1.0 + v * (1.0 - s)))

    return pl.pallas_call(body, name=name, out_shape=jax.ShapeDtypeStruct(c_ctx.shape, F32))(parts, c_ctx)


def _rope_tables(n_ctx, n_tok):
    half = HEAD_DIM // 4
    inv = ROPE_THETA ** (-jnp.arange(0, HEAD_DIM // 2, 2, dtype=F32) / (HEAD_DIM // 2))
    rows = n_tok // GRID_W
    row_pos = jnp.repeat(jnp.arange(rows, dtype=F32), GRID_W)
    col_pos = jnp.tile(jnp.arange(GRID_W, dtype=F32), rows)
    ang_r, ang_c = row_pos[:, None] * inv[None, :], col_pos[:, None] * inv[None, :]
    cos = jnp.concatenate([jnp.cos(ang_r), jnp.cos(ang_r), jnp.cos(ang_c), jnp.cos(ang_c)], axis=1)
    sin = jnp.concatenate([-jnp.sin(ang_r), jnp.sin(ang_r), -jnp.sin(ang_c), jnp.sin(ang_c)], axis=1)
    assert cos.shape[1] == 4 * half == HEAD_DIM
    cos = jnp.concatenate([jnp.ones((n_ctx, HEAD_DIM), F32), cos], axis=0)
    sin = jnp.concatenate([jnp.zeros((n_ctx, HEAD_DIM), F32), sin], axis=0)
    return jnp.tile(cos, (1, 2)), jnp.tile(sin, (1, 2))


def kernel(x, c, ctx, c_ctx, w_ada, b_ada, g_norm1, g_norm2, w_in, conv_w, conv_b, sink, g_out_conv, g_out_attn, w_out, w_mlp1, w_mlp2, g_final, loss_target, m_c_ctx, m_w_ada, m_b_ada, m_g_norm1, m_g_norm2, m_w_in, m_conv_w, m_conv_b, m_sink, m_g_out_conv, m_g_out_attn, m_w_out, m_w_mlp1, m_w_mlp2, m_g_final, v_c_ctx, v_w_ada, v_b_ada, v_g_norm1, v_g_norm2, v_w_in, v_conv_w, v_conv_b, v_sink, v_g_out_conv, v_g_out_attn, v_w_out, v_w_mlp1, v_w_mlp2, v_g_final):
    n_lat, d = x.shape[1], x.shape[2]
    n_ctx = ctx.shape[1]
    n = n_ctx + n_lat
    depth = w_in.shape[0]
    dc = d // 2
    hd, kd = dc, N_KV_HEADS * HEAD_DIM
    n_heads = hd // HEAD_DIM
    d_in = 3 * dc + hd + 2 * kd
    cin, c_ada, r_out, c_ff, r_ff = w_in.shape[2], w_ada.shape[2], w_out.shape[1], w_mlp1.shape[2], w_mlp2.shape[1]
    d_ff = N_DEV * c_ff
    cw = conv_w.shape[2]
    assert d_in == N_DEV * cin and n_ctx % BLOCK == 0 and n_lat % BLOCK == 0 and hd % kd == 0
    dev = 4 * lax.axis_index("x") + 2 * lax.axis_index("y") + lax.axis_index("c")

    c_all, conv_w_all = _all_gather([c, conv_w], "gather_cond")
    conv_w_full = jnp.transpose(conv_w_all, (1, 2, 0, 3)).reshape(depth, 3, dc)
    c16 = jnp.concatenate([c_all.reshape(N_DEV, d), jnp.broadcast_to(c_ctx[None, :], (8, d))], axis=0)
    sc16 = _silu16(c16, "silu_cond")

    b_ada_loc = lax.dynamic_index_in_dim(b_ada.reshape(depth, N_DEV, c_ada), dev, axis=1, keepdims=False)
    tn_ada = _pick(c_ada, FEAT)

    def add_bias(acc, i, j, extra, outs):
        outs[0][...] = acc + extra[0][...]

    mod_loc = []
    for l in range(depth):
        mod_loc.append(_matmul(
            sc16, w_ada, dims="nn", shape=(16, c_ada, d), tiles=(16, tn_ada, _pick(d, KDIM)),
            b_spec=_w_spec("nat", l, None, "nn", tn_ada, _pick(d, KDIM)), epilogue=add_bias,
            extras=[(b_ada_loc[l][None, :], pl.BlockSpec((1, tn_ada), lambda i, j, k: (0, j)))], name=f"ada_fwd{l}"))
    (mod_all,) = _all_gather([jnp.stack(mod_loc)], "gather_mod")
    mod_full = jnp.transpose(mod_all, (1, 2, 0, 3)).reshape(depth, 16, N_MOD * d)
    mod_mine = lax.dynamic_index_in_dim(mod_full, dev, axis=1, keepdims=True)
    mod8 = jnp.concatenate([mod_mine, mod_full[:, 8:9], jnp.zeros((depth, 6, N_MOD * d), F32)], axis=1)

    w_in_b = _to_bf16(w_in.reshape(depth * d, cin), "cast_w_in").reshape(depth, d, cin)
    w_out_b = _to_bf16(w_out.reshape(depth * r_out, d), "cast_w_out").reshape(depth, r_out, d)
    w1_b = _to_bf16(w_mlp1.reshape(depth * d, c_ff), "cast_w_mlp1").reshape(depth, d, c_ff)
    w2_b = _to_bf16(w_mlp2.reshape(depth * r_ff, d), "cast_w_mlp2").reshape(depth, r_ff, d)
    (gath0,) = _comm_call([_job_gather_ici([w_in_b, w_out_b, w1_b, w2_b], 0)], "gather_w0_ici")
    (gath0,) = _comm_call([_job_gather_d2d(gath0)], "gather_w0_d2d")

    def full_in(g):
        return jnp.transpose(g, (1, 0, 2)).reshape(d, d_in)

    w_in_full, w_out_full = [full_in(gath0[0])], [gath0[1].reshape(d, d)]
    g_w1, w2_full = [gath0[2]], [gath0[3].reshape(d_ff, d)]
    pend_w2 = None

    t = jnp.concatenate([ctx[0], x[0]], axis=0)
    cs, sn = _rope_tables(n_ctx, n_lat)
    tm = _pick(n, TOK)
    tk_d = _pick(d, KDIM)
    tm_res = _pick(n, (1056, 768, 256, 128))
    tn_in = _pick(d_in, (768, 512, 256, 128))

    def resid_epilogue(tile_rows):
        def epi(acc, i, j, extra, outs):
            is_ctx = _row_ids(i, tile_rows) < n_ctx
            outs[0][...] = extra[0][...] + _sel(is_ctx, extra[1]) * acc
            outs[1][...] = acc.astype(BF16)
        return epi

    def sq_relu_epilogue(acc, i, j, extra, outs):
        outs[0][...] = acc.astype(BF16)
        rl = jnp.maximum(acc, 0.0)
        outs[1][...] = (rl * rl).astype(BF16)

    def d_sq_relu_epilogue(acc, i, j, extra, outs):
        outs[0][...] = (acc * (2.0 * jnp.maximum(extra[0][...].astype(F32), 0.0))).astype(BF16)

    saved = []
    for l in range(depth):
        cpar = jnp.concatenate([conv_w_full[l], conv_b[l][None], g_out_conv[l][None], g_out_attn[l][None],
                                jnp.zeros((2, dc), F32)], axis=0)
        more = l + 1 < depth
        h = _norm_mod(t, g_norm1[l][None], mod8[l], 0, n_ctx, f"norm1_{l}")
        jobs = ([_job_gather_d2d(pend_w2)] if pend_w2 is not None else []) + (
            [_job_gather_ici([w_in_b, w_out_b], l + 1)] if more else [])
        p = _matmul(h, w_in_full[l], dims="nn", shape=(n, d_in, d), tiles=(tm, tn_in, tk_d), jobs=jobs, name=f"in_proj{l}")
        if jobs:
            p, job_outs = p
            if pend_w2 is not None:
                w2_full.append(job_outs[0][0].reshape(d_ff, d))
            pend_io = job_outs[-1] if more else None
        qkv = _rope_split(p, cs, sn, dc, hd, kd, f"rope{l}")
        ao = _attention(qkv, sink[l], n_ctx, hd, kd, f"attn{l}")
        mg = _mix_fwd(p, ao, cpar, n_ctx, dc, f"mix{l}")
        tn = _pick(d, (512, 256, 128))
        res = _matmul(
            mg, w_out_full[l], dims="nn", shape=(n, d, d), tiles=(tm, tn, tk_d),
            out_dtypes=(F32, BF16), epilogue=resid_epilogue(tm),
            extras=[(t, pl.BlockSpec((tm, tn), lambda i, j, k: (i, j))),
                    (mod8[l], pl.BlockSpec((8, tn), lambda i, j, k, tn=tn: (0, 2 * (d // tn) + j)))],
            jobs=[_job_gather_d2d(pend_io)] if more else [], name=f"out_proj{l}")
        if more:
            (t2, z), ((g_in_next, g_out_next),) = res
            w_in_full.append(full_in(g_in_next))
            w_out_full.append(g_out_next.reshape(d, d))
        else:
            t2, z = res
        h2 = _norm_mod(t2, g_norm2[l][None], mod8[l], 3, n_ctx, f"norm2_{l}")
        tn = _pick(c_ff, FEAT)
        res = _matmul(h2, g_w1[l], dims="nn", shape=(n, d_ff, d), tiles=(tm, tn, tk_d),
                      b_spec=_w_spec("cols", None, c_ff, "nn", tn, tk_d), out_dtypes=(BF16, BF16),
                      epilogue=sq_relu_epilogue, jobs=[_job_gather_ici([w1_b], l + 1)] if more else [], name=f"mlp_up{l}")
        if more:
            (a, s), ((pend_w1,),) = res
        else:
            a, s = res
        tn = _pick(d, FEAT)
        tk = _pick(d_ff, (1024, 512, 256, 128))
        res = _matmul(
            s, w2_full[l], dims="nn", shape=(n, d, d_ff), tiles=(tm_res, tn, tk),
            out_dtypes=(F32, BF16), epilogue=resid_epilogue(tm_res),
            extras=[(t2, pl.BlockSpec((tm_res, tn), lambda i, j, k: (i, j))),
                    (mod8[l], pl.BlockSpec((8, tn), lambda i, j, k, tn=tn: (0, 5 * (d // tn) + j)))],
            jobs=[_job_gather_ici([w2_b], l + 1), _job_gather_d2d([pend_w1])] if more else [], name=f"mlp_down{l}")
        if more:
            (t3, o), (pend_w2, (g_w1_next,)) = res
            g_w1.append(g_w1_next)
        else:
            (t3, o), pend_w2 = res, None
        saved.append((t, h, p, qkv, ao, mg, z, t2, h2, a, s, o, cpar))
        t = t3

    d_t, loss_tile, dg_final = _loss_bwd(t, g_final[None], loss_target[0], n_ctx, "loss")
    loss = lax.psum(loss_tile[0, 0], ("x", "y", "c"))

    buf_in = lax.empty((N_CHIP, depth, d, cin), BF16)
    buf_out = lax.empty((N_CHIP, depth, r_out, d), BF16)
    buf_w1 = lax.empty((N_CHIP, depth, d, c_ff), BF16)
    buf_w2 = lax.empty((N_CHIP, depth, r_ff, d), BF16)
    pend_in, pend_layer = None, None
    tkn = _pick(n, TOK)
    small = [None] * depth
    for l in reversed(range(depth)):
        t_in, h, p, qkv, ao, mg, z, t2, h2, a, s, o, cpar = saved[l]
        dob, dgate2 = _gate_bwd(d_t, o, mod8[l], 5, n_ctx, f"gate2_bwd{l}")
        tm_g = _pick(d_ff, FEAT)
        tn = _pick(d, FEAT)
        gw2 = _matmul(s, dob, dims="tn", shape=(d_ff, d, n), tiles=(tm_g, tn, tkn), name=f"mlp_down_dw{l}")
        gw2 = gw2.reshape(N_DEV, r_ff, d)
        tn = _pick(d_ff, FEAT)
        jobs = [_job_scatter_d2d([gw2])] + ([_job_scatter_ici([pend_in], [buf_in], pend_layer)] if pend_in is not None else [])
        da, job_outs = _matmul(dob, w2_full[l], dims="nt", shape=(n, d_ff, d), tiles=(tm, tn, tk_d),
                               out_dtypes=(BF16,), epilogue=d_sq_relu_epilogue,
                               extras=[(a, pl.BlockSpec((tm, tn), lambda i, j, k: (i, j)))], jobs=jobs, name=f"mlp_down_dx{l}")
        if pend_in is not None:
            (buf_in,) = job_outs[1]
        pair_w2 = _pair_add(gw2, job_outs[0][0], f"pair_w2_{l}")
        tm_g = _pick(d, FEAT)
        tn = _pick(c_ff, FEAT)
        gw1 = _matmul(h2, da, dims="tn", shape=(d, d_ff, n), tiles=(tm_g, tn, tkn),
                      out_specs=(_g_spec("cols", None, c_ff, tm_g, tn),),
                      out_shapes=(jax.ShapeDtypeStruct((N_DEV, d, c_ff), F32),), name=f"mlp_up_dw{l}")
        tn = _pick(d, FEAT)
        tk = _pick(c_ff, (1024, 512, 256, 128))
        dh2, job_outs = _matmul(da, g_w1[l], dims="nt", shape=(n, d, d_ff), tiles=(tm, tn, tk),
                                b_spec=_w_spec("cols", None, c_ff, "nt", tn, tk),
                                jobs=[_job_scatter_d2d([gw1]), _job_scatter_ici([pair_w2], [buf_w2], l)], name=f"mlp_up_dx{l}")
        (buf_w2,) = job_outs[1]
        pair_w1 = _pair_add(gw1, job_outs[0][0], f"pair_w1_{l}")
        d_t2, dss2, dgn2 = _norm_mod_bwd(dh2, t2, d_t, g_norm2[l][None], mod8[l], 3, n_ctx, f"norm2_bwd{l}")
        dzb, dgate1 = _gate_bwd(d_t2, z, mod8[l], 2, n_ctx, f"gate1_bwd{l}")
        tm_g = _pick(d, FEAT)
        tn = _pick(d, FEAT)
        gout = _matmul(mg, dzb, dims="tn", shape=(d, d, n), tiles=(tm_g, tn, tkn), name=f"out_proj_dw{l}")
        gout = gout.reshape(N_DEV, r_out, d)
        d_mg, job_outs = _matmul(dzb, w_out_full[l], dims="nt", shape=(n, d, d), tiles=(tm, tn, tk_d),
                                 jobs=[_job_scatter_d2d([gout])], name=f"out_proj_dx{l}")
        pair_out = _pair_add(gout, job_outs[0][0], f"pair_out_{l}")
        d_q, parts, d_kv_ctx, d_sink, d_goa = _attention_bwd(qkv, sink[l], ao, d_mg, cpar, n_ctx, hd, kd, f"attn_bwd{l}")
        d_p, conv_acc = _mix_bwd(d_mg, p, cpar, d_q, parts, d_kv_ctx, cs, sn, n_ctx, dc, hd, kd, f"mix_bwd{l}")
        tm_g = _pick(d, FEAT)
        tn = _pick(d_in, (1536, 768, 512, 256, 128))
        gin = _matmul(h, d_p, dims="tn", shape=(d, d_in, n), tiles=(tm_g, tn, tkn), name=f"in_proj_dw{l}")
        gin = jnp.transpose(gin.reshape(d, N_DEV, cin), (1, 0, 2))
        tn = _pick(d, FEAT)
        tk = _pick(d_in, (1536, 768, 512, 256, 128))
        dh, job_outs = _matmul(d_p, w_in_full[l], dims="nt", shape=(n, d, d_in), tiles=(tm, tn, tk),
                               jobs=[_job_scatter_d2d([gin]), _job_scatter_ici([pair_w1, pair_out], [buf_w1, buf_out], l)],
                               name=f"in_proj_dx{l}")
        buf_w1, buf_out = job_outs[1]
        pend_in, pend_layer = _pair_add(gin, job_outs[0][0], f"pair_in_{l}"), l
        d_t, dss1, dgn1 = _norm_mod_bwd(dh, t_in, d_t2, g_norm1[l][None], mod8[l], 0, n_ctx, f"norm1_bwd{l}")
        d_mod2 = jnp.concatenate([dss1[0:2], dgate1[0:2], dss2[0:2], dgate2[0:2]], axis=1)
        small[l] = (d_mod2, dgn1[0], dgn2[0], conv_acc, d_sink[0, 0:n_heads], d_goa[0])
    grad_x = d_t[n_ctx:][None]

    def pack(l):
        d_mod2, dgn1, dgn2, conv_acc, d_sink, d_goa = small[l]
        row0 = [d_mod2[0], dgn1, dgn2, conv_acc[3], d_sink, conv_acc[4], d_goa, conv_acc[0:3].reshape(-1)]
        row1 = [d_mod2[1]] + [jnp.zeros_like(v) for v in row0[1:]]
        return jnp.stack([jnp.concatenate(row0), jnp.concatenate(row1)])
    per_layer = N_MOD * d + 2 * d + dc + n_heads + 2 * dc + 3 * dc
    packed = jnp.concatenate([pack(l) for l in range(depth)] +
                             [jnp.stack([dg_final[0], jnp.zeros((d,), F32)])], axis=1)
    f_tot = depth * per_layer + d
    f_pad = -f_tot % 1024
    packed = jnp.pad(packed, ((0, 0), (0, f_pad)))
    (small_all,) = _all_gather([packed], "gather_small")
    small_parts = small_all.reshape(2 * N_DEV, 1, f_tot + f_pad)

    def section(arr, l, off, size):
        return lax.slice_in_dim(arr, l * per_layer + off, l * per_layer + off + size, axis=-1)

    offs = {}
    o_ = 0
    for nm_, sz in (("mod", N_MOD * d), ("gn1", d), ("gn2", d), ("cb", dc), ("sink", n_heads), ("goc", dc), ("goa", dc), ("cw", 3 * dc)):
        offs[nm_] = (o_, sz)
        o_ += sz

    def packw(b_ada_, gn1_, gn2_, cb_, sk_, goc_, goa_, gf_):
        rows = []
        for l in range(depth):
            rows += [b_ada_[l], gn1_[l], gn2_[l], cb_[l], sk_[l], goc_[l], goa_[l], jnp.zeros((3 * dc,), F32)]
        return jnp.pad(jnp.concatenate(rows + [gf_]), (0, f_pad))[None]
    pw = packw(b_ada, g_norm1, g_norm2, conv_b, sink, g_out_conv, g_out_attn, g_final)
    pm = packw(m_b_ada, m_g_norm1, m_g_norm2, m_conv_b, m_sink, m_g_out_conv, m_g_out_attn, m_g_final)
    pv = packw(v_b_ada, v_g_norm1, v_g_norm2, v_conv_b, v_sink, v_g_out_conv, v_g_out_attn, v_g_final)
    sg, sd, sm, sv = _adamw(pw, pm, pv, small_parts, "adamw_small")

    def unpack(arr):
        arr = arr[0]
        out = {}
        for nm_ in ("mod", "gn1", "gn2", "cb", "sink", "goc", "goa", "cw"):
            off, size = offs[nm_]
            out[nm_] = jnp.stack([section(arr, l, off, size) for l in range(depth)])
        out["gf"] = arr[depth * per_layer:depth * per_layer + d]
        return out
    ug, ud, um, uv = unpack(sg), unpack(sd), unpack(sm), unpack(sv)

    cw_grad_full = ug["cw"].reshape(depth, 3, N_DEV, cw)
    cw_grad = lax.dynamic_index_in_dim(cw_grad_full, dev, axis=2, keepdims=False).reshape(1, depth * 3, cw)
    cwg, cwd, cwm, cwv = _adamw(conv_w.reshape(depth * 3, cw), m_conv_w.reshape(depth * 3, cw),
                                v_conv_w.reshape(depth * 3, cw), cw_grad, "adamw_conv_w")
    cw_shape = conv_w.shape

    mod_rows = small_all[:, :, :depth * per_layer].reshape(N_DEV, 2, depth, per_layer)[:, :, :, :N_MOD * d]
    dm16 = jnp.concatenate([mod_rows[:, 0], mod_rows[:, 1]], axis=0)
    dm16 = jnp.transpose(dm16, (1, 0, 2)).reshape(depth, 16, N_DEV, c_ada)
    dm16_loc = lax.dynamic_index_in_dim(dm16, dev, axis=2, keepdims=False)
    gb_ada = lax.empty((depth, d, c_ada), F32)
    dsc_parts = []
    tm_g = _pick(d, FEAT)
    for l in range(depth):
        gb_ada = _matmul(sc16, dm16_loc[l], dims="tn", shape=(d, c_ada, 16), tiles=(tm_g, tn_ada, 16),
                         out_specs=(_g_spec("nat", l, None, tm_g, tn_ada),), out_shapes=(jax.ShapeDtypeStruct(gb_ada.shape, F32),),
                         alias=gb_ada, name=f"ada_dw{l}")
        tn = _pick(d, FEAT)
        tk = _pick(c_ada, KDIM)
        dsc_parts.append(_matmul(dm16_loc[l], w_ada, dims="nt", shape=(16, d, c_ada), tiles=(16, tn, tk),
                                 b_spec=_w_spec("nat", l, None, "nt", tn, tk), name=f"ada_dx{l}"))
    (dsc_all,) = _all_gather([jnp.stack(dsc_parts)[:, 8:16]], "gather_dcond")
    g_cctx = _cctx_grad(dsc_all.reshape(N_DEV * depth * 8, d), c_ctx[None], "c_ctx_grad")
    ccg, ccd, ccm, ccv = _adamw(c_ctx[None], m_c_ctx[None], v_c_ctx[None], g_cctx[None], "adamw_c_ctx")
    adg, add, adm, adv = _adamw(w_ada.reshape(depth * d, c_ada), m_w_ada.reshape(depth * d, c_ada),
                                v_w_ada.reshape(depth * d, c_ada), gb_ada.reshape(1, depth * d, c_ada), "adamw_w_ada")

    ((buf_in,),) = _comm_call([_job_scatter_ici([pend_in], [buf_in], pend_layer)], "rs_tail")
    parts4 = [buf_in, buf_out, buf_w1, buf_w2]
    big = []
    for k, (w_, m_, v_) in enumerate(((w_in, m_w_in, v_w_in), (w_out, m_w_out, v_w_out),
                                      (w_mlp1, m_w_mlp1, v_w_mlp1), (w_mlp2, m_w_mlp2, v_w_mlp2))):
        r2, c2 = w_.shape[0] * w_.shape[1], w_.shape[2]
        res = _adamw(w_.reshape(r2, c2), m_.reshape(r2, c2), v_.reshape(r2, c2), parts4[k].reshape(N_CHIP, r2, c2),
                     f"adamw_big{k}")
        big.append([a_.reshape(w_.shape) for a_ in res])

    def leaf(i):
        return (
            (ccg, ccd, ccm, ccv)[i][0], (adg, add, adm, adv)[i].reshape(w_ada.shape),
            (ug, ud, um, uv)[i]["mod"], (ug, ud, um, uv)[i]["gn1"], (ug, ud, um, uv)[i]["gn2"], big[0][i],
            (cwg, cwd, cwm, cwv)[i].reshape(cw_shape), (ug, ud, um, uv)[i]["cb"], (ug, ud, um, uv)[i]["sink"],
            (ug, ud, um, uv)[i]["goc"], (ug, ud, um, uv)[i]["goa"], big[1][i], big[2][i], big[3][i], (ug, ud, um, uv)[i]["gf"])

    return (loss, grad_x) + leaf(0) + leaf(1) + leaf(2) + leaf(3)
```

```python
import functools

import jax
import jax.numpy as jnp
from jax import lax
from jax.experimental import pallas as pl
from jax.experimental.pallas import tpu as pltpu

HEAD_DIM = 64
N_KV_HEADS = 4
BLOCK = 128
GRID_W = 64
ROPE_THETA = 10000.0
EPS = 1e-6
N_MOD = 6
SCALE = HEAD_DIM ** -0.5
NEG_INF = -1e30
ADAM_LR = 0.001
ADAM_B1 = 0.9
ADAM_B2 = 0.999
ADAM_EPS = 1e-08
ADAM_WD = 0.01
ADAM_STEP = 10
N_DEV = 8
N_CHIP = 4
VMEM_LIMIT_BYTES = 48 * 1024 * 1024
MESH = pl.DeviceIdType.MESH
BF16 = jnp.bfloat16
F32 = jnp.float32
ANY = pl.BlockSpec(memory_space=pl.ANY)


def _pick(dim, prefs):
    for p in prefs:
        if p <= dim and dim % p == 0:
            return p
    return dim


def _params(sem):
    return pltpu.CompilerParams(dimension_semantics=sem, vmem_limit_bytes=VMEM_LIMIT_BYTES)


def _row_ids(i, tr):
    return i * tr + lax.broadcasted_iota(jnp.int32, (tr, 1), 0)


def _colsum(v):
    return jnp.sum(v, axis=0, keepdims=True)


def _rowmean(v):
    return jnp.mean(v, axis=1, keepdims=True)


def _all_gather(xs, name):
    na = len(xs)

    def body(*refs):
        x_refs, o_refs = refs[:na], refs[na:2 * na]
        send_sems, recv_sems, local_sems = refs[2 * na:]
        x, y, c = lax.axis_index("x"), lax.axis_index("y"), lax.axis_index("c")
        me, sibling = (x, y, c), (x, y, 1 - c)
        chips = [(1 - x, y), (x, 1 - y), (1 - x, 1 - y)]

        def slot(a, px, py, pc):
            return o_refs[a].at[4 * px + 2 * py + pc]

        def copy(a, k, block, to, src=None):
            return pltpu.make_async_remote_copy(
                src_ref=slot(a, *block) if src is None else src, dst_ref=slot(a, *block),
                send_sem=send_sems.at[a, k], recv_sem=recv_sems.at[a, k], device_id=to, device_id_type=MESH)

        mine = [pltpu.make_async_copy(x_refs[a], slot(a, *me), local_sems.at[a]) for a in range(na)]
        for cp in mine:
            cp.start()
        first = []
        for a in range(na):
            first.append(copy(a, 0, me, sibling, src=x_refs[a]))
            first += [copy(a, 1 + j, me, (*chip, c), src=x_refs[a]) for j, chip in enumerate(chips)]
        for cp in first:
            cp.start()
        passed = []
        for j, chip in enumerate(chips):
            for a in range(na):
                copy(a, 1 + j, (*chip, c), me).wait_recv()
                fwd = copy(a, 4 + j, (*chip, c), sibling)
                fwd.start()
                passed.append(fwd)
        for a in range(na):
            copy(a, 0, sibling, me).wait_recv()
            for j, chip in enumerate(chips):
                copy(a, 4 + j, (*chip, 1 - c), me).wait_recv()
        for cp in first + passed:
            cp.wait_send()
        for cp in mine:
            cp.wait()

    outs = pl.pallas_call(
        body, name=name,
        out_shape=tuple(jax.ShapeDtypeStruct((N_DEV,) + x.shape, x.dtype) for x in xs),
        in_specs=[ANY] * na, out_specs=tuple([ANY] * na),
        scratch_shapes=[pltpu.SemaphoreType.DMA((na, 7)), pltpu.SemaphoreType.DMA((na, 7)),
                        pltpu.SemaphoreType.DMA((na,))],
    )(*xs)
    return list(outs)


class _Job:
    def __init__(self, ins, outs, alias, sems, copies):
        self.ins, self.outs, self.alias, self.sems, self._copies = ins, outs, alias, sems, copies

    def start(self, in_refs, out_refs, sems):
        local, sends, _ = self._copies(in_refs, out_refs, sems)
        for make in local + sends:
            make().start()

    def wait(self, in_refs, out_refs, sems):
        local, sends, arrivals = self._copies(in_refs, out_refs, sems)
        for make in arrivals:
            make().wait_recv()
        for make in sends:
            make().wait_send()
        for make in local:
            make().wait()


def _other_chips():
    x, y = lax.axis_index("x"), lax.axis_index("y")
    return [(1 - x, y), (x, 1 - y), (1 - x, 1 - y)]


def _remote(src, dst, send_sem, recv_sem, to):
    return functools.partial(pltpu.make_async_remote_copy, src_ref=src, dst_ref=dst, send_sem=send_sem, recv_sem=recv_sem,
                             device_id=to, device_id_type=MESH)


def _local(src, dst, sem):
    return functools.partial(pltpu.make_async_copy, src, dst, sem)


def _job_gather_ici(xs, layer):
    na = len(xs)

    def copies(in_refs, out_refs, sems):
        send, recv, loc = sems
        x, y, c = lax.axis_index("x"), lax.axis_index("y"), lax.axis_index("c")
        me = 4 * x + 2 * y + c
        local, sends, arrivals = [], [], []
        for a in range(na):
            src = in_refs[a].at[layer]
            local.append(_local(src, out_refs[a].at[me], loc.at[a]))
            for j, (px, py) in enumerate(_other_chips()):
                sends.append(_remote(src, out_refs[a].at[me], send.at[a, j], recv.at[a, j], (px, py, c)))
                arrivals.append(_remote(src, out_refs[a].at[4 * px + 2 * py + c], send.at[a, j], recv.at[a, j], (px, py, c)))
        return local, sends, arrivals

    outs = [jax.ShapeDtypeStruct((N_DEV,) + x.shape[1:], x.dtype) for x in xs]
    sems = [pltpu.SemaphoreType.DMA((na, 3)), pltpu.SemaphoreType.DMA((na, 3)), pltpu.SemaphoreType.DMA((na,))]
    return _Job(list(xs), outs, {}, sems, copies)


def _job_gather_d2d(gs):
    na = len(gs)

    def copies(in_refs, out_refs, sems):
        send, recv = sems
        x, y, c = lax.axis_index("x"), lax.axis_index("y"), lax.axis_index("c")
        sends, arrivals = [], []
        for a in range(na):
            for k in range(N_CHIP):
                mine, theirs = 2 * k + c, 2 * k + (1 - c)
                sends.append(_remote(in_refs[a].at[mine], out_refs[a].at[mine], send.at[a, k], recv.at[a, k], (x, y, 1 - c)))
                arrivals.append(_remote(in_refs[a].at[theirs], out_refs[a].at[theirs], send.at[a, k], recv.at[a, k], (x, y, 1 - c)))
        return [], sends, arrivals

    outs = [jax.ShapeDtypeStruct(g.shape, g.dtype) for g in gs]
    sems = [pltpu.SemaphoreType.DMA((na, N_CHIP)), pltpu.SemaphoreType.DMA((na, N_CHIP))]
    return _Job(list(gs), outs, {a: a for a in range(na)}, sems, copies)


def _job_scatter_d2d(gs):
    na = len(gs)

    def copies(in_refs, out_refs, sems):
        send, recv = sems
        x, y, c = lax.axis_index("x"), lax.axis_index("y"), lax.axis_index("c")
        sends, arrivals = [], []
        for a in range(na):
            for k in range(N_CHIP):
                cp = _remote(in_refs[a].at[2 * k + (1 - c)], out_refs[a].at[k], send.at[a, k], recv.at[a, k], (x, y, 1 - c))
                sends.append(cp)
                arrivals.append(cp)
        return [], sends, arrivals

    outs = [jax.ShapeDtypeStruct((N_CHIP,) + g.shape[1:], g.dtype) for g in gs]
    sems = [pltpu.SemaphoreType.DMA((na, N_CHIP)), pltpu.SemaphoreType.DMA((na, N_CHIP))]
    return _Job(list(gs), outs, {}, sems, copies)


def _job_scatter_ici(pairs, bufs, layer):
    na = len(pairs)

    def copies(in_refs, out_refs, sems):
        send, recv, loc = sems
        x, y, c = lax.axis_index("x"), lax.axis_index("y"), lax.axis_index("c")
        my_chip = 2 * x + y
        local, sends, arrivals = [], [], []
        for a in range(na):
            local.append(_local(in_refs[a].at[my_chip], out_refs[a].at[my_chip, layer], loc.at[a]))
            for j, (px, py) in enumerate(_other_chips()):
                src = in_refs[a].at[2 * px + py]
                sends.append(_remote(src, out_refs[a].at[my_chip, layer], send.at[a, j], recv.at[a, j], (px, py, c)))
                arrivals.append(_remote(src, out_refs[a].at[2 * px + py, layer], send.at[a, j], recv.at[a, j], (px, py, c)))
        return local, sends, arrivals

    outs = [jax.ShapeDtypeStruct(b.shape, b.dtype) for b in bufs]
    sems = [pltpu.SemaphoreType.DMA((na, 3)), pltpu.SemaphoreType.DMA((na, 3)), pltpu.SemaphoreType.DMA((na,))]
    return _Job(list(pairs) + list(bufs), outs, {na + a: a for a in range(na)}, sems, copies)


def _split_jobs(jobs, in_refs, out_refs, sem_refs):
    out, i0, o0, s0 = [], 0, 0, 0
    for jb in jobs:
        out.append((jb, in_refs[i0:i0 + len(jb.ins)], out_refs[o0:o0 + len(jb.outs)], sem_refs[s0:s0 + len(jb.sems)]))
        i0, o0, s0 = i0 + len(jb.ins), o0 + len(jb.outs), s0 + len(jb.sems)
    return out


def _comm_call(jobs, name):
    n_in = sum(len(jb.ins) for jb in jobs)
    n_out = sum(len(jb.outs) for jb in jobs)

    def body(*refs):
        parts = _split_jobs(jobs, refs[:n_in], refs[n_in:n_in + n_out], refs[n_in + n_out:])
        for jb, i_r, o_r, s_r in parts:
            jb.start(i_r, o_r, s_r)
        for jb, i_r, o_r, s_r in parts:
            jb.wait(i_r, o_r, s_r)

    io_alias, i0, o0 = {}, 0, 0
    for jb in jobs:
        io_alias.update({i0 + a: o0 + b for a, b in jb.alias.items()})
        i0, o0 = i0 + len(jb.ins), o0 + len(jb.outs)
    outs = pl.pallas_call(
        body, name=name, out_shape=tuple(s for jb in jobs for s in jb.outs),
        in_specs=[ANY] * n_in, out_specs=tuple([ANY] * n_out),
        scratch_shapes=[s for jb in jobs for s in jb.sems], input_output_aliases=io_alias,
    )(*[v for jb in jobs for v in jb.ins])
    res, o0 = [], 0
    for jb in jobs:
        res.append(list(outs[o0:o0 + len(jb.outs)]))
        o0 += len(jb.outs)
    return res


def _to_bf16(x2d, name):
    r, c = x2d.shape
    tr = _pick(r, (512, 256, 128, 64, 32, 16))

    def body(x_ref, o_ref):
        o_ref[...] = x_ref[...].astype(BF16)

    return pl.pallas_call(
        body, name=name, out_shape=jax.ShapeDtypeStruct((r, c), BF16), grid=(r // tr,),
        in_specs=[pl.BlockSpec((tr, c), lambda i: (i, 0))], out_specs=pl.BlockSpec((tr, c), lambda i: (i, 0)),
        compiler_params=_params(("parallel",)),
    )(x2d)


def _pair_add(own8, got4, name):
    _, r, c = own8.shape
    tr = _pick(r, (512, 256, 128, 64, 32, 16))
    core = lax.axis_index("c").astype(jnp.int32).reshape(1)

    def body(c_ref, a_ref, b_ref, o_ref):
        o_ref[...] = (a_ref[...] + b_ref[...]).astype(BF16)

    return pl.pallas_call(
        body, name=name, out_shape=jax.ShapeDtypeStruct((N_CHIP, r, c), BF16),
        grid_spec=pltpu.PrefetchScalarGridSpec(
            num_scalar_prefetch=1, grid=(N_CHIP, r // tr),
            in_specs=[pl.BlockSpec((None, tr, c), lambda k, i, cr: (2 * k + cr[0], i, 0)),
                      pl.BlockSpec((None, tr, c), lambda k, i, cr: (k, i, 0))],
            out_specs=pl.BlockSpec((None, tr, c), lambda k, i, cr: (k, i, 0))),
        compiler_params=_params(("parallel", "parallel")),
    )(core, own8, got4)


def _adamw(w, m, v, parts, name):
    r, c = w.shape
    n_parts = parts.shape[0]
    tr = _pick(r, (256, 128, 64, 32, 16, 8))
    tc = _pick(c, (1024, 512, 256, 128))

    def body(w_ref, m_ref, v_ref, p_ref, g_ref, d_ref, nm_ref, nv_ref):
        g = p_ref[0].astype(F32)
        for k in range(1, n_parts):
            g = g + p_ref[k].astype(F32)
        wv = w_ref[...]
        nm = ADAM_B1 * m_ref[...] + (1.0 - ADAM_B1) * g
        nv = ADAM_B2 * v_ref[...] + (1.0 - ADAM_B2) * (g * g)
        m_hat = nm / (1.0 - ADAM_B1 ** ADAM_STEP)
        v_hat = nv / (1.0 - ADAM_B2 ** ADAM_STEP)
        g_ref[...] = g
        d_ref[...] = -ADAM_LR * (m_hat / (jnp.sqrt(v_hat) + ADAM_EPS) + ADAM_WD * wv)
        nm_ref[...] = nm
        nv_ref[...] = nv

    tile = pl.BlockSpec((tr, tc), lambda i, j: (i, j))
    sh = jax.ShapeDtypeStruct((r, c), F32)
    return pl.pallas_call(
        body, name=name, out_shape=(sh, sh, sh, sh), grid=(r // tr, c // tc),
        in_specs=[tile, tile, tile, pl.BlockSpec((n_parts, tr, tc), lambda i, j: (0, i, j))],
        out_specs=(tile, tile, tile, tile),
        compiler_params=_params(("parallel", "parallel")),
    )(w, m, v, parts)


def _matmul(a, b, *, dims, shape, tiles, b_spec=None, out_specs=None, out_shapes=None, out_dtypes=(F32,),
            epilogue=None, extras=(), alias=None, jobs=(), name):
    m_dim, n_dim, k_dim = shape
    tm, tn, tk = tiles
    assert m_dim % tm == 0 and n_dim % tn == 0 and k_dim % tk == 0, (name, shape, tiles)
    nk = k_dim // tk
    n_extra = len(extras)
    n_alias = 0 if alias is None else 1
    n_out = len(out_dtypes)
    if dims == "tn":
        a_spec = pl.BlockSpec((tk, tm), lambda i, j, k: (k, i))
        contract = (((0,), (0,)), ((), ()))
    else:
        a_spec = pl.BlockSpec((tm, tk), lambda i, j, k: (i, k))
        contract = (((1,), (1,)), ((), ())) if dims == "nt" else (((1,), (0,)), ((), ()))
    if b_spec is None:
        b_spec = (pl.BlockSpec((tn, tk), lambda i, j, k: (j, k)) if dims == "nt"
                  else pl.BlockSpec((tk, tn), lambda i, j, k: (k, j)))
    if out_specs is None:
        out_specs = tuple(pl.BlockSpec((tm, tn), lambda i, j, k: (i, j)) for _ in range(n_out))
    if out_shapes is None:
        out_shapes = tuple(jax.ShapeDtypeStruct((m_dim, n_dim), d) for d in out_dtypes)

    n_local_in = 2 + n_extra + n_alias
    n_job_in = sum(len(jb.ins) for jb in jobs)
    n_job_out = sum(len(jb.outs) for jb in jobs)
    n_acc = 1 if nk > 1 else 0
    grid = (m_dim // tm, n_dim // tn, nk)

    def body(*refs):
        a_ref, b_ref = refs[0], refs[1]
        extra_refs = refs[2:2 + n_extra]
        o0 = n_local_in + n_job_in
        out_refs = refs[o0:o0 + n_out]
        s0 = o0 + n_out + n_job_out
        job_parts = _split_jobs(jobs, refs[n_local_in:o0], refs[o0 + n_out:s0], refs[s0 + n_acc:])
        i, j, k = pl.program_id(0), pl.program_id(1), pl.program_id(2)

        if jobs:
            @pl.when((i == 0) & (j == 0) & (k == 0))
            def _():
                for jb, i_r, o_r, s_r in job_parts:
                    jb.start(i_r, o_r, s_r)

        def finish(acc):
            if epilogue is None:
                out_refs[0][...] = acc.astype(out_refs[0].dtype)
            else:
                epilogue(acc, i, j, extra_refs, out_refs)

        def product():
            return lax.dot_general(a_ref[...].astype(BF16), b_ref[...].astype(BF16), contract, preferred_element_type=F32)

        if nk == 1:
            finish(product())
        else:
            acc_ref = refs[s0]

            @pl.when(k == 0)
            def _():
                acc_ref[...] = jnp.zeros_like(acc_ref)

            acc_ref[...] += product()

            @pl.when(k == nk - 1)
            def _():
                finish(acc_ref[...])

        if jobs:
            @pl.when((i == grid[0] - 1) & (j == grid[1] - 1) & (k == nk - 1))
            def _():
                for jb, i_r, o_r, s_r in job_parts:
                    jb.wait(i_r, o_r, s_r)

    ins = [a, b] + [e[0] for e in extras]
    in_specs = [a_spec, b_spec] + [e[1] for e in extras]
    io_alias = {}
    if alias is not None:
        ins.append(alias)
        in_specs.append(ANY)
        io_alias = {len(ins) - 1: 0}
    all_out_shapes, all_out_specs = list(out_shapes), list(out_specs)
    for jb in jobs:
        io_alias.update({len(ins) + a_: len(all_out_shapes) + b_ for a_, b_ in jb.alias.items()})
        ins += jb.ins
        in_specs += [ANY] * len(jb.ins)
        all_out_shapes += jb.outs
        all_out_specs += [ANY] * len(jb.outs)
    scratch = ([pltpu.VMEM((tm, tn), F32)] if nk > 1 else []) + [s for jb in jobs for s in jb.sems]
    outs = pl.pallas_call(
        body, name=name, out_shape=tuple(all_out_shapes), grid=grid,
        in_specs=in_specs, out_specs=tuple(all_out_specs), scratch_shapes=scratch,
        input_output_aliases=io_alias,
        compiler_params=_params(("arbitrary",) * 3 if jobs else ("parallel", "parallel", "arbitrary")),
    )(*ins)
    main = outs[0] if n_out == 1 else tuple(outs[:n_out])
    if not jobs:
        return main
    job_outs, o0 = [], n_out
    for jb in jobs:
        job_outs.append(list(outs[o0:o0 + len(jb.outs)]))
        o0 += len(jb.outs)
    return main, job_outs


def _lead(layer, block, index):
    if layer is None:
        return pl.BlockSpec(block, index)
    return pl.BlockSpec((None,) + block, lambda i, j, k: (layer,) + index(i, j, k))


def _w_spec(kind, layer, per, dims, tn, tk):
    if kind == "nat":
        if dims == "nn":
            return _lead(layer, (tk, tn), lambda i, j, k: (k, j))
        return _lead(layer, (tn, tk), lambda i, j, k: (j, k))
    if dims == "nn":
        q = per // tn
        return _lead(layer, (None, tk, tn), lambda i, j, k: (j // q, k, j % q))
    q = per // tk
    return _lead(layer, (None, tn, tk), lambda i, j, k: (k // q, j, k % q))


def _g_spec(kind, layer, per, tm, tn):
    if kind == "nat":
        return _lead(layer, (tm, tn), lambda i, j, k: (i, j))
    q = per // tn
    return _lead(layer, (None, tm, tn), lambda i, j, k: (j // q, i, j % q))


TOK = (1408, 768, 256, 128)
FEAT = (1024, 512, 256, 128)
KDIM = (2048, 1536, 1024, 512, 256, 128)


def _sel(is_ctx, ref):
    return jnp.where(is_ctx, ref[1:2, :], ref[0:1, :])


def _row_tile(n, n_ctx):
    tr = _pick(n_ctx, (256, 128))
    assert n % tr == 0 and n_ctx % tr == 0
    return tr


def _add_by_segment(acc_ref, cols, ctx_tile, v):
    zero = jnp.zeros_like(v)
    acc_ref[0:1, cols] += jnp.where(ctx_tile, zero, v)
    acc_ref[1:2, cols] += jnp.where(ctx_tile, v, zero)


def _norm_mod(t, gain, mod8, shift_k, n_ctx, name):
    n, d = t.shape
    tr = _row_tile(n, n_ctx)

    def body(t_ref, g_ref, sh_ref, sc_ref, o_ref):
        x = t_ref[...]
        r = lax.rsqrt(_rowmean(x * x) + EPS)
        y = (x * r) * g_ref[...]
        ctx_tile = pl.program_id(0) * tr < n_ctx
        o_ref[...] = (y * (1.0 + _sel(ctx_tile, sc_ref)) + _sel(ctx_tile, sh_ref)).astype(BF16)

    return pl.pallas_call(
        body, name=name, out_shape=jax.ShapeDtypeStruct((n, d), BF16), grid=(n // tr,),
        in_specs=[pl.BlockSpec((tr, d), lambda i: (i, 0)), pl.BlockSpec((1, d), lambda i: (0, 0)),
                  pl.BlockSpec((8, d), lambda i: (0, shift_k)), pl.BlockSpec((8, d), lambda i: (0, shift_k + 1))],
        out_specs=pl.BlockSpec((tr, d), lambda i: (i, 0)),
        compiler_params=_params(("parallel",)),
    )(t, gain, mod8, mod8)


def _norm_mod_bwd(dh, t, d_res, gain, mod8, shift_k, n_ctx, name):
    n, d = t.shape
    tr = _row_tile(n, n_ctx)

    def body(dh_ref, t_ref, dr_ref, g_ref, sc_ref, dt_ref, dss_ref, dg_ref):
        i = pl.program_id(0)

        @pl.when(i == 0)
        def _():
            dss_ref[...] = jnp.zeros_like(dss_ref)
            dg_ref[...] = jnp.zeros_like(dg_ref)

        x = t_ref[...]
        r = lax.rsqrt(_rowmean(x * x) + EPS)
        xn = x * r
        g = g_ref[...]
        y = xn * g
        dhv = dh_ref[...]
        ctx_tile = i * tr < n_ctx
        _add_by_segment(dss_ref, slice(0, d), ctx_tile, _colsum(dhv))
        _add_by_segment(dss_ref, slice(d, 2 * d), ctx_tile, _colsum(dhv * y))
        dy = dhv * (1.0 + _sel(ctx_tile, sc_ref))
        dg_ref[0:1, :] += _colsum(dy * xn)
        dxn = dy * g
        dt_ref[...] = dr_ref[...] + r * (dxn - xn * _rowmean(dxn * xn))

    row = pl.BlockSpec((tr, d), lambda i: (i, 0))
    return pl.pallas_call(
        body, name=name,
        out_shape=(jax.ShapeDtypeStruct((n, d), F32), jax.ShapeDtypeStruct((8, 2 * d), F32), jax.ShapeDtypeStruct((8, d), F32)),
        grid=(n // tr,),
        in_specs=[row, row, row, pl.BlockSpec((1, d), lambda i: (0, 0)), pl.BlockSpec((8, d), lambda i: (0, shift_k + 1))],
        out_specs=(row, pl.BlockSpec((8, 2 * d), lambda i: (0, 0)), pl.BlockSpec((8, d), lambda i: (0, 0))),
        compiler_params=_params(("arbitrary",)),
    )(dh, t, d_res, gain, mod8)


def _gate_bwd(d_t, branch, mod8, gate_k, n_ctx, name):
    n, d = d_t.shape
    tr = _row_tile(n, n_ctx)

    def body(dt_ref, o_ref, gt_ref, dob_ref, dgate_ref):
        i = pl.program_id(0)

        @pl.when(i == 0)
        def _():
            dgate_ref[...] = jnp.zeros_like(dgate_ref)

        dv = dt_ref[...]
        ctx_tile = i * tr < n_ctx
        dob_ref[...] = (dv * _sel(ctx_tile, gt_ref)).astype(BF16)
        _add_by_segment(dgate_ref, slice(None), ctx_tile, _colsum(dv * o_ref[...].astype(F32)))

    row = pl.BlockSpec((tr, d), lambda i: (i, 0))
    return pl.pallas_call(
        body, name=name, out_shape=(jax.ShapeDtypeStruct((n, d), BF16), jax.ShapeDtypeStruct((8, d), F32)),
        grid=(n // tr,),
        in_specs=[row, row, pl.BlockSpec((8, d), lambda i: (0, gate_k))],
        out_specs=(row, pl.BlockSpec((8, d), lambda i: (0, 0))),
        compiler_params=_params(("arbitrary",)),
    )(d_t, branch, mod8)


def _swap16(v):
    w = v.shape[1]
    lane = lax.broadcasted_iota(jnp.int32, v.shape, 1)
    return jnp.where((lane % 32) < 16, pltpu.roll(v, w - 16, 1), pltpu.roll(v, 16, 1))


def _rope(v, cs, sn, sign):
    reps = v.shape[1] // 128
    c = jnp.tile(cs, (1, reps)) if reps > 1 else cs
    s = jnp.tile(sn, (1, reps)) if reps > 1 else sn
    return v * c + sign * (_swap16(v) * s)


def _rope_split(p, cs, sn, dc, hd, kd, name):
    n, d_in = p.shape
    tr = _pick(n, (256, 128))
    q0 = 3 * dc

    def body(p_ref, cs_ref, sn_ref, o_ref):
        cs_v, sn_v = cs_ref[...], sn_ref[...]
        q = _rope(p_ref[:, q0:q0 + hd], cs_v, sn_v, 1.0) * SCALE
        k = _rope(p_ref[:, q0 + hd:q0 + hd + kd], cs_v, sn_v, 1.0)
        v = p_ref[:, q0 + hd + kd:q0 + hd + 2 * kd]
        o_ref[...] = jnp.concatenate([q, k, v], axis=1).astype(BF16)

    return pl.pallas_call(
        body, name=name, out_shape=jax.ShapeDtypeStruct((n, hd + 2 * kd), BF16), grid=(n // tr,),
        in_specs=[pl.BlockSpec((tr, d_in), lambda i: (i, 0)), pl.BlockSpec((tr, 128), lambda i: (i, 0)),
                  pl.BlockSpec((tr, 128), lambda i: (i, 0))],
        out_specs=pl.BlockSpec((tr, hd + 2 * kd), lambda i: (i, 0)),
        compiler_params=_params(("parallel",)),
    )(p, cs, sn)


def _attn_specs(nb, n_ctx, hd, kd):
    kci = hd // kd
    specs = [pl.BlockSpec((BLOCK, hd), lambda b: (b, 0)),
             pl.BlockSpec((n_ctx, kd), lambda b: (0, kci)), pl.BlockSpec((n_ctx, kd), lambda b: (0, kci + 1))]
    for col in (kci, kci + 1):
        specs.append(pl.BlockSpec((BLOCK, kd), lambda b, col=col: (jnp.maximum(b - 1, 0), col)))
        specs.append(pl.BlockSpec((BLOCK, kd), lambda b, col=col: (b, col)))
        specs.append(pl.BlockSpec((BLOCK, kd), lambda b, col=col: (jnp.minimum(b + 1, nb - 1), col)))
    return specs


def _band_valid(b, group, n_ctx, n):
    q_pos = b * BLOCK + lax.broadcasted_iota(jnp.int32, (group * BLOCK, 1), 0) % BLOCK
    k_pos = (b - 1) * BLOCK + lax.broadcasted_iota(jnp.int32, (1, 3 * BLOCK), 1)
    return (jnp.abs(k_pos - q_pos) <= BLOCK) & (k_pos >= n_ctx) & (k_pos < n) & (q_pos >= n_ctx)


NT = (((1,), (1,)), ((), ()))
NN = (((1,), (0,)), ((), ()))
TN = (((0,), (0,)), ((), ()))


def _dot(a, b, dn):
    return lax.dot_general(a, b, dn, preferred_element_type=F32)


def _keys_of_block(b, kv_refs, group, n_ctx, n):
    kc_ref, vc_ref, k0, k1, k2, v0, v1, v2 = kv_refs
    bias = jnp.where(_band_valid(b, group, n_ctx, n), 0.0, NEG_INF)
    kcat = jnp.concatenate([kc_ref[...], k0[...], k1[...], k2[...]], axis=0)
    vcat = jnp.concatenate([vc_ref[...], v0[...], v1[...], v2[...]], axis=0)
    return bias, kcat, vcat


def _stack_heads(v, h, group):
    return jnp.concatenate([v[:, (h * group + j) * HEAD_DIM:(h * group + j + 1) * HEAD_DIM] for j in range(group)], axis=0)


def _scores(qg, keys, bias, n_ctx):
    s = _dot(qg, keys, NT)
    return jnp.concatenate([s[:, :n_ctx], s[:, n_ctx:] + bias], axis=1)


def _softmax_sink(s, sink_ref, h, group):
    snk = jnp.concatenate([jnp.full((BLOCK, 1), sink_ref[h * group + j], F32) for j in range(group)], axis=0)
    m = jnp.maximum(jnp.max(s, axis=1, keepdims=True), snk)
    e, e_s = jnp.exp(s - m), jnp.exp(snk - m)
    inv = 1.0 / (jnp.sum(e, axis=1, keepdims=True) + e_s)
    return e * inv, e_s * inv


def _attention(qkv, sink, n_ctx, hd, kd, name, jobs=()):
    n = qkv.shape[0]
    nb = n // BLOCK
    n_kv = kd // HEAD_DIM
    group = hd // kd
    n_job_in = sum(len(jb.ins) for jb in jobs)
    n_job_out = sum(len(jb.outs) for jb in jobs)

    def body(*refs):
        q_ref, kc_ref, vc_ref, k0, k1, k2, v0, v1, v2, sink_ref = refs[:10]
        o_ref = refs[10 + n_job_in]
        job_parts = _split_jobs(jobs, refs[10:10 + n_job_in], refs[11 + n_job_in:11 + n_job_in + n_job_out],
                                refs[11 + n_job_in + n_job_out:])
        b = pl.program_id(0)

        if jobs:
            @pl.when(b == 0)
            def _():
                for jb, i_r, o_r, s_r in job_parts:
                    jb.start(i_r, o_r, s_r)

        bias, kcat, vcat = _keys_of_block(b, (kc_ref, vc_ref, k0, k1, k2, v0, v1, v2), group, n_ctx, n)
        q = q_ref[...]

        def scores(h):
            return _scores(_stack_heads(q, h, group), kcat[:, h * HEAD_DIM:(h + 1) * HEAD_DIM], bias, n_ctx)

        heads, s_next = [], scores(0)
        for h in range(n_kv):
            s = s_next
            if h + 1 < n_kv:
                s_next = scores(h + 1)
            p, _ = _softmax_sink(s, sink_ref, h, group)
            o = _dot(p.astype(BF16), vcat[:, h * HEAD_DIM:(h + 1) * HEAD_DIM], NN)
            heads += [o[j * BLOCK:(j + 1) * BLOCK, :] for j in range(group)]
        o_ref[...] = jnp.concatenate(heads, axis=1)

        if jobs:
            @pl.when(b == nb - 1)
            def _():
                for jb, i_r, o_r, s_r in job_parts:
                    jb.wait(i_r, o_r, s_r)

    io_alias, i0, o0 = {}, 10, 1
    for jb in jobs:
        io_alias.update({i0 + a_: o0 + b_ for a_, b_ in jb.alias.items()})
        i0, o0 = i0 + len(jb.ins), o0 + len(jb.outs)
    outs = pl.pallas_call(
        body, name=name, grid=(nb,),
        out_shape=(jax.ShapeDtypeStruct((n, hd), F32),) + tuple(s for jb in jobs for s in jb.outs),
        in_specs=_attn_specs(nb, n_ctx, hd, kd) + [pl.BlockSpec(memory_space=pltpu.SMEM)] + [ANY] * n_job_in,
        out_specs=(pl.BlockSpec((BLOCK, hd), lambda b: (b, 0)),) + tuple([ANY] * n_job_out),
        scratch_shapes=[s for jb in jobs for s in jb.sems], input_output_aliases=io_alias,
        compiler_params=_params(("arbitrary",) if jobs else ("parallel",)),
    )(qkv, qkv, qkv, qkv, qkv, qkv, qkv, qkv, qkv, sink, *[v for jb in jobs for v in jb.ins])
    if not jobs:
        return outs[0]
    job_outs, o0 = [], 1
    for jb in jobs:
        job_outs.append(list(outs[o0:o0 + len(jb.outs)]))
        o0 += len(jb.outs)
    return outs[0], job_outs


def _attention_bwd(qkv, sink, ao, d_mg, cpar, n_ctx, hd, kd, name):
    n = qkv.shape[0]
    nb = n // BLOCK
    n_kv = kd // HEAD_DIM
    group = hd // kd
    n_heads = n_kv * group

    def body(q_ref, kc_ref, vc_ref, k0, k1, k2, v0, v1, v2, sink_ref, ao_ref, dmg_ref, cp_ref,
             dq_ref, part_ref, dctx_ref, dsink_ref, dgain_ref):
        b = pl.program_id(0)

        @pl.when(b == 0)
        def _():
            dctx_ref[...] = jnp.zeros_like(dctx_ref)
            dsink_ref[...] = jnp.zeros_like(dsink_ref)
            dgain_ref[...] = jnp.zeros_like(dgain_ref)

        ao_v = ao_ref[...]
        ra = lax.rsqrt(_rowmean(ao_v * ao_v) + EPS)
        an = ao_v * ra
        dmg = dmg_ref[...]
        dgain_ref[0:1, :] += _colsum(dmg * an)
        d_an = dmg * cp_ref[5:6, :]
        d_ao = (ra * (d_an - an * _rowmean(d_an * an))).astype(BF16)

        bias, kcat, vcat = _keys_of_block(b, (kc_ref, vc_ref, k0, k1, k2, v0, v1, v2), group, n_ctx, n)
        q = q_ref[...]
        lane = lax.broadcasted_iota(jnp.int32, (1, 128), 1)
        dsink_row = jnp.zeros((1, 128), F32)

        def first_half(h):
            hs = slice(h * HEAD_DIM, (h + 1) * HEAD_DIM)
            qg, dog = _stack_heads(q, h, group), _stack_heads(d_ao, h, group)
            return qg, dog, _scores(qg, kcat[:, hs], bias, n_ctx), _dot(dog, vcat[:, hs], NT)

        dq_heads, dk, dv = [], [], []
        nxt = first_half(0)
        for h in range(n_kv):
            qg, dog, s, d_p = nxt
            if h + 1 < n_kv:
                nxt = first_half(h + 1)
            p, p_s = _softmax_sink(s, sink_ref, h, group)
            delta = jnp.sum(p * d_p, axis=1, keepdims=True)
            ds = (p * (d_p - delta)).astype(BF16)
            psd = p_s * delta
            for j in range(group):
                val = -jnp.sum(psd[j * BLOCK:(j + 1) * BLOCK, :], axis=0, keepdims=True)
                dsink_row = dsink_row + jnp.where(lane == h * group + j, val, 0.0)
            dq = _dot(ds, kcat[:, h * HEAD_DIM:(h + 1) * HEAD_DIM], NN) * SCALE
            dq_heads += [dq[j * BLOCK:(j + 1) * BLOCK, :] for j in range(group)]
            dk.append(_dot(ds, qg, TN))
            dv.append(_dot(p.astype(BF16), dog, TN))
        dq_ref[...] = jnp.concatenate(dq_heads, axis=1)
        d_kv = jnp.concatenate(dk + dv, axis=1)
        dctx_ref[...] += d_kv[:n_ctx]
        for j in range(3):
            part_ref[j] = d_kv[n_ctx + j * BLOCK:n_ctx + (j + 1) * BLOCK, :]
        dsink_ref[0:1, :] += dsink_row

    assert n_heads <= 128
    out_shape = (jax.ShapeDtypeStruct((n, hd), F32), jax.ShapeDtypeStruct((nb, 3, BLOCK, 2 * kd), F32),
                 jax.ShapeDtypeStruct((n_ctx, 2 * kd), F32), jax.ShapeDtypeStruct((8, 128), F32),
                 jax.ShapeDtypeStruct((8, hd), F32))
    return pl.pallas_call(
        body, name=name, out_shape=out_shape, grid=(nb,),
        in_specs=_attn_specs(nb, n_ctx, hd, kd) + [
            pl.BlockSpec(memory_space=pltpu.SMEM), pl.BlockSpec((BLOCK, hd), lambda b: (b, 0)),
            pl.BlockSpec((BLOCK, hd), lambda b: (b, 1)), pl.BlockSpec((8, hd), lambda b: (0, 0))],
        out_specs=(pl.BlockSpec((BLOCK, hd), lambda b: (b, 0)),
                   pl.BlockSpec((None, 3, BLOCK, 2 * kd), lambda b: (b, 0, 0, 0)),
                   pl.BlockSpec((n_ctx, 2 * kd), lambda b: (0, 0)), pl.BlockSpec((8, 128), lambda b: (0, 0)),
                   pl.BlockSpec((8, hd), lambda b: (0, 0))),
        compiler_params=_params(("arbitrary",)),
    )(qkv, qkv, qkv, qkv, qkv, qkv, qkv, qkv, qkv, sink, ao, d_mg, cpar)


def _halo_specs(tr, n, width, col=0):
    q = tr // 8
    return [pl.BlockSpec((8, width), lambda i: (jnp.maximum(i * q - 1, 0), col)),
            pl.BlockSpec((8, width), lambda i: (jnp.minimum((i + 1) * q, n // 8 - 1), col))]


def _mix_fwd(p, ao, cpar, n_ctx, dc, name):
    n, d_in = p.shape
    tr = _pick(n, (256, 128))

    def body(p_ref, pp_ref, pn_ref, ao_ref, cp_ref, o_ref):
        i = pl.program_id(0)
        bg = p_ref[:, 0:dc]
        u = p_ref[:, dc:2 * dc] * p_ref[:, 2 * dc:3 * dc]
        u_before = pp_ref[7:8, dc:2 * dc] * pp_ref[7:8, 2 * dc:3 * dc]
        u_after = pn_ref[0:1, dc:2 * dc] * pn_ref[0:1, 2 * dc:3 * dc]
        loc = lax.broadcasted_iota(jnp.int32, (tr, 1), 0)
        gid = i * tr + loc
        has_prev = (gid != 0) & (gid != n_ctx)
        has_next = (gid != n_ctx - 1) & (gid != n - 1)
        u_m1 = jnp.where(has_prev, jnp.where(loc == 0, u_before, pltpu.roll(u, 1, 0)), 0.0)
        u_p1 = jnp.where(has_next, jnp.where(loc == tr - 1, u_after, pltpu.roll(u, tr - 1, 0)), 0.0)
        cv = u_m1 * cp_ref[0:1, :] + u * cp_ref[1:2, :] + u_p1 * cp_ref[2:3, :] + cp_ref[3:4, :]
        co = bg * cv
        nc = (co * lax.rsqrt(_rowmean(co * co) + EPS)) * cp_ref[4:5, :]
        ao_v = ao_ref[...]
        na = (ao_v * lax.rsqrt(_rowmean(ao_v * ao_v) + EPS)) * cp_ref[5:6, :]
        o_ref[...] = jnp.concatenate([nc, na], axis=1).astype(BF16)

    return pl.pallas_call(
        body, name=name, out_shape=jax.ShapeDtypeStruct((n, 2 * dc), BF16), grid=(n // tr,),
        in_specs=[pl.BlockSpec((tr, d_in), lambda i: (i, 0))] + _halo_specs(tr, n, d_in)
        + [pl.BlockSpec((tr, dc), lambda i: (i, 0)), pl.BlockSpec((8, dc), lambda i: (0, 0))],
        out_specs=pl.BlockSpec((tr, 2 * dc), lambda i: (i, 0)),
        compiler_params=_params(("parallel",)),
    )(p, p, p, ao, cpar)


def _mix_bwd(d_mg, p, cpar, d_q, parts, d_ctx, cs, sn, n_ctx, dc, hd, kd, name):
    n, d_in = p.shape
    tr = BLOCK
    nb = n // tr
    ext = tr + 16
    n_ctx_blocks = n_ctx // BLOCK

    def body(dm_ref, dmp_ref, dmn_ref, p_ref, pp_ref, pn_ref, cp_ref, dq_ref, pa_ref, pb_ref, pc_ref, dctx_ref,
             cs_ref, sn_ref, dp_ref, acc_ref):
        i = pl.program_id(0)

        @pl.when(i == 0)
        def _():
            acc_ref[...] = jnp.zeros_like(acc_ref)

        def cat(before, here, after):
            return jnp.concatenate([before, here, after], axis=0)

        bg = cat(pp_ref[:, 0:dc], p_ref[:, 0:dc], pn_ref[:, 0:dc])
        cg = cat(pp_ref[:, dc:2 * dc], p_ref[:, dc:2 * dc], pn_ref[:, dc:2 * dc])
        hh = cat(pp_ref[:, 2 * dc:3 * dc], p_ref[:, 2 * dc:3 * dc], pn_ref[:, 2 * dc:3 * dc])
        dme = cat(dmp_ref[...], dm_ref[...], dmn_ref[...])
        gid = i * tr - 8 + lax.broadcasted_iota(jnp.int32, (ext, 1), 0)
        inside = (gid >= 0) & (gid < n)
        has_prev = inside & (gid != 0) & (gid != n_ctx)
        has_next = inside & (gid != n_ctx - 1) & (gid != n - 1)
        w0, w1, w2, bias, gain = cp_ref[0:1, :], cp_ref[1:2, :], cp_ref[2:3, :], cp_ref[3:4, :], cp_ref[4:5, :]
        u = jnp.where(inside, cg * hh, 0.0)
        u_m1 = jnp.where(has_prev, pltpu.roll(u, 1, 0), 0.0)
        u_p1 = jnp.where(has_next, pltpu.roll(u, ext - 1, 0), 0.0)
        cv = u_m1 * w0 + u * w1 + u_p1 * w2 + bias
        co = bg * cv
        rc = lax.rsqrt(_rowmean(co * co) + EPS)
        cn = co * rc
        d_cn = dme * gain
        d_co = rc * (d_cn - cn * _rowmean(d_cn * cn))
        d_cv = jnp.where(inside, d_co * bg, 0.0)
        d_bg = d_co * cv
        d_cv_p1 = jnp.where(has_next, pltpu.roll(d_cv, ext - 1, 0), 0.0)
        d_cv_m1 = jnp.where(has_prev, pltpu.roll(d_cv, 1, 0), 0.0)
        d_u = d_cv_p1 * w0 + d_cv * w1 + d_cv_m1 * w2
        mid = slice(8, 8 + tr)
        acc_ref[0:1, :] += _colsum((d_cv * u_m1)[mid])
        acc_ref[1:2, :] += _colsum((d_cv * u)[mid])
        acc_ref[2:3, :] += _colsum((d_cv * u_p1)[mid])
        acc_ref[3:4, :] += _colsum(d_cv[mid])
        acc_ref[4:5, :] += _colsum((dme * cn)[mid])

        d_kv = (jnp.where(i >= 1, pa_ref[...], 0.0) + pb_ref[...] + jnp.where(i + 1 < nb, pc_ref[...], 0.0))
        ctx_rows = dctx_ref[pl.ds(pl.multiple_of(jnp.minimum(i, n_ctx_blocks - 1) * BLOCK, BLOCK), BLOCK), :]
        d_kv = d_kv + jnp.where(i < n_ctx_blocks, ctx_rows, 0.0)
        cs_v, sn_v = cs_ref[...], sn_ref[...]
        d_qu = _rope(dq_ref[...], cs_v, sn_v, -1.0)
        d_ku = _rope(d_kv[:, 0:kd], cs_v, sn_v, -1.0)
        dp_ref[...] = jnp.concatenate(
            [d_bg[mid], (d_u * hh)[mid], (d_u * cg)[mid], d_qu, d_ku, d_kv[:, kd:2 * kd]], axis=1).astype(BF16)

    part = lambda sel, which: pl.BlockSpec((None, None, BLOCK, 2 * kd), lambda i: (sel(i), which, 0, 0))
    return pl.pallas_call(
        body, name=name, out_shape=(jax.ShapeDtypeStruct((n, d_in), BF16), jax.ShapeDtypeStruct((8, dc), F32)),
        grid=(nb,),
        in_specs=[pl.BlockSpec((tr, dc), lambda i: (i, 0))] + _halo_specs(tr, n, dc)
        + [pl.BlockSpec((tr, d_in), lambda i: (i, 0))] + _halo_specs(tr, n, d_in)
        + [pl.BlockSpec((8, dc), lambda i: (0, 0)), pl.BlockSpec((tr, hd), lambda i: (i, 0)),
           part(lambda i: jnp.maximum(i - 1, 0), 2), part(lambda i: i, 1), part(lambda i: jnp.minimum(i + 1, nb - 1), 0),
           pl.BlockSpec((n_ctx, 2 * kd), lambda i: (0, 0)),
           pl.BlockSpec((tr, 128), lambda i: (i, 0)), pl.BlockSpec((tr, 128), lambda i: (i, 0))],
        out_specs=(pl.BlockSpec((tr, d_in), lambda i: (i, 0)), pl.BlockSpec((8, dc), lambda i: (0, 0))),
        compiler_params=_params(("arbitrary",)),
    )(d_mg, d_mg, d_mg, p, p, p, cpar, d_q, parts, parts, parts, d_ctx, cs, sn)


def _loss_bwd(t, gain, target, n_ctx, name):
    n, d = t.shape
    tr = _pick(n_ctx, (256, 128))
    first = n_ctx // tr

    def body(t_ref, g_ref, y_ref, dt_ref, loss_ref, dg_ref):
        i = pl.program_id(0)

        @pl.when(i == 0)
        def _():
            loss_ref[...] = jnp.zeros_like(loss_ref)
            dg_ref[...] = jnp.zeros_like(dg_ref)

        @pl.when(i < first)
        def _():
            dt_ref[...] = jnp.zeros_like(dt_ref)

        @pl.when(i >= first)
        def _():
            x = t_ref[...]
            g = g_ref[...]
            r = lax.rsqrt(_rowmean(x * x) + EPS)
            xn = x * r
            err = xn * g - y_ref[...]
            loss_ref[...] += 0.5 * _colsum(_rowmean(err * err))
            dy = err * (1.0 / d)
            dg_ref[0:1, :] += _colsum(dy * xn)
            dxn = dy * g
            dt_ref[...] = r * (dxn - xn * _rowmean(dxn * xn))

    row = pl.BlockSpec((tr, d), lambda i: (i, 0))
    return pl.pallas_call(
        body, name=name,
        out_shape=(jax.ShapeDtypeStruct((n, d), F32), jax.ShapeDtypeStruct((8, 128), F32), jax.ShapeDtypeStruct((8, d), F32)),
        grid=(n // tr,),
        in_specs=[row, pl.BlockSpec((1, d), lambda i: (0, 0)), pl.BlockSpec((tr, d), lambda i: (jnp.maximum(i - first, 0), 0))],
        out_specs=(row, pl.BlockSpec((8, 128), lambda i: (0, 0)), pl.BlockSpec((8, d), lambda i: (0, 0))),
        compiler_params=_params(("arbitrary",)),
    )(t, gain, target)


def _silu16(c16, name):
    def body(c_ref, o_ref):
        v = c_ref[...]
        o_ref[...] = (v * jax.nn.sigmoid(v)).astype(BF16)

    return pl.pallas_call(body, name=name, out_shape=jax.ShapeDtypeStruct(c16.shape, BF16))(c16)


def _cctx_grad(parts, c_ctx, name):
    def body(p_ref, c_ref, o_ref):
        g = _colsum(p_ref[...])
        v = c_ref[...]
        s = jax.nn.sigmoid(v)
        o_ref[...] = g * (s * (1.0 + v * (1.0 - s)))

    return pl.pallas_call(body, name=name, out_shape=jax.ShapeDtypeStruct(c_ctx.shape, F32))(parts, c_ctx)


def _rope_tables(n_ctx, n_tok):
    half = HEAD_DIM // 4
    inv = ROPE_THETA ** (-jnp.arange(0, HEAD_DIM // 2, 2, dtype=F32) / (HEAD_DIM // 2))
    rows = n_tok // GRID_W
    row_pos = jnp.repeat(jnp.arange(rows, dtype=F32), GRID_W)
    col_pos = jnp.tile(jnp.arange(GRID_W, dtype=F32), rows)
    ang_r, ang_c = row_pos[:, None] * inv[None, :], col_pos[:, None] * inv[None, :]
    cos = jnp.concatenate([jnp.cos(ang_r), jnp.cos(ang_r), jnp.cos(ang_c), jnp.cos(ang_c)], axis=1)
    sin = jnp.concatenate([-jnp.sin(ang_r), jnp.sin(ang_r), -jnp.sin(ang_c), jnp.sin(ang_c)], axis=1)
    assert cos.shape[1] == 4 * half == HEAD_DIM
    cos = jnp.concatenate([jnp.ones((n_ctx, HEAD_DIM), F32), cos], axis=0)
    sin = jnp.concatenate([jnp.zeros((n_ctx, HEAD_DIM), F32), sin], axis=0)
    return jnp.tile(cos, (1, 2)), jnp.tile(sin, (1, 2))


def kernel(x, c, ctx, c_ctx, w_ada, b_ada, g_norm1, g_norm2, w_in, conv_w, conv_b, sink, g_out_conv, g_out_attn, w_out, w_mlp1, w_mlp2, g_final, loss_target, m_c_ctx, m_w_ada, m_b_ada, m_g_norm1, m_g_norm2, m_w_in, m_conv_w, m_conv_b, m_sink, m_g_out_conv, m_g_out_attn, m_w_out, m_w_mlp1, m_w_mlp2, m_g_final, v_c_ctx, v_w_ada, v_b_ada, v_g_norm1, v_g_norm2, v_w_in, v_conv_w, v_conv_b, v_sink, v_g_out_conv, v_g_out_attn, v_w_out, v_w_mlp1, v_w_mlp2, v_g_final):
    n_lat, d = x.shape[1], x.shape[2]
    n_ctx = ctx.shape[1]
    n = n_ctx + n_lat
    depth = w_in.shape[0]
    dc = d // 2
    hd, kd = dc, N_KV_HEADS * HEAD_DIM
    n_heads = hd // HEAD_DIM
    d_in = 3 * dc + hd + 2 * kd
    cin, c_ada, r_out, c_ff, r_ff = w_in.shape[2], w_ada.shape[2], w_out.shape[1], w_mlp1.shape[2], w_mlp2.shape[1]
    d_ff = N_DEV * c_ff
    cw = conv_w.shape[2]
    assert d_in == N_DEV * cin and n_ctx % BLOCK == 0 and n_lat % BLOCK == 0 and hd % kd == 0
    dev = 4 * lax.axis_index("x") + 2 * lax.axis_index("y") + lax.axis_index("c")

    c_all, conv_w_all = _all_gather([c, conv_w], "gather_cond")
    conv_w_full = jnp.transpose(conv_w_all, (1, 2, 0, 3)).reshape(depth, 3, dc)
    c16 = jnp.concatenate([c_all.reshape(N_DEV, d), jnp.broadcast_to(c_ctx[None, :], (8, d))], axis=0)
    sc16 = _silu16(c16, "silu_cond")

    b_ada_loc = lax.dynamic_index_in_dim(b_ada.reshape(depth, N_DEV, c_ada), dev, axis=1, keepdims=False)
    tn_ada = _pick(c_ada, FEAT)

    def add_bias(acc, i, j, extra, outs):
        outs[0][...] = acc + extra[0][...]

    mod_loc = []
    for l in range(depth):
        mod_loc.append(_matmul(
            sc16, w_ada, dims="nn", shape=(16, c_ada, d), tiles=(16, tn_ada, _pick(d, KDIM)),
            b_spec=_w_spec("nat", l, None, "nn", tn_ada, _pick(d, KDIM)), epilogue=add_bias,
            extras=[(b_ada_loc[l][None, :], pl.BlockSpec((1, tn_ada), lambda i, j, k: (0, j)))], name=f"ada_fwd{l}"))
    (mod_all,) = _all_gather([jnp.stack(mod_loc)], "gather_mod")
    mod_full = jnp.transpose(mod_all, (1, 2, 0, 3)).reshape(depth, 16, N_MOD * d)
    mod_mine = lax.dynamic_index_in_dim(mod_full, dev, axis=1, keepdims=True)
    mod8 = jnp.concatenate([mod_mine, mod_full[:, 8:9], jnp.zeros((depth, 6, N_MOD * d), F32)], axis=1)

    w_in_b = _to_bf16(w_in.reshape(depth * d, cin), "cast_w_in").reshape(depth, d, cin)
    w_out_b = _to_bf16(w_out.reshape(depth * r_out, d), "cast_w_out").reshape(depth, r_out, d)
    w1_b = _to_bf16(w_mlp1.reshape(depth * d, c_ff), "cast_w_mlp1").reshape(depth, d, c_ff)
    w2_b = _to_bf16(w_mlp2.reshape(depth * r_ff, d), "cast_w_mlp2").reshape(depth, r_ff, d)
    (gath0,) = _comm_call([_job_gather_ici([w_in_b, w_out_b], 0)], "gather_w0_ici")
    (gath0,) = _comm_call([_job_gather_d2d(gath0)], "gather_w0_d2d")

    def full_in(g):
        return jnp.transpose(g, (1, 0, 2)).reshape(d, d_in)

    w_in_full, w_out_full = [full_in(gath0[0])], [gath0[1].reshape(d, d)]
    g_w1, w2_full = [], []
    pend_w2 = None

    t = jnp.concatenate([ctx[0], x[0]], axis=0)
    cs, sn = _rope_tables(n_ctx, n_lat)
    tm = _pick(n, TOK)
    tk_d = _pick(d, KDIM)
    tm_res = _pick(n, (1056, 768, 256, 128))
    tn_in = _pick(d_in, (768, 512, 256, 128))

    def resid_epilogue(tile_rows):
        def epi(acc, i, j, extra, outs):
            is_ctx = _row_ids(i, tile_rows) < n_ctx
            outs[0][...] = extra[0][...] + _sel(is_ctx, extra[1]) * acc
            outs[1][...] = acc.astype(BF16)
        return epi

    def sq_relu_epilogue(acc, i, j, extra, outs):
        outs[0][...] = acc.astype(BF16)
        rl = jnp.maximum(acc, 0.0)
        outs[1][...] = (rl * rl).astype(BF16)

    def d_sq_relu_epilogue(acc, i, j, extra, outs):
        outs[0][...] = (acc * (2.0 * jnp.maximum(extra[0][...].astype(F32), 0.0))).astype(BF16)

    saved = []
    for l in range(depth):
        cpar = jnp.concatenate([conv_w_full[l], conv_b[l][None], g_out_conv[l][None], g_out_attn[l][None],
                                jnp.zeros((2, dc), F32)], axis=0)
        more = l + 1 < depth
        h = _norm_mod(t, g_norm1[l][None], mod8[l], 0, n_ctx, f"norm1_{l}")
        jobs = ([_job_gather_d2d(pend_w2)] if pend_w2 is not None else []) + (
            [_job_gather_ici([w_in_b, w_out_b], l + 1)] if more else [])
        p = _matmul(h, w_in_full[l], dims="nn", shape=(n, d_in, d), tiles=(tm, tn_in, tk_d), jobs=jobs, name=f"in_proj{l}")
        if jobs:
            p, job_outs = p
            if pend_w2 is not None:
                w2_full.append(job_outs[0][0].reshape(d_ff, d))
            pend_io = job_outs[-1] if more else None
        qkv = _rope_split(p, cs, sn, dc, hd, kd, f"rope{l}")
        if l == 0:
            ao, (pend_mlp0,) = _attention(qkv, sink[l], n_ctx, hd, kd, f"attn{l}", jobs=[_job_gather_ici([w1_b, w2_b], 0)])
        else:
            ao = _attention(qkv, sink[l], n_ctx, hd, kd, f"attn{l}")
        mg = _mix_fwd(p, ao, cpar, n_ctx, dc, f"mix{l}")
        tn = _pick(d, (512, 256, 128))
        jobs = ([_job_gather_d2d(pend_io)] if more else []) + ([_job_gather_d2d(pend_mlp0)] if l == 0 else [])
        res = _matmul(
            mg, w_out_full[l], dims="nn", shape=(n, d, d), tiles=(tm, tn, tk_d),
            out_dtypes=(F32, BF16), epilogue=resid_epilogue(tm),
            extras=[(t, pl.BlockSpec((tm, tn), lambda i, j, k: (i, j))),
                    (mod8[l], pl.BlockSpec((8, tn), lambda i, j, k, tn=tn: (0, 2 * (d // tn) + j)))],
            jobs=jobs, name=f"out_proj{l}")
        if jobs:
            (t2, z), job_outs = res
            if more:
                w_in_full.append(full_in(job_outs[0][0]))
                w_out_full.append(job_outs[0][1].reshape(d, d))
            if l == 0:
                g_w1.append(job_outs[-1][0])
                w2_full.append(job_outs[-1][1].reshape(d_ff, d))
        else:
            t2, z = res
        h2 = _norm_mod(t2, g_norm2[l][None], mod8[l], 3, n_ctx, f"norm2_{l}")
        tn = _pick(c_ff, FEAT)
        res = _matmul(h2, g_w1[l], dims="nn", shape=(n, d_ff, d), tiles=(tm, tn, tk_d),
                      b_spec=_w_spec("cols", None, c_ff, "nn", tn, tk_d), out_dtypes=(BF16, BF16),
                      epilogue=sq_relu_epilogue, jobs=[_job_gather_ici([w1_b], l + 1)] if more else [], name=f"mlp_up{l}")
        if more:
            (a, s), ((pend_w1,),) = res
        else:
            a, s = res
        tn = _pick(d, FEAT)
        tk = _pick(d_ff, (1024, 512, 256, 128))
        res = _matmul(
            s, w2_full[l], dims="nn", shape=(n, d, d_ff), tiles=(tm_res, tn, tk),
            out_dtypes=(F32, BF16), epilogue=resid_epilogue(tm_res),
            extras=[(t2, pl.BlockSpec((tm_res, tn), lambda i, j, k: (i, j))),
                    (mod8[l], pl.BlockSpec((8, tn), lambda i, j, k, tn=tn: (0, 5 * (d // tn) + j)))],
            jobs=[_job_gather_ici([w2_b], l + 1), _job_gather_d2d([pend_w1])] if more else [], name=f"mlp_down{l}")
        if more:
            (t3, o), (pend_w2, (g_w1_next,)) = res
            g_w1.append(g_w1_next)
        else:
            (t3, o), pend_w2 = res, None
        saved.append((t, h, p, qkv, ao, mg, z, t2, h2, a, s, o, cpar))
        t = t3

    d_t, loss_tile, dg_final = _loss_bwd(t, g_final[None], loss_target[0], n_ctx, "loss")
    loss = lax.psum(loss_tile[0, 0], ("x", "y", "c"))

    buf_in = lax.empty((N_CHIP, depth, d, cin), BF16)
    buf_out = lax.empty((N_CHIP, depth, r_out, d), BF16)
    buf_w1 = lax.empty((N_CHIP, depth, d, c_ff), BF16)
    buf_w2 = lax.empty((N_CHIP, depth, r_ff, d), BF16)
    pend_in, pend_layer = None, None
    tkn = _pick(n, TOK)
    small = [None] * depth
    for l in reversed(range(depth)):
        t_in, h, p, qkv, ao, mg, z, t2, h2, a, s, o, cpar = saved[l]
        dob, dgate2 = _gate_bwd(d_t, o, mod8[l], 5, n_ctx, f"gate2_bwd{l}")
        tm_g = _pick(d_ff, FEAT)
        tn = _pick(d, FEAT)
        gw2 = _matmul(s, dob, dims="tn", shape=(d_ff, d, n), tiles=(tm_g, tn, tkn), name=f"mlp_down_dw{l}")
        gw2 = gw2.reshape(N_DEV, r_ff, d)
        tn = _pick(d_ff, FEAT)
        jobs = [_job_scatter_d2d([gw2])] + ([_job_scatter_ici([pend_in], [buf_in], pend_layer)] if pend_in is not None else [])
        da, job_outs = _matmul(dob, w2_full[l], dims="nt", shape=(n, d_ff, d), tiles=(tm, tn, tk_d),
                               out_dtypes=(BF16,), epilogue=d_sq_relu_epilogue,
                               extras=[(a, pl.BlockSpec((tm, tn), lambda i, j, k: (i, j)))], jobs=jobs, name=f"mlp_down_dx{l}")
        if pend_in is not None:
            (buf_in,) = job_outs[1]
        pair_w2 = _pair_add(gw2, job_outs[0][0], f"pair_w2_{l}")
        tm_g = _pick(d, FEAT)
        tn = _pick(c_ff, FEAT)
        gw1 = _matmul(h2, da, dims="tn", shape=(d, d_ff, n), tiles=(tm_g, tn, tkn),
                      out_specs=(_g_spec("cols", None, c_ff, tm_g, tn),),
                      out_shapes=(jax.ShapeDtypeStruct((N_DEV, d, c_ff), F32),), name=f"mlp_up_dw{l}")
        tn = _pick(d, FEAT)
        tk = _pick(c_ff, (1024, 512, 256, 128))
        dh2, job_outs = _matmul(da, g_w1[l], dims="nt", shape=(n, d, d_ff), tiles=(tm, tn, tk),
                                b_spec=_w_spec("cols", None, c_ff, "nt", tn, tk),
                                jobs=[_job_scatter_d2d([gw1]), _job_scatter_ici([pair_w2], [buf_w2], l)], name=f"mlp_up_dx{l}")
        (buf_w2,) = job_outs[1]
        pair_w1 = _pair_add(gw1, job_outs[0][0], f"pair_w1_{l}")
        d_t2, dss2, dgn2 = _norm_mod_bwd(dh2, t2, d_t, g_norm2[l][None], mod8[l], 3, n_ctx, f"norm2_bwd{l}")
        dzb, dgate1 = _gate_bwd(d_t2, z, mod8[l], 2, n_ctx, f"gate1_bwd{l}")
        tm_g = _pick(d, FEAT)
        tn = _pick(d, FEAT)
        gout = _matmul(mg, dzb, dims="tn", shape=(d, d, n), tiles=(tm_g, tn, tkn), name=f"out_proj_dw{l}")
        gout = gout.reshape(N_DEV, r_out, d)
        d_mg, job_outs = _matmul(dzb, w_out_full[l], dims="nt", shape=(n, d, d), tiles=(tm, tn, tk_d),
                                 jobs=[_job_scatter_d2d([gout])], name=f"out_proj_dx{l}")
        pair_out = _pair_add(gout, job_outs[0][0], f"pair_out_{l}")
        d_q, parts, d_kv_ctx, d_sink, d_goa = _attention_bwd(qkv, sink[l], ao, d_mg, cpar, n_ctx, hd, kd, f"attn_bwd{l}")
        d_p, conv_acc = _mix_bwd(d_mg, p, cpar, d_q, parts, d_kv_ctx, cs, sn, n_ctx, dc, hd, kd, f"mix_bwd{l}")
        tm_g = _pick(d, FEAT)
        tn = _pick(d_in, (1536, 768, 512, 256, 128))
        gin = _matmul(h, d_p, dims="tn", shape=(d, d_in, n), tiles=(tm_g, tn, tkn), name=f"in_proj_dw{l}")
        gin = jnp.transpose(gin.reshape(d, N_DEV, cin), (1, 0, 2))
        tn = _pick(d, FEAT)
        tk = _pick(d_in, (1536, 768, 512, 256, 128))
        dh, job_outs = _matmul(d_p, w_in_full[l], dims="nt", shape=(n, d, d_in), tiles=(tm, tn, tk),
                               jobs=[_job_scatter_d2d([gin]), _job_scatter_ici([pair_w1, pair_out], [buf_w1, buf_out], l)],
                               name=f"in_proj_dx{l}")
        buf_w1, buf_out = job_outs[1]
        pend_in, pend_layer = _pair_add(gin, job_outs[0][0], f"pair_in_{l}"), l
        d_t, dss1, dgn1 = _norm_mod_bwd(dh, t_in, d_t2, g_norm1[l][None], mod8[l], 0, n_ctx, f"norm1_bwd{l}")
        d_mod2 = jnp.concatenate([dss1[0:2], dgate1[0:2], dss2[0:2], dgate2[0:2]], axis=1)
        small[l] = (d_mod2, dgn1[0], dgn2[0], conv_acc, d_sink[0, 0:n_heads], d_goa[0])
    grad_x = d_t[n_ctx:][None]

    def pack(l):
        d_mod2, dgn1, dgn2, conv_acc, d_sink, d_goa = small[l]
        row0 = [d_mod2[0], dgn1, dgn2, conv_acc[3], d_sink, conv_acc[4], d_goa, conv_acc[0:3].reshape(-1)]
        row1 = [d_mod2[1]] + [jnp.zeros_like(v) for v in row0[1:]]
        return jnp.stack([jnp.concatenate(row0), jnp.concatenate(row1)])
    per_layer = N_MOD * d + 2 * d + dc + n_heads + 2 * dc + 3 * dc
    packed = jnp.concatenate([pack(l) for l in range(depth)] +
                             [jnp.stack([dg_final[0], jnp.zeros((d,), F32)])], axis=1)
    f_tot = depth * per_layer + d
    f_pad = -f_tot % 1024
    packed = jnp.pad(packed, ((0, 0), (0, f_pad)))
    (small_all,) = _all_gather([packed], "gather_small")
    small_parts = small_all.reshape(2 * N_DEV, 1, f_tot + f_pad)

    def section(arr, l, off, size):
        return lax.slice_in_dim(arr, l * per_layer + off, l * per_layer + off + size, axis=-1)

    offs = {}
    o_ = 0
    for nm_, sz in (("mod", N_MOD * d), ("gn1", d), ("gn2", d), ("cb", dc), ("sink", n_heads), ("goc", dc), ("goa", dc), ("cw", 3 * dc)):
        offs[nm_] = (o_, sz)
        o_ += sz

    def packw(b_ada_, gn1_, gn2_, cb_, sk_, goc_, goa_, gf_):
        rows = []
        for l in range(depth):
            rows += [b_ada_[l], gn1_[l], gn2_[l], cb_[l], sk_[l], goc_[l], goa_[l], jnp.zeros((3 * dc,), F32)]
        return jnp.pad(jnp.concatenate(rows + [gf_]), (0, f_pad))[None]
    pw = packw(b_ada, g_norm1, g_norm2, conv_b, sink, g_out_conv, g_out_attn, g_final)
    pm = packw(m_b_ada, m_g_norm1, m_g_norm2, m_conv_b, m_sink, m_g_out_conv, m_g_out_attn, m_g_final)
    pv = packw(v_b_ada, v_g_norm1, v_g_norm2, v_conv_b, v_sink, v_g_out_conv, v_g_out_attn, v_g_final)
    sg, sd, sm, sv = _adamw(pw, pm, pv, small_parts, "adamw_small")

    def unpack(arr):
        arr = arr[0]
        out = {}
        for nm_ in ("mod", "gn1", "gn2", "cb", "sink", "goc", "goa", "cw"):
            off, size = offs[nm_]
            out[nm_] = jnp.stack([section(arr, l, off, size) for l in range(depth)])
        out["gf"] = arr[depth * per_layer:depth * per_layer + d]
        return out
    ug, ud, um, uv = unpack(sg), unpack(sd), unpack(sm), unpack(sv)

    cw_grad_full = ug["cw"].reshape(depth, 3, N_DEV, cw)
    cw_grad = lax.dynamic_index_in_dim(cw_grad_full, dev, axis=2, keepdims=False).reshape(1, depth * 3, cw)
    cwg, cwd, cwm, cwv = _adamw(conv_w.reshape(depth * 3, cw), m_conv_w.reshape(depth * 3, cw),
                                v_conv_w.reshape(depth * 3, cw), cw_grad, "adamw_conv_w")
    cw_shape = conv_w.shape

    mod_rows = small_all[:, :, :depth * per_layer].reshape(N_DEV, 2, depth, per_layer)[:, :, :, :N_MOD * d]
    dm16 = jnp.concatenate([mod_rows[:, 0], mod_rows[:, 1]], axis=0)
    dm16 = jnp.transpose(dm16, (1, 0, 2)).reshape(depth, 16, N_DEV, c_ada)
    dm16_loc = lax.dynamic_index_in_dim(dm16, dev, axis=2, keepdims=False)
    gb_ada = lax.empty((depth, d, c_ada), F32)
    dsc_parts = []
    tm_g = _pick(d, FEAT)
    for l in range(depth):
        gb_ada = _matmul(sc16, dm16_loc[l], dims="tn", shape=(d, c_ada, 16), tiles=(tm_g, tn_ada, 16),
                         out_specs=(_g_spec("nat", l, None, tm_g, tn_ada),), out_shapes=(jax.ShapeDtypeStruct(gb_ada.shape, F32),),
                         alias=gb_ada, name=f"ada_dw{l}")
        tn = _pick(d, FEAT)
        tk = _pick(c_ada, KDIM)
        dsc_parts.append(_matmul(dm16_loc[l], w_ada, dims="nt", shape=(16, d, c_ada), tiles=(16, tn, tk),
                                 b_spec=_w_spec("nat", l, None, "nt", tn, tk), name=f"ada_dx{l}"))
    (dsc_all,) = _all_gather([jnp.stack(dsc_parts)[:, 8:16]], "gather_dcond")
    g_cctx = _cctx_grad(dsc_all.reshape(N_DEV * depth * 8, d), c_ctx[None], "c_ctx_grad")
    ccg, ccd, ccm, ccv = _adamw(c_ctx[None], m_c_ctx[None], v_c_ctx[None], g_cctx[None], "adamw_c_ctx")
    adg, add, adm, adv = _adamw(w_ada.reshape(depth * d, c_ada), m_w_ada.reshape(depth * d, c_ada),
                                v_w_ada.reshape(depth * d, c_ada), gb_ada.reshape(1, depth * d, c_ada), "adamw_w_ada")

    ((buf_in,),) = _comm_call([_job_scatter_ici([pend_in], [buf_in], pend_layer)], "rs_tail")
    parts4 = [buf_in, buf_out, buf_w1, buf_w2]
    big = []
    for k, (w_, m_, v_) in enumerate(((w_in, m_w_in, v_w_in), (w_out, m_w_out, v_w_out),
                                      (w_mlp1, m_w_mlp1, v_w_mlp1), (w_mlp2, m_w_mlp2, v_w_mlp2))):
        r2, c2 = w_.shape[0] * w_.shape[1], w_.shape[2]
        res = _adamw(w_.reshape(r2, c2), m_.reshape(r2, c2), v_.reshape(r2, c2), parts4[k].reshape(N_CHIP, r2, c2),
                     f"adamw_big{k}")
        big.append([a_.reshape(w_.shape) for a_ in res])

    def leaf(i):
        return (
            (ccg, ccd, ccm, ccv)[i][0], (adg, add, adm, adv)[i].reshape(w_ada.shape),
            (ug, ud, um, uv)[i]["mod"], (ug, ud, um, uv)[i]["gn1"], (ug, ud, um, uv)[i]["gn2"], big[0][i],
            (cwg, cwd, cwm, cwv)[i].reshape(cw_shape), (ug, ud, um, uv)[i]["cb"], (ug, ud, um, uv)[i]["sink"],
            (ug, ud, um, uv)[i]["goc"], (ug, ud, um, uv)[i]["goa"], big[1][i], big[2][i], big[3][i], (ug, ud, um, uv)[i]["gf"])

    return (loss, grad_x) + leaf(0) + leaf(1) + leaf(2) + leaf(3)
```

```python
import functools

import jax
import jax.numpy as jnp
from jax import lax
from jax.experimental import pallas as pl
from jax.experimental.pallas import tpu as pltpu

HEAD_DIM = 64
N_KV_HEADS = 4
BLOCK = 128
GRID_W = 64
ROPE_THETA = 10000.0
EPS = 1e-6
N_MOD = 6
SCALE = HEAD_DIM ** -0.5
NEG_INF = -1e30
ADAM_LR = 0.001
ADAM_B1 = 0.9
ADAM_B2 = 0.999
ADAM_EPS = 1e-08
ADAM_WD = 0.01
ADAM_STEP = 10
N_DEV = 8
N_CHIP = 4
VMEM_LIMIT_BYTES = 48 * 1024 * 1024
MESH = pl.DeviceIdType.MESH
BF16 = jnp.bfloat16
F32 = jnp.float32
ANY = pl.BlockSpec(memory_space=pl.ANY)


def _pick(dim, prefs):
    for p in prefs:
        if p <= dim and dim % p == 0:
            return p
    return dim


def _params(sem):
    return pltpu.CompilerParams(dimension_semantics=sem, vmem_limit_bytes=VMEM_LIMIT_BYTES)


def _row_ids(i, tr):
    return i * tr + lax.broadcasted_iota(jnp.int32, (tr, 1), 0)


def _colsum(v):
    return jnp.sum(v, axis=0, keepdims=True)


def _rowmean(v):
    return jnp.mean(v, axis=1, keepdims=True)


def _all_gather(xs, name):
    na = len(xs)

    def body(*refs):
        x_refs, o_refs = refs[:na], refs[na:2 * na]
        send_sems, recv_sems, local_sems = refs[2 * na:]
        x, y, c = lax.axis_index("x"), lax.axis_index("y"), lax.axis_index("c")
        me, sibling = (x, y, c), (x, y, 1 - c)
        chips = [(1 - x, y), (x, 1 - y), (1 - x, 1 - y)]

        def slot(a, px, py, pc):
            return o_refs[a].at[4 * px + 2 * py + pc]

        def copy(a, k, block, to, src=None):
            return pltpu.make_async_remote_copy(
                src_ref=slot(a, *block) if src is None else src, dst_ref=slot(a, *block),
                send_sem=send_sems.at[a, k], recv_sem=recv_sems.at[a, k], device_id=to, device_id_type=MESH)

        mine = [pltpu.make_async_copy(x_refs[a], slot(a, *me), local_sems.at[a]) for a in range(na)]
        for cp in mine:
            cp.start()
        first = []
        for a in range(na):
            first.append(copy(a, 0, me, sibling, src=x_refs[a]))
            first += [copy(a, 1 + j, me, (*chip, c), src=x_refs[a]) for j, chip in enumerate(chips)]
        for cp in first:
            cp.start()
        passed = []
        for j, chip in enumerate(chips):
            for a in range(na):
                copy(a, 1 + j, (*chip, c), me).wait_recv()
                fwd = copy(a, 4 + j, (*chip, c), sibling)
                fwd.start()
                passed.append(fwd)
        for a in range(na):
            copy(a, 0, sibling, me).wait_recv()
            for j, chip in enumerate(chips):
                copy(a, 4 + j, (*chip, 1 - c), me).wait_recv()
        for cp in first + passed:
            cp.wait_send()
        for cp in mine:
            cp.wait()

    outs = pl.pallas_call(
        body, name=name,
        out_shape=tuple(jax.ShapeDtypeStruct((N_DEV,) + x.shape, x.dtype) for x in xs),
        in_specs=[ANY] * na, out_specs=tuple([ANY] * na),
        scratch_shapes=[pltpu.SemaphoreType.DMA((na, 7)), pltpu.SemaphoreType.DMA((na, 7)),
                        pltpu.SemaphoreType.DMA((na,))],
    )(*xs)
    return list(outs)


class _Job:
    def __init__(self, ins, outs, alias, sems, copies):
        self.ins, self.outs, self.alias, self.sems, self._copies = ins, outs, alias, sems, copies

    def start(self, in_refs, out_refs, sems):
        local, sends, _ = self._copies(in_refs, out_refs, sems)
        for make in local + sends:
            make().start()

    def wait(self, in_refs, out_refs, sems):
        local, sends, arrivals = self._copies(in_refs, out_refs, sems)
        for make in arrivals:
            make().wait_recv()
        for make in sends:
            make().wait_send()
        for make in local:
            make().wait()


def _other_chips():
    x, y = lax.axis_index("x"), lax.axis_index("y")
    return [(1 - x, y), (x, 1 - y), (1 - x, 1 - y)]


def _remote(src, dst, send_sem, recv_sem, to):
    return functools.partial(pltpu.make_async_remote_copy, src_ref=src, dst_ref=dst, send_sem=send_sem, recv_sem=recv_sem,
                             device_id=to, device_id_type=MESH)


def _local(src, dst, sem):
    return functools.partial(pltpu.make_async_copy, src, dst, sem)


def _job_gather_ici(xs, layer):
    na = len(xs)

    def copies(in_refs, out_refs, sems):
        send, recv, loc = sems
        x, y, c = lax.axis_index("x"), lax.axis_index("y"), lax.axis_index("c")
        me = 4 * x + 2 * y + c
        local, sends, arrivals = [], [], []
        for a in range(na):
            src = in_refs[a].at[layer]
            local.append(_local(src, out_refs[a].at[me], loc.at[a]))
            for j, (px, py) in enumerate(_other_chips()):
                sends.append(_remote(src, out_refs[a].at[me], send.at[a, j], recv.at[a, j], (px, py, c)))
                arrivals.append(_remote(src, out_refs[a].at[4 * px + 2 * py + c], send.at[a, j], recv.at[a, j], (px, py, c)))
        return local, sends, arrivals

    outs = [jax.ShapeDtypeStruct((N_DEV,) + x.shape[1:], x.dtype) for x in xs]
    sems = [pltpu.SemaphoreType.DMA((na, 3)), pltpu.SemaphoreType.DMA((na, 3)), pltpu.SemaphoreType.DMA((na,))]
    return _Job(list(xs), outs, {}, sems, copies)


def _job_gather_d2d(gs):
    na = len(gs)

    def copies(in_refs, out_refs, sems):
        send, recv = sems
        x, y, c = lax.axis_index("x"), lax.axis_index("y"), lax.axis_index("c")
        sends, arrivals = [], []
        for a in range(na):
            for k in range(N_CHIP):
                mine, theirs = 2 * k + c, 2 * k + (1 - c)
                sends.append(_remote(in_refs[a].at[mine], out_refs[a].at[mine], send.at[a, k], recv.at[a, k], (x, y, 1 - c)))
                arrivals.append(_remote(in_refs[a].at[theirs], out_refs[a].at[theirs], send.at[a, k], recv.at[a, k], (x, y, 1 - c)))
        return [], sends, arrivals

    outs = [jax.ShapeDtypeStruct(g.shape, g.dtype) for g in gs]
    sems = [pltpu.SemaphoreType.DMA((na, N_CHIP)), pltpu.SemaphoreType.DMA((na, N_CHIP))]
    return _Job(list(gs), outs, {a: a for a in range(na)}, sems, copies)


def _job_scatter_d2d(gs):
    na = len(gs)

    def copies(in_refs, out_refs, sems):
        send, recv = sems
        x, y, c = lax.axis_index("x"), lax.axis_index("y"), lax.axis_index("c")
        sends, arrivals = [], []
        for a in range(na):
            for k in range(N_CHIP):
                cp = _remote(in_refs[a].at[2 * k + (1 - c)], out_refs[a].at[k], send.at[a, k], recv.at[a, k], (x, y, 1 - c))
                sends.append(cp)
                arrivals.append(cp)
        return [], sends, arrivals

    outs = [jax.ShapeDtypeStruct((N_CHIP,) + g.shape[1:], g.dtype) for g in gs]
    sems = [pltpu.SemaphoreType.DMA((na, N_CHIP)), pltpu.SemaphoreType.DMA((na, N_CHIP))]
    return _Job(list(gs), outs, {}, sems, copies)


def _job_scatter_ici(pairs, bufs, layer):
    na = len(pairs)

    def copies(in_refs, out_refs, sems):
        send, recv, loc = sems
        x, y, c = lax.axis_index("x"), lax.axis_index("y"), lax.axis_index("c")
        my_chip = 2 * x + y
        local, sends, arrivals = [], [], []
        for a in range(na):
            local.append(_local(in_refs[a].at[my_chip], out_refs[a].at[my_chip, layer], loc.at[a]))
            for j, (px, py) in enumerate(_other_chips()):
                src = in_refs[a].at[2 * px + py]
                sends.append(_remote(src, out_refs[a].at[my_chip, layer], send.at[a, j], recv.at[a, j], (px, py, c)))
                arrivals.append(_remote(src, out_refs[a].at[2 * px + py, layer], send.at[a, j], recv.at[a, j], (px, py, c)))
        return local, sends, arrivals

    outs = [jax.ShapeDtypeStruct(b.shape, b.dtype) for b in bufs]
    sems = [pltpu.SemaphoreType.DMA((na, 3)), pltpu.SemaphoreType.DMA((na, 3)), pltpu.SemaphoreType.DMA((na,))]
    return _Job(list(pairs) + list(bufs), outs, {na + a: a for a in range(na)}, sems, copies)


def _split_jobs(jobs, in_refs, out_refs, sem_refs):
    out, i0, o0, s0 = [], 0, 0, 0
    for jb in jobs:
        out.append((jb, in_refs[i0:i0 + len(jb.ins)], out_refs[o0:o0 + len(jb.outs)], sem_refs[s0:s0 + len(jb.sems)]))
        i0, o0, s0 = i0 + len(jb.ins), o0 + len(jb.outs), s0 + len(jb.sems)
    return out


def _comm_call(jobs, name):
    n_in = sum(len(jb.ins) for jb in jobs)
    n_out = sum(len(jb.outs) for jb in jobs)

    def body(*refs):
        parts = _split_jobs(jobs, refs[:n_in], refs[n_in:n_in + n_out], refs[n_in + n_out:])
        for jb, i_r, o_r, s_r in parts:
            jb.start(i_r, o_r, s_r)
        for jb, i_r, o_r, s_r in parts:
            jb.wait(i_r, o_r, s_r)

    io_alias, i0, o0 = {}, 0, 0
    for jb in jobs:
        io_alias.update({i0 + a: o0 + b for a, b in jb.alias.items()})
        i0, o0 = i0 + len(jb.ins), o0 + len(jb.outs)
    outs = pl.pallas_call(
        body, name=name, out_shape=tuple(s for jb in jobs for s in jb.outs),
        in_specs=[ANY] * n_in, out_specs=tuple([ANY] * n_out),
        scratch_shapes=[s for jb in jobs for s in jb.sems], input_output_aliases=io_alias,
    )(*[v for jb in jobs for v in jb.ins])
    res, o0 = [], 0
    for jb in jobs:
        res.append(list(outs[o0:o0 + len(jb.outs)]))
        o0 += len(jb.outs)
    return res


def _to_bf16(x2d, name):
    r, c = x2d.shape
    tr = _pick(r, (512, 256, 128, 64, 32, 16))

    def body(x_ref, o_ref):
        o_ref[...] = x_ref[...].astype(BF16)

    return pl.pallas_call(
        body, name=name, out_shape=jax.ShapeDtypeStruct((r, c), BF16), grid=(r // tr,),
        in_specs=[pl.BlockSpec((tr, c), lambda i: (i, 0))], out_specs=pl.BlockSpec((tr, c), lambda i: (i, 0)),
        compiler_params=_params(("parallel",)),
    )(x2d)


def _pair_add(own8, got4, name):
    _, r, c = own8.shape
    tr = _pick(r, (512, 256, 128, 64, 32, 16))
    core = lax.axis_index("c").astype(jnp.int32).reshape(1)

    def body(c_ref, a_ref, b_ref, o_ref):
        o_ref[...] = (a_ref[...] + b_ref[...]).astype(BF16)

    return pl.pallas_call(
        body, name=name, out_shape=jax.ShapeDtypeStruct((N_CHIP, r, c), BF16),
        grid_spec=pltpu.PrefetchScalarGridSpec(
            num_scalar_prefetch=1, grid=(N_CHIP, r // tr),
            in_specs=[pl.BlockSpec((None, tr, c), lambda k, i, cr: (2 * k + cr[0], i, 0)),
                      pl.BlockSpec((None, tr, c), lambda k, i, cr: (k, i, 0))],
            out_specs=pl.BlockSpec((None, tr, c), lambda k, i, cr: (k, i, 0))),
        compiler_params=_params(("parallel", "parallel")),
    )(core, own8, got4)


def _adamw(w, m, v, parts, name):
    r, c = w.shape
    n_parts = parts.shape[0]
    tr = _pick(r, (256, 128, 64, 32, 16, 8))
    tc = _pick(c, (1024, 512, 256, 128))

    def body(w_ref, m_ref, v_ref, p_ref, g_ref, d_ref, nm_ref, nv_ref):
        g = p_ref[0].astype(F32)
        for k in range(1, n_parts):
            g = g + p_ref[k].astype(F32)
        wv = w_ref[...]
        nm = ADAM_B1 * m_ref[...] + (1.0 - ADAM_B1) * g
        nv = ADAM_B2 * v_ref[...] + (1.0 - ADAM_B2) * (g * g)
        m_hat = nm / (1.0 - ADAM_B1 ** ADAM_STEP)
        v_hat = nv / (1.0 - ADAM_B2 ** ADAM_STEP)
        g_ref[...] = g
        d_ref[...] = -ADAM_LR * (m_hat / (jnp.sqrt(v_hat) + ADAM_EPS) + ADAM_WD * wv)
        nm_ref[...] = nm
        nv_ref[...] = nv

    tile = pl.BlockSpec((tr, tc), lambda i, j: (i, j))
    sh = jax.ShapeDtypeStruct((r, c), F32)
    return pl.pallas_call(
        body, name=name, out_shape=(sh, sh, sh, sh), grid=(r // tr, c // tc),
        in_specs=[tile, tile, tile, pl.BlockSpec((n_parts, tr, tc), lambda i, j: (0, i, j))],
        out_specs=(tile, tile, tile, tile),
        compiler_params=_params(("parallel", "parallel")),
    )(w, m, v, parts)


def _matmul(a, b, *, dims, shape, tiles, b_spec=None, out_specs=None, out_shapes=None, out_dtypes=(F32,),
            epilogue=None, extras=(), alias=None, jobs=(), name):
    m_dim, n_dim, k_dim = shape
    tm, tn, tk = tiles
    assert m_dim % tm == 0 and n_dim % tn == 0 and k_dim % tk == 0, (name, shape, tiles)
    nk = k_dim // tk
    n_extra = len(extras)
    n_alias = 0 if alias is None else 1
    n_out = len(out_dtypes)
    if dims == "tn":
        a_spec = pl.BlockSpec((tk, tm), lambda i, j, k: (k, i))
        contract = (((0,), (0,)), ((), ()))
    else:
        a_spec = pl.BlockSpec((tm, tk), lambda i, j, k: (i, k))
        contract = (((1,), (1,)), ((), ())) if dims == "nt" else (((1,), (0,)), ((), ()))
    if b_spec is None:
        b_spec = (pl.BlockSpec((tn, tk), lambda i, j, k: (j, k)) if dims == "nt"
                  else pl.BlockSpec((tk, tn), lambda i, j, k: (k, j)))
    if out_specs is None:
        out_specs = tuple(pl.BlockSpec((tm, tn), lambda i, j, k: (i, j)) for _ in range(n_out))
    if out_shapes is None:
        out_shapes = tuple(jax.ShapeDtypeStruct((m_dim, n_dim), d) for d in out_dtypes)

    n_local_in = 2 + n_extra + n_alias
    n_job_in = sum(len(jb.ins) for jb in jobs)
    n_job_out = sum(len(jb.outs) for jb in jobs)
    n_acc = 1 if nk > 1 else 0
    grid = (m_dim // tm, n_dim // tn, nk)

    def body(*refs):
        a_ref, b_ref = refs[0], refs[1]
        extra_refs = refs[2:2 + n_extra]
        o0 = n_local_in + n_job_in
        out_refs = refs[o0:o0 + n_out]
        s0 = o0 + n_out + n_job_out
        job_parts = _split_jobs(jobs, refs[n_local_in:o0], refs[o0 + n_out:s0], refs[s0 + n_acc:])
        i, j, k = pl.program_id(0), pl.program_id(1), pl.program_id(2)

        if jobs:
            @pl.when((i == 0) & (j == 0) & (k == 0))
            def _():
                for jb, i_r, o_r, s_r in job_parts:
                    jb.start(i_r, o_r, s_r)

        def finish(acc):
            if epilogue is None:
                out_refs[0][...] = acc.astype(out_refs[0].dtype)
            else:
                epilogue(acc, i, j, extra_refs, out_refs)

        def product():
            return lax.dot_general(a_ref[...].astype(BF16), b_ref[...].astype(BF16), contract, preferred_element_type=F32)

        if nk == 1:
            finish(product())
        else:
            acc_ref = refs[s0]

            @pl.when(k == 0)
            def _():
                acc_ref[...] = jnp.zeros_like(acc_ref)

            acc_ref[...] += product()

            @pl.when(k == nk - 1)
            def _():
                finish(acc_ref[...])

        if jobs:
            @pl.when((i == grid[0] - 1) & (j == grid[1] - 1) & (k == nk - 1))
            def _():
                for jb, i_r, o_r, s_r in job_parts:
                    jb.wait(i_r, o_r, s_r)

    ins = [a, b] + [e[0] for e in extras]
    in_specs = [a_spec, b_spec] + [e[1] for e in extras]
    io_alias = {}
    if alias is not None:
        ins.append(alias)
        in_specs.append(ANY)
        io_alias = {len(ins) - 1: 0}
    all_out_shapes, all_out_specs = list(out_shapes), list(out_specs)
    for jb in jobs:
        io_alias.update({len(ins) + a_: len(all_out_shapes) + b_ for a_, b_ in jb.alias.items()})
        ins += jb.ins
        in_specs += [ANY] * len(jb.ins)
        all_out_shapes += jb.outs
        all_out_specs += [ANY] * len(jb.outs)
    scratch = ([pltpu.VMEM((tm, tn), F32)] if nk > 1 else []) + [s for jb in jobs for s in jb.sems]
    outs = pl.pallas_call(
        body, name=name, out_shape=tuple(all_out_shapes), grid=grid,
        in_specs=in_specs, out_specs=tuple(all_out_specs), scratch_shapes=scratch,
        input_output_aliases=io_alias,
        compiler_params=_params(("arbitrary",) * 3 if jobs else ("parallel", "parallel", "arbitrary")),
    )(*ins)
    main = outs[0] if n_out == 1 else tuple(outs[:n_out])
    if not jobs:
        return main
    job_outs, o0 = [], n_out
    for jb in jobs:
        job_outs.append(list(outs[o0:o0 + len(jb.outs)]))
        o0 += len(jb.outs)
    return main, job_outs


def _lead(layer, block, index):
    if layer is None:
        return pl.BlockSpec(block, index)
    return pl.BlockSpec((None,) + block, lambda i, j, k: (layer,) + index(i, j, k))


def _w_spec(kind, layer, per, dims, tn, tk):
    if kind == "nat":
        if dims == "nn":
            return _lead(layer, (tk, tn), lambda i, j, k: (k, j))
        return _lead(layer, (tn, tk), lambda i, j, k: (j, k))
    if dims == "nn":
        q = per // tn
        return _lead(layer, (None, tk, tn), lambda i, j, k: (j // q, k, j % q))
    q = per // tk
    return _lead(layer, (None, tn, tk), lambda i, j, k: (k // q, j, k % q))


def _g_spec(kind, layer, per, tm, tn):
    if kind == "nat":
        return _lead(layer, (tm, tn), lambda i, j, k: (i, j))
    q = per // tn
    return _lead(layer, (None, tm, tn), lambda i, j, k: (j // q, i, j % q))


TOK = (1408, 768, 256, 128)
FEAT = (1024, 512, 256, 128)
KDIM = (2048, 1536, 1024, 512, 256, 128)


def _sel(is_ctx, ref):
    return jnp.where(is_ctx, ref[1:2, :], ref[0:1, :])


def _row_tile(n, n_ctx):
    tr = _pick(n_ctx, (256, 128))
    assert n % tr == 0 and n_ctx % tr == 0
    return tr


def _add_by_segment(acc_ref, cols, ctx_tile, v):
    zero = jnp.zeros_like(v)
    acc_ref[0:1, cols] += jnp.where(ctx_tile, zero, v)
    acc_ref[1:2, cols] += jnp.where(ctx_tile, v, zero)


def _norm_mod(t, gain, mod8, shift_k, n_ctx, name):
    n, d = t.shape
    tr = _row_tile(n, n_ctx)

    def body(t_ref, g_ref, sh_ref, sc_ref, o_ref):
        x = t_ref[...]
        r = lax.rsqrt(_rowmean(x * x) + EPS)
        y = (x * r) * g_ref[...]
        ctx_tile = pl.program_id(0) * tr < n_ctx
        o_ref[...] = (y * (1.0 + _sel(ctx_tile, sc_ref)) + _sel(ctx_tile, sh_ref)).astype(BF16)

    return pl.pallas_call(
        body, name=name, out_shape=jax.ShapeDtypeStruct((n, d), BF16), grid=(n // tr,),
        in_specs=[pl.BlockSpec((tr, d), lambda i: (i, 0)), pl.BlockSpec((1, d), lambda i: (0, 0)),
                  pl.BlockSpec((8, d), lambda i: (0, shift_k)), pl.BlockSpec((8, d), lambda i: (0, shift_k + 1))],
        out_specs=pl.BlockSpec((tr, d), lambda i: (i, 0)),
        compiler_params=_params(("parallel",)),
    )(t, gain, mod8, mod8)


def _norm_mod_bwd(dh, t, d_res, gain, mod8, shift_k, n_ctx, name, gated=None):
    n, d = t.shape
    tr = _row_tile(n, n_ctx)

    def body(*refs):
        if gated is None:
            dh_ref, t_ref, dr_ref, g_ref, sc_ref, dt_ref, dss_ref, dg_ref = refs
        else:
            dh_ref, t_ref, dr_ref, g_ref, sc_ref, br_ref, gt_ref, dt_ref, dss_ref, dg_ref, dob_ref, dgate_ref = refs
        i = pl.program_id(0)

        @pl.when(i == 0)
        def _():
            dss_ref[...] = jnp.zeros_like(dss_ref)
            dg_ref[...] = jnp.zeros_like(dg_ref)
            if gated is not None:
                dgate_ref[...] = jnp.zeros_like(dgate_ref)

        x = t_ref[...]
        r = lax.rsqrt(_rowmean(x * x) + EPS)
        xn = x * r
        g = g_ref[...]
        y = xn * g
        dhv = dh_ref[...]
        ctx_tile = i * tr < n_ctx
        _add_by_segment(dss_ref, slice(0, d), ctx_tile, _colsum(dhv))
        _add_by_segment(dss_ref, slice(d, 2 * d), ctx_tile, _colsum(dhv * y))
        dy = dhv * (1.0 + _sel(ctx_tile, sc_ref))
        dg_ref[0:1, :] += _colsum(dy * xn)
        dxn = dy * g
        d_t = dr_ref[...] + r * (dxn - xn * _rowmean(dxn * xn))
        dt_ref[...] = d_t
        if gated is not None:
            dob_ref[...] = (d_t * _sel(ctx_tile, gt_ref)).astype(BF16)
            _add_by_segment(dgate_ref, slice(None), ctx_tile, _colsum(d_t * br_ref[...].astype(F32)))

    row = pl.BlockSpec((tr, d), lambda i: (i, 0))
    acc = pl.BlockSpec((8, d), lambda i: (0, 0))
    ins = [dh, t, d_res, gain, mod8]
    in_specs = [row, row, row, pl.BlockSpec((1, d), lambda i: (0, 0)), pl.BlockSpec((8, d), lambda i: (0, shift_k + 1))]
    out_shape = [jax.ShapeDtypeStruct((n, d), F32), jax.ShapeDtypeStruct((8, 2 * d), F32), jax.ShapeDtypeStruct((8, d), F32)]
    out_specs = [row, pl.BlockSpec((8, 2 * d), lambda i: (0, 0)), acc]
    if gated is not None:
        branch, gate_mod8, gate_k = gated
        ins += [branch, gate_mod8]
        in_specs += [row, pl.BlockSpec((8, d), lambda i: (0, gate_k))]
        out_shape += [jax.ShapeDtypeStruct((n, d), BF16), jax.ShapeDtypeStruct((8, d), F32)]
        out_specs += [row, acc]
    return pl.pallas_call(
        body, name=name, out_shape=tuple(out_shape), grid=(n // tr,), in_specs=in_specs, out_specs=tuple(out_specs),
        compiler_params=_params(("arbitrary",)),
    )(*ins)


def _gate_bwd(d_t, branch, mod8, gate_k, n_ctx, name):
    n, d = d_t.shape
    tr = _row_tile(n, n_ctx)

    def body(dt_ref, o_ref, gt_ref, dob_ref, dgate_ref):
        i = pl.program_id(0)

        @pl.when(i == 0)
        def _():
            dgate_ref[...] = jnp.zeros_like(dgate_ref)

        dv = dt_ref[...]
        ctx_tile = i * tr < n_ctx
        dob_ref[...] = (dv * _sel(ctx_tile, gt_ref)).astype(BF16)
        _add_by_segment(dgate_ref, slice(None), ctx_tile, _colsum(dv * o_ref[...].astype(F32)))

    row = pl.BlockSpec((tr, d), lambda i: (i, 0))
    return pl.pallas_call(
        body, name=name, out_shape=(jax.ShapeDtypeStruct((n, d), BF16), jax.ShapeDtypeStruct((8, d), F32)),
        grid=(n // tr,),
        in_specs=[row, row, pl.BlockSpec((8, d), lambda i: (0, gate_k))],
        out_specs=(row, pl.BlockSpec((8, d), lambda i: (0, 0))),
        compiler_params=_params(("arbitrary",)),
    )(d_t, branch, mod8)


def _swap16(v):
    w = v.shape[1]
    lane = lax.broadcasted_iota(jnp.int32, v.shape, 1)
    return jnp.where((lane % 32) < 16, pltpu.roll(v, w - 16, 1), pltpu.roll(v, 16, 1))


def _rope(v, cs, sn, sign):
    reps = v.shape[1] // 128
    c = jnp.tile(cs, (1, reps)) if reps > 1 else cs
    s = jnp.tile(sn, (1, reps)) if reps > 1 else sn
    return v * c + sign * (_swap16(v) * s)


def _rope_split(p, cs, sn, dc, hd, kd, name):
    n, d_in = p.shape
    tr = _pick(n, (256, 128))
    q0 = 3 * dc

    def body(p_ref, cs_ref, sn_ref, o_ref):
        cs_v, sn_v = cs_ref[...], sn_ref[...]
        q = _rope(p_ref[:, q0:q0 + hd], cs_v, sn_v, 1.0) * SCALE
        k = _rope(p_ref[:, q0 + hd:q0 + hd + kd], cs_v, sn_v, 1.0)
        v = p_ref[:, q0 + hd + kd:q0 + hd + 2 * kd]
        o_ref[...] = jnp.concatenate([q, k, v], axis=1).astype(BF16)

    return pl.pallas_call(
        body, name=name, out_shape=jax.ShapeDtypeStruct((n, hd + 2 * kd), BF16), grid=(n // tr,),
        in_specs=[pl.BlockSpec((tr, d_in), lambda i: (i, 0)), pl.BlockSpec((tr, 128), lambda i: (i, 0)),
                  pl.BlockSpec((tr, 128), lambda i: (i, 0))],
        out_specs=pl.BlockSpec((tr, hd + 2 * kd), lambda i: (i, 0)),
        compiler_params=_params(("parallel",)),
    )(p, cs, sn)


def _attn_specs(nb, n_ctx, hd, kd):
    kci = hd // kd
    specs = [pl.BlockSpec((BLOCK, hd), lambda b: (b, 0)),
             pl.BlockSpec((n_ctx, kd), lambda b: (0, kci)), pl.BlockSpec((n_ctx, kd), lambda b: (0, kci + 1))]
    for col in (kci, kci + 1):
        specs.append(pl.BlockSpec((BLOCK, kd), lambda b, col=col: (jnp.maximum(b - 1, 0), col)))
        specs.append(pl.BlockSpec((BLOCK, kd), lambda b, col=col: (b, col)))
        specs.append(pl.BlockSpec((BLOCK, kd), lambda b, col=col: (jnp.minimum(b + 1, nb - 1), col)))
    return specs


def _band_valid(b, group, n_ctx, n):
    q_pos = b * BLOCK + lax.broadcasted_iota(jnp.int32, (group * BLOCK, 1), 0) % BLOCK
    k_pos = (b - 1) * BLOCK + lax.broadcasted_iota(jnp.int32, (1, 3 * BLOCK), 1)
    return (jnp.abs(k_pos - q_pos) <= BLOCK) & (k_pos >= n_ctx) & (k_pos < n) & (q_pos >= n_ctx)


NT = (((1,), (1,)), ((), ()))
NN = (((1,), (0,)), ((), ()))
TN = (((0,), (0,)), ((), ()))


def _dot(a, b, dn):
    return lax.dot_general(a, b, dn, preferred_element_type=F32)


def _keys_of_block(b, kv_refs, group, n_ctx, n):
    kc_ref, vc_ref, k0, k1, k2, v0, v1, v2 = kv_refs
    bias = jnp.where(_band_valid(b, group, n_ctx, n), 0.0, NEG_INF)
    kcat = jnp.concatenate([kc_ref[...], k0[...], k1[...], k2[...]], axis=0)
    vcat = jnp.concatenate([vc_ref[...], v0[...], v1[...], v2[...]], axis=0)
    return bias, kcat, vcat


def _stack_heads(v, h, group):
    return jnp.concatenate([v[:, (h * group + j) * HEAD_DIM:(h * group + j + 1) * HEAD_DIM] for j in range(group)], axis=0)


def _scores(qg, keys, bias, n_ctx):
    s = _dot(qg, keys, NT)
    return jnp.concatenate([s[:, :n_ctx], s[:, n_ctx:] + bias], axis=1)


def _softmax_sink(s, sink_ref, h, group):
    snk = jnp.concatenate([jnp.full((BLOCK, 1), sink_ref[h * group + j], F32) for j in range(group)], axis=0)
    m = jnp.maximum(jnp.max(s, axis=1, keepdims=True), snk)
    e, e_s = jnp.exp(s - m), jnp.exp(snk - m)
    inv = 1.0 / (jnp.sum(e, axis=1, keepdims=True) + e_s)
    return e * inv, e_s * inv


def _attention(qkv, sink, n_ctx, hd, kd, name, jobs=()):
    n = qkv.shape[0]
    nb = n // BLOCK
    n_kv = kd // HEAD_DIM
    group = hd // kd
    n_job_in = sum(len(jb.ins) for jb in jobs)
    n_job_out = sum(len(jb.outs) for jb in jobs)

    def body(*refs):
        q_ref, kc_ref, vc_ref, k0, k1, k2, v0, v1, v2, sink_ref = refs[:10]
        o_ref = refs[10 + n_job_in]
        job_parts = _split_jobs(jobs, refs[10:10 + n_job_in], refs[11 + n_job_in:11 + n_job_in + n_job_out],
                                refs[11 + n_job_in + n_job_out:])
        b = pl.program_id(0)

        if jobs:
            @pl.when(b == 0)
            def _():
                for jb, i_r, o_r, s_r in job_parts:
                    jb.start(i_r, o_r, s_r)

        bias, kcat, vcat = _keys_of_block(b, (kc_ref, vc_ref, k0, k1, k2, v0, v1, v2), group, n_ctx, n)
        q = q_ref[...]

        def scores(h):
            return _scores(_stack_heads(q, h, group), kcat[:, h * HEAD_DIM:(h + 1) * HEAD_DIM], bias, n_ctx)

        heads, s_next = [], scores(0)
        for h in range(n_kv):
            s = s_next
            if h + 1 < n_kv:
                s_next = scores(h + 1)
            p, _ = _softmax_sink(s, sink_ref, h, group)
            o = _dot(p.astype(BF16), vcat[:, h * HEAD_DIM:(h + 1) * HEAD_DIM], NN)
            heads += [o[j * BLOCK:(j + 1) * BLOCK, :] for j in range(group)]
        o_ref[...] = jnp.concatenate(heads, axis=1)

        if jobs:
            @pl.when(b == nb - 1)
            def _():
                for jb, i_r, o_r, s_r in job_parts:
                    jb.wait(i_r, o_r, s_r)

    io_alias, i0, o0 = {}, 10, 1
    for jb in jobs:
        io_alias.update({i0 + a_: o0 + b_ for a_, b_ in jb.alias.items()})
        i0, o0 = i0 + len(jb.ins), o0 + len(jb.outs)
    outs = pl.pallas_call(
        body, name=name, grid=(nb,),
        out_shape=(jax.ShapeDtypeStruct((n, hd), F32),) + tuple(s for jb in jobs for s in jb.outs),
        in_specs=_attn_specs(nb, n_ctx, hd, kd) + [pl.BlockSpec(memory_space=pltpu.SMEM)] + [ANY] * n_job_in,
        out_specs=(pl.BlockSpec((BLOCK, hd), lambda b: (b, 0)),) + tuple([ANY] * n_job_out),
        scratch_shapes=[s for jb in jobs for s in jb.sems], input_output_aliases=io_alias,
        compiler_params=_params(("arbitrary",) if jobs else ("parallel",)),
    )(qkv, qkv, qkv, qkv, qkv, qkv, qkv, qkv, qkv, sink, *[v for jb in jobs for v in jb.ins])
    if not jobs:
        return outs[0]
    job_outs, o0 = [], 1
    for jb in jobs:
        job_outs.append(list(outs[o0:o0 + len(jb.outs)]))
        o0 += len(jb.outs)
    return outs[0], job_outs


def _attention_bwd(qkv, sink, ao, d_mg, cpar, n_ctx, hd, kd, name):
    n = qkv.shape[0]
    nb = n // BLOCK
    n_kv = kd // HEAD_DIM
    group = hd // kd
    n_heads = n_kv * group

    def body(q_ref, kc_ref, vc_ref, k0, k1, k2, v0, v1, v2, sink_ref, ao_ref, dmg_ref, cp_ref,
             dq_ref, part_ref, dctx_ref, dsink_ref, dgain_ref):
        b = pl.program_id(0)

        @pl.when(b == 0)
        def _():
            dctx_ref[...] = jnp.zeros_like(dctx_ref)
            dsink_ref[...] = jnp.zeros_like(dsink_ref)
            dgain_ref[...] = jnp.zeros_like(dgain_ref)

        ao_v = ao_ref[...]
        ra = lax.rsqrt(_rowmean(ao_v * ao_v) + EPS)
        an = ao_v * ra
        dmg = dmg_ref[...]
        dgain_ref[0:1, :] += _colsum(dmg * an)
        d_an = dmg * cp_ref[5:6, :]
        d_ao = (ra * (d_an - an * _rowmean(d_an * an))).astype(BF16)

        bias, kcat, vcat = _keys_of_block(b, (kc_ref, vc_ref, k0, k1, k2, v0, v1, v2), group, n_ctx, n)
        q = q_ref[...]
        lane = lax.broadcasted_iota(jnp.int32, (1, 128), 1)
        dsink_row = jnp.zeros((1, 128), F32)

        def first_half(h):
            hs = slice(h * HEAD_DIM, (h + 1) * HEAD_DIM)
            qg, dog = _stack_heads(q, h, group), _stack_heads(d_ao, h, group)
            return qg, dog, _scores(qg, kcat[:, hs], bias, n_ctx), _dot(dog, vcat[:, hs], NT)

        dq_heads, dk, dv = [], [], []
        nxt = first_half(0)
        for h in range(n_kv):
            qg, dog, s, d_p = nxt
            if h + 1 < n_kv:
                nxt = first_half(h + 1)
            p, p_s = _softmax_sink(s, sink_ref, h, group)
            delta = jnp.sum(p * d_p, axis=1, keepdims=True)
            ds = (p * (d_p - delta)).astype(BF16)
            psd = p_s * delta
            for j in range(group):
                val = -jnp.sum(psd[j * BLOCK:(j + 1) * BLOCK, :], axis=0, keepdims=True)
                dsink_row = dsink_row + jnp.where(lane == h * group + j, val, 0.0)
            dq = _dot(ds, kcat[:, h * HEAD_DIM:(h + 1) * HEAD_DIM], NN) * SCALE
            dq_heads += [dq[j * BLOCK:(j + 1) * BLOCK, :] for j in range(group)]
            dk.append(_dot(ds, qg, TN))
            dv.append(_dot(p.astype(BF16), dog, TN))
        dq_ref[...] = jnp.concatenate(dq_heads, axis=1)
        d_kv = jnp.concatenate(dk + dv, axis=1)
        dctx_ref[...] += d_kv[:n_ctx]
        for j in range(3):
            part_ref[j] = d_kv[n_ctx + j * BLOCK:n_ctx + (j + 1) * BLOCK, :]
        dsink_ref[0:1, :] += dsink_row

    assert n_heads <= 128
    out_shape = (jax.ShapeDtypeStruct((n, hd), F32), jax.ShapeDtypeStruct((nb, 3, BLOCK, 2 * kd), F32),
                 jax.ShapeDtypeStruct((n_ctx, 2 * kd), F32), jax.ShapeDtypeStruct((8, 128), F32),
                 jax.ShapeDtypeStruct((8, hd), F32))
    return pl.pallas_call(
        body, name=name, out_shape=out_shape, grid=(nb,),
        in_specs=_attn_specs(nb, n_ctx, hd, kd) + [
            pl.BlockSpec(memory_space=pltpu.SMEM), pl.BlockSpec((BLOCK, hd), lambda b: (b, 0)),
            pl.BlockSpec((BLOCK, hd), lambda b: (b, 1)), pl.BlockSpec((8, hd), lambda b: (0, 0))],
        out_specs=(pl.BlockSpec((BLOCK, hd), lambda b: (b, 0)),
                   pl.BlockSpec((None, 3, BLOCK, 2 * kd), lambda b: (b, 0, 0, 0)),
                   pl.BlockSpec((n_ctx, 2 * kd), lambda b: (0, 0)), pl.BlockSpec((8, 128), lambda b: (0, 0)),
                   pl.BlockSpec((8, hd), lambda b: (0, 0))),
        compiler_params=_params(("arbitrary",)),
    )(qkv, qkv, qkv, qkv, qkv, qkv, qkv, qkv, qkv, sink, ao, d_mg, cpar)


def _halo_specs(tr, n, width, col=0):
    q = tr // 8
    return [pl.BlockSpec((8, width), lambda i: (jnp.maximum(i * q - 1, 0), col)),
            pl.BlockSpec((8, width), lambda i: (jnp.minimum((i + 1) * q, n // 8 - 1), col))]


def _mix_fwd(p, ao, cpar, n_ctx, dc, name):
    n, d_in = p.shape
    tr = _pick(n, (256, 128))

    def body(p_ref, pp_ref, pn_ref, ao_ref, cp_ref, o_ref):
        i = pl.program_id(0)
        bg = p_ref[:, 0:dc]
        u = p_ref[:, dc:2 * dc] * p_ref[:, 2 * dc:3 * dc]
        u_before = pp_ref[7:8, dc:2 * dc] * pp_ref[7:8, 2 * dc:3 * dc]
        u_after = pn_ref[0:1, dc:2 * dc] * pn_ref[0:1, 2 * dc:3 * dc]
        loc = lax.broadcasted_iota(jnp.int32, (tr, 1), 0)
        gid = i * tr + loc
        has_prev = (gid != 0) & (gid != n_ctx)
        has_next = (gid != n_ctx - 1) & (gid != n - 1)
        u_m1 = jnp.where(has_prev, jnp.where(loc == 0, u_before, pltpu.roll(u, 1, 0)), 0.0)
        u_p1 = jnp.where(has_next, jnp.where(loc == tr - 1, u_after, pltpu.roll(u, tr - 1, 0)), 0.0)
        cv = u_m1 * cp_ref[0:1, :] + u * cp_ref[1:2, :] + u_p1 * cp_ref[2:3, :] + cp_ref[3:4, :]
        co = bg * cv
        nc = (co * lax.rsqrt(_rowmean(co * co) + EPS)) * cp_ref[4:5, :]
        ao_v = ao_ref[...]
        na = (ao_v * lax.rsqrt(_rowmean(ao_v * ao_v) + EPS)) * cp_ref[5:6, :]
        o_ref[...] = jnp.concatenate([nc, na], axis=1).astype(BF16)

    return pl.pallas_call(
        body, name=name, out_shape=jax.ShapeDtypeStruct((n, 2 * dc), BF16), grid=(n // tr,),
        in_specs=[pl.BlockSpec((tr, d_in), lambda i: (i, 0))] + _halo_specs(tr, n, d_in)
        + [pl.BlockSpec((tr, dc), lambda i: (i, 0)), pl.BlockSpec((8, dc), lambda i: (0, 0))],
        out_specs=pl.BlockSpec((tr, 2 * dc), lambda i: (i, 0)),
        compiler_params=_params(("parallel",)),
    )(p, p, p, ao, cpar)


def _mix_bwd(d_mg, p, cpar, d_q, parts, d_ctx, cs, sn, n_ctx, dc, hd, kd, name):
    n, d_in = p.shape
    tr = BLOCK
    nb = n // tr
    ext = tr + 16
    n_ctx_blocks = n_ctx // BLOCK

    def body(dm_ref, dmp_ref, dmn_ref, p_ref, pp_ref, pn_ref, cp_ref, dq_ref, pa_ref, pb_ref, pc_ref, dctx_ref,
             cs_ref, sn_ref, dp_ref, acc_ref):
        i = pl.program_id(0)

        @pl.when(i == 0)
        def _():
            acc_ref[...] = jnp.zeros_like(acc_ref)

        def cat(before, here, after):
            return jnp.concatenate([before, here, after], axis=0)

        bg = cat(pp_ref[:, 0:dc], p_ref[:, 0:dc], pn_ref[:, 0:dc])
        cg = cat(pp_ref[:, dc:2 * dc], p_ref[:, dc:2 * dc], pn_ref[:, dc:2 * dc])
        hh = cat(pp_ref[:, 2 * dc:3 * dc], p_ref[:, 2 * dc:3 * dc], pn_ref[:, 2 * dc:3 * dc])
        dme = cat(dmp_ref[...], dm_ref[...], dmn_ref[...])
        gid = i * tr - 8 + lax.broadcasted_iota(jnp.int32, (ext, 1), 0)
        inside = (gid >= 0) & (gid < n)
        has_prev = inside & (gid != 0) & (gid != n_ctx)
        has_next = inside & (gid != n_ctx - 1) & (gid != n - 1)
        w0, w1, w2, bias, gain = cp_ref[0:1, :], cp_ref[1:2, :], cp_ref[2:3, :], cp_ref[3:4, :], cp_ref[4:5, :]
        u = jnp.where(inside, cg * hh, 0.0)
        u_m1 = jnp.where(has_prev, pltpu.roll(u, 1, 0), 0.0)
        u_p1 = jnp.where(has_next, pltpu.roll(u, ext - 1, 0), 0.0)
        cv = u_m1 * w0 + u * w1 + u_p1 * w2 + bias
        co = bg * cv
        rc = lax.rsqrt(_rowmean(co * co) + EPS)
        cn = co * rc
        d_cn = dme * gain
        d_co = rc * (d_cn - cn * _rowmean(d_cn * cn))
        d_cv = jnp.where(inside, d_co * bg, 0.0)
        d_bg = d_co * cv
        d_cv_p1 = jnp.where(has_next, pltpu.roll(d_cv, ext - 1, 0), 0.0)
        d_cv_m1 = jnp.where(has_prev, pltpu.roll(d_cv, 1, 0), 0.0)
        d_u = d_cv_p1 * w0 + d_cv * w1 + d_cv_m1 * w2
        mid = slice(8, 8 + tr)
        acc_ref[0:1, :] += _colsum((d_cv * u_m1)[mid])
        acc_ref[1:2, :] += _colsum((d_cv * u)[mid])
        acc_ref[2:3, :] += _colsum((d_cv * u_p1)[mid])
        acc_ref[3:4, :] += _colsum(d_cv[mid])
        acc_ref[4:5, :] += _colsum((dme * cn)[mid])

        d_kv = (jnp.where(i >= 1, pa_ref[...], 0.0) + pb_ref[...] + jnp.where(i + 1 < nb, pc_ref[...], 0.0))
        ctx_rows = dctx_ref[pl.ds(pl.multiple_of(jnp.minimum(i, n_ctx_blocks - 1) * BLOCK, BLOCK), BLOCK), :]
        d_kv = d_kv + jnp.where(i < n_ctx_blocks, ctx_rows, 0.0)
        cs_v, sn_v = cs_ref[...], sn_ref[...]
        d_qu = _rope(dq_ref[...], cs_v, sn_v, -1.0)
        d_ku = _rope(d_kv[:, 0:kd], cs_v, sn_v, -1.0)
        dp_ref[...] = jnp.concatenate(
            [d_bg[mid], (d_u * hh)[mid], (d_u * cg)[mid], d_qu, d_ku, d_kv[:, kd:2 * kd]], axis=1).astype(BF16)

    part = lambda sel, which: pl.BlockSpec((None, None, BLOCK, 2 * kd), lambda i: (sel(i), which, 0, 0))
    return pl.pallas_call(
        body, name=name, out_shape=(jax.ShapeDtypeStruct((n, d_in), BF16), jax.ShapeDtypeStruct((8, dc), F32)),
        grid=(nb,),
        in_specs=[pl.BlockSpec((tr, dc), lambda i: (i, 0))] + _halo_specs(tr, n, dc)
        + [pl.BlockSpec((tr, d_in), lambda i: (i, 0))] + _halo_specs(tr, n, d_in)
        + [pl.BlockSpec((8, dc), lambda i: (0, 0)), pl.BlockSpec((tr, hd), lambda i: (i, 0)),
           part(lambda i: jnp.maximum(i - 1, 0), 2), part(lambda i: i, 1), part(lambda i: jnp.minimum(i + 1, nb - 1), 0),
           pl.BlockSpec((n_ctx, 2 * kd), lambda i: (0, 0)),
           pl.BlockSpec((tr, 128), lambda i: (i, 0)), pl.BlockSpec((tr, 128), lambda i: (i, 0))],
        out_specs=(pl.BlockSpec((tr, d_in), lambda i: (i, 0)), pl.BlockSpec((8, dc), lambda i: (0, 0))),
        compiler_params=_params(("arbitrary",)),
    )(d_mg, d_mg, d_mg, p, p, p, cpar, d_q, parts, parts, parts, d_ctx, cs, sn)


def _loss_bwd(t, gain, target, n_ctx, name):
    n, d = t.shape
    tr = _pick(n_ctx, (256, 128))
    first = n_ctx // tr

    def body(t_ref, g_ref, y_ref, dt_ref, loss_ref, dg_ref):
        i = pl.program_id(0)

        @pl.when(i == 0)
        def _():
            loss_ref[...] = jnp.zeros_like(loss_ref)
            dg_ref[...] = jnp.zeros_like(dg_ref)

        @pl.when(i < first)
        def _():
            dt_ref[...] = jnp.zeros_like(dt_ref)

        @pl.when(i >= first)
        def _():
            x = t_ref[...]
            g = g_ref[...]
            r = lax.rsqrt(_rowmean(x * x) + EPS)
            xn = x * r
            err = xn * g - y_ref[...]
            loss_ref[...] += 0.5 * _colsum(_rowmean(err * err))
            dy = err * (1.0 / d)
            dg_ref[0:1, :] += _colsum(dy * xn)
            dxn = dy * g
            dt_ref[...] = r * (dxn - xn * _rowmean(dxn * xn))

    row = pl.BlockSpec((tr, d), lambda i: (i, 0))
    return pl.pallas_call(
        body, name=name,
        out_shape=(jax.ShapeDtypeStruct((n, d), F32), jax.ShapeDtypeStruct((8, 128), F32), jax.ShapeDtypeStruct((8, d), F32)),
        grid=(n // tr,),
        in_specs=[row, pl.BlockSpec((1, d), lambda i: (0, 0)), pl.BlockSpec((tr, d), lambda i: (jnp.maximum(i - first, 0), 0))],
        out_specs=(row, pl.BlockSpec((8, 128), lambda i: (0, 0)), pl.BlockSpec((8, d), lambda i: (0, 0))),
        compiler_params=_params(("arbitrary",)),
    )(t, gain, target)


def _silu16(c16, name):
    def body(c_ref, o_ref):
        v = c_ref[...]
        o_ref[...] = (v * jax.nn.sigmoid(v)).astype(BF16)

    return pl.pallas_call(body, name=name, out_shape=jax.ShapeDtypeStruct(c16.shape, BF16))(c16)


def _cctx_grad(parts, c_ctx, name):
    def body(p_ref, c_ref, o_ref):
        g = _colsum(p_ref[...])
        v = c_ref[...]
        s = jax.nn.sigmoid(v)
        o_ref[...] = g * (s * (1.0 + v * (1.0 - s)))

    return pl.pallas_call(body, name=name, out_shape=jax.ShapeDtypeStruct(c_ctx.shape, F32))(parts, c_ctx)


def _rope_tables(n_ctx, n_tok):
    half = HEAD_DIM // 4
    inv = ROPE_THETA ** (-jnp.arange(0, HEAD_DIM // 2, 2, dtype=F32) / (HEAD_DIM // 2))
    rows = n_tok // GRID_W
    row_pos = jnp.repeat(jnp.arange(rows, dtype=F32), GRID_W)
    col_pos = jnp.tile(jnp.arange(GRID_W, dtype=F32), rows)
    ang_r, ang_c = row_pos[:, None] * inv[None, :], col_pos[:, None] * inv[None, :]
    cos = jnp.concatenate([jnp.cos(ang_r), jnp.cos(ang_r), jnp.cos(ang_c), jnp.cos(ang_c)], axis=1)
    sin = jnp.concatenate([-jnp.sin(ang_r), jnp.sin(ang_r), -jnp.sin(ang_c), jnp.sin(ang_c)], axis=1)
    assert cos.shape[1] == 4 * half == HEAD_DIM
    cos = jnp.concatenate([jnp.ones((n_ctx, HEAD_DIM), F32), cos], axis=0)
    sin = jnp.concatenate([jnp.zeros((n_ctx, HEAD_DIM), F32), sin], axis=0)
    return jnp.tile(cos, (1, 2)), jnp.tile(sin, (1, 2))


def kernel(x, c, ctx, c_ctx, w_ada, b_ada, g_norm1, g_norm2, w_in, conv_w, conv_b, sink, g_out_conv, g_out_attn, w_out, w_mlp1, w_mlp2, g_final, loss_target, m_c_ctx, m_w_ada, m_b_ada, m_g_norm1, m_g_norm2, m_w_in, m_conv_w, m_conv_b, m_sink, m_g_out_conv, m_g_out_attn, m_w_out, m_w_mlp1, m_w_mlp2, m_g_final, v_c_ctx, v_w_ada, v_b_ada, v_g_norm1, v_g_norm2, v_w_in, v_conv_w, v_conv_b, v_sink, v_g_out_conv, v_g_out_attn, v_w_out, v_w_mlp1, v_w_mlp2, v_g_final):
    n_lat, d = x.shape[1], x.shape[2]
    n_ctx = ctx.shape[1]
    n = n_ctx + n_lat
    depth = w_in.shape[0]
    dc = d // 2
    hd, kd = dc, N_KV_HEADS * HEAD_DIM
    n_heads = hd // HEAD_DIM
    d_in = 3 * dc + hd + 2 * kd
    cin, c_ada, r_out, c_ff, r_ff = w_in.shape[2], w_ada.shape[2], w_out.shape[1], w_mlp1.shape[2], w_mlp2.shape[1]
    d_ff = N_DEV * c_ff
    cw = conv_w.shape[2]
    assert d_in == N_DEV * cin and n_ctx % BLOCK == 0 and n_lat % BLOCK == 0 and hd % kd == 0
    dev = 4 * lax.axis_index("x") + 2 * lax.axis_index("y") + lax.axis_index("c")

    c_all, conv_w_all = _all_gather([c, conv_w], "gather_cond")
    conv_w_full = jnp.transpose(conv_w_all, (1, 2, 0, 3)).reshape(depth, 3, dc)
    c16 = jnp.concatenate([c_all.reshape(N_DEV, d), jnp.broadcast_to(c_ctx[None, :], (8, d))], axis=0)
    sc16 = _silu16(c16, "silu_cond")

    b_ada_loc = lax.dynamic_index_in_dim(b_ada.reshape(depth, N_DEV, c_ada), dev, axis=1, keepdims=False)
    tn_ada = _pick(c_ada, FEAT)

    def add_bias(acc, i, j, extra, outs):
        outs[0][...] = acc + extra[0][...]

    mod_loc = []
    for l in range(depth):
        mod_loc.append(_matmul(
            sc16, w_ada, dims="nn", shape=(16, c_ada, d), tiles=(16, tn_ada, _pick(d, KDIM)),
            b_spec=_w_spec("nat", l, None, "nn", tn_ada, _pick(d, KDIM)), epilogue=add_bias,
            extras=[(b_ada_loc[l][None, :], pl.BlockSpec((1, tn_ada), lambda i, j, k: (0, j)))], name=f"ada_fwd{l}"))
    (mod_all,) = _all_gather([jnp.stack(mod_loc)], "gather_mod")
    mod_full = jnp.transpose(mod_all, (1, 2, 0, 3)).reshape(depth, 16, N_MOD * d)
    mod_mine = lax.dynamic_index_in_dim(mod_full, dev, axis=1, keepdims=True)
    mod8 = jnp.concatenate([mod_mine, mod_full[:, 8:9], jnp.zeros((depth, 6, N_MOD * d), F32)], axis=1)

    w_in_b = _to_bf16(w_in.reshape(depth * d, cin), "cast_w_in").reshape(depth, d, cin)
    w_out_b = _to_bf16(w_out.reshape(depth * r_out, d), "cast_w_out").reshape(depth, r_out, d)
    w1_b = _to_bf16(w_mlp1.reshape(depth * d, c_ff), "cast_w_mlp1").reshape(depth, d, c_ff)
    w2_b = _to_bf16(w_mlp2.reshape(depth * r_ff, d), "cast_w_mlp2").reshape(depth, r_ff, d)
    (gath0,) = _comm_call([_job_gather_ici([w_in_b, w_out_b], 0)], "gather_w0_ici")
    (gath0,) = _comm_call([_job_gather_d2d(gath0)], "gather_w0_d2d")

    def full_in(g):
        return jnp.transpose(g, (1, 0, 2)).reshape(d, d_in)

    w_in_full, w_out_full = [full_in(gath0[0])], [gath0[1].reshape(d, d)]
    g_w1, w2_full = [], []
    pend_w2 = None

    t = jnp.concatenate([ctx[0], x[0]], axis=0)
    cs, sn = _rope_tables(n_ctx, n_lat)
    tm = _pick(n, TOK)
    tk_d = _pick(d, KDIM)
    tm_res = _pick(n, (1056, 768, 256, 128))
    tn_in = _pick(d_in, (768, 512, 256, 128))

    def resid_epilogue(tile_rows):
        def epi(acc, i, j, extra, outs):
            is_ctx = _row_ids(i, tile_rows) < n_ctx
            outs[0][...] = extra[0][...] + _sel(is_ctx, extra[1]) * acc
            outs[1][...] = acc.astype(BF16)
        return epi

    def sq_relu_epilogue(acc, i, j, extra, outs):
        outs[0][...] = acc.astype(BF16)
        rl = jnp.maximum(acc, 0.0)
        outs[1][...] = (rl * rl).astype(BF16)

    def d_sq_relu_epilogue(acc, i, j, extra, outs):
        outs[0][...] = (acc * (2.0 * jnp.maximum(extra[0][...].astype(F32), 0.0))).astype(BF16)

    saved = []
    for l in range(depth):
        cpar = jnp.concatenate([conv_w_full[l], conv_b[l][None], g_out_conv[l][None], g_out_attn[l][None],
                                jnp.zeros((2, dc), F32)], axis=0)
        more = l + 1 < depth
        h = _norm_mod(t, g_norm1[l][None], mod8[l], 0, n_ctx, f"norm1_{l}")
        jobs = ([_job_gather_d2d(pend_w2)] if pend_w2 is not None else []) + (
            [_job_gather_ici([w_in_b, w_out_b], l + 1)] if more else [])
        p = _matmul(h, w_in_full[l], dims="nn", shape=(n, d_in, d), tiles=(tm, tn_in, tk_d), jobs=jobs, name=f"in_proj{l}")
        if jobs:
            p, job_outs = p
            if pend_w2 is not None:
                w2_full.append(job_outs[0][0].reshape(d_ff, d))
            pend_io = job_outs[-1] if more else None
        qkv = _rope_split(p, cs, sn, dc, hd, kd, f"rope{l}")
        if l == 0:
            ao, (pend_mlp0,) = _attention(qkv, sink[l], n_ctx, hd, kd, f"attn{l}", jobs=[_job_gather_ici([w1_b, w2_b], 0)])
        else:
            ao = _attention(qkv, sink[l], n_ctx, hd, kd, f"attn{l}")
        mg = _mix_fwd(p, ao, cpar, n_ctx, dc, f"mix{l}")
        tn = _pick(d, (512, 256, 128))
        jobs = ([_job_gather_d2d(pend_io)] if more else []) + ([_job_gather_d2d(pend_mlp0)] if l == 0 else [])
        res = _matmul(
            mg, w_out_full[l], dims="nn", shape=(n, d, d), tiles=(tm, tn, tk_d),
            out_dtypes=(F32, BF16), epilogue=resid_epilogue(tm),
            extras=[(t, pl.BlockSpec((tm, tn), lambda i, j, k: (i, j))),
                    (mod8[l], pl.BlockSpec((8, tn), lambda i, j, k, tn=tn: (0, 2 * (d // tn) + j)))],
            jobs=jobs, name=f"out_proj{l}")
        if jobs:
            (t2, z), job_outs = res
            if more:
                w_in_full.append(full_in(job_outs[0][0]))
                w_out_full.append(job_outs[0][1].reshape(d, d))
            if l == 0:
                g_w1.append(job_outs[-1][0])
                w2_full.append(job_outs[-1][1].reshape(d_ff, d))
        else:
            t2, z = res
        h2 = _norm_mod(t2, g_norm2[l][None], mod8[l], 3, n_ctx, f"norm2_{l}")
        tn = _pick(c_ff, FEAT)
        res = _matmul(h2, g_w1[l], dims="nn", shape=(n, d_ff, d), tiles=(tm, tn, tk_d),
                      b_spec=_w_spec("cols", None, c_ff, "nn", tn, tk_d), out_dtypes=(BF16, BF16),
                      epilogue=sq_relu_epilogue, jobs=[_job_gather_ici([w1_b], l + 1)] if more else [], name=f"mlp_up{l}")
        if more:
            (a, s), ((pend_w1,),) = res
        else:
            a, s = res
        tn = _pick(d, FEAT)
        tk = _pick(d_ff, (1024, 512, 256, 128))
        res = _matmul(
            s, w2_full[l], dims="nn", shape=(n, d, d_ff), tiles=(tm_res, tn, tk),
            out_dtypes=(F32, BF16), epilogue=resid_epilogue(tm_res),
            extras=[(t2, pl.BlockSpec((tm_res, tn), lambda i, j, k: (i, j))),
                    (mod8[l], pl.BlockSpec((8, tn), lambda i, j, k, tn=tn: (0, 5 * (d // tn) + j)))],
            jobs=[_job_gather_ici([w2_b], l + 1), _job_gather_d2d([pend_w1])] if more else [], name=f"mlp_down{l}")
        if more:
            (t3, o), (pend_w2, (g_w1_next,)) = res
            g_w1.append(g_w1_next)
        else:
            (t3, o), pend_w2 = res, None
        saved.append((t, h, p, qkv, ao, mg, z, t2, h2, a, s, o, cpar))
        t = t3

    d_t, loss_tile, dg_final = _loss_bwd(t, g_final[None], loss_target[0], n_ctx, "loss")
    loss = lax.psum(loss_tile[0, 0], ("x", "y", "c"))

    buf_in = lax.empty((N_CHIP, depth, cin, d), BF16)
    buf_out = lax.empty((N_CHIP, depth, r_out, d), BF16)
    buf_w1 = lax.empty((N_CHIP, depth, d, c_ff), BF16)
    buf_w2 = lax.empty((N_CHIP, depth, r_ff, d), BF16)
    pend_in, pend_layer = None, None
    tkn = _pick(n, TOK)
    small = [None] * depth
    dob, dgate2 = _gate_bwd(d_t, saved[depth - 1][11], mod8[depth - 1], 5, n_ctx, "gate2_bwd_last")
    for l in reversed(range(depth)):
        t_in, h, p, qkv, ao, mg, z, t2, h2, a, s, o, cpar = saved[l]
        tm_g = _pick(d_ff, FEAT)
        tn = _pick(d, FEAT)
        gw2 = _matmul(s, dob, dims="tn", shape=(d_ff, d, n), tiles=(tm_g, tn, tkn), name=f"mlp_down_dw{l}")
        gw2 = gw2.reshape(N_DEV, r_ff, d)
        tn = _pick(d_ff, FEAT)
        jobs = [_job_scatter_d2d([gw2])] + ([_job_scatter_ici([pend_in], [buf_in], pend_layer)] if pend_in is not None else [])
        da, job_outs = _matmul(dob, w2_full[l], dims="nt", shape=(n, d_ff, d), tiles=(tm, tn, tk_d),
                               out_dtypes=(BF16,), epilogue=d_sq_relu_epilogue,
                               extras=[(a, pl.BlockSpec((tm, tn), lambda i, j, k: (i, j)))], jobs=jobs, name=f"mlp_down_dx{l}")
        if pend_in is not None:
            (buf_in,) = job_outs[1]
        pair_w2 = _pair_add(gw2, job_outs[0][0], f"pair_w2_{l}")
        tm_g = _pick(d, FEAT)
        tn = _pick(c_ff, FEAT)
        gw1 = _matmul(h2, da, dims="tn", shape=(d, d_ff, n), tiles=(tm_g, tn, tkn),
                      out_specs=(_g_spec("cols", None, c_ff, tm_g, tn),),
                      out_shapes=(jax.ShapeDtypeStruct((N_DEV, d, c_ff), F32),), name=f"mlp_up_dw{l}")
        tn = _pick(d, FEAT)
        tk = _pick(c_ff, (1024, 512, 256, 128))
        dh2, job_outs = _matmul(da, g_w1[l], dims="nt", shape=(n, d, d_ff), tiles=(tm, tn, tk),
                                b_spec=_w_spec("cols", None, c_ff, "nt", tn, tk),
                                jobs=[_job_scatter_d2d([gw1]), _job_scatter_ici([pair_w2], [buf_w2], l)], name=f"mlp_up_dx{l}")
        (buf_w2,) = job_outs[1]
        pair_w1 = _pair_add(gw1, job_outs[0][0], f"pair_w1_{l}")
        d_t2, dss2, dgn2, dzb, dgate1 = _norm_mod_bwd(dh2, t2, d_t, g_norm2[l][None], mod8[l], 3, n_ctx, f"norm2_bwd{l}",
                                                      gated=(z, mod8[l], 2))
        tm_g = _pick(d, FEAT)
        tn = _pick(d, FEAT)
        gout = _matmul(mg, dzb, dims="tn", shape=(d, d, n), tiles=(tm_g, tn, tkn), name=f"out_proj_dw{l}")
        gout = gout.reshape(N_DEV, r_out, d)
        tn = _pick(d, FEAT)
        d_mg, job_outs = _matmul(dzb, w_out_full[l], dims="nt", shape=(n, d, d), tiles=(tm, tn, tk_d),
                                 jobs=[_job_scatter_d2d([gout])], name=f"out_proj_dx{l}")
        pair_out = _pair_add(gout, job_outs[0][0], f"pair_out_{l}")
        d_q, parts, d_kv_ctx, d_sink, d_goa = _attention_bwd(qkv, sink[l], ao, d_mg, cpar, n_ctx, hd, kd, f"attn_bwd{l}")
        d_p, conv_acc = _mix_bwd(d_mg, p, cpar, d_q, parts, d_kv_ctx, cs, sn, n_ctx, dc, hd, kd, f"mix_bwd{l}")
        tm_g = _pick(d_in, (1536, 768, 512, 256, 128))
        tn = _pick(d, FEAT)
        gin = _matmul(d_p, h, dims="tn", shape=(d_in, d, n), tiles=(tm_g, tn, tkn), name=f"in_proj_dw{l}")
        gin = gin.reshape(N_DEV, cin, d)
        tn = _pick(d, FEAT)
        tk = _pick(d_in, (1536, 768, 512, 256, 128))
        dh, job_outs = _matmul(d_p, w_in_full[l], dims="nt", shape=(n, d, d_in), tiles=(tm, tn, tk),
                               jobs=[_job_scatter_d2d([gin]), _job_scatter_ici([pair_w1, pair_out], [buf_w1, buf_out], l)],
                               name=f"in_proj_dx{l}")
        buf_w1, buf_out = job_outs[1]
        pend_in, pend_layer = _pair_add(gin, job_outs[0][0], f"pair_in_{l}"), l
        d_mod2_tail = [dgate1[0:2], dss2[0:2], dgate2[0:2]]
        if l > 0:
            d_t, dss1, dgn1, dob, dgate2 = _norm_mod_bwd(dh, t_in, d_t2, g_norm1[l][None], mod8[l], 0, n_ctx, f"norm1_bwd{l}",
                                                         gated=(saved[l - 1][11], mod8[l - 1], 5))
        else:
            d_t, dss1, dgn1 = _norm_mod_bwd(dh, t_in, d_t2, g_norm1[l][None], mod8[l], 0, n_ctx, f"norm1_bwd{l}")
        d_mod2 = jnp.concatenate([dss1[0:2]] + d_mod2_tail, axis=1)
        small[l] = (d_mod2, dgn1[0], dgn2[0], conv_acc, d_sink[0, 0:n_heads], d_goa[0])
    grad_x = d_t[n_ctx:][None]

    def pack(l):
        d_mod2, dgn1, dgn2, conv_acc, d_sink, d_goa = small[l]
        row0 = [d_mod2[0], dgn1, dgn2, conv_acc[3], d_sink, conv_acc[4], d_goa, conv_acc[0:3].reshape(-1)]
        row1 = [d_mod2[1]] + [jnp.zeros_like(v) for v in row0[1:]]
        return jnp.stack([jnp.concatenate(row0), jnp.concatenate(row1)])
    per_layer = N_MOD * d + 2 * d + dc + n_heads + 2 * dc + 3 * dc
    packed = jnp.concatenate([pack(l) for l in range(depth)] +
                             [jnp.stack([dg_final[0], jnp.zeros((d,), F32)])], axis=1)
    f_tot = depth * per_layer + d
    f_pad = -f_tot % 1024
    packed = jnp.pad(packed, ((0, 0), (0, f_pad)))
    (small_all,) = _all_gather([packed], "gather_small")
    small_parts = small_all.reshape(2 * N_DEV, 1, f_tot + f_pad)

    def section(arr, l, off, size):
        return lax.slice_in_dim(arr, l * per_layer + off, l * per_layer + off + size, axis=-1)

    offs = {}
    o_ = 0
    for nm_, sz in (("mod", N_MOD * d), ("gn1", d), ("gn2", d), ("cb", dc), ("sink", n_heads), ("goc", dc), ("goa", dc), ("cw", 3 * dc)):
        offs[nm_] = (o_, sz)
        o_ += sz

    def packw(b_ada_, gn1_, gn2_, cb_, sk_, goc_, goa_, gf_):
        rows = []
        for l in range(depth):
            rows += [b_ada_[l], gn1_[l], gn2_[l], cb_[l], sk_[l], goc_[l], goa_[l], jnp.zeros((3 * dc,), F32)]
        return jnp.pad(jnp.concatenate(rows + [gf_]), (0, f_pad))[None]
    pw = packw(b_ada, g_norm1, g_norm2, conv_b, sink, g_out_conv, g_out_attn, g_final)
    pm = packw(m_b_ada, m_g_norm1, m_g_norm2, m_conv_b, m_sink, m_g_out_conv, m_g_out_attn, m_g_final)
    pv = packw(v_b_ada, v_g_norm1, v_g_norm2, v_conv_b, v_sink, v_g_out_conv, v_g_out_attn, v_g_final)
    sg, sd, sm, sv = _adamw(pw, pm, pv, small_parts, "adamw_small")

    def unpack(arr):
        arr = arr[0]
        out = {}
        for nm_ in ("mod", "gn1", "gn2", "cb", "sink", "goc", "goa", "cw"):
            off, size = offs[nm_]
            out[nm_] = jnp.stack([section(arr, l, off, size) for l in range(depth)])
        out["gf"] = arr[depth * per_layer:depth * per_layer + d]
        return out
    ug, ud, um, uv = unpack(sg), unpack(sd), unpack(sm), unpack(sv)

    cw_grad_full = ug["cw"].reshape(depth, 3, N_DEV, cw)
    cw_grad = lax.dynamic_index_in_dim(cw_grad_full, dev, axis=2, keepdims=False).reshape(1, depth * 3, cw)
    cwg, cwd, cwm, cwv = _adamw(conv_w.reshape(depth * 3, cw), m_conv_w.reshape(depth * 3, cw),
                                v_conv_w.reshape(depth * 3, cw), cw_grad, "adamw_conv_w")
    cw_shape = conv_w.shape

    mod_rows = small_all[:, :, :depth * per_layer].reshape(N_DEV, 2, depth, per_layer)[:, :, :, :N_MOD * d]
    dm16 = jnp.concatenate([mod_rows[:, 0], mod_rows[:, 1]], axis=0)
    dm16 = jnp.transpose(dm16, (1, 0, 2)).reshape(depth, 16, N_DEV, c_ada)
    dm16_loc = lax.dynamic_index_in_dim(dm16, dev, axis=2, keepdims=False)
    gb_ada = lax.empty((depth, d, c_ada), F32)
    dsc_parts = []
    tm_g = _pick(d, FEAT)
    for l in range(depth):
        gb_ada = _matmul(sc16, dm16_loc[l], dims="tn", shape=(d, c_ada, 16), tiles=(tm_g, tn_ada, 16),
                         out_specs=(_g_spec("nat", l, None, tm_g, tn_ada),), out_shapes=(jax.ShapeDtypeStruct(gb_ada.shape, F32),),
                         alias=gb_ada, name=f"ada_dw{l}")
        tn = _pick(d, FEAT)
        tk = _pick(c_ada, KDIM)
        dsc_parts.append(_matmul(dm16_loc[l], w_ada, dims="nt", shape=(16, d, c_ada), tiles=(16, tn, tk),
                                 b_spec=_w_spec("nat", l, None, "nt", tn, tk), name=f"ada_dx{l}"))
    (dsc_all,) = _all_gather([jnp.stack(dsc_parts)[:, 8:16]], "gather_dcond")
    g_cctx = _cctx_grad(dsc_all.reshape(N_DEV * depth * 8, d), c_ctx[None], "c_ctx_grad")
    ccg, ccd, ccm, ccv = _adamw(c_ctx[None], m_c_ctx[None], v_c_ctx[None], g_cctx[None], "adamw_c_ctx")
    adg, add, adm, adv = _adamw(w_ada.reshape(depth * d, c_ada), m_w_ada.reshape(depth * d, c_ada),
                                v_w_ada.reshape(depth * d, c_ada), gb_ada.reshape(1, depth * d, c_ada), "adamw_w_ada")

    ((buf_in,),) = _comm_call([_job_scatter_ici([pend_in], [buf_in], pend_layer)], "rs_tail")
    parts4 = [jnp.swapaxes(buf_in, 2, 3), buf_out, buf_w1, buf_w2]
    big = []
    for k, (w_, m_, v_) in enumerate(((w_in, m_w_in, v_w_in), (w_out, m_w_out, v_w_out),
                                      (w_mlp1, m_w_mlp1, v_w_mlp1), (w_mlp2, m_w_mlp2, v_w_mlp2))):
        r2, c2 = w_.shape[0] * w_.shape[1], w_.shape[2]
        res = _adamw(w_.reshape(r2, c2), m_.reshape(r2, c2), v_.reshape(r2, c2), parts4[k].reshape(N_CHIP, r2, c2),
                     f"adamw_big{k}")
        big.append([a_.reshape(w_.shape) for a_ in res])

    def leaf(i):
        return (
            (ccg, ccd, ccm, ccv)[i][0], (adg, add, adm, adv)[i].reshape(w_ada.shape),
            (ug, ud, um, uv)[i]["mod"], (ug, ud, um, uv)[i]["gn1"], (ug, ud, um, uv)[i]["gn2"], big[0][i],
            (cwg, cwd, cwm, cwv)[i].reshape(cw_shape), (ug, ud, um, uv)[i]["cb"], (ug, ud, um, uv)[i]["sink"],
            (ug, ud, um, uv)[i]["goc"], (ug, ud, um, uv)[i]["goa"], big[1][i], big[2][i], big[3][i], (ug, ud, um, uv)[i]["gf"])

    return (loss, grad_x) + leaf(0) + leaf(1) + leaf(2) + leaf(3)
```

```python
import functools
import math

import jax
import jax.numpy as jnp
from jax import lax
from jax.experimental import pallas as pl
from jax.experimental.pallas import tpu as pltpu

HEAD_DIM = 64
N_KV_HEADS = 4
BLOCK = 128
GRID_W = 64
ROPE_THETA = 10000.0
EPS = 1e-6
N_MOD = 6
SCALE = HEAD_DIM ** -0.5
NEG_INF = -1e30
ADAM_LR = 0.001
ADAM_B1 = 0.9
ADAM_B2 = 0.999
ADAM_EPS = 1e-08
ADAM_WD = 0.01
ADAM_STEP = 10
N_DEV = 8
N_CHIP = 4
VMEM_LIMIT_BYTES = 48 * 1024 * 1024
MESH = pl.DeviceIdType.MESH
BF16 = jnp.bfloat16
F32 = jnp.float32
ANY = pl.BlockSpec(memory_space=pl.ANY)


def _pick(dim, prefs):
    for p in prefs:
        if p <= dim and dim % p == 0:
            return p
    return dim


def _params(sem):
    return pltpu.CompilerParams(dimension_semantics=sem, vmem_limit_bytes=VMEM_LIMIT_BYTES)


def _row_ids(i, tr):
    return i * tr + lax.broadcasted_iota(jnp.int32, (tr, 1), 0)


def _colsum(v):
    return jnp.sum(v, axis=0, keepdims=True)


def _rowmean(v):
    return jnp.mean(v, axis=1, keepdims=True)


def _all_gather(xs, name):
    na = len(xs)

    def body(*refs):
        x_refs, o_refs = refs[:na], refs[na:2 * na]
        send_sems, recv_sems, local_sems = refs[2 * na:]
        x, y, c = lax.axis_index("x"), lax.axis_index("y"), lax.axis_index("c")
        me, sibling = (x, y, c), (x, y, 1 - c)
        chips = [(1 - x, y), (x, 1 - y), (1 - x, 1 - y)]

        def slot(a, px, py, pc):
            return o_refs[a].at[4 * px + 2 * py + pc]

        def copy(a, k, block, to, src=None):
            return pltpu.make_async_remote_copy(
                src_ref=slot(a, *block) if src is None else src, dst_ref=slot(a, *block),
                send_sem=send_sems.at[a, k], recv_sem=recv_sems.at[a, k], device_id=to, device_id_type=MESH)

        mine = [pltpu.make_async_copy(x_refs[a], slot(a, *me), local_sems.at[a]) for a in range(na)]
        for cp in mine:
            cp.start()
        first = []
        for a in range(na):
            first.append(copy(a, 0, me, sibling, src=x_refs[a]))
            first += [copy(a, 1 + j, me, (*chip, c), src=x_refs[a]) for j, chip in enumerate(chips)]
        for cp in first:
            cp.start()
        passed = []
        for j, chip in enumerate(chips):
            for a in range(na):
                copy(a, 1 + j, (*chip, c), me).wait_recv()
                fwd = copy(a, 4 + j, (*chip, c), sibling)
                fwd.start()
                passed.append(fwd)
        for a in range(na):
            copy(a, 0, sibling, me).wait_recv()
            for j, chip in enumerate(chips):
                copy(a, 4 + j, (*chip, 1 - c), me).wait_recv()
        for cp in first + passed:
            cp.wait_send()
        for cp in mine:
            cp.wait()

    outs = pl.pallas_call(
        body, name=name,
        out_shape=tuple(jax.ShapeDtypeStruct((N_DEV,) + x.shape, x.dtype) for x in xs),
        in_specs=[ANY] * na, out_specs=tuple([ANY] * na),
        scratch_shapes=[pltpu.SemaphoreType.DMA((na, 7)), pltpu.SemaphoreType.DMA((na, 7)),
                        pltpu.SemaphoreType.DMA((na,))],
    )(*xs)
    return list(outs)


class _Job:
    def __init__(self, ins, outs, alias, sems, copies):
        self.ins, self.outs, self.alias, self.sems, self._copies = ins, outs, alias, sems, copies

    def start(self, in_refs, out_refs, sems):
        local, sends, _ = self._copies(in_refs, out_refs, sems)
        for make in local + sends:
            make().start()

    def wait(self, in_refs, out_refs, sems):
        local, sends, arrivals = self._copies(in_refs, out_refs, sems)
        for make in arrivals:
            make().wait_recv()
        for make in sends:
            make().wait_send()
        for make in local:
            make().wait()


def _other_chips():
    x, y = lax.axis_index("x"), lax.axis_index("y")
    return [(1 - x, y), (x, 1 - y), (1 - x, 1 - y)]


def _remote(src, dst, send_sem, recv_sem, to):
    return functools.partial(pltpu.make_async_remote_copy, src_ref=src, dst_ref=dst, send_sem=send_sem, recv_sem=recv_sem,
                             device_id=to, device_id_type=MESH)


def _local(src, dst, sem):
    return functools.partial(pltpu.make_async_copy, src, dst, sem)


def _job_gather_ici(xs, layer):
    na = len(xs)

    def copies(in_refs, out_refs, sems):
        send, recv, loc = sems
        x, y, c = lax.axis_index("x"), lax.axis_index("y"), lax.axis_index("c")
        me = 4 * x + 2 * y + c
        local, sends, arrivals = [], [], []
        for a in range(na):
            src = in_refs[a].at[layer]
            local.append(_local(src, out_refs[a].at[me], loc.at[a]))
            for j, (px, py) in enumerate(_other_chips()):
                sends.append(_remote(src, out_refs[a].at[me], send.at[a, j], recv.at[a, j], (px, py, c)))
                arrivals.append(_remote(src, out_refs[a].at[4 * px + 2 * py + c], send.at[a, j], recv.at[a, j], (px, py, c)))
        return local, sends, arrivals

    outs = [jax.ShapeDtypeStruct((N_DEV,) + x.shape[1:], x.dtype) for x in xs]
    sems = [pltpu.SemaphoreType.DMA((na, 3)), pltpu.SemaphoreType.DMA((na, 3)), pltpu.SemaphoreType.DMA((na,))]
    return _Job(list(xs), outs, {}, sems, copies)


def _job_gather_d2d(gs):
    na = len(gs)

    def copies(in_refs, out_refs, sems):
        send, recv = sems
        x, y, c = lax.axis_index("x"), lax.axis_index("y"), lax.axis_index("c")
        sends, arrivals = [], []
        for a in range(na):
            for k in range(N_CHIP):
                mine, theirs = 2 * k + c, 2 * k + (1 - c)
                sends.append(_remote(in_refs[a].at[mine], out_refs[a].at[mine], send.at[a, k], recv.at[a, k], (x, y, 1 - c)))
                arrivals.append(_remote(in_refs[a].at[theirs], out_refs[a].at[theirs], send.at[a, k], recv.at[a, k], (x, y, 1 - c)))
        return [], sends, arrivals

    outs = [jax.ShapeDtypeStruct(g.shape, g.dtype) for g in gs]
    sems = [pltpu.SemaphoreType.DMA((na, N_CHIP)), pltpu.SemaphoreType.DMA((na, N_CHIP))]
    return _Job(list(gs), outs, {a: a for a in range(na)}, sems, copies)


def _job_scatter_d2d(gs):
    na = len(gs)

    def copies(in_refs, out_refs, sems):
        send, recv = sems
        x, y, c = lax.axis_index("x"), lax.axis_index("y"), lax.axis_index("c")
        sends, arrivals = [], []
        for a in range(na):
            for k in range(N_CHIP):
                cp = _remote(in_refs[a].at[2 * k + (1 - c)], out_refs[a].at[k], send.at[a, k], recv.at[a, k], (x, y, 1 - c))
                sends.append(cp)
                arrivals.append(cp)
        return [], sends, arrivals

    outs = [jax.ShapeDtypeStruct((N_CHIP,) + g.shape[1:], g.dtype) for g in gs]
    sems = [pltpu.SemaphoreType.DMA((na, N_CHIP)), pltpu.SemaphoreType.DMA((na, N_CHIP))]
    return _Job(list(gs), outs, {}, sems, copies)


def _job_scatter_ici(pairs, bufs, layer):
    na = len(pairs)

    def copies(in_refs, out_refs, sems):
        send, recv, loc = sems
        x, y, c = lax.axis_index("x"), lax.axis_index("y"), lax.axis_index("c")
        my_chip = 2 * x + y
        local, sends, arrivals = [], [], []
        for a in range(na):
            local.append(_local(in_refs[a].at[my_chip], out_refs[a].at[my_chip, layer], loc.at[a]))
            for j, (px, py) in enumerate(_other_chips()):
                src = in_refs[a].at[2 * px + py]
                sends.append(_remote(src, out_refs[a].at[my_chip, layer], send.at[a, j], recv.at[a, j], (px, py, c)))
                arrivals.append(_remote(src, out_refs[a].at[2 * px + py, layer], send.at[a, j], recv.at[a, j], (px, py, c)))
        return local, sends, arrivals

    outs = [jax.ShapeDtypeStruct(b.shape, b.dtype) for b in bufs]
    sems = [pltpu.SemaphoreType.DMA((na, 3)), pltpu.SemaphoreType.DMA((na, 3)), pltpu.SemaphoreType.DMA((na,))]
    return _Job(list(pairs) + list(bufs), outs, {na + a: a for a in range(na)}, sems, copies)


def _split_jobs(jobs, in_refs, out_refs, sem_refs):
    out, i0, o0, s0 = [], 0, 0, 0
    for jb in jobs:
        out.append((jb, in_refs[i0:i0 + len(jb.ins)], out_refs[o0:o0 + len(jb.outs)], sem_refs[s0:s0 + len(jb.sems)]))
        i0, o0, s0 = i0 + len(jb.ins), o0 + len(jb.outs), s0 + len(jb.sems)
    return out


def _carry(body, jobs, n_in, n_out, n_steps):
    if not jobs:
        return body
    n_ji = sum(len(jb.ins) for jb in jobs)
    n_jo = sum(len(jb.outs) for jb in jobs)

    def wrapped(*refs):
        o0 = n_in + n_ji
        parts = _split_jobs(jobs, refs[n_in:o0], refs[o0 + n_out:o0 + n_out + n_jo], refs[o0 + n_out + n_jo:])

        @pl.when(pl.program_id(0) == 0)
        def _():
            for jb, i_r, o_r, s_r in parts:
                jb.start(i_r, o_r, s_r)

        body(*refs[:n_in], *refs[o0:o0 + n_out])

        @pl.when(pl.program_id(0) == n_steps - 1)
        def _():
            for jb, i_r, o_r, s_r in parts:
                jb.wait(i_r, o_r, s_r)

    return wrapped


def _carry_args(jobs, n_in, n_out):
    io_alias, i0, o0 = {}, n_in, n_out
    for jb in jobs:
        io_alias.update({i0 + a: o0 + b for a, b in jb.alias.items()})
        i0, o0 = i0 + len(jb.ins), o0 + len(jb.outs)
    ins = [v for jb in jobs for v in jb.ins]
    outs = [s for jb in jobs for s in jb.outs]
    return ins, [ANY] * len(ins), outs, [ANY] * len(outs), [s for jb in jobs for s in jb.sems], io_alias


def _carry_results(jobs, outs, n_out):
    res, o0 = [], n_out
    for jb in jobs:
        res.append(list(outs[o0:o0 + len(jb.outs)]))
        o0 += len(jb.outs)
    return res


def _comm_call(jobs, name):
    n_in = sum(len(jb.ins) for jb in jobs)
    n_out = sum(len(jb.outs) for jb in jobs)

    def body(*refs):
        parts = _split_jobs(jobs, refs[:n_in], refs[n_in:n_in + n_out], refs[n_in + n_out:])
        for jb, i_r, o_r, s_r in parts:
            jb.start(i_r, o_r, s_r)
        for jb, i_r, o_r, s_r in parts:
            jb.wait(i_r, o_r, s_r)

    io_alias, i0, o0 = {}, 0, 0
    for jb in jobs:
        io_alias.update({i0 + a: o0 + b for a, b in jb.alias.items()})
        i0, o0 = i0 + len(jb.ins), o0 + len(jb.outs)
    outs = pl.pallas_call(
        body, name=name, out_shape=tuple(s for jb in jobs for s in jb.outs),
        in_specs=[ANY] * n_in, out_specs=tuple([ANY] * n_out),
        scratch_shapes=[s for jb in jobs for s in jb.sems], input_output_aliases=io_alias,
    )(*[v for jb in jobs for v in jb.ins])
    res, o0 = [], 0
    for jb in jobs:
        res.append(list(outs[o0:o0 + len(jb.outs)]))
        o0 += len(jb.outs)
    return res


def _to_bf16(x2d, name):
    r, c = x2d.shape
    tr = _pick(r, (512, 256, 128, 64, 32, 16))

    def body(x_ref, o_ref):
        o_ref[...] = x_ref[...].astype(BF16)

    return pl.pallas_call(
        body, name=name, out_shape=jax.ShapeDtypeStruct((r, c), BF16), grid=(r // tr,),
        in_specs=[pl.BlockSpec((tr, c), lambda i: (i, 0))], out_specs=pl.BlockSpec((tr, c), lambda i: (i, 0)),
        compiler_params=_params(("parallel",)),
    )(x2d)


def _pair_add(own8, got4, name):
    _, r, c = own8.shape
    tr = _pick(r, (512, 256, 128, 64, 32, 16))
    core = lax.axis_index("c").astype(jnp.int32).reshape(1)

    def body(c_ref, a_ref, b_ref, o_ref):
        o_ref[...] = (a_ref[...] + b_ref[...]).astype(BF16)

    return pl.pallas_call(
        body, name=name, out_shape=jax.ShapeDtypeStruct((N_CHIP, r, c), BF16),
        grid_spec=pltpu.PrefetchScalarGridSpec(
            num_scalar_prefetch=1, grid=(N_CHIP, r // tr),
            in_specs=[pl.BlockSpec((None, tr, c), lambda k, i, cr: (2 * k + cr[0], i, 0)),
                      pl.BlockSpec((None, tr, c), lambda k, i, cr: (k, i, 0))],
            out_specs=pl.BlockSpec((None, tr, c), lambda k, i, cr: (k, i, 0))),
        compiler_params=_params(("parallel", "parallel")),
    )(core, own8, got4)


def _adamw(w, m, v, parts, name):
    r, c = w.shape
    n_parts = parts.shape[0]
    tr = _pick(r, (256, 128, 64, 32, 16, 8))
    tc = _pick(c, (1024, 512, 256, 128))

    def body(w_ref, m_ref, v_ref, p_ref, g_ref, d_ref, nm_ref, nv_ref):
        g = p_ref[0].astype(F32)
        for k in range(1, n_parts):
            g = g + p_ref[k].astype(F32)
        wv = w_ref[...]
        nm = ADAM_B1 * m_ref[...] + (1.0 - ADAM_B1) * g
        nv = ADAM_B2 * v_ref[...] + (1.0 - ADAM_B2) * (g * g)
        m_hat = nm / (1.0 - ADAM_B1 ** ADAM_STEP)
        v_hat = nv / (1.0 - ADAM_B2 ** ADAM_STEP)
        g_ref[...] = g
        d_ref[...] = -ADAM_LR * (m_hat / (jnp.sqrt(v_hat) + ADAM_EPS) + ADAM_WD * wv)
        nm_ref[...] = nm
        nv_ref[...] = nv

    tile = pl.BlockSpec((tr, tc), lambda i, j: (i, j))
    sh = jax.ShapeDtypeStruct((r, c), F32)
    return pl.pallas_call(
        body, name=name, out_shape=(sh, sh, sh, sh), grid=(r // tr, c // tc),
        in_specs=[tile, tile, tile, pl.BlockSpec((n_parts, tr, tc), lambda i, j: (0, i, j))],
        out_specs=(tile, tile, tile, tile),
        compiler_params=_params(("parallel", "parallel")),
    )(w, m, v, parts)


def _matmul(a, b, *, dims, shape, tiles, b_spec=None, out_specs=None, out_shapes=None, out_dtypes=(F32,),
            epilogue=None, extras=(), alias=None, jobs=(), name):
    m_dim, n_dim, k_dim = shape
    tm, tn, tk = tiles
    assert m_dim % tm == 0 and n_dim % tn == 0 and k_dim % tk == 0, (name, shape, tiles)
    nk = k_dim // tk
    n_extra = len(extras)
    n_alias = 0 if alias is None else 1
    n_out = len(out_dtypes)
    if dims == "tn":
        a_spec = pl.BlockSpec((tk, tm), lambda i, j, k: (k, i))
        contract = (((0,), (0,)), ((), ()))
    else:
        a_spec = pl.BlockSpec((tm, tk), lambda i, j, k: (i, k))
        contract = (((1,), (1,)), ((), ())) if dims == "nt" else (((1,), (0,)), ((), ()))
    if b_spec is None:
        b_spec = (pl.BlockSpec((tn, tk), lambda i, j, k: (j, k)) if dims == "nt"
                  else pl.BlockSpec((tk, tn), lambda i, j, k: (k, j)))
    if out_specs is None:
        out_specs = tuple(pl.BlockSpec((tm, tn), lambda i, j, k: (i, j)) for _ in range(n_out))
    if out_shapes is None:
        out_shapes = tuple(jax.ShapeDtypeStruct((m_dim, n_dim), d) for d in out_dtypes)

    n_local_in = 2 + n_extra + n_alias
    n_job_in = sum(len(jb.ins) for jb in jobs)
    n_job_out = sum(len(jb.outs) for jb in jobs)
    n_acc = 1 if nk > 1 else 0
    grid = (m_dim // tm, n_dim // tn, nk)

    def body(*refs):
        a_ref, b_ref = refs[0], refs[1]
        extra_refs = refs[2:2 + n_extra]
        o0 = n_local_in + n_job_in
        out_refs = refs[o0:o0 + n_out]
        s0 = o0 + n_out + n_job_out
        job_parts = _split_jobs(jobs, refs[n_local_in:o0], refs[o0 + n_out:s0], refs[s0 + n_acc:])
        i, j, k = pl.program_id(0), pl.program_id(1), pl.program_id(2)

        if jobs:
            @pl.when((i == 0) & (j == 0) & (k == 0))
            def _():
                for jb, i_r, o_r, s_r in job_parts:
                    jb.start(i_r, o_r, s_r)

        def finish(acc):
            if epilogue is None:
                out_refs[0][...] = acc.astype(out_refs[0].dtype)
            else:
                epilogue(acc, i, j, extra_refs, out_refs)

        def product():
            return lax.dot_general(a_ref[...].astype(BF16), b_ref[...].astype(BF16), contract, preferred_element_type=F32)

        if nk == 1:
            finish(product())
        else:
            acc_ref = refs[s0]

            @pl.when(k == 0)
            def _():
                acc_ref[...] = jnp.zeros_like(acc_ref)

            acc_ref[...] += product()

            @pl.when(k == nk - 1)
            def _():
                finish(acc_ref[...])

        if jobs:
            @pl.when((i == grid[0] - 1) & (j == grid[1] - 1) & (k == nk - 1))
            def _():
                for jb, i_r, o_r, s_r in job_parts:
                    jb.wait(i_r, o_r, s_r)

    ins = [a, b] + [e[0] for e in extras]
    in_specs = [a_spec, b_spec] + [e[1] for e in extras]
    io_alias = {}
    if alias is not None:
        ins.append(alias)
        in_specs.append(ANY)
        io_alias = {len(ins) - 1: 0}
    all_out_shapes, all_out_specs = list(out_shapes), list(out_specs)
    for jb in jobs:
        io_alias.update({len(ins) + a_: len(all_out_shapes) + b_ for a_, b_ in jb.alias.items()})
        ins += jb.ins
        in_specs += [ANY] * len(jb.ins)
        all_out_shapes += jb.outs
        all_out_specs += [ANY] * len(jb.outs)
    scratch = ([pltpu.VMEM((tm, tn), F32)] if nk > 1 else []) + [s for jb in jobs for s in jb.sems]
    outs = pl.pallas_call(
        body, name=name, out_shape=tuple(all_out_shapes), grid=grid,
        in_specs=in_specs, out_specs=tuple(all_out_specs), scratch_shapes=scratch,
        input_output_aliases=io_alias,
        compiler_params=_params(("arbitrary",) * 3 if jobs else ("parallel", "parallel", "arbitrary")),
    )(*ins)
    main = outs[0] if n_out == 1 else tuple(outs[:n_out])
    if not jobs:
        return main
    job_outs, o0 = [], n_out
    for jb in jobs:
        job_outs.append(list(outs[o0:o0 + len(jb.outs)]))
        o0 += len(jb.outs)
    return main, job_outs


def _lead(layer, block, index):
    if layer is None:
        return pl.BlockSpec(block, index)
    return pl.BlockSpec((None,) + block, lambda i, j, k: (layer,) + index(i, j, k))


def _w_spec(kind, layer, per, dims, tn, tk):
    if kind == "nat":
        if dims == "nn":
            return _lead(layer, (tk, tn), lambda i, j, k: (k, j))
        return _lead(layer, (tn, tk), lambda i, j, k: (j, k))
    if dims == "nn":
        q = per // tn
        return _lead(layer, (None, tk, tn), lambda i, j, k: (j // q, k, j % q))
    q = per // tk
    return _lead(layer, (None, tn, tk), lambda i, j, k: (k // q, j, k % q))


def _g_spec(kind, layer, per, tm, tn):
    if kind == "nat":
        return _lead(layer, (tm, tn), lambda i, j, k: (i, j))
    q = per // tn
    return _lead(layer, (None, tm, tn), lambda i, j, k: (j // q, i, j % q))


TOK = (1408, 768, 256, 128)
FEAT = (1024, 512, 256, 128)
KDIM = (2048, 1536, 1024, 512, 256, 128)


def _sel(is_ctx, ref):
    return jnp.where(is_ctx, ref[1:2, :], ref[0:1, :])


def _row_tile(n, n_ctx):
    tr = _pick(n_ctx, (256, 128))
    assert n % tr == 0 and n_ctx % tr == 0
    return tr


def _add_by_segment(acc_ref, cols, ctx_tile, v):
    zero = jnp.zeros_like(v)
    acc_ref[0:1, cols] += jnp.where(ctx_tile, zero, v)
    acc_ref[1:2, cols] += jnp.where(ctx_tile, v, zero)


def _norm_mod(t, gain, mod8, shift_k, n_ctx, name, jobs=()):
    n, d = t.shape
    tr = _row_tile(n, n_ctx)

    def body(t_ref, g_ref, sh_ref, sc_ref, o_ref):
        x = t_ref[...]
        r = lax.rsqrt(_rowmean(x * x) + EPS)
        y = (x * r) * g_ref[...]
        ctx_tile = pl.program_id(0) * tr < n_ctx
        o_ref[...] = (y * (1.0 + _sel(ctx_tile, sc_ref)) + _sel(ctx_tile, sh_ref)).astype(BF16)

    j_ins, j_in_specs, j_outs, j_out_specs, j_sems, io_alias = _carry_args(jobs, 4, 1)
    outs = pl.pallas_call(
        _carry(body, jobs, 4, 1, n // tr), name=name, grid=(n // tr,),
        out_shape=(jax.ShapeDtypeStruct((n, d), BF16),) + tuple(j_outs),
        in_specs=[pl.BlockSpec((tr, d), lambda i: (i, 0)), pl.BlockSpec((1, d), lambda i: (0, 0)),
                  pl.BlockSpec((8, d), lambda i: (0, shift_k)), pl.BlockSpec((8, d), lambda i: (0, shift_k + 1))] + j_in_specs,
        out_specs=(pl.BlockSpec((tr, d), lambda i: (i, 0)),) + tuple(j_out_specs),
        scratch_shapes=j_sems, input_output_aliases=io_alias,
        compiler_params=_params(("arbitrary",) if jobs else ("parallel",)),
    )(t, gain, mod8, mod8, *j_ins)
    return (outs[0], _carry_results(jobs, outs, 1)) if jobs else outs[0]


def _norm_mod_bwd(dh, t, d_res, gain, mod8, shift_k, n_ctx, name, gated=None, latent_only=False, jobs=()):
    n, d = t.shape
    tr = _row_tile(n, n_ctx)

    def body(*refs):
        if gated is None:
            dh_ref, t_ref, dr_ref, g_ref, sc_ref, dt_ref, dss_ref, dg_ref = refs
        else:
            dh_ref, t_ref, dr_ref, g_ref, sc_ref, br_ref, gt_ref, dt_ref, dss_ref, dg_ref, dob_ref, dgate_ref = refs
        i = pl.program_id(0)

        @pl.when(i == 0)
        def _():
            dss_ref[...] = jnp.zeros_like(dss_ref)
            dg_ref[...] = jnp.zeros_like(dg_ref)
            if gated is not None:
                dgate_ref[...] = jnp.zeros_like(dgate_ref)

        x = t_ref[...]
        r = lax.rsqrt(_rowmean(x * x) + EPS)
        xn = x * r
        g = g_ref[...]
        y = xn * g
        dhv = dh_ref[...]
        ctx_tile = i * tr < n_ctx
        _add_by_segment(dss_ref, slice(0, d), ctx_tile, _colsum(dhv))
        _add_by_segment(dss_ref, slice(d, 2 * d), ctx_tile, _colsum(dhv * y))
        dy = dhv * (1.0 + _sel(ctx_tile, sc_ref))
        dg_ref[0:1, :] += _colsum(dy * xn)
        dxn = dy * g
        d_t = dr_ref[...] + r * (dxn - xn * _rowmean(dxn * xn))
        dt_ref[...] = d_t
        if gated is not None:
            dob_ref[...] = (d_t * _sel(ctx_tile, gt_ref)).astype(BF16)
            _add_by_segment(dgate_ref, slice(None), ctx_tile, _colsum(d_t * br_ref[...].astype(F32)))

    row = pl.BlockSpec((tr, d), lambda i: (i, 0))
    acc = pl.BlockSpec((8, d), lambda i: (0, 0))
    ins = [dh, t, d_res, gain, mod8]
    in_specs = [row, row, row, pl.BlockSpec((1, d), lambda i: (0, 0)), pl.BlockSpec((8, d), lambda i: (0, shift_k + 1))]
    if latent_only:
        first = n_ctx // tr
        out_shape = [jax.ShapeDtypeStruct((n - n_ctx, d), F32)]
        out_specs = [pl.BlockSpec((tr, d), lambda i: (jnp.maximum(i - first, 0), 0))]
    else:
        out_shape, out_specs = [jax.ShapeDtypeStruct((n, d), F32)], [row]
    out_shape += [jax.ShapeDtypeStruct((8, 2 * d), F32), jax.ShapeDtypeStruct((8, d), F32)]
    out_specs += [pl.BlockSpec((8, 2 * d), lambda i: (0, 0)), acc]
    if gated is not None:
        branch, gate_mod8, gate_k = gated
        ins += [branch, gate_mod8]
        in_specs += [row, pl.BlockSpec((8, d), lambda i: (0, gate_k))]
        out_shape += [jax.ShapeDtypeStruct((n, d), BF16), jax.ShapeDtypeStruct((8, d), F32)]
        out_specs += [row, acc]
    n_in, n_out = len(ins), len(out_shape)
    j_ins, j_in_specs, j_outs, j_out_specs, j_sems, io_alias = _carry_args(jobs, n_in, n_out)
    outs = pl.pallas_call(
        _carry(body, jobs, n_in, n_out, n // tr), name=name, out_shape=tuple(out_shape + j_outs), grid=(n // tr,),
        in_specs=in_specs + j_in_specs, out_specs=tuple(out_specs + j_out_specs), scratch_shapes=j_sems,
        input_output_aliases=io_alias, compiler_params=_params(("arbitrary",)),
    )(*ins, *j_ins)
    return (tuple(outs[:n_out]), _carry_results(jobs, outs, n_out)) if jobs else outs


def _swap16(v):
    w = v.shape[1]
    lane = lax.broadcasted_iota(jnp.int32, v.shape, 1)
    return jnp.where((lane % 32) < 16, pltpu.roll(v, w - 16, 1), pltpu.roll(v, 16, 1))


def _rope(v, cs, sn, sign):
    reps = v.shape[1] // 128
    c = jnp.tile(cs, (1, reps)) if reps > 1 else cs
    s = jnp.tile(sn, (1, reps)) if reps > 1 else sn
    return v * c + sign * (_swap16(v) * s)


def _attn_specs(nb, n_ctx, hd, kd):
    kci = hd // kd
    specs = [pl.BlockSpec((BLOCK, hd), lambda b: (b, 0)),
             pl.BlockSpec((n_ctx, kd), lambda b: (0, kci)), pl.BlockSpec((n_ctx, kd), lambda b: (0, kci + 1))]
    for col in (kci, kci + 1):
        specs.append(pl.BlockSpec((BLOCK, kd), lambda b, col=col: (jnp.maximum(b - 1, 0), col)))
        specs.append(pl.BlockSpec((BLOCK, kd), lambda b, col=col: (b, col)))
        specs.append(pl.BlockSpec((BLOCK, kd), lambda b, col=col: (jnp.minimum(b + 1, nb - 1), col)))
    return specs


def _band_valid(b, group, n_ctx, n):
    q_pos = b * BLOCK + lax.broadcasted_iota(jnp.int32, (group * BLOCK, 1), 0) % BLOCK
    k_pos = (b - 1) * BLOCK + lax.broadcasted_iota(jnp.int32, (1, 3 * BLOCK), 1)
    return (jnp.abs(k_pos - q_pos) <= BLOCK) & (k_pos >= n_ctx) & (k_pos < n) & (q_pos >= n_ctx)


NT = (((1,), (1,)), ((), ()))
NN = (((1,), (0,)), ((), ()))
TN = (((0,), (0,)), ((), ()))


def _dot(a, b, dn):
    return lax.dot_general(a, b, dn, preferred_element_type=F32)


def _keys_of_block(b, kv_refs, group, n_ctx, n):
    kc_ref, vc_ref, k0, k1, k2, v0, v1, v2 = kv_refs
    bias = jnp.where(_band_valid(b, group, n_ctx, n), 0.0, NEG_INF)
    kcat = jnp.concatenate([kc_ref[...], k0[...], k1[...], k2[...]], axis=0)
    vcat = jnp.concatenate([vc_ref[...], v0[...], v1[...], v2[...]], axis=0)
    return bias, kcat, vcat


def _stack_heads(v, h, group):
    return jnp.concatenate([v[:, (h * group + j) * HEAD_DIM:(h * group + j + 1) * HEAD_DIM] for j in range(group)], axis=0)


def _scores(qg, keys, bias, n_ctx):
    s = _dot(qg, keys, NT)
    return jnp.concatenate([s[:, :n_ctx], s[:, n_ctx:] + bias], axis=1)


def _softmax_sink(s, sink_ref, h, group):
    snk = jnp.concatenate([jnp.full((BLOCK, 1), sink_ref[h * group + j], F32) for j in range(group)], axis=0)
    m = jnp.maximum(jnp.max(s, axis=1, keepdims=True), snk)
    e, e_s = jnp.exp(s - m), jnp.exp(snk - m)
    inv = 1.0 / (jnp.sum(e, axis=1, keepdims=True) + e_s)
    return e * inv, e_s * inv


def _attention(qkv, sink, n_ctx, hd, kd, name, jobs=()):
    n = qkv.shape[0]
    nb = n // BLOCK
    n_kv = kd // HEAD_DIM
    group = hd // kd
    n_job_in = sum(len(jb.ins) for jb in jobs)
    n_job_out = sum(len(jb.outs) for jb in jobs)

    def body(*refs):
        q_ref, kc_ref, vc_ref, k0, k1, k2, v0, v1, v2, sink_ref = refs[:10]
        o_ref = refs[10 + n_job_in]
        job_parts = _split_jobs(jobs, refs[10:10 + n_job_in], refs[11 + n_job_in:11 + n_job_in + n_job_out],
                                refs[11 + n_job_in + n_job_out:])
        b = pl.program_id(0)

        if jobs:
            @pl.when(b == 0)
            def _():
                for jb, i_r, o_r, s_r in job_parts:
                    jb.start(i_r, o_r, s_r)

        bias, kcat, vcat = _keys_of_block(b, (kc_ref, vc_ref, k0, k1, k2, v0, v1, v2), group, n_ctx, n)
        q = q_ref[...]

        def scores(h):
            return _scores(_stack_heads(q, h, group), kcat[:, h * HEAD_DIM:(h + 1) * HEAD_DIM], bias, n_ctx)

        heads, s_next = [], scores(0)
        for h in range(n_kv):
            s = s_next
            if h + 1 < n_kv:
                s_next = scores(h + 1)
            p, _ = _softmax_sink(s, sink_ref, h, group)
            o = _dot(p.astype(BF16), vcat[:, h * HEAD_DIM:(h + 1) * HEAD_DIM], NN)
            heads += [o[j * BLOCK:(j + 1) * BLOCK, :] for j in range(group)]
        o_ref[...] = jnp.concatenate(heads, axis=1)

        if jobs:
            @pl.when(b == nb - 1)
            def _():
                for jb, i_r, o_r, s_r in job_parts:
                    jb.wait(i_r, o_r, s_r)

    io_alias, i0, o0 = {}, 10, 1
    for jb in jobs:
        io_alias.update({i0 + a_: o0 + b_ for a_, b_ in jb.alias.items()})
        i0, o0 = i0 + len(jb.ins), o0 + len(jb.outs)
    outs = pl.pallas_call(
        body, name=name, grid=(nb,),
        out_shape=(jax.ShapeDtypeStruct((n, hd), F32),) + tuple(s for jb in jobs for s in jb.outs),
        in_specs=_attn_specs(nb, n_ctx, hd, kd) + [pl.BlockSpec(memory_space=pltpu.SMEM)] + [ANY] * n_job_in,
        out_specs=(pl.BlockSpec((BLOCK, hd), lambda b: (b, 0)),) + tuple([ANY] * n_job_out),
        scratch_shapes=[s for jb in jobs for s in jb.sems], input_output_aliases=io_alias,
        compiler_params=_params(("arbitrary",) if jobs else ("parallel",)),
    )(qkv, qkv, qkv, qkv, qkv, qkv, qkv, qkv, qkv, sink, *[v for jb in jobs for v in jb.ins])
    if not jobs:
        return outs[0]
    job_outs, o0 = [], 1
    for jb in jobs:
        job_outs.append(list(outs[o0:o0 + len(jb.outs)]))
        o0 += len(jb.outs)
    return outs[0], job_outs


def _attention_bwd(qkv, sink, ao, d_mg, cpar, n_ctx, hd, kd, name):
    n = qkv.shape[0]
    nb = n // BLOCK
    n_kv = kd // HEAD_DIM
    group = hd // kd
    n_heads = n_kv * group

    def body(q_ref, kc_ref, vc_ref, k0, k1, k2, v0, v1, v2, sink_ref, ao_ref, dmg_ref, cp_ref,
             dq_ref, part_ref, dctx_ref, dsink_ref, dgain_ref):
        b = pl.program_id(0)

        @pl.when(b == 0)
        def _():
            dctx_ref[...] = jnp.zeros_like(dctx_ref)
            dsink_ref[...] = jnp.zeros_like(dsink_ref)
            dgain_ref[...] = jnp.zeros_like(dgain_ref)

        ao_v = ao_ref[...]
        ra = lax.rsqrt(_rowmean(ao_v * ao_v) + EPS)
        an = ao_v * ra
        dmg = dmg_ref[...]
        dgain_ref[0:1, :] += _colsum(dmg * an)
        d_an = dmg * cp_ref[5:6, :]
        d_ao = (ra * (d_an - an * _rowmean(d_an * an))).astype(BF16)

        bias, kcat, vcat = _keys_of_block(b, (kc_ref, vc_ref, k0, k1, k2, v0, v1, v2), group, n_ctx, n)
        q = q_ref[...]
        lane = lax.broadcasted_iota(jnp.int32, (1, 128), 1)
        dsink_row = jnp.zeros((1, 128), F32)

        def first_half(h):
            hs = slice(h * HEAD_DIM, (h + 1) * HEAD_DIM)
            qg, dog = _stack_heads(q, h, group), _stack_heads(d_ao, h, group)
            return qg, dog, _scores(qg, kcat[:, hs], bias, n_ctx), _dot(dog, vcat[:, hs], NT)

        dq_heads, dk, dv = [], [], []
        nxt = first_half(0)
        for h in range(n_kv):
            qg, dog, s, d_p = nxt
            if h + 1 < n_kv:
                nxt = first_half(h + 1)
            p, p_s = _softmax_sink(s, sink_ref, h, group)
            delta = jnp.sum(p * d_p, axis=1, keepdims=True)
            ds = (p * (d_p - delta)).astype(BF16)
            psd = p_s * delta
            for j in range(group):
                val = -jnp.sum(psd[j * BLOCK:(j + 1) * BLOCK, :], axis=0, keepdims=True)
                dsink_row = dsink_row + jnp.where(lane == h * group + j, val, 0.0)
            dq = _dot(ds, kcat[:, h * HEAD_DIM:(h + 1) * HEAD_DIM], NN) * SCALE
            dq_heads += [dq[j * BLOCK:(j + 1) * BLOCK, :] for j in range(group)]
            dk.append(_dot(ds, qg, TN))
            dv.append(_dot(p.astype(BF16), dog, TN))
        dq_ref[...] = jnp.concatenate(dq_heads, axis=1)
        d_kv = jnp.concatenate(dk + dv, axis=1)
        dctx_ref[...] += d_kv[:n_ctx]
        for j in range(3):
            part_ref[j] = d_kv[n_ctx + j * BLOCK:n_ctx + (j + 1) * BLOCK, :]
        dsink_ref[0:1, :] += dsink_row

    assert n_heads <= 128
    out_shape = (jax.ShapeDtypeStruct((n, hd), F32), jax.ShapeDtypeStruct((nb, 3, BLOCK, 2 * kd), F32),
                 jax.ShapeDtypeStruct((n_ctx, 2 * kd), F32), jax.ShapeDtypeStruct((8, 128), F32),
                 jax.ShapeDtypeStruct((8, hd), F32))
    return pl.pallas_call(
        body, name=name, out_shape=out_shape, grid=(nb,),
        in_specs=_attn_specs(nb, n_ctx, hd, kd) + [
            pl.BlockSpec(memory_space=pltpu.SMEM), pl.BlockSpec((BLOCK, hd), lambda b: (b, 0)),
            pl.BlockSpec((BLOCK, hd), lambda b: (b, 1)), pl.BlockSpec((8, hd), lambda b: (0, 0))],
        out_specs=(pl.BlockSpec((BLOCK, hd), lambda b: (b, 0)),
                   pl.BlockSpec((None, 3, BLOCK, 2 * kd), lambda b: (b, 0, 0, 0)),
                   pl.BlockSpec((n_ctx, 2 * kd), lambda b: (0, 0)), pl.BlockSpec((8, 128), lambda b: (0, 0)),
                   pl.BlockSpec((8, hd), lambda b: (0, 0))),
        compiler_params=_params(("arbitrary",)),
    )(qkv, qkv, qkv, qkv, qkv, qkv, qkv, qkv, qkv, sink, ao, d_mg, cpar)


def _halo_specs(tr, n, width, col=0):
    q = tr // 8
    return [pl.BlockSpec((8, width), lambda i: (jnp.maximum(i * q - 1, 0), col)),
            pl.BlockSpec((8, width), lambda i: (jnp.minimum((i + 1) * q, n // 8 - 1), col))]


def _mix_fwd(p, ao, cpar, n_ctx, dc, name):
    n = p.shape[0]
    tr = _pick(n, (256, 128))

    def body(p_ref, pp_ref, pn_ref, ao_ref, cp_ref, o_ref):
        i = pl.program_id(0)
        bg = p_ref[:, 0:dc]
        u = p_ref[:, dc:2 * dc] * p_ref[:, 2 * dc:3 * dc]
        u_before = pp_ref[7:8, dc:2 * dc] * pp_ref[7:8, 2 * dc:3 * dc]
        u_after = pn_ref[0:1, dc:2 * dc] * pn_ref[0:1, 2 * dc:3 * dc]
        loc = lax.broadcasted_iota(jnp.int32, (tr, 1), 0)
        gid = i * tr + loc
        has_prev = (gid != 0) & (gid != n_ctx)
        has_next = (gid != n_ctx - 1) & (gid != n - 1)
        u_m1 = jnp.where(has_prev, jnp.where(loc == 0, u_before, pltpu.roll(u, 1, 0)), 0.0)
        u_p1 = jnp.where(has_next, jnp.where(loc == tr - 1, u_after, pltpu.roll(u, tr - 1, 0)), 0.0)
        cv = u_m1 * cp_ref[0:1, :] + u * cp_ref[1:2, :] + u_p1 * cp_ref[2:3, :] + cp_ref[3:4, :]
        co = bg * cv
        nc = (co * lax.rsqrt(_rowmean(co * co) + EPS)) * cp_ref[4:5, :]
        ao_v = ao_ref[...]
        na = (ao_v * lax.rsqrt(_rowmean(ao_v * ao_v) + EPS)) * cp_ref[5:6, :]
        o_ref[...] = jnp.concatenate([nc, na], axis=1).astype(BF16)

    return pl.pallas_call(
        body, name=name, out_shape=jax.ShapeDtypeStruct((n, 2 * dc), BF16), grid=(n // tr,),
        in_specs=[pl.BlockSpec((tr, 3 * dc), lambda i: (i, 0))] + _halo_specs(tr, n, 3 * dc)
        + [pl.BlockSpec((tr, dc), lambda i: (i, 0)), pl.BlockSpec((8, dc), lambda i: (0, 0))],
        out_specs=pl.BlockSpec((tr, 2 * dc), lambda i: (i, 0)),
        compiler_params=_params(("parallel",)),
    )(p, p, p, ao, cpar)


def _mix_bwd(d_mg, p, cpar, d_q, parts, d_ctx, cs, sn, n_ctx, dc, hd, kd, name):
    n = p.shape[0]
    d_in = 3 * dc + hd + 2 * kd
    tr = BLOCK
    nb = n // tr
    ext = tr + 16
    n_ctx_blocks = n_ctx // BLOCK

    def body(dm_ref, dmp_ref, dmn_ref, p_ref, pp_ref, pn_ref, cp_ref, dq_ref, pa_ref, pb_ref, pc_ref, dctx_ref,
             cs_ref, sn_ref, dp_ref, acc_ref):
        i = pl.program_id(0)

        @pl.when(i == 0)
        def _():
            acc_ref[...] = jnp.zeros_like(acc_ref)

        def cat(before, here, after):
            return jnp.concatenate([before, here, after], axis=0)

        bg = cat(pp_ref[:, 0:dc], p_ref[:, 0:dc], pn_ref[:, 0:dc])
        cg = cat(pp_ref[:, dc:2 * dc], p_ref[:, dc:2 * dc], pn_ref[:, dc:2 * dc])
        hh = cat(pp_ref[:, 2 * dc:3 * dc], p_ref[:, 2 * dc:3 * dc], pn_ref[:, 2 * dc:3 * dc])
        dme = cat(dmp_ref[...], dm_ref[...], dmn_ref[...])
        gid = i * tr - 8 + lax.broadcasted_iota(jnp.int32, (ext, 1), 0)
        inside = (gid >= 0) & (gid < n)
        has_prev = inside & (gid != 0) & (gid != n_ctx)
        has_next = inside & (gid != n_ctx - 1) & (gid != n - 1)
        w0, w1, w2, bias, gain = cp_ref[0:1, :], cp_ref[1:2, :], cp_ref[2:3, :], cp_ref[3:4, :], cp_ref[4:5, :]
        u = jnp.where(inside, cg * hh, 0.0)
        u_m1 = jnp.where(has_prev, pltpu.roll(u, 1, 0), 0.0)
        u_p1 = jnp.where(has_next, pltpu.roll(u, ext - 1, 0), 0.0)
        cv = u_m1 * w0 + u * w1 + u_p1 * w2 + bias
        co = bg * cv
        rc = lax.rsqrt(_rowmean(co * co) + EPS)
        cn = co * rc
        d_cn = dme * gain
        d_co = rc * (d_cn - cn * _rowmean(d_cn * cn))
        d_cv = jnp.where(inside, d_co * bg, 0.0)
        d_bg = d_co * cv
        d_cv_p1 = jnp.where(has_next, pltpu.roll(d_cv, ext - 1, 0), 0.0)
        d_cv_m1 = jnp.where(has_prev, pltpu.roll(d_cv, 1, 0), 0.0)
        d_u = d_cv_p1 * w0 + d_cv * w1 + d_cv_m1 * w2
        mid = slice(8, 8 + tr)
        acc_ref[0:1, :] += _colsum((d_cv * u_m1)[mid])
        acc_ref[1:2, :] += _colsum((d_cv * u)[mid])
        acc_ref[2:3, :] += _colsum((d_cv * u_p1)[mid])
        acc_ref[3:4, :] += _colsum(d_cv[mid])
        acc_ref[4:5, :] += _colsum((dme * cn)[mid])

        d_kv = (jnp.where(i >= 1, pa_ref[...], 0.0) + pb_ref[...] + jnp.where(i + 1 < nb, pc_ref[...], 0.0))
        ctx_rows = dctx_ref[pl.ds(pl.multiple_of(jnp.minimum(i, n_ctx_blocks - 1) * BLOCK, BLOCK), BLOCK), :]
        d_kv = d_kv + jnp.where(i < n_ctx_blocks, ctx_rows, 0.0)
        cs_v, sn_v = cs_ref[...], sn_ref[...]
        d_qu = _rope(dq_ref[...], cs_v, sn_v, -1.0)
        d_ku = _rope(d_kv[:, 0:kd], cs_v, sn_v, -1.0)
        dp_ref[...] = jnp.concatenate(
            [d_bg[mid], (d_u * hh)[mid], (d_u * cg)[mid], d_qu, d_ku, d_kv[:, kd:2 * kd]], axis=1).astype(BF16)

    part = lambda sel, which: pl.BlockSpec((None, None, BLOCK, 2 * kd), lambda i: (sel(i), which, 0, 0))
    return pl.pallas_call(
        body, name=name, out_shape=(jax.ShapeDtypeStruct((n, d_in), BF16), jax.ShapeDtypeStruct((8, dc), F32)),
        grid=(nb,),
        in_specs=[pl.BlockSpec((tr, dc), lambda i: (i, 0))] + _halo_specs(tr, n, dc)
        + [pl.BlockSpec((tr, 3 * dc), lambda i: (i, 0))] + _halo_specs(tr, n, 3 * dc)
        + [pl.BlockSpec((8, dc), lambda i: (0, 0)), pl.BlockSpec((tr, hd), lambda i: (i, 0)),
           part(lambda i: jnp.maximum(i - 1, 0), 2), part(lambda i: i, 1), part(lambda i: jnp.minimum(i + 1, nb - 1), 0),
           pl.BlockSpec((n_ctx, 2 * kd), lambda i: (0, 0)),
           pl.BlockSpec((tr, 128), lambda i: (i, 0)), pl.BlockSpec((tr, 128), lambda i: (i, 0))],
        out_specs=(pl.BlockSpec((tr, d_in), lambda i: (i, 0)), pl.BlockSpec((8, dc), lambda i: (0, 0))),
        compiler_params=_params(("arbitrary",)),
    )(d_mg, d_mg, d_mg, p, p, p, cpar, d_q, parts, parts, parts, d_ctx, cs, sn)


def _loss_bwd(t, gain, target, branch, mod8, gate_k, n_ctx, name):
    n, d = t.shape
    tr = _pick(n_ctx, (256, 128))
    first = n_ctx // tr

    def body(t_ref, g_ref, y_ref, br_ref, gt_ref, dt_ref, loss_ref, dg_ref, dob_ref, dgate_ref):
        i = pl.program_id(0)

        @pl.when(i == 0)
        def _():
            loss_ref[...] = jnp.zeros_like(loss_ref)
            dg_ref[...] = jnp.zeros_like(dg_ref)
            dgate_ref[...] = jnp.zeros_like(dgate_ref)

        @pl.when(i < first)
        def _():
            dt_ref[...] = jnp.zeros_like(dt_ref)
            dob_ref[...] = jnp.zeros_like(dob_ref)

        @pl.when(i >= first)
        def _():
            x = t_ref[...]
            g = g_ref[...]
            r = lax.rsqrt(_rowmean(x * x) + EPS)
            xn = x * r
            err = xn * g - y_ref[...]
            loss_ref[...] += 0.5 * _colsum(_rowmean(err * err))
            dy = err * (1.0 / d)
            dg_ref[0:1, :] += _colsum(dy * xn)
            dxn = dy * g
            d_t = r * (dxn - xn * _rowmean(dxn * xn))
            dt_ref[...] = d_t
            dob_ref[...] = (d_t * gt_ref[0:1, :]).astype(BF16)
            dgate_ref[0:1, :] += _colsum(d_t * br_ref[...].astype(F32))

    row = pl.BlockSpec((tr, d), lambda i: (i, 0))
    acc = pl.BlockSpec((8, d), lambda i: (0, 0))
    return pl.pallas_call(
        body, name=name,
        out_shape=(jax.ShapeDtypeStruct((n, d), F32), jax.ShapeDtypeStruct((8, 128), F32), jax.ShapeDtypeStruct((8, d), F32),
                   jax.ShapeDtypeStruct((n, d), BF16), jax.ShapeDtypeStruct((8, d), F32)),
        grid=(n // tr,),
        in_specs=[row, pl.BlockSpec((1, d), lambda i: (0, 0)), pl.BlockSpec((tr, d), lambda i: (jnp.maximum(i - first, 0), 0)),
                  row, pl.BlockSpec((8, d), lambda i: (0, gate_k))],
        out_specs=(row, pl.BlockSpec((8, 128), lambda i: (0, 0)), acc, row, acc),
        compiler_params=_params(("arbitrary",)),
    )(t, gain, target, branch, mod8)


def _silu16(c16, name):
    def body(c_ref, o_ref):
        v = c_ref[...]
        o_ref[...] = (v * jax.nn.sigmoid(v)).astype(BF16)

    return pl.pallas_call(body, name=name, out_shape=jax.ShapeDtypeStruct(c16.shape, BF16))(c16)


def _cctx_grad(parts, c_ctx, name):
    def body(p_ref, c_ref, o_ref):
        g = _colsum(p_ref[...])
        v = c_ref[...]
        s = jax.nn.sigmoid(v)
        o_ref[...] = g * (s * (1.0 + v * (1.0 - s)))

    return pl.pallas_call(body, name=name, out_shape=jax.ShapeDtypeStruct(c_ctx.shape, F32))(parts, c_ctx)


def _rope_tables(n_ctx, n_tok):
    half = HEAD_DIM // 4
    inv = ROPE_THETA ** (-jnp.arange(0, HEAD_DIM // 2, 2, dtype=F32) / (HEAD_DIM // 2))
    rows = n_tok // GRID_W
    row_pos = jnp.repeat(jnp.arange(rows, dtype=F32), GRID_W)
    col_pos = jnp.tile(jnp.arange(GRID_W, dtype=F32), rows)
    ang_r, ang_c = row_pos[:, None] * inv[None, :], col_pos[:, None] * inv[None, :]
    cos = jnp.concatenate([jnp.cos(ang_r), jnp.cos(ang_r), jnp.cos(ang_c), jnp.cos(ang_c)], axis=1)
    sin = jnp.concatenate([-jnp.sin(ang_r), jnp.sin(ang_r), -jnp.sin(ang_c), jnp.sin(ang_c)], axis=1)
    assert cos.shape[1] == 4 * half == HEAD_DIM
    cos = jnp.concatenate([jnp.ones((n_ctx, HEAD_DIM), F32), cos], axis=0)
    sin = jnp.concatenate([jnp.zeros((n_ctx, HEAD_DIM), F32), sin], axis=0)
    return jnp.tile(cos, (1, 2)), jnp.tile(sin, (1, 2))


def kernel(x, c, ctx, c_ctx, w_ada, b_ada, g_norm1, g_norm2, w_in, conv_w, conv_b, sink, g_out_conv, g_out_attn, w_out, w_mlp1, w_mlp2, g_final, loss_target, m_c_ctx, m_w_ada, m_b_ada, m_g_norm1, m_g_norm2, m_w_in, m_conv_w, m_conv_b, m_sink, m_g_out_conv, m_g_out_attn, m_w_out, m_w_mlp1, m_w_mlp2, m_g_final, v_c_ctx, v_w_ada, v_b_ada, v_g_norm1, v_g_norm2, v_w_in, v_conv_w, v_conv_b, v_sink, v_g_out_conv, v_g_out_attn, v_w_out, v_w_mlp1, v_w_mlp2, v_g_final):
    n_lat, d = x.shape[1], x.shape[2]
    n_ctx = ctx.shape[1]
    n = n_ctx + n_lat
    depth = w_in.shape[0]
    dc = d // 2
    hd, kd = dc, N_KV_HEADS * HEAD_DIM
    n_heads = hd // HEAD_DIM
    d_in = 3 * dc + hd + 2 * kd
    cin, c_ada, r_out, c_ff, r_ff = w_in.shape[2], w_ada.shape[2], w_out.shape[1], w_mlp1.shape[2], w_mlp2.shape[1]
    d_ff = N_DEV * c_ff
    cw = conv_w.shape[2]
    assert d_in == N_DEV * cin and n_ctx % BLOCK == 0 and n_lat % BLOCK == 0 and hd % kd == 0
    dev = 4 * lax.axis_index("x") + 2 * lax.axis_index("y") + lax.axis_index("c")

    c_all, conv_w_all = _all_gather([c, conv_w], "gather_cond")
    conv_w_full = jnp.transpose(conv_w_all, (1, 2, 0, 3)).reshape(depth, 3, dc)
    c16 = jnp.concatenate([c_all.reshape(N_DEV, d), jnp.broadcast_to(c_ctx[None, :], (8, d))], axis=0)
    sc16 = _silu16(c16, "silu_cond")

    b_ada_loc = lax.dynamic_index_in_dim(b_ada.reshape(depth, N_DEV, c_ada), dev, axis=1, keepdims=False)
    tn_ada = _pick(c_ada, FEAT)

    def add_bias(acc, i, j, extra, outs):
        outs[0][...] = acc + extra[0][...]

    mod_loc = []
    for l in range(depth):
        mod_loc.append(_matmul(
            sc16, w_ada, dims="nn", shape=(16, c_ada, d), tiles=(16, tn_ada, _pick(d, KDIM)),
            b_spec=_w_spec("nat", l, None, "nn", tn_ada, _pick(d, KDIM)), epilogue=add_bias,
            extras=[(b_ada_loc[l][None, :], pl.BlockSpec((1, tn_ada), lambda i, j, k: (0, j)))], name=f"ada_fwd{l}"))
    (mod_all,) = _all_gather([jnp.stack(mod_loc)], "gather_mod")
    mod_full = jnp.transpose(mod_all, (1, 2, 0, 3)).reshape(depth, 16, N_MOD * d)
    mod_mine = lax.dynamic_index_in_dim(mod_full, dev, axis=1, keepdims=True)
    mod8 = jnp.concatenate([mod_mine, mod_full[:, 8:9], jnp.zeros((depth, 6, N_MOD * d), F32)], axis=1)

    w_in_b = _to_bf16(w_in.reshape(depth * d, cin), "cast_w_in").reshape(depth, d, cin)
    w_out_b = _to_bf16(w_out.reshape(depth * r_out, d), "cast_w_out").reshape(depth, r_out, d)
    w1_b = _to_bf16(w_mlp1.reshape(depth * d, c_ff), "cast_w_mlp1").reshape(depth, d, c_ff)
    w2_b = _to_bf16(w_mlp2.reshape(depth * r_ff, d), "cast_w_mlp2").reshape(depth, r_ff, d)
    def full_in(g):
        return jnp.transpose(g, (1, 0, 2)).reshape(d, d_in)

    w_in_full, w_out_full, g_w1, w2_full = [], [], [], []
    pend_w2 = None

    t = jnp.concatenate([ctx[0], x[0]], axis=0)
    cs, sn = _rope_tables(n_ctx, n_lat)
    tm = _pick(n, TOK)
    tk_d = _pick(d, KDIM)
    tm_res = _pick(n, (1056, 768, 256, 128))
    tn_in = _pick(math.gcd(3 * dc, hd + 2 * kd), (512, 256, 128))
    conv_tiles = 3 * dc // tn_in
    bounds = [(0, hd, "q"), (hd, hd + kd, "k"), (hd + kd, hd + 2 * kd, "v")]
    qkv_tiles = []
    for jj in range((hd + 2 * kd) // tn_in):
        lo, hi = jj * tn_in, (jj + 1) * tn_in
        segs = [(max(lo, a) - lo, min(hi, b) - lo, kind) for a, b, kind in bounds if a < hi and b > lo]
        assert all((e - s) % 128 == 0 for s, e, _ in segs)
        qkv_tiles.append(segs)

    def in_proj_epilogue(acc, i, j, extra, outs):
        @pl.when(j < conv_tiles)
        def _():
            outs[0][...] = acc

        for jj, segs in enumerate(qkv_tiles):
            @pl.when(j == conv_tiles + jj)
            def _(segs=segs):
                cs_v, sn_v = extra[0][...], extra[1][...]
                pieces = []
                for s0, s1, kind in segs:
                    v = acc[:, s0:s1]
                    if kind == "q":
                        v = _rope(v, cs_v, sn_v, 1.0) * SCALE
                    elif kind == "k":
                        v = _rope(v, cs_v, sn_v, 1.0)
                    pieces.append(v)
                outs[1][...] = (pieces[0] if len(pieces) == 1 else jnp.concatenate(pieces, axis=1)).astype(BF16)

    def in_proj_specs(rows):
        return dict(
            out_specs=(pl.BlockSpec((rows, tn_in), lambda i, j, k: (i, jnp.minimum(j, conv_tiles - 1))),
                       pl.BlockSpec((rows, tn_in), lambda i, j, k: (i, jnp.maximum(j - conv_tiles, 0)))),
            out_shapes=(jax.ShapeDtypeStruct((n, 3 * dc), F32), jax.ShapeDtypeStruct((n, hd + 2 * kd), BF16)),
            out_dtypes=(F32, BF16), epilogue=in_proj_epilogue,
            extras=[(cs, pl.BlockSpec((rows, 128), lambda i, j, k: (i, 0))), (sn, pl.BlockSpec((rows, 128), lambda i, j, k: (i, 0)))])

    def resid_epilogue(tile_rows):
        def epi(acc, i, j, extra, outs):
            is_ctx = _row_ids(i, tile_rows) < n_ctx
            outs[0][...] = extra[0][...] + _sel(is_ctx, extra[1]) * acc
            outs[1][...] = acc.astype(BF16)
        return epi

    def sq_relu_epilogue(acc, i, j, extra, outs):
        outs[0][...] = acc.astype(BF16)
        rl = jnp.maximum(acc, 0.0)
        outs[1][...] = (rl * rl).astype(BF16)

    def d_sq_relu_epilogue(acc, i, j, extra, outs):
        outs[0][...] = (acc * (2.0 * jnp.maximum(extra[0][...].astype(F32), 0.0))).astype(BF16)

    saved = []
    for l in range(depth):
        cpar = jnp.concatenate([conv_w_full[l], conv_b[l][None], g_out_conv[l][None], g_out_attn[l][None],
                                jnp.zeros((2, dc), F32)], axis=0)
        more = l + 1 < depth
        if l == 0:
            h, (gath0,) = _norm_mod(t, g_norm1[l][None], mod8[l], 0, n_ctx, f"norm1_{l}",
                                    jobs=[_job_gather_ici([w_in_b, w_out_b], 0)])
            (gath0,) = _comm_call([_job_gather_d2d(gath0)], "gather_w0_d2d")
            w_in_full.append(full_in(gath0[0]))
            w_out_full.append(gath0[1].reshape(d, d))
        else:
            h = _norm_mod(t, g_norm1[l][None], mod8[l], 0, n_ctx, f"norm1_{l}")
        jobs = ([_job_gather_d2d(pend_w2)] if pend_w2 is not None else []) + (
            [_job_gather_ici([w_in_b, w_out_b], l + 1)] if more else [])
        res = _matmul(h, w_in_full[l], dims="nn", shape=(n, d_in, d), tiles=(tm, tn_in, tk_d), jobs=jobs, name=f"in_proj{l}",
                      **in_proj_specs(tm))
        if jobs:
            (p, qkv), job_outs = res
            if pend_w2 is not None:
                w2_full.append(job_outs[0][0].reshape(d_ff, d))
            pend_io = job_outs[-1] if more else None
        else:
            p, qkv = res
        if l == 0:
            ao, (pend_mlp0,) = _attention(qkv, sink[l], n_ctx, hd, kd, f"attn{l}", jobs=[_job_gather_ici([w1_b, w2_b], 0)])
        else:
            ao = _attention(qkv, sink[l], n_ctx, hd, kd, f"attn{l}")
        mg = _mix_fwd(p, ao, cpar, n_ctx, dc, f"mix{l}")
        tn = _pick(d, (512, 256, 128))
        jobs = ([_job_gather_d2d(pend_io)] if more else []) + ([_job_gather_d2d(pend_mlp0)] if l == 0 else [])
        res = _matmul(
            mg, w_out_full[l], dims="nn", shape=(n, d, d), tiles=(tm, tn, tk_d),
            out_dtypes=(F32, BF16), epilogue=resid_epilogue(tm),
            extras=[(t, pl.BlockSpec((tm, tn), lambda i, j, k: (i, j))),
                    (mod8[l], pl.BlockSpec((8, tn), lambda i, j, k, tn=tn: (0, 2 * (d // tn) + j)))],
            jobs=jobs, name=f"out_proj{l}")
        if jobs:
            (t2, z), job_outs = res
            if more:
                w_in_full.append(full_in(job_outs[0][0]))
                w_out_full.append(job_outs[0][1].reshape(d, d))
            if l == 0:
                g_w1.append(job_outs[-1][0])
                w2_full.append(job_outs[-1][1].reshape(d_ff, d))
        else:
            t2, z = res
        h2 = _norm_mod(t2, g_norm2[l][None], mod8[l], 3, n_ctx, f"norm2_{l}")
        tn = _pick(c_ff, FEAT)
        res = _matmul(h2, g_w1[l], dims="nn", shape=(n, d_ff, d), tiles=(tm, tn, tk_d),
                      b_spec=_w_spec("cols", None, c_ff, "nn", tn, tk_d), out_dtypes=(BF16, BF16),
                      epilogue=sq_relu_epilogue, jobs=[_job_gather_ici([w1_b], l + 1)] if more else [], name=f"mlp_up{l}")
        if more:
            (a, s), ((pend_w1,),) = res
        else:
            a, s = res
        tn = _pick(d, FEAT)
        tk = _pick(d_ff, (1024, 512, 256, 128))
        res = _matmul(
            s, w2_full[l], dims="nn", shape=(n, d, d_ff), tiles=(tm_res, tn, tk),
            out_dtypes=(F32, BF16), epilogue=resid_epilogue(tm_res),
            extras=[(t2, pl.BlockSpec((tm_res, tn), lambda i, j, k: (i, j))),
                    (mod8[l], pl.BlockSpec((8, tn), lambda i, j, k, tn=tn: (0, 5 * (d // tn) + j)))],
            jobs=[_job_gather_ici([w2_b], l + 1), _job_gather_d2d([pend_w1])] if more else [], name=f"mlp_down{l}")
        if more:
            (t3, o), (pend_w2, (g_w1_next,)) = res
            g_w1.append(g_w1_next)
        else:
            (t3, o), pend_w2 = res, None
        saved.append((t, h, p, qkv, ao, mg, z, t2, h2, a, s, o, cpar))
        t = t3

    d_t, loss_tile, dg_final, dob, dgate2 = _loss_bwd(t, g_final[None], loss_target[0], saved[depth - 1][11], mod8[depth - 1], 5,
                                                      n_ctx, "loss")
    loss = lax.psum(loss_tile[0, 0], ("x", "y", "c"))

    buf_in = lax.empty((N_CHIP, depth, cin, d), BF16)
    buf_out = lax.empty((N_CHIP, depth, r_out, d), BF16)
    buf_w1 = lax.empty((N_CHIP, depth, d, c_ff), BF16)
    buf_w2 = lax.empty((N_CHIP, depth, r_ff, d), BF16)
    pend_in, pend_layer = None, None
    tkn = _pick(n, TOK)
    small = [None] * depth
    for l in reversed(range(depth)):
        t_in, h, p, qkv, ao, mg, z, t2, h2, a, s, o, cpar = saved[l]
        tm_g = _pick(d_ff, FEAT)
        tn = _pick(d, FEAT)
        gw2 = _matmul(s, dob, dims="tn", shape=(d_ff, d, n), tiles=(tm_g, tn, tkn), name=f"mlp_down_dw{l}")
        gw2 = gw2.reshape(N_DEV, r_ff, d)
        tn = _pick(d_ff, FEAT)
        jobs = [_job_scatter_d2d([gw2])] + ([_job_scatter_ici([pend_in], [buf_in], pend_layer)] if pend_in is not None else [])
        da, job_outs = _matmul(dob, w2_full[l], dims="nt", shape=(n, d_ff, d), tiles=(tm, tn, tk_d),
                               out_dtypes=(BF16,), epilogue=d_sq_relu_epilogue,
                               extras=[(a, pl.BlockSpec((tm, tn), lambda i, j, k: (i, j)))], jobs=jobs, name=f"mlp_down_dx{l}")
        if pend_in is not None:
            (buf_in,) = job_outs[1]
        pair_w2 = _pair_add(gw2, job_outs[0][0], f"pair_w2_{l}")
        tm_g = _pick(d, FEAT)
        tn = _pick(c_ff, FEAT)
        gw1 = _matmul(h2, da, dims="tn", shape=(d, d_ff, n), tiles=(tm_g, tn, tkn),
                      out_specs=(_g_spec("cols", None, c_ff, tm_g, tn),),
                      out_shapes=(jax.ShapeDtypeStruct((N_DEV, d, c_ff), F32),), name=f"mlp_up_dw{l}")
        tn = _pick(d, FEAT)
        tk = _pick(c_ff, (1024, 512, 256, 128))
        dh2, job_outs = _matmul(da, g_w1[l], dims="nt", shape=(n, d, d_ff), tiles=(tm, tn, tk),
                                b_spec=_w_spec("cols", None, c_ff, "nt", tn, tk),
                                jobs=[_job_scatter_d2d([gw1]), _job_scatter_ici([pair_w2], [buf_w2], l)], name=f"mlp_up_dx{l}")
        (buf_w2,) = job_outs[1]
        pair_w1 = _pair_add(gw1, job_outs[0][0], f"pair_w1_{l}")
        d_t2, dss2, dgn2, dzb, dgate1 = _norm_mod_bwd(dh2, t2, d_t, g_norm2[l][None], mod8[l], 3, n_ctx, f"norm2_bwd{l}",
                                                      gated=(z, mod8[l], 2))
        tm_g = _pick(d, FEAT)
        tn = _pick(d, FEAT)
        gout = _matmul(mg, dzb, dims="tn", shape=(d, d, n), tiles=(tm_g, tn, tkn), name=f"out_proj_dw{l}")
        gout = gout.reshape(N_DEV, r_out, d)
        tn = _pick(d, FEAT)
        d_mg, job_outs = _matmul(dzb, w_out_full[l], dims="nt", shape=(n, d, d), tiles=(tm, tn, tk_d),
                                 jobs=[_job_scatter_d2d([gout])], name=f"out_proj_dx{l}")
        pair_out = _pair_add(gout, job_outs[0][0], f"pair_out_{l}")
        d_q, parts, d_kv_ctx, d_sink, d_goa = _attention_bwd(qkv, sink[l], ao, d_mg, cpar, n_ctx, hd, kd, f"attn_bwd{l}")
        d_p, conv_acc = _mix_bwd(d_mg, p, cpar, d_q, parts, d_kv_ctx, cs, sn, n_ctx, dc, hd, kd, f"mix_bwd{l}")
        tm_g = _pick(d_in, (1536, 768, 512, 256, 128))
        tn = _pick(d, FEAT)
        gin = _matmul(d_p, h, dims="tn", shape=(d_in, d, n), tiles=(tm_g, tn, tkn), name=f"in_proj_dw{l}")
        gin = gin.reshape(N_DEV, cin, d)
        tn = _pick(d, FEAT)
        tk = _pick(d_in, (1536, 768, 512, 256, 128))
        dh, job_outs = _matmul(d_p, w_in_full[l], dims="nt", shape=(n, d, d_in), tiles=(tm, tn, tk),
                               jobs=[_job_scatter_d2d([gin]), _job_scatter_ici([pair_w1, pair_out], [buf_w1, buf_out], l)],
                               name=f"in_proj_dx{l}")
        buf_w1, buf_out = job_outs[1]
        pend_in, pend_layer = _pair_add(gin, job_outs[0][0], f"pair_in_{l}"), l
        d_mod2_tail = [dgate1[0:2], dss2[0:2], dgate2[0:2]]
        if l > 0:
            d_t, dss1, dgn1, dob, dgate2 = _norm_mod_bwd(dh, t_in, d_t2, g_norm1[l][None], mod8[l], 0, n_ctx, f"norm1_bwd{l}",
                                                         gated=(saved[l - 1][11], mod8[l - 1], 5))
        else:
            (grad_x, dss1, dgn1), ((buf_in,),) = _norm_mod_bwd(
                dh, t_in, d_t2, g_norm1[l][None], mod8[l], 0, n_ctx, f"norm1_bwd{l}", latent_only=True,
                jobs=[_job_scatter_ici([pend_in], [buf_in], pend_layer)])
        d_mod2 = jnp.concatenate([dss1[0:2]] + d_mod2_tail, axis=1)
        small[l] = (d_mod2, dgn1[0], dgn2[0], conv_acc, d_sink[0, 0:n_heads], d_goa[0])
    grad_x = grad_x[None]

    def pack(l):
        d_mod2, dgn1, dgn2, conv_acc, d_sink, d_goa = small[l]
        row0 = [d_mod2[0], dgn1, dgn2, conv_acc[3], d_sink, conv_acc[4], d_goa, conv_acc[0:3].reshape(-1)]
        row1 = [d_mod2[1]] + [jnp.zeros_like(v) for v in row0[1:]]
        return jnp.stack([jnp.concatenate(row0), jnp.concatenate(row1)])
    per_layer = N_MOD * d + 2 * d + dc + n_heads + 2 * dc + 3 * dc
    packed = jnp.concatenate([pack(l) for l in range(depth)] +
                             [jnp.stack([dg_final[0], jnp.zeros((d,), F32)])], axis=1)
    f_tot = depth * per_layer + d
    f_pad = -f_tot % 1024
    packed = jnp.pad(packed, ((0, 0), (0, f_pad)))
    (small_all,) = _all_gather([packed], "gather_small")
    small_parts = small_all.reshape(2 * N_DEV, 1, f_tot + f_pad)

    def section(arr, l, off, size):
        return lax.slice_in_dim(arr, l * per_layer + off, l * per_layer + off + size, axis=-1)

    offs = {}
    o_ = 0
    for nm_, sz in (("mod", N_MOD * d), ("gn1", d), ("gn2", d), ("cb", dc), ("sink", n_heads), ("goc", dc), ("goa", dc), ("cw", 3 * dc)):
        offs[nm_] = (o_, sz)
        o_ += sz

    def packw(b_ada_, gn1_, gn2_, cb_, sk_, goc_, goa_, gf_):
        rows = []
        for l in range(depth):
            rows += [b_ada_[l], gn1_[l], gn2_[l], cb_[l], sk_[l], goc_[l], goa_[l], jnp.zeros((3 * dc,), F32)]
        return jnp.pad(jnp.concatenate(rows + [gf_]), (0, f_pad))[None]
    pw = packw(b_ada, g_norm1, g_norm2, conv_b, sink, g_out_conv, g_out_attn, g_final)
    pm = packw(m_b_ada, m_g_norm1, m_g_norm2, m_conv_b, m_sink, m_g_out_conv, m_g_out_attn, m_g_final)
    pv = packw(v_b_ada, v_g_norm1, v_g_norm2, v_conv_b, v_sink, v_g_out_conv, v_g_out_attn, v_g_final)
    sg, sd, sm, sv = _adamw(pw, pm, pv, small_parts, "adamw_small")

    def unpack(arr):
        arr = arr[0]
        out = {}
        for nm_ in ("mod", "gn1", "gn2", "cb", "sink", "goc", "goa", "cw"):
            off, size = offs[nm_]
            out[nm_] = jnp.stack([section(arr, l, off, size) for l in range(depth)])
        out["gf"] = arr[depth * per_layer:depth * per_layer + d]
        return out
    ug, ud, um, uv = unpack(sg), unpack(sd), unpack(sm), unpack(sv)

    cw_grad_full = ug["cw"].reshape(depth, 3, N_DEV, cw)
    cw_grad = lax.dynamic_index_in_dim(cw_grad_full, dev, axis=2, keepdims=False).reshape(1, depth * 3, cw)
    cwg, cwd, cwm, cwv = _adamw(conv_w.reshape(depth * 3, cw), m_conv_w.reshape(depth * 3, cw),
                                v_conv_w.reshape(depth * 3, cw), cw_grad, "adamw_conv_w")
    cw_shape = conv_w.shape

    mod_rows = small_all[:, :, :depth * per_layer].reshape(N_DEV, 2, depth, per_layer)[:, :, :, :N_MOD * d]
    dm16 = jnp.concatenate([mod_rows[:, 0], mod_rows[:, 1]], axis=0)
    dm16 = jnp.transpose(dm16, (1, 0, 2)).reshape(depth, 16, N_DEV, c_ada)
    dm16_loc = lax.dynamic_index_in_dim(dm16, dev, axis=2, keepdims=False)
    gb_ada = lax.empty((depth, d, c_ada), F32)
    dsc_parts = []
    tm_g = _pick(d, FEAT)
    for l in range(depth):
        gb_ada = _matmul(sc16, dm16_loc[l], dims="tn", shape=(d, c_ada, 16), tiles=(tm_g, tn_ada, 16),
                         out_specs=(_g_spec("nat", l, None, tm_g, tn_ada),), out_shapes=(jax.ShapeDtypeStruct(gb_ada.shape, F32),),
                         alias=gb_ada, name=f"ada_dw{l}")
        tn = _pick(d, FEAT)
        tk = _pick(c_ada, KDIM)
        dsc_parts.append(_matmul(dm16_loc[l], w_ada, dims="nt", shape=(16, d, c_ada), tiles=(16, tn, tk),
                                 b_spec=_w_spec("nat", l, None, "nt", tn, tk), name=f"ada_dx{l}"))
    (dsc_all,) = _all_gather([jnp.stack(dsc_parts)[:, 8:16]], "gather_dcond")
    g_cctx = _cctx_grad(dsc_all.reshape(N_DEV * depth * 8, d), c_ctx[None], "c_ctx_grad")
    ccg, ccd, ccm, ccv = _adamw(c_ctx[None], m_c_ctx[None], v_c_ctx[None], g_cctx[None], "adamw_c_ctx")
    adg, add, adm, adv = _adamw(w_ada.reshape(depth * d, c_ada), m_w_ada.reshape(depth * d, c_ada),
                                v_w_ada.reshape(depth * d, c_ada), gb_ada.reshape(1, depth * d, c_ada), "adamw_w_ada")

    parts4 =[jnp.swapaxes(buf_in, 2, 3), buf_out, buf_w1, buf_w2]
    big = []
    for k, (w_, m_, v_) in enumerate(((w_in, m_w_in, v_w_in), (w_out, m_w_out, v_w_out),
                                      (w_mlp1, m_w_mlp1, v_w_mlp1), (w_mlp2, m_w_mlp2, v_w_mlp2))):
        r2, c2 = w_.shape[0] * w_.shape[1], w_.shape[2]
        res = _adamw(w_.reshape(r2, c2), m_.reshape(r2, c2), v_.reshape(r2, c2), parts4[k].reshape(N_CHIP, r2, c2),
                     f"adamw_big{k}")
        big.append([a_.reshape(w_.shape) for a_ in res])

    def leaf(i):
        return (
            (ccg, ccd, ccm, ccv)[i][0], (adg, add, adm, adv)[i].reshape(w_ada.shape),
            (ug, ud, um, uv)[i]["mod"], (ug, ud, um, uv)[i]["gn1"], (ug, ud, um, uv)[i]["gn2"], big[0][i],
            (cwg, cwd, cwm, cwv)[i].reshape(cw_shape), (ug, ud, um, uv)[i]["cb"], (ug, ud, um, uv)[i]["sink"],
            (ug, ud, um, uv)[i]["goc"], (ug, ud, um, uv)[i]["goa"], big[1][i], big[2][i], big[3][i], (ug, ud, um, uv)[i]["gf"])

    return (loss, grad_x) + leaf(0) + leaf(1) + leaf(2) + leaf(3)
```

```python
import functools
import math

import jax
import jax.numpy as jnp
from jax import lax
from jax.experimental import pallas as pl
from jax.experimental.pallas import tpu as pltpu

HEAD_DIM = 64
N_KV_HEADS = 4
BLOCK = 128
GRID_W = 64
ROPE_THETA = 10000.0
EPS = 1e-6
N_MOD = 6
SCALE = HEAD_DIM ** -0.5
NEG_INF = -1e30
ADAM_LR = 0.001
ADAM_B1 = 0.9
ADAM_B2 = 0.999
ADAM_EPS = 1e-08
ADAM_WD = 0.01
ADAM_STEP = 10
N_DEV = 8
N_CHIP = 4
VMEM_LIMIT_BYTES = 48 * 1024 * 1024
MESH = pl.DeviceIdType.MESH
BF16 = jnp.bfloat16
F32 = jnp.float32
ANY = pl.BlockSpec(memory_space=pl.ANY)


def _pick(dim, prefs):
    for p in prefs:
        if p <= dim and dim % p == 0:
            return p
    return dim


def _params(sem):
    return pltpu.CompilerParams(dimension_semantics=sem, vmem_limit_bytes=VMEM_LIMIT_BYTES)


def _row_ids(i, tr):
    return i * tr + lax.broadcasted_iota(jnp.int32, (tr, 1), 0)


def _colsum(v):
    return jnp.sum(v, axis=0, keepdims=True)


def _rowmean(v):
    return jnp.mean(v, axis=1, keepdims=True)


def _all_gather(xs, name):
    na = len(xs)

    def body(*refs):
        x_refs, o_refs = refs[:na], refs[na:2 * na]
        send_sems, recv_sems, local_sems = refs[2 * na:]
        x, y, c = lax.axis_index("x"), lax.axis_index("y"), lax.axis_index("c")
        me, sibling = (x, y, c), (x, y, 1 - c)
        chips = [(1 - x, y), (x, 1 - y), (1 - x, 1 - y)]

        def slot(a, px, py, pc):
            return o_refs[a].at[4 * px + 2 * py + pc]

        def copy(a, k, block, to, src=None):
            return pltpu.make_async_remote_copy(
                src_ref=slot(a, *block) if src is None else src, dst_ref=slot(a, *block),
                send_sem=send_sems.at[a, k], recv_sem=recv_sems.at[a, k], device_id=to, device_id_type=MESH)

        mine = [pltpu.make_async_copy(x_refs[a], slot(a, *me), local_sems.at[a]) for a in range(na)]
        for cp in mine:
            cp.start()
        first = []
        for a in range(na):
            first.append(copy(a, 0, me, sibling, src=x_refs[a]))
            first += [copy(a, 1 + j, me, (*chip, c), src=x_refs[a]) for j, chip in enumerate(chips)]
        for cp in first:
            cp.start()
        passed = []
        for j, chip in enumerate(chips):
            for a in range(na):
                copy(a, 1 + j, (*chip, c), me).wait_recv()
                fwd = copy(a, 4 + j, (*chip, c), sibling)
                fwd.start()
                passed.append(fwd)
        for a in range(na):
            copy(a, 0, sibling, me).wait_recv()
            for j, chip in enumerate(chips):
                copy(a, 4 + j, (*chip, 1 - c), me).wait_recv()
        for cp in first + passed:
            cp.wait_send()
        for cp in mine:
            cp.wait()

    outs = pl.pallas_call(
        body, name=name,
        out_shape=tuple(jax.ShapeDtypeStruct((N_DEV,) + x.shape, x.dtype) for x in xs),
        in_specs=[ANY] * na, out_specs=tuple([ANY] * na),
        scratch_shapes=[pltpu.SemaphoreType.DMA((na, 7)), pltpu.SemaphoreType.DMA((na, 7)),
                        pltpu.SemaphoreType.DMA((na,))],
    )(*xs)
    return list(outs)


class _Job:
    def __init__(self, ins, outs, alias, sems, copies):
        self.ins, self.outs, self.alias, self.sems, self._copies = ins, outs, alias, sems, copies

    def start(self, in_refs, out_refs, sems):
        local, sends, _ = self._copies(in_refs, out_refs, sems)
        for make in local + sends:
            make().start()

    def wait(self, in_refs, out_refs, sems):
        local, sends, arrivals = self._copies(in_refs, out_refs, sems)
        for make in arrivals:
            make().wait_recv()
        for make in sends:
            make().wait_send()
        for make in local:
            make().wait()


def _other_chips():
    x, y = lax.axis_index("x"), lax.axis_index("y")
    return [(1 - x, y), (x, 1 - y), (1 - x, 1 - y)]


def _remote(src, dst, send_sem, recv_sem, to):
    return functools.partial(pltpu.make_async_remote_copy, src_ref=src, dst_ref=dst, send_sem=send_sem, recv_sem=recv_sem,
                             device_id=to, device_id_type=MESH)


def _local(src, dst, sem):
    return functools.partial(pltpu.make_async_copy, src, dst, sem)


def _job_gather_ici(xs, layer):
    na = len(xs)

    def copies(in_refs, out_refs, sems):
        send, recv, loc = sems
        x, y, c = lax.axis_index("x"), lax.axis_index("y"), lax.axis_index("c")
        me = 4 * x + 2 * y + c
        local, sends, arrivals = [], [], []
        for a in range(na):
            src = in_refs[a].at[layer]
            local.append(_local(src, out_refs[a].at[me], loc.at[a]))
            for j, (px, py) in enumerate(_other_chips()):
                sends.append(_remote(src, out_refs[a].at[me], send.at[a, j], recv.at[a, j], (px, py, c)))
                arrivals.append(_remote(src, out_refs[a].at[4 * px + 2 * py + c], send.at[a, j], recv.at[a, j], (px, py, c)))
        return local, sends, arrivals

    outs = [jax.ShapeDtypeStruct((N_DEV,) + x.shape[1:], x.dtype) for x in xs]
    sems = [pltpu.SemaphoreType.DMA((na, 3)), pltpu.SemaphoreType.DMA((na, 3)), pltpu.SemaphoreType.DMA((na,))]
    return _Job(list(xs), outs, {}, sems, copies)


def _job_gather_d2d(gs):
    na = len(gs)

    def copies(in_refs, out_refs, sems):
        send, recv = sems
        x, y, c = lax.axis_index("x"), lax.axis_index("y"), lax.axis_index("c")
        sends, arrivals = [], []
        for a in range(na):
            for k in range(N_CHIP):
                mine, theirs = 2 * k + c, 2 * k + (1 - c)
                sends.append(_remote(in_refs[a].at[mine], out_refs[a].at[mine], send.at[a, k], recv.at[a, k], (x, y, 1 - c)))
                arrivals.append(_remote(in_refs[a].at[theirs], out_refs[a].at[theirs], send.at[a, k], recv.at[a, k], (x, y, 1 - c)))
        return [], sends, arrivals

    outs = [jax.ShapeDtypeStruct(g.shape, g.dtype) for g in gs]
    sems = [pltpu.SemaphoreType.DMA((na, N_CHIP)), pltpu.SemaphoreType.DMA((na, N_CHIP))]
    return _Job(list(gs), outs, {a: a for a in range(na)}, sems, copies)


def _job_scatter_d2d(gs):
    na = len(gs)

    def copies(in_refs, out_refs, sems):
        send, recv = sems
        x, y, c = lax.axis_index("x"), lax.axis_index("y"), lax.axis_index("c")
        sends, arrivals = [], []
        for a in range(na):
            for k in range(N_CHIP):
                cp = _remote(in_refs[a].at[2 * k + (1 - c)], out_refs[a].at[k], send.at[a, k], recv.at[a, k], (x, y, 1 - c))
                sends.append(cp)
                arrivals.append(cp)
        return [], sends, arrivals

    outs = [jax.ShapeDtypeStruct((N_CHIP,) + g.shape[1:], g.dtype) for g in gs]
    sems = [pltpu.SemaphoreType.DMA((na, N_CHIP)), pltpu.SemaphoreType.DMA((na, N_CHIP))]
    return _Job(list(gs), outs, {}, sems, copies)


def _job_scatter_ici(pairs, bufs, layer):
    na = len(pairs)

    def copies(in_refs, out_refs, sems):
        send, recv, loc = sems
        x, y, c = lax.axis_index("x"), lax.axis_index("y"), lax.axis_index("c")
        my_chip = 2 * x + y
        local, sends, arrivals = [], [], []
        for a in range(na):
            local.append(_local(in_refs[a].at[my_chip], out_refs[a].at[my_chip, layer], loc.at[a]))
            for j, (px, py) in enumerate(_other_chips()):
                src = in_refs[a].at[2 * px + py]
                sends.append(_remote(src, out_refs[a].at[my_chip, layer], send.at[a, j], recv.at[a, j], (px, py, c)))
                arrivals.append(_remote(src, out_refs[a].at[2 * px + py, layer], send.at[a, j], recv.at[a, j], (px, py, c)))
        return local, sends, arrivals

    outs = [jax.ShapeDtypeStruct(b.shape, b.dtype) for b in bufs]
    sems = [pltpu.SemaphoreType.DMA((na, 3)), pltpu.SemaphoreType.DMA((na, 3)), pltpu.SemaphoreType.DMA((na,))]
    return _Job(list(pairs) + list(bufs), outs, {na + a: a for a in range(na)}, sems, copies)


def _split_jobs(jobs, in_refs, out_refs, sem_refs):
    out, i0, o0, s0 = [], 0, 0, 0
    for jb in jobs:
        out.append((jb, in_refs[i0:i0 + len(jb.ins)], out_refs[o0:o0 + len(jb.outs)], sem_refs[s0:s0 + len(jb.sems)]))
        i0, o0, s0 = i0 + len(jb.ins), o0 + len(jb.outs), s0 + len(jb.sems)
    return out


def _carry(body, jobs, n_in, n_out, n_steps):
    if not jobs:
        return body
    n_ji = sum(len(jb.ins) for jb in jobs)
    n_jo = sum(len(jb.outs) for jb in jobs)

    def wrapped(*refs):
        o0 = n_in + n_ji
        parts = _split_jobs(jobs, refs[n_in:o0], refs[o0 + n_out:o0 + n_out + n_jo], refs[o0 + n_out + n_jo:])

        @pl.when(pl.program_id(0) == 0)
        def _():
            for jb, i_r, o_r, s_r in parts:
                jb.start(i_r, o_r, s_r)

        body(*refs[:n_in], *refs[o0:o0 + n_out])

        @pl.when(pl.program_id(0) == n_steps - 1)
        def _():
            for jb, i_r, o_r, s_r in parts:
                jb.wait(i_r, o_r, s_r)

    return wrapped


def _carry_args(jobs, n_in, n_out):
    io_alias, i0, o0 = {}, n_in, n_out
    for jb in jobs:
        io_alias.update({i0 + a: o0 + b for a, b in jb.alias.items()})
        i0, o0 = i0 + len(jb.ins), o0 + len(jb.outs)
    ins = [v for jb in jobs for v in jb.ins]
    outs = [s for jb in jobs for s in jb.outs]
    return ins, [ANY] * len(ins), outs, [ANY] * len(outs), [s for jb in jobs for s in jb.sems], io_alias


def _carry_results(jobs, outs, n_out):
    res, o0 = [], n_out
    for jb in jobs:
        res.append(list(outs[o0:o0 + len(jb.outs)]))
        o0 += len(jb.outs)
    return res


def _comm_call(jobs, name):
    n_in = sum(len(jb.ins) for jb in jobs)
    n_out = sum(len(jb.outs) for jb in jobs)

    def body(*refs):
        parts = _split_jobs(jobs, refs[:n_in], refs[n_in:n_in + n_out], refs[n_in + n_out:])
        for jb, i_r, o_r, s_r in parts:
            jb.start(i_r, o_r, s_r)
        for jb, i_r, o_r, s_r in parts:
            jb.wait(i_r, o_r, s_r)

    io_alias, i0, o0 = {}, 0, 0
    for jb in jobs:
        io_alias.update({i0 + a: o0 + b for a, b in jb.alias.items()})
        i0, o0 = i0 + len(jb.ins), o0 + len(jb.outs)
    outs = pl.pallas_call(
        body, name=name, out_shape=tuple(s for jb in jobs for s in jb.outs),
        in_specs=[ANY] * n_in, out_specs=tuple([ANY] * n_out),
        scratch_shapes=[s for jb in jobs for s in jb.sems], input_output_aliases=io_alias,
    )(*[v for jb in jobs for v in jb.ins])
    res, o0 = [], 0
    for jb in jobs:
        res.append(list(outs[o0:o0 + len(jb.outs)]))
        o0 += len(jb.outs)
    return res


def _to_bf16(x2d, name):
    r, c = x2d.shape
    tr = _pick(r, (512, 256, 128, 64, 32, 16))

    def body(x_ref, o_ref):
        o_ref[...] = x_ref[...].astype(BF16)

    return pl.pallas_call(
        body, name=name, out_shape=jax.ShapeDtypeStruct((r, c), BF16), grid=(r // tr,),
        in_specs=[pl.BlockSpec((tr, c), lambda i: (i, 0))], out_specs=pl.BlockSpec((tr, c), lambda i: (i, 0)),
        compiler_params=_params(("parallel",)),
    )(x2d)


def _pair_add(own8, got4, name):
    _, r, c = own8.shape
    tr = _pick(r, (512, 256, 128, 64, 32, 16))
    core = lax.axis_index("c").astype(jnp.int32).reshape(1)

    def body(c_ref, a_ref, b_ref, o_ref):
        o_ref[...] = (a_ref[...] + b_ref[...]).astype(BF16)

    return pl.pallas_call(
        body, name=name, out_shape=jax.ShapeDtypeStruct((N_CHIP, r, c), BF16),
        grid_spec=pltpu.PrefetchScalarGridSpec(
            num_scalar_prefetch=1, grid=(N_CHIP, r // tr),
            in_specs=[pl.BlockSpec((None, tr, c), lambda k, i, cr: (2 * k + cr[0], i, 0)),
                      pl.BlockSpec((None, tr, c), lambda k, i, cr: (k, i, 0))],
            out_specs=pl.BlockSpec((None, tr, c), lambda k, i, cr: (k, i, 0))),
        compiler_params=_params(("parallel", "parallel")),
    )(core, own8, got4)


def _adamw(w, m, v, parts, name):
    r, c = w.shape
    n_parts = parts.shape[0]
    tr = _pick(r, (256, 128, 64, 32, 16, 8))
    tc = _pick(c, (1024, 512, 256, 128))

    def body(w_ref, m_ref, v_ref, p_ref, g_ref, d_ref, nm_ref, nv_ref):
        g = p_ref[0].astype(F32)
        for k in range(1, n_parts):
            g = g + p_ref[k].astype(F32)
        wv = w_ref[...]
        nm = ADAM_B1 * m_ref[...] + (1.0 - ADAM_B1) * g
        nv = ADAM_B2 * v_ref[...] + (1.0 - ADAM_B2) * (g * g)
        m_hat = nm / (1.0 - ADAM_B1 ** ADAM_STEP)
        v_hat = nv / (1.0 - ADAM_B2 ** ADAM_STEP)
        g_ref[...] = g
        d_ref[...] = -ADAM_LR * (m_hat / (jnp.sqrt(v_hat) + ADAM_EPS) + ADAM_WD * wv)
        nm_ref[...] = nm
        nv_ref[...] = nv

    tile = pl.BlockSpec((tr, tc), lambda i, j: (i, j))
    sh = jax.ShapeDtypeStruct((r, c), F32)
    return pl.pallas_call(
        body, name=name, out_shape=(sh, sh, sh, sh), grid=(r // tr, c // tc),
        in_specs=[tile, tile, tile, pl.BlockSpec((n_parts, tr, tc), lambda i, j: (0, i, j))],
        out_specs=(tile, tile, tile, tile),
        compiler_params=_params(("parallel", "parallel")),
    )(w, m, v, parts)


def _matmul(a, b, *, dims, shape, tiles, b_spec=None, out_specs=None, out_shapes=None, out_dtypes=(F32,),
            epilogue=None, extras=(), alias=None, jobs=(), name):
    m_dim, n_dim, k_dim = shape
    tm, tn, tk = tiles
    assert m_dim % tm == 0 and n_dim % tn == 0 and k_dim % tk == 0, (name, shape, tiles)
    nk = k_dim // tk
    n_extra = len(extras)
    n_alias = 0 if alias is None else 1
    n_out = len(out_dtypes)
    if dims == "tn":
        a_spec = pl.BlockSpec((tk, tm), lambda i, j, k: (k, i))
        contract = (((0,), (0,)), ((), ()))
    else:
        a_spec = pl.BlockSpec((tm, tk), lambda i, j, k: (i, k))
        contract = (((1,), (1,)), ((), ())) if dims == "nt" else (((1,), (0,)), ((), ()))
    if b_spec is None:
        b_spec = (pl.BlockSpec((tn, tk), lambda i, j, k: (j, k)) if dims == "nt"
                  else pl.BlockSpec((tk, tn), lambda i, j, k: (k, j)))
    if out_specs is None:
        out_specs = tuple(pl.BlockSpec((tm, tn), lambda i, j, k: (i, j)) for _ in range(n_out))
    if out_shapes is None:
        out_shapes = tuple(jax.ShapeDtypeStruct((m_dim, n_dim), d) for d in out_dtypes)

    n_local_in = 2 + n_extra + n_alias
    n_job_in = sum(len(jb.ins) for jb in jobs)
    n_job_out = sum(len(jb.outs) for jb in jobs)
    n_acc = 1 if nk > 1 else 0
    grid = (m_dim // tm, n_dim // tn, nk)

    def body(*refs):
        a_ref, b_ref = refs[0], refs[1]
        extra_refs = refs[2:2 + n_extra]
        o0 = n_local_in + n_job_in
        out_refs = refs[o0:o0 + n_out]
        s0 = o0 + n_out + n_job_out
        job_parts = _split_jobs(jobs, refs[n_local_in:o0], refs[o0 + n_out:s0], refs[s0 + n_acc:])
        i, j, k = pl.program_id(0), pl.program_id(1), pl.program_id(2)

        if jobs:
            @pl.when((i == 0) & (j == 0) & (k == 0))
            def _():
                for jb, i_r, o_r, s_r in job_parts:
                    jb.start(i_r, o_r, s_r)

        def finish(acc):
            if epilogue is None:
                out_refs[0][...] = acc.astype(out_refs[0].dtype)
            else:
                epilogue(acc, i, j, extra_refs, out_refs)

        def product():
            return lax.dot_general(a_ref[...].astype(BF16), b_ref[...].astype(BF16), contract, preferred_element_type=F32)

        if nk == 1:
            finish(product())
        else:
            acc_ref = refs[s0]

            @pl.when(k == 0)
            def _():
                acc_ref[...] = jnp.zeros_like(acc_ref)

            acc_ref[...] += product()

            @pl.when(k == nk - 1)
            def _():
                finish(acc_ref[...])

        if jobs:
            @pl.when((i == grid[0] - 1) & (j == grid[1] - 1) & (k == nk - 1))
            def _():
                for jb, i_r, o_r, s_r in job_parts:
                    jb.wait(i_r, o_r, s_r)

    ins = [a, b] + [e[0] for e in extras]
    in_specs = [a_spec, b_spec] + [e[1] for e in extras]
    io_alias = {}
    if alias is not None:
        ins.append(alias)
        in_specs.append(ANY)
        io_alias = {len(ins) - 1: 0}
    all_out_shapes, all_out_specs = list(out_shapes), list(out_specs)
    for jb in jobs:
        io_alias.update({len(ins) + a_: len(all_out_shapes) + b_ for a_, b_ in jb.alias.items()})
        ins += jb.ins
        in_specs += [ANY] * len(jb.ins)
        all_out_shapes += jb.outs
        all_out_specs += [ANY] * len(jb.outs)
    scratch = ([pltpu.VMEM((tm, tn), F32)] if nk > 1 else []) + [s for jb in jobs for s in jb.sems]
    outs = pl.pallas_call(
        body, name=name, out_shape=tuple(all_out_shapes), grid=grid,
        in_specs=in_specs, out_specs=tuple(all_out_specs), scratch_shapes=scratch,
        input_output_aliases=io_alias,
        compiler_params=_params(("arbitrary",) * 3 if jobs else ("parallel", "parallel", "arbitrary")),
    )(*ins)
    main = outs[0] if n_out == 1 else tuple(outs[:n_out])
    if not jobs:
        return main
    job_outs, o0 = [], n_out
    for jb in jobs:
        job_outs.append(list(outs[o0:o0 + len(jb.outs)]))
        o0 += len(jb.outs)
    return main, job_outs


def _lead(layer, block, index):
    if layer is None:
        return pl.BlockSpec(block, index)
    return pl.BlockSpec((None,) + block, lambda i, j, k: (layer,) + index(i, j, k))


def _w_spec(kind, layer, per, dims, tn, tk):
    if kind == "nat":
        if dims == "nn":
            return _lead(layer, (tk, tn), lambda i, j, k: (k, j))
        return _lead(layer, (tn, tk), lambda i, j, k: (j, k))
    if dims == "nn":
        q = per // tn
        return _lead(layer, (None, tk, tn), lambda i, j, k: (j // q, k, j % q))
    q = per // tk
    return _lead(layer, (None, tn, tk), lambda i, j, k: (k // q, j, k % q))


def _g_spec(kind, layer, per, tm, tn):
    if kind == "nat":
        return _lead(layer, (tm, tn), lambda i, j, k: (i, j))
    q = per // tn
    return _lead(layer, (None, tm, tn), lambda i, j, k: (j // q, i, j % q))


TOK = (1408, 768, 256, 128)
FEAT = (1024, 512, 256, 128)
KDIM = (2048, 1536, 1024, 512, 256, 128)


def _sel(is_ctx, ref):
    return jnp.where(is_ctx, ref[1:2, :], ref[0:1, :])


def _row_tile(n, n_ctx):
    tr = _pick(n_ctx, (256, 128))
    assert n % tr == 0 and n_ctx % tr == 0
    return tr


def _add_by_segment(acc_ref, cols, ctx_tile, v):
    zero = jnp.zeros_like(v)
    acc_ref[0:1, cols] += jnp.where(ctx_tile, zero, v)
    acc_ref[1:2, cols] += jnp.where(ctx_tile, v, zero)


def _norm_mod(t, gain, mod8, shift_k, n_ctx, name, jobs=()):
    n, d = t.shape
    tr = _row_tile(n, n_ctx)

    def body(t_ref, g_ref, sh_ref, sc_ref, o_ref):
        x = t_ref[...]
        r = lax.rsqrt(_rowmean(x * x) + EPS)
        y = (x * r) * g_ref[...]
        ctx_tile = pl.program_id(0) * tr < n_ctx
        o_ref[...] = (y * (1.0 + _sel(ctx_tile, sc_ref)) + _sel(ctx_tile, sh_ref)).astype(BF16)

    j_ins, j_in_specs, j_outs, j_out_specs, j_sems, io_alias = _carry_args(jobs, 4, 1)
    outs = pl.pallas_call(
        _carry(body, jobs, 4, 1, n // tr), name=name, grid=(n // tr,),
        out_shape=(jax.ShapeDtypeStruct((n, d), BF16),) + tuple(j_outs),
        in_specs=[pl.BlockSpec((tr, d), lambda i: (i, 0)), pl.BlockSpec((1, d), lambda i: (0, 0)),
                  pl.BlockSpec((8, d), lambda i: (0, shift_k)), pl.BlockSpec((8, d), lambda i: (0, shift_k + 1))] + j_in_specs,
        out_specs=(pl.BlockSpec((tr, d), lambda i: (i, 0)),) + tuple(j_out_specs),
        scratch_shapes=j_sems, input_output_aliases=io_alias,
        compiler_params=_params(("arbitrary",) if jobs else ("parallel",)),
    )(t, gain, mod8, mod8, *j_ins)
    return (outs[0], _carry_results(jobs, outs, 1)) if jobs else outs[0]


def _norm_mod_bwd(dh, t, d_res, gain, mod8, shift_k, n_ctx, name, gated=None, latent_only=False, jobs=()):
    n, d = t.shape
    tr = _row_tile(n, n_ctx)

    def body(*refs):
        if gated is None:
            dh_ref, t_ref, dr_ref, g_ref, sc_ref, dt_ref, dss_ref, dg_ref = refs
        else:
            dh_ref, t_ref, dr_ref, g_ref, sc_ref, br_ref, gt_ref, dt_ref, dss_ref, dg_ref, dob_ref, dgate_ref = refs
        i = pl.program_id(0)

        @pl.when(i == 0)
        def _():
            dss_ref[...] = jnp.zeros_like(dss_ref)
            dg_ref[...] = jnp.zeros_like(dg_ref)
            if gated is not None:
                dgate_ref[...] = jnp.zeros_like(dgate_ref)

        x = t_ref[...]
        r = lax.rsqrt(_rowmean(x * x) + EPS)
        xn = x * r
        g = g_ref[...]
        y = xn * g
        dhv = dh_ref[...]
        ctx_tile = i * tr < n_ctx
        _add_by_segment(dss_ref, slice(0, d), ctx_tile, _colsum(dhv))
        _add_by_segment(dss_ref, slice(d, 2 * d), ctx_tile, _colsum(dhv * y))
        dy = dhv * (1.0 + _sel(ctx_tile, sc_ref))
        dg_ref[0:1, :] += _colsum(dy * xn)
        dxn = dy * g
        d_t = dr_ref[...] + r * (dxn - xn * _rowmean(dxn * xn))
        dt_ref[...] = d_t
        if gated is not None:
            dob_ref[...] = (d_t * _sel(ctx_tile, gt_ref)).astype(BF16)
            _add_by_segment(dgate_ref, slice(None), ctx_tile, _colsum(d_t * br_ref[...].astype(F32)))

    row = pl.BlockSpec((tr, d), lambda i: (i, 0))
    acc = pl.BlockSpec((8, d), lambda i: (0, 0))
    ins = [dh, t, d_res, gain, mod8]
    in_specs = [row, row, row, pl.BlockSpec((1, d), lambda i: (0, 0)), pl.BlockSpec((8, d), lambda i: (0, shift_k + 1))]
    if latent_only:
        first = n_ctx // tr
        out_shape = [jax.ShapeDtypeStruct((n - n_ctx, d), F32)]
        out_specs = [pl.BlockSpec((tr, d), lambda i: (jnp.maximum(i - first, 0), 0))]
    else:
        out_shape, out_specs = [jax.ShapeDtypeStruct((n, d), F32)], [row]
    out_shape += [jax.ShapeDtypeStruct((8, 2 * d), F32), jax.ShapeDtypeStruct((8, d), F32)]
    out_specs += [pl.BlockSpec((8, 2 * d), lambda i: (0, 0)), acc]
    if gated is not None:
        branch, gate_mod8, gate_k = gated
        ins += [branch, gate_mod8]
        in_specs += [row, pl.BlockSpec((8, d), lambda i: (0, gate_k))]
        out_shape += [jax.ShapeDtypeStruct((n, d), BF16), jax.ShapeDtypeStruct((8, d), F32)]
        out_specs += [row, acc]
    n_in, n_out = len(ins), len(out_shape)
    j_ins, j_in_specs, j_outs, j_out_specs, j_sems, io_alias = _carry_args(jobs, n_in, n_out)
    outs = pl.pallas_call(
        _carry(body, jobs, n_in, n_out, n // tr), name=name, out_shape=tuple(out_shape + j_outs), grid=(n // tr,),
        in_specs=in_specs + j_in_specs, out_specs=tuple(out_specs + j_out_specs), scratch_shapes=j_sems,
        input_output_aliases=io_alias, compiler_params=_params(("arbitrary",)),
    )(*ins, *j_ins)
    return (tuple(outs[:n_out]), _carry_results(jobs, outs, n_out)) if jobs else outs


def _swap16(v):
    w = v.shape[1]
    lane = lax.broadcasted_iota(jnp.int32, v.shape, 1)
    return jnp.where((lane % 32) < 16, pltpu.roll(v, w - 16, 1), pltpu.roll(v, 16, 1))


def _rope(v, cs, sn, sign):
    reps = v.shape[1] // 128
    c = jnp.tile(cs, (1, reps)) if reps > 1 else cs
    s = jnp.tile(sn, (1, reps)) if reps > 1 else sn
    return v * c + sign * (_swap16(v) * s)


def _attn_specs(nb, n_ctx, hd, kd):
    kci = hd // kd
    specs = [pl.BlockSpec((BLOCK, hd), lambda b: (b, 0)),
             pl.BlockSpec((n_ctx, kd), lambda b: (0, kci)), pl.BlockSpec((n_ctx, kd), lambda b: (0, kci + 1))]
    for col in (kci, kci + 1):
        specs.append(pl.BlockSpec((BLOCK, kd), lambda b, col=col: (jnp.maximum(b - 1, 0), col)))
        specs.append(pl.BlockSpec((BLOCK, kd), lambda b, col=col: (b, col)))
        specs.append(pl.BlockSpec((BLOCK, kd), lambda b, col=col: (jnp.minimum(b + 1, nb - 1), col)))
    return specs


def _band_valid(b, group, n_ctx, n):
    q_pos = b * BLOCK + lax.broadcasted_iota(jnp.int32, (group * BLOCK, 1), 0) % BLOCK
    k_pos = (b - 1) * BLOCK + lax.broadcasted_iota(jnp.int32, (1, 3 * BLOCK), 1)
    return (jnp.abs(k_pos - q_pos) <= BLOCK) & (k_pos >= n_ctx) & (k_pos < n) & (q_pos >= n_ctx)


NT = (((1,), (1,)), ((), ()))
NN = (((1,), (0,)), ((), ()))
TN = (((0,), (0,)), ((), ()))


def _dot(a, b, dn):
    return lax.dot_general(a, b, dn, preferred_element_type=F32)


def _keys_of_block(b, kv_refs, group, n_ctx, n):
    kc_ref, vc_ref, k0, k1, k2, v0, v1, v2 = kv_refs
    bias = jnp.where(_band_valid(b, group, n_ctx, n), 0.0, NEG_INF)
    kcat = jnp.concatenate([kc_ref[...], k0[...], k1[...], k2[...]], axis=0)
    vcat = jnp.concatenate([vc_ref[...], v0[...], v1[...], v2[...]], axis=0)
    return bias, kcat, vcat


def _stack_heads(v, h, group):
    return jnp.concatenate([v[:, (h * group + j) * HEAD_DIM:(h * group + j + 1) * HEAD_DIM] for j in range(group)], axis=0)


def _scores(qg, keys, bias, n_ctx):
    s = _dot(qg, keys, NT)
    return jnp.concatenate([s[:, :n_ctx], s[:, n_ctx:] + bias], axis=1)


def _softmax_terms(s, sink_ref, h, group):
    snk = jnp.concatenate([jnp.full((BLOCK, 1), sink_ref[h * group + j], F32) for j in range(group)], axis=0)
    m = jnp.maximum(jnp.max(s, axis=1, keepdims=True), snk)
    e, e_s = jnp.exp(s - m), jnp.exp(snk - m)
    return e, e_s, 1.0 / (jnp.sum(e, axis=1, keepdims=True) + e_s)


def _softmax_sink(s, sink_ref, h, group):
    e, e_s, inv = _softmax_terms(s, sink_ref, h, group)
    return e * inv, e_s * inv


def _attention(qkv, sink, n_ctx, hd, kd, name, jobs=()):
    n = qkv.shape[0]
    nb = n // BLOCK
    n_kv = kd // HEAD_DIM
    group = hd // kd
    n_job_in = sum(len(jb.ins) for jb in jobs)
    n_job_out = sum(len(jb.outs) for jb in jobs)

    def body(*refs):
        q_ref, kc_ref, vc_ref, k0, k1, k2, v0, v1, v2, sink_ref = refs[:10]
        o_ref = refs[10 + n_job_in]
        job_parts = _split_jobs(jobs, refs[10:10 + n_job_in], refs[11 + n_job_in:11 + n_job_in + n_job_out],
                                refs[11 + n_job_in + n_job_out:])
        b = pl.program_id(0)

        if jobs:
            @pl.when(b == 0)
            def _():
                for jb, i_r, o_r, s_r in job_parts:
                    jb.start(i_r, o_r, s_r)

        bias, kcat, vcat = _keys_of_block(b, (kc_ref, vc_ref, k0, k1, k2, v0, v1, v2), group, n_ctx, n)
        q = q_ref[...]

        def scores(h):
            return _scores(_stack_heads(q, h, group), kcat[:, h * HEAD_DIM:(h + 1) * HEAD_DIM], bias, n_ctx)

        heads, s_next = [], scores(0)
        for h in range(n_kv):
            s = s_next
            if h + 1 < n_kv:
                s_next = scores(h + 1)
            e, _, inv = _softmax_terms(s, sink_ref, h, group)
            o = _dot(e.astype(BF16), vcat[:, h * HEAD_DIM:(h + 1) * HEAD_DIM], NN) * inv
            heads += [o[j * BLOCK:(j + 1) * BLOCK, :] for j in range(group)]
        o_ref[...] = jnp.concatenate(heads, axis=1)

        if jobs:
            @pl.when(b == nb - 1)
            def _():
                for jb, i_r, o_r, s_r in job_parts:
                    jb.wait(i_r, o_r, s_r)

    io_alias, i0, o0 = {}, 10, 1
    for jb in jobs:
        io_alias.update({i0 + a_: o0 + b_ for a_, b_ in jb.alias.items()})
        i0, o0 = i0 + len(jb.ins), o0 + len(jb.outs)
    outs = pl.pallas_call(
        body, name=name, grid=(nb,),
        out_shape=(jax.ShapeDtypeStruct((n, hd), F32),) + tuple(s for jb in jobs for s in jb.outs),
        in_specs=_attn_specs(nb, n_ctx, hd, kd) + [pl.BlockSpec(memory_space=pltpu.SMEM)] + [ANY] * n_job_in,
        out_specs=(pl.BlockSpec((BLOCK, hd), lambda b: (b, 0)),) + tuple([ANY] * n_job_out),
        scratch_shapes=[s for jb in jobs for s in jb.sems], input_output_aliases=io_alias,
        compiler_params=_params(("arbitrary",) if jobs else ("parallel",)),
    )(qkv, qkv, qkv, qkv, qkv, qkv, qkv, qkv, qkv, sink, *[v for jb in jobs for v in jb.ins])
    if not jobs:
        return outs[0]
    job_outs, o0 = [], 1
    for jb in jobs:
        job_outs.append(list(outs[o0:o0 + len(jb.outs)]))
        o0 += len(jb.outs)
    return outs[0], job_outs


def _attention_bwd(qkv, sink, ao, d_mg, cpar, n_ctx, hd, kd, name):
    n = qkv.shape[0]
    nb = n // BLOCK
    n_kv = kd // HEAD_DIM
    group = hd // kd
    n_heads = n_kv * group

    def body(q_ref, kc_ref, vc_ref, k0, k1, k2, v0, v1, v2, sink_ref, ao_ref, dmg_ref, cp_ref,
             dq_ref, part_ref, dctx_ref, dsink_ref, dgain_ref):
        b = pl.program_id(0)

        @pl.when(b == 0)
        def _():
            dctx_ref[...] = jnp.zeros_like(dctx_ref)
            dsink_ref[...] = jnp.zeros_like(dsink_ref)
            dgain_ref[...] = jnp.zeros_like(dgain_ref)

        ao_v = ao_ref[...]
        ra = lax.rsqrt(_rowmean(ao_v * ao_v) + EPS)
        an = ao_v * ra
        dmg = dmg_ref[...]
        dgain_ref[0:1, :] += _colsum(dmg * an)
        d_an = dmg * cp_ref[5:6, :]
        d_ao = (ra * (d_an - an * _rowmean(d_an * an))).astype(BF16)

        bias, kcat, vcat = _keys_of_block(b, (kc_ref, vc_ref, k0, k1, k2, v0, v1, v2), group, n_ctx, n)
        q = q_ref[...]
        lane = lax.broadcasted_iota(jnp.int32, (1, 128), 1)
        dsink_row = jnp.zeros((1, 128), F32)

        def first_half(h):
            hs = slice(h * HEAD_DIM, (h + 1) * HEAD_DIM)
            qg, dog = _stack_heads(q, h, group), _stack_heads(d_ao, h, group)
            return qg, dog, _scores(qg, kcat[:, hs], bias, n_ctx), _dot(dog, vcat[:, hs], NT)

        dq_heads, dk, dv = [], [], []
        nxt = first_half(0)
        for h in range(n_kv):
            qg, dog, s, d_p = nxt
            if h + 1 < n_kv:
                nxt = first_half(h + 1)
            p, p_s = _softmax_sink(s, sink_ref, h, group)
            delta = jnp.sum(p * d_p, axis=1, keepdims=True)
            ds = (p * (d_p - delta)).astype(BF16)
            psd = p_s * delta
            for j in range(group):
                val = -jnp.sum(psd[j * BLOCK:(j + 1) * BLOCK, :], axis=0, keepdims=True)
                dsink_row = dsink_row + jnp.where(lane == h * group + j, val, 0.0)
            dq = _dot(ds, kcat[:, h * HEAD_DIM:(h + 1) * HEAD_DIM], NN) * SCALE
            dq_heads += [dq[j * BLOCK:(j + 1) * BLOCK, :] for j in range(group)]
            dk.append(_dot(ds, qg, TN))
            dv.append(_dot(p.astype(BF16), dog, TN))
        dq_ref[...] = jnp.concatenate(dq_heads, axis=1)
        d_kv = jnp.concatenate(dk + dv, axis=1)
        dctx_ref[...] += d_kv[:n_ctx]
        for j in range(3):
            part_ref[j] = d_kv[n_ctx + j * BLOCK:n_ctx + (j + 1) * BLOCK, :]
        dsink_ref[0:1, :] += dsink_row

    assert n_heads <= 128
    out_shape = (jax.ShapeDtypeStruct((n, hd), F32), jax.ShapeDtypeStruct((nb, 3, BLOCK, 2 * kd), F32),
                 jax.ShapeDtypeStruct((n_ctx, 2 * kd), F32), jax.ShapeDtypeStruct((8, 128), F32),
                 jax.ShapeDtypeStruct((8, hd), F32))
    return pl.pallas_call(
        body, name=name, out_shape=out_shape, grid=(nb,),
        in_specs=_attn_specs(nb, n_ctx, hd, kd) + [
            pl.BlockSpec(memory_space=pltpu.SMEM), pl.BlockSpec((BLOCK, hd), lambda b: (b, 0)),
            pl.BlockSpec((BLOCK, hd), lambda b: (b, 1)), pl.BlockSpec((8, hd), lambda b: (0, 0))],
        out_specs=(pl.BlockSpec((BLOCK, hd), lambda b: (b, 0)),
                   pl.BlockSpec((None, 3, BLOCK, 2 * kd), lambda b: (b, 0, 0, 0)),
                   pl.BlockSpec((n_ctx, 2 * kd), lambda b: (0, 0)), pl.BlockSpec((8, 128), lambda b: (0, 0)),
                   pl.BlockSpec((8, hd), lambda b: (0, 0))),
        compiler_params=_params(("arbitrary",)),
    )(qkv, qkv, qkv, qkv, qkv, qkv, qkv, qkv, qkv, sink, ao, d_mg, cpar)


def _halo_specs(tr, n, width, col=0):
    q = tr // 8
    return [pl.BlockSpec((8, width), lambda i: (jnp.maximum(i * q - 1, 0), col)),
            pl.BlockSpec((8, width), lambda i: (jnp.minimum((i + 1) * q, n // 8 - 1), col))]


def _mix_fwd(p, ao, cpar, n_ctx, dc, name):
    n = p.shape[0]
    tr = _pick(n, (256, 128))

    def body(p_ref, pp_ref, pn_ref, ao_ref, cp_ref, o_ref):
        i = pl.program_id(0)
        bg = p_ref[:, 0:dc]
        u = p_ref[:, dc:2 * dc] * p_ref[:, 2 * dc:3 * dc]
        u_before = pp_ref[7:8, dc:2 * dc] * pp_ref[7:8, 2 * dc:3 * dc]
        u_after = pn_ref[0:1, dc:2 * dc] * pn_ref[0:1, 2 * dc:3 * dc]
        loc = lax.broadcasted_iota(jnp.int32, (tr, 1), 0)
        gid = i * tr + loc
        has_prev = (gid != 0) & (gid != n_ctx)
        has_next = (gid != n_ctx - 1) & (gid != n - 1)
        u_m1 = jnp.where(has_prev, jnp.where(loc == 0, u_before, pltpu.roll(u, 1, 0)), 0.0)
        u_p1 = jnp.where(has_next, jnp.where(loc == tr - 1, u_after, pltpu.roll(u, tr - 1, 0)), 0.0)
        cv = u_m1 * cp_ref[0:1, :] + u * cp_ref[1:2, :] + u_p1 * cp_ref[2:3, :] + cp_ref[3:4, :]
        co = bg * cv
        nc = (co * lax.rsqrt(_rowmean(co * co) + EPS)) * cp_ref[4:5, :]
        ao_v = ao_ref[...]
        na = (ao_v * lax.rsqrt(_rowmean(ao_v * ao_v) + EPS)) * cp_ref[5:6, :]
        o_ref[...] = jnp.concatenate([nc, na], axis=1).astype(BF16)

    return pl.pallas_call(
        body, name=name, out_shape=jax.ShapeDtypeStruct((n, 2 * dc), BF16), grid=(n // tr,),
        in_specs=[pl.BlockSpec((tr, 3 * dc), lambda i: (i, 0))] + _halo_specs(tr, n, 3 * dc)
        + [pl.BlockSpec((tr, dc), lambda i: (i, 0)), pl.BlockSpec((8, dc), lambda i: (0, 0))],
        out_specs=pl.BlockSpec((tr, 2 * dc), lambda i: (i, 0)),
        compiler_params=_params(("parallel",)),
    )(p, p, p, ao, cpar)


def _mix_bwd(d_mg, p, cpar, d_q, parts, d_ctx, cs, sn, n_ctx, dc, hd, kd, name):
    n = p.shape[0]
    d_in = 3 * dc + hd + 2 * kd
    tr = BLOCK
    nb = n // tr
    ext = tr + 16
    n_ctx_blocks = n_ctx // BLOCK

    def body(dm_ref, dmp_ref, dmn_ref, p_ref, pp_ref, pn_ref, cp_ref, dq_ref, pa_ref, pb_ref, pc_ref, dctx_ref,
             cs_ref, sn_ref, dp_ref, acc_ref):
        i = pl.program_id(0)

        @pl.when(i == 0)
        def _():
            acc_ref[...] = jnp.zeros_like(acc_ref)

        def cat(before, here, after):
            return jnp.concatenate([before, here, after], axis=0)

        bg = cat(pp_ref[:, 0:dc], p_ref[:, 0:dc], pn_ref[:, 0:dc])
        cg = cat(pp_ref[:, dc:2 * dc], p_ref[:, dc:2 * dc], pn_ref[:, dc:2 * dc])
        hh = cat(pp_ref[:, 2 * dc:3 * dc], p_ref[:, 2 * dc:3 * dc], pn_ref[:, 2 * dc:3 * dc])
        dme = cat(dmp_ref[...], dm_ref[...], dmn_ref[...])
        gid = i * tr - 8 + lax.broadcasted_iota(jnp.int32, (ext, 1), 0)
        inside = (gid >= 0) & (gid < n)
        has_prev = inside & (gid != 0) & (gid != n_ctx)
        has_next = inside & (gid != n_ctx - 1) & (gid != n - 1)
        w0, w1, w2, bias, gain = cp_ref[0:1, :], cp_ref[1:2, :], cp_ref[2:3, :], cp_ref[3:4, :], cp_ref[4:5, :]
        u = jnp.where(inside, cg * hh, 0.0)
        u_m1 = jnp.where(has_prev, pltpu.roll(u, 1, 0), 0.0)
        u_p1 = jnp.where(has_next, pltpu.roll(u, ext - 1, 0), 0.0)
        cv = u_m1 * w0 + u * w1 + u_p1 * w2 + bias
        co = bg * cv
        rc = lax.rsqrt(_rowmean(co * co) + EPS)
        cn = co * rc
        d_cn = dme * gain
        d_co = rc * (d_cn - cn * _rowmean(d_cn * cn))
        d_cv = jnp.where(inside, d_co * bg, 0.0)
        d_bg = d_co * cv
        d_cv_p1 = jnp.where(has_next, pltpu.roll(d_cv, ext - 1, 0), 0.0)
        d_cv_m1 = jnp.where(has_prev, pltpu.roll(d_cv, 1, 0), 0.0)
        d_u = d_cv_p1 * w0 + d_cv * w1 + d_cv_m1 * w2
        mid = slice(8, 8 + tr)
        acc_ref[0:1, :] += _colsum((d_cv * u_m1)[mid])
        acc_ref[1:2, :] += _colsum((d_cv * u)[mid])
        acc_ref[2:3, :] += _colsum((d_cv * u_p1)[mid])
        acc_ref[3:4, :] += _colsum(d_cv[mid])
        acc_ref[4:5, :] += _colsum((dme * cn)[mid])

        d_kv = (jnp.where(i >= 1, pa_ref[...], 0.0) + pb_ref[...] + jnp.where(i + 1 < nb, pc_ref[...], 0.0))
        ctx_rows = dctx_ref[pl.ds(pl.multiple_of(jnp.minimum(i, n_ctx_blocks - 1) * BLOCK, BLOCK), BLOCK), :]
        d_kv = d_kv + jnp.where(i < n_ctx_blocks, ctx_rows, 0.0)
        cs_v, sn_v = cs_ref[...], sn_ref[...]
        d_qu = _rope(dq_ref[...], cs_v, sn_v, -1.0)
        d_ku = _rope(d_kv[:, 0:kd], cs_v, sn_v, -1.0)
        dp_ref[...] = jnp.concatenate(
            [d_bg[mid], (d_u * hh)[mid], (d_u * cg)[mid], d_qu, d_ku, d_kv[:, kd:2 * kd]], axis=1).astype(BF16)

    part = lambda sel, which: pl.BlockSpec((None, None, BLOCK, 2 * kd), lambda i: (sel(i), which, 0, 0))
    return pl.pallas_call(
        body, name=name, out_shape=(jax.ShapeDtypeStruct((n, d_in), BF16), jax.ShapeDtypeStruct((8, dc), F32)),
        grid=(nb,),
        in_specs=[pl.BlockSpec((tr, dc), lambda i: (i, 0))] + _halo_specs(tr, n, dc)
        + [pl.BlockSpec((tr, 3 * dc), lambda i: (i, 0))] + _halo_specs(tr, n, 3 * dc)
        + [pl.BlockSpec((8, dc), lambda i: (0, 0)), pl.BlockSpec((tr, hd), lambda i: (i, 0)),
           part(lambda i: jnp.maximum(i - 1, 0), 2), part(lambda i: i, 1), part(lambda i: jnp.minimum(i + 1, nb - 1), 0),
           pl.BlockSpec((n_ctx, 2 * kd), lambda i: (0, 0)),
           pl.BlockSpec((tr, 128), lambda i: (i, 0)), pl.BlockSpec((tr, 128), lambda i: (i, 0))],
        out_specs=(pl.BlockSpec((tr, d_in), lambda i: (i, 0)), pl.BlockSpec((8, dc), lambda i: (0, 0))),
        compiler_params=_params(("arbitrary",)),
    )(d_mg, d_mg, d_mg, p, p, p, cpar, d_q, parts, parts, parts, d_ctx, cs, sn)


def _loss_bwd(t, gain, target, branch, mod8, gate_k, n_ctx, name):
    n, d = t.shape
    tr = _pick(n_ctx, (256, 128))
    first = n_ctx // tr

    def body(t_ref, g_ref, y_ref, br_ref, gt_ref, dt_ref, loss_ref, dg_ref, dob_ref, dgate_ref):
        i = pl.program_id(0)

        @pl.when(i == 0)
        def _():
            loss_ref[...] = jnp.zeros_like(loss_ref)
            dg_ref[...] = jnp.zeros_like(dg_ref)
            dgate_ref[...] = jnp.zeros_like(dgate_ref)

        @pl.when(i < first)
        def _():
            dt_ref[...] = jnp.zeros_like(dt_ref)
            dob_ref[...] = jnp.zeros_like(dob_ref)

        @pl.when(i >= first)
        def _():
            x = t_ref[...]
            g = g_ref[...]
            r = lax.rsqrt(_rowmean(x * x) + EPS)
            xn = x * r
            err = xn * g - y_ref[...]
            loss_ref[...] += 0.5 * _colsum(_rowmean(err * err))
            dy = err * (1.0 / d)
            dg_ref[0:1, :] += _colsum(dy * xn)
            dxn = dy * g
            d_t = r * (dxn - xn * _rowmean(dxn * xn))
            dt_ref[...] = d_t
            dob_ref[...] = (d_t * gt_ref[0:1, :]).astype(BF16)
            dgate_ref[0:1, :] += _colsum(d_t * br_ref[...].astype(F32))

    row = pl.BlockSpec((tr, d), lambda i: (i, 0))
    acc = pl.BlockSpec((8, d), lambda i: (0, 0))
    return pl.pallas_call(
        body, name=name,
        out_shape=(jax.ShapeDtypeStruct((n, d), F32), jax.ShapeDtypeStruct((8, 128), F32), jax.ShapeDtypeStruct((8, d), F32),
                   jax.ShapeDtypeStruct((n, d), BF16), jax.ShapeDtypeStruct((8, d), F32)),
        grid=(n // tr,),
        in_specs=[row, pl.BlockSpec((1, d), lambda i: (0, 0)), pl.BlockSpec((tr, d), lambda i: (jnp.maximum(i - first, 0), 0)),
                  row, pl.BlockSpec((8, d), lambda i: (0, gate_k))],
        out_specs=(row, pl.BlockSpec((8, 128), lambda i: (0, 0)), acc, row, acc),
        compiler_params=_params(("arbitrary",)),
    )(t, gain, target, branch, mod8)


def _silu16(c16, name):
    def body(c_ref, o_ref):
        v = c_ref[...]
        o_ref[...] = (v * jax.nn.sigmoid(v)).astype(BF16)

    return pl.pallas_call(body, name=name, out_shape=jax.ShapeDtypeStruct(c16.shape, BF16))(c16)


def _cctx_grad(parts, c_ctx, name):
    def body(p_ref, c_ref, o_ref):
        g = _colsum(p_ref[...])
        v = c_ref[...]
        s = jax.nn.sigmoid(v)
        o_ref[...] = g * (s * (1.0 + v * (1.0 - s)))

    return pl.pallas_call(body, name=name, out_shape=jax.ShapeDtypeStruct(c_ctx.shape, F32))(parts, c_ctx)


def _rope_tables(n_ctx, n_tok):
    half = HEAD_DIM // 4
    inv = ROPE_THETA ** (-jnp.arange(0, HEAD_DIM // 2, 2, dtype=F32) / (HEAD_DIM // 2))
    rows = n_tok // GRID_W
    row_pos = jnp.repeat(jnp.arange(rows, dtype=F32), GRID_W)
    col_pos = jnp.tile(jnp.arange(GRID_W, dtype=F32), rows)
    ang_r, ang_c = row_pos[:, None] * inv[None, :], col_pos[:, None] * inv[None, :]
    cos = jnp.concatenate([jnp.cos(ang_r), jnp.cos(ang_r), jnp.cos(ang_c), jnp.cos(ang_c)], axis=1)
    sin = jnp.concatenate([-jnp.sin(ang_r), jnp.sin(ang_r), -jnp.sin(ang_c), jnp.sin(ang_c)], axis=1)
    assert cos.shape[1] == 4 * half == HEAD_DIM
    cos = jnp.concatenate([jnp.ones((n_ctx, HEAD_DIM), F32), cos], axis=0)
    sin = jnp.concatenate([jnp.zeros((n_ctx, HEAD_DIM), F32), sin], axis=0)
    return jnp.tile(cos, (1, 2)), jnp.tile(sin, (1, 2))


def kernel(x, c, ctx, c_ctx, w_ada, b_ada, g_norm1, g_norm2, w_in, conv_w, conv_b, sink, g_out_conv, g_out_attn, w_out, w_mlp1, w_mlp2, g_final, loss_target, m_c_ctx, m_w_ada, m_b_ada, m_g_norm1, m_g_norm2, m_w_in, m_conv_w, m_conv_b, m_sink, m_g_out_conv, m_g_out_attn, m_w_out, m_w_mlp1, m_w_mlp2, m_g_final, v_c_ctx, v_w_ada, v_b_ada, v_g_norm1, v_g_norm2, v_w_in, v_conv_w, v_conv_b, v_sink, v_g_out_conv, v_g_out_attn, v_w_out, v_w_mlp1, v_w_mlp2, v_g_final):
    n_lat, d = x.shape[1], x.shape[2]
    n_ctx = ctx.shape[1]
    n = n_ctx + n_lat
    depth = w_in.shape[0]
    dc = d // 2
    hd, kd = dc, N_KV_HEADS * HEAD_DIM
    n_heads = hd // HEAD_DIM
    d_in = 3 * dc + hd + 2 * kd
    cin, c_ada, r_out, c_ff, r_ff = w_in.shape[2], w_ada.shape[2], w_out.shape[1], w_mlp1.shape[2], w_mlp2.shape[1]
    d_ff = N_DEV * c_ff
    cw = conv_w.shape[2]
    assert d_in == N_DEV * cin and n_ctx % BLOCK == 0 and n_lat % BLOCK == 0 and hd % kd == 0
    dev = 4 * lax.axis_index("x") + 2 * lax.axis_index("y") + lax.axis_index("c")

    c_all, conv_w_all = _all_gather([c, conv_w], "gather_cond")
    conv_w_full = jnp.transpose(conv_w_all, (1, 2, 0, 3)).reshape(depth, 3, dc)
    c16 = jnp.concatenate([c_all.reshape(N_DEV, d), jnp.broadcast_to(c_ctx[None, :], (8, d))], axis=0)
    sc16 = _silu16(c16, "silu_cond")

    b_ada_loc = lax.dynamic_index_in_dim(b_ada.reshape(depth, N_DEV, c_ada), dev, axis=1, keepdims=False)
    tn_ada = _pick(c_ada, FEAT)

    def add_bias(acc, i, j, extra, outs):
        outs[0][...] = acc + extra[0][...]

    mod_loc = []
    for l in range(depth):
        mod_loc.append(_matmul(
            sc16, w_ada, dims="nn", shape=(16, c_ada, d), tiles=(16, tn_ada, _pick(d, KDIM)),
            b_spec=_w_spec("nat", l, None, "nn", tn_ada, _pick(d, KDIM)), epilogue=add_bias,
            extras=[(b_ada_loc[l][None, :], pl.BlockSpec((1, tn_ada), lambda i, j, k: (0, j)))], name=f"ada_fwd{l}"))
    (mod_all,) = _all_gather([jnp.stack(mod_loc)], "gather_mod")
    mod_full = jnp.transpose(mod_all, (1, 2, 0, 3)).reshape(depth, 16, N_MOD * d)
    mod_mine = lax.dynamic_index_in_dim(mod_full, dev, axis=1, keepdims=True)
    mod8 = jnp.concatenate([mod_mine, mod_full[:, 8:9], jnp.zeros((depth, 6, N_MOD * d), F32)], axis=1)

    w_in_b = _to_bf16(w_in.reshape(depth * d, cin), "cast_w_in").reshape(depth, d, cin)
    w_out_b = _to_bf16(w_out.reshape(depth * r_out, d), "cast_w_out").reshape(depth, r_out, d)
    w1_b = _to_bf16(w_mlp1.reshape(depth * d, c_ff), "cast_w_mlp1").reshape(depth, d, c_ff)
    w2_b = _to_bf16(w_mlp2.reshape(depth * r_ff, d), "cast_w_mlp2").reshape(depth, r_ff, d)
    def full_in(g):
        return jnp.transpose(g, (1, 0, 2)).reshape(d, d_in)

    w_in_full, w_out_full, g_w1, w2_full = [], [], [], []
    pend_w2 = None

    t = jnp.concatenate([ctx[0], x[0]], axis=0)
    cs, sn = _rope_tables(n_ctx, n_lat)
    tm = _pick(n, TOK)
    tk_d = _pick(d, KDIM)
    tm_res = _pick(n, (1056, 768, 256, 128))
    tn_in = _pick(math.gcd(3 * dc, hd + 2 * kd), (512, 256, 128))
    conv_tiles = 3 * dc // tn_in
    bounds = [(0, hd, "q"), (hd, hd + kd, "k"), (hd + kd, hd + 2 * kd, "v")]
    qkv_tiles = []
    for jj in range((hd + 2 * kd) // tn_in):
        lo, hi = jj * tn_in, (jj + 1) * tn_in
        segs = [(max(lo, a) - lo, min(hi, b) - lo, kind) for a, b, kind in bounds if a < hi and b > lo]
        assert all((e - s) % 128 == 0 for s, e, _ in segs)
        qkv_tiles.append(segs)

    def in_proj_epilogue(acc, i, j, extra, outs):
        @pl.when(j < conv_tiles)
        def _():
            outs[0][...] = acc

        for jj, segs in enumerate(qkv_tiles):
            @pl.when(j == conv_tiles + jj)
            def _(segs=segs):
                cs_v, sn_v = extra[0][...], extra[1][...]
                pieces = []
                for s0, s1, kind in segs:
                    v = acc[:, s0:s1]
                    if kind == "q":
                        v = _rope(v, cs_v, sn_v, 1.0) * SCALE
                    elif kind == "k":
                        v = _rope(v, cs_v, sn_v, 1.0)
                    pieces.append(v)
                outs[1][...] = (pieces[0] if len(pieces) == 1 else jnp.concatenate(pieces, axis=1)).astype(BF16)

    def in_proj_specs(rows):
        return dict(
            out_specs=(pl.BlockSpec((rows, tn_in), lambda i, j, k: (i, jnp.minimum(j, conv_tiles - 1))),
                       pl.BlockSpec((rows, tn_in), lambda i, j, k: (i, jnp.maximum(j - conv_tiles, 0)))),
            out_shapes=(jax.ShapeDtypeStruct((n, 3 * dc), F32), jax.ShapeDtypeStruct((n, hd + 2 * kd), BF16)),
            out_dtypes=(F32, BF16), epilogue=in_proj_epilogue,
            extras=[(cs, pl.BlockSpec((rows, 128), lambda i, j, k: (i, 0))), (sn, pl.BlockSpec((rows, 128), lambda i, j, k: (i, 0)))])

    def resid_epilogue(tile_rows):
        def epi(acc, i, j, extra, outs):
            is_ctx = _row_ids(i, tile_rows) < n_ctx
            outs[0][...] = extra[0][...] + _sel(is_ctx, extra[1]) * acc
            outs[1][...] = acc.astype(BF16)
        return epi

    def sq_relu_epilogue(acc, i, j, extra, outs):
        outs[0][...] = acc.astype(BF16)
        rl = jnp.maximum(acc, 0.0)
        outs[1][...] = (rl * rl).astype(BF16)

    def d_sq_relu_epilogue(acc, i, j, extra, outs):
        outs[0][...] = (acc * (2.0 * jnp.maximum(extra[0][...].astype(F32), 0.0))).astype(BF16)

    saved = []
    for l in range(depth):
        cpar = jnp.concatenate([conv_w_full[l], conv_b[l][None], g_out_conv[l][None], g_out_attn[l][None],
                                jnp.zeros((2, dc), F32)], axis=0)
        more = l + 1 < depth
        if l == 0:
            h, (gath0,) = _norm_mod(t, g_norm1[l][None], mod8[l], 0, n_ctx, f"norm1_{l}",
                                    jobs=[_job_gather_ici([w_in_b, w_out_b], 0)])
            (gath0,) = _comm_call([_job_gather_d2d(gath0)], "gather_w0_d2d")
            w_in_full.append(full_in(gath0[0]))
            w_out_full.append(gath0[1].reshape(d, d))
        else:
            h = _norm_mod(t, g_norm1[l][None], mod8[l], 0, n_ctx, f"norm1_{l}")
        jobs = ([_job_gather_d2d(pend_w2)] if pend_w2 is not None else []) + (
            [_job_gather_ici([w_in_b, w_out_b], l + 1)] if more else [])
        res = _matmul(h, w_in_full[l], dims="nn", shape=(n, d_in, d), tiles=(tm, tn_in, tk_d), jobs=jobs, name=f"in_proj{l}",
                      **in_proj_specs(tm))
        if jobs:
            (p, qkv), job_outs = res
            if pend_w2 is not None:
                w2_full.append(job_outs[0][0].reshape(d_ff, d))
            pend_io = job_outs[-1] if more else None
        else:
            p, qkv = res
        if l == 0:
            ao, (pend_mlp0,) = _attention(qkv, sink[l], n_ctx, hd, kd, f"attn{l}", jobs=[_job_gather_ici([w1_b, w2_b], 0)])
        else:
            ao = _attention(qkv, sink[l], n_ctx, hd, kd, f"attn{l}")
        mg = _mix_fwd(p, ao, cpar, n_ctx, dc, f"mix{l}")
        tn = _pick(d, (512, 256, 128))
        jobs = ([_job_gather_d2d(pend_io)] if more else []) + ([_job_gather_d2d(pend_mlp0)] if l == 0 else [])
        res = _matmul(
            mg, w_out_full[l], dims="nn", shape=(n, d, d), tiles=(tm, tn, tk_d),
            out_dtypes=(F32, BF16), epilogue=resid_epilogue(tm),
            extras=[(t, pl.BlockSpec((tm, tn), lambda i, j, k: (i, j))),
                    (mod8[l], pl.BlockSpec((8, tn), lambda i, j, k, tn=tn: (0, 2 * (d // tn) + j)))],
            jobs=jobs, name=f"out_proj{l}")
        if jobs:
            (t2, z), job_outs = res
            if more:
                w_in_full.append(full_in(job_outs[0][0]))
                w_out_full.append(job_outs[0][1].reshape(d, d))
            if l == 0:
                g_w1.append(job_outs[-1][0])
                w2_full.append(job_outs[-1][1].reshape(d_ff, d))
        else:
            t2, z = res
        h2 = _norm_mod(t2, g_norm2[l][None], mod8[l], 3, n_ctx, f"norm2_{l}")
        tn = _pick(c_ff, FEAT)
        res = _matmul(h2, g_w1[l], dims="nn", shape=(n, d_ff, d), tiles=(tm, tn, tk_d),
                      b_spec=_w_spec("cols", None, c_ff, "nn", tn, tk_d), out_dtypes=(BF16, BF16),
                      epilogue=sq_relu_epilogue, jobs=[_job_gather_ici([w1_b], l + 1)] if more else [], name=f"mlp_up{l}")
        if more:
            (a, s), ((pend_w1,),) = res
        else:
            a, s = res
        tn = _pick(d, FEAT)
        tk = _pick(d_ff, (1024, 512, 256, 128))
        res = _matmul(
            s, w2_full[l], dims="nn", shape=(n, d, d_ff), tiles=(tm_res, tn, tk),
            out_dtypes=(F32, BF16), epilogue=resid_epilogue(tm_res),
            extras=[(t2, pl.BlockSpec((tm_res, tn), lambda i, j, k: (i, j))),
                    (mod8[l], pl.BlockSpec((8, tn), lambda i, j, k, tn=tn: (0, 5 * (d // tn) + j)))],
            jobs=[_job_gather_ici([w2_b], l + 1), _job_gather_d2d([pend_w1])] if more else [], name=f"mlp_down{l}")
        if more:
            (t3, o), (pend_w2, (g_w1_next,)) = res
            g_w1.append(g_w1_next)
        else:
            (t3, o), pend_w2 = res, None
        saved.append((t, h, p, qkv, ao, mg, z, t2, h2, a, s, o, cpar))
        t = t3

    d_t, loss_tile, dg_final, dob, dgate2 = _loss_bwd(t, g_final[None], loss_target[0], saved[depth - 1][11], mod8[depth - 1], 5,
                                                      n_ctx, "loss")
    loss = lax.psum(loss_tile[0, 0], ("x", "y", "c"))

    buf_in = lax.empty((N_CHIP, depth, cin, d), BF16)
    buf_out = lax.empty((N_CHIP, depth, r_out, d), BF16)
    buf_w1 = lax.empty((N_CHIP, depth, d, c_ff), BF16)
    buf_w2 = lax.empty((N_CHIP, depth, r_ff, d), BF16)
    pend_in, pend_layer = None, None
    tkn = _pick(n, TOK)
    small = [None] * depth
    for l in reversed(range(depth)):
        t_in, h, p, qkv, ao, mg, z, t2, h2, a, s, o, cpar = saved[l]
        tm_g = _pick(d_ff, FEAT)
        tn = _pick(d, FEAT)
        gw2 = _matmul(s, dob, dims="tn", shape=(d_ff, d, n), tiles=(tm_g, tn, tkn), name=f"mlp_down_dw{l}")
        gw2 = gw2.reshape(N_DEV, r_ff, d)
        tn = _pick(d_ff, FEAT)
        jobs = [_job_scatter_d2d([gw2])] + ([_job_scatter_ici([pend_in], [buf_in], pend_layer)] if pend_in is not None else [])
        da, job_outs = _matmul(dob, w2_full[l], dims="nt", shape=(n, d_ff, d), tiles=(tm, tn, tk_d),
                               out_dtypes=(BF16,), epilogue=d_sq_relu_epilogue,
                               extras=[(a, pl.BlockSpec((tm, tn), lambda i, j, k: (i, j)))], jobs=jobs, name=f"mlp_down_dx{l}")
        if pend_in is not None:
            (buf_in,) = job_outs[1]
        pair_w2 = _pair_add(gw2, job_outs[0][0], f"pair_w2_{l}")
        tm_g = _pick(d, FEAT)
        tn = _pick(c_ff, FEAT)
        gw1 = _matmul(h2, da, dims="tn", shape=(d, d_ff, n), tiles=(tm_g, tn, tkn),
                      out_specs=(_g_spec("cols", None, c_ff, tm_g, tn),),
                      out_shapes=(jax.ShapeDtypeStruct((N_DEV, d, c_ff), F32),), name=f"mlp_up_dw{l}")
        tn = _pick(d, FEAT)
        tk = _pick(c_ff, (1024, 512, 256, 128))
        dh2, job_outs = _matmul(da, g_w1[l], dims="nt", shape=(n, d, d_ff), tiles=(tm, tn, tk),
                                b_spec=_w_spec("cols", None, c_ff, "nt", tn, tk),
                                jobs=[_job_scatter_d2d([gw1]), _job_scatter_ici([pair_w2], [buf_w2], l)], name=f"mlp_up_dx{l}")
        (buf_w2,) = job_outs[1]
        pair_w1 = _pair_add(gw1, job_outs[0][0], f"pair_w1_{l}")
        d_t2, dss2, dgn2, dzb, dgate1 = _norm_mod_bwd(dh2, t2, d_t, g_norm2[l][None], mod8[l], 3, n_ctx, f"norm2_bwd{l}",
                                                      gated=(z, mod8[l], 2))
        tm_g = _pick(d, FEAT)
        tn = _pick(d, FEAT)
        gout = _matmul(mg, dzb, dims="tn", shape=(d, d, n), tiles=(tm_g, tn, tkn), name=f"out_proj_dw{l}")
        gout = gout.reshape(N_DEV, r_out, d)
        tn = _pick(d, FEAT)
        d_mg, job_outs = _matmul(dzb, w_out_full[l], dims="nt", shape=(n, d, d), tiles=(tm, tn, tk_d),
                                 jobs=[_job_scatter_d2d([gout])], name=f"out_proj_dx{l}")
        pair_out = _pair_add(gout, job_outs[0][0], f"pair_out_{l}")
        d_q, parts, d_kv_ctx, d_sink, d_goa = _attention_bwd(qkv, sink[l], ao, d_mg, cpar, n_ctx, hd, kd, f"attn_bwd{l}")
        d_p, conv_acc = _mix_bwd(d_mg, p, cpar, d_q, parts, d_kv_ctx, cs, sn, n_ctx, dc, hd, kd, f"mix_bwd{l}")
        tm_g = _pick(d_in, (1536, 768, 512, 256, 128))
        tn = _pick(d, FEAT)
        gin = _matmul(d_p, h, dims="tn", shape=(d_in, d, n), tiles=(tm_g, tn, tkn), name=f"in_proj_dw{l}")
        gin = gin.reshape(N_DEV, cin, d)
        tn = _pick(d, FEAT)
        tk = _pick(d_in, (1536, 768, 512, 256, 128))
        dh, job_outs = _matmul(d_p, w_in_full[l], dims="nt", shape=(n, d, d_in), tiles=(tm, tn, tk),
                               jobs=[_job_scatter_d2d([gin]), _job_scatter_ici([pair_w1, pair_out], [buf_w1, buf_out], l)],
                               name=f"in_proj_dx{l}")
        buf_w1, buf_out = job_outs[1]
        pend_in, pend_layer = _pair_add(gin, job_outs[0][0], f"pair_in_{l}"), l
        d_mod2_tail = [dgate1[0:2], dss2[0:2], dgate2[0:2]]
        if l > 0:
            d_t, dss1, dgn1, dob, dgate2 = _norm_mod_bwd(dh, t_in, d_t2, g_norm1[l][None], mod8[l], 0, n_ctx, f"norm1_bwd{l}",
                                                         gated=(saved[l - 1][11], mod8[l - 1], 5))
        else:
            (grad_x, dss1, dgn1), ((buf_in,),) = _norm_mod_bwd(
                dh, t_in, d_t2, g_norm1[l][None], mod8[l], 0, n_ctx, f"norm1_bwd{l}", latent_only=True,
                jobs=[_job_scatter_ici([pend_in], [buf_in], pend_layer)])
        d_mod2 = jnp.concatenate([dss1[0:2]] + d_mod2_tail, axis=1)
        small[l] = (d_mod2, dgn1[0], dgn2[0], conv_acc, d_sink[0, 0:n_heads], d_goa[0])
    grad_x = grad_x[None]

    def pack(l):
        d_mod2, dgn1, dgn2, conv_acc, d_sink, d_goa = small[l]
        row0 = [d_mod2[0], dgn1, dgn2, conv_acc[3], d_sink, conv_acc[4], d_goa, conv_acc[0:3].reshape(-1)]
        row1 = [d_mod2[1]] + [jnp.zeros_like(v) for v in row0[1:]]
        return jnp.stack([jnp.concatenate(row0), jnp.concatenate(row1)])
    per_layer = N_MOD * d + 2 * d + dc + n_heads + 2 * dc + 3 * dc
    packed = jnp.concatenate([pack(l) for l in range(depth)] +
                             [jnp.stack([dg_final[0], jnp.zeros((d,), F32)])], axis=1)
    f_tot = depth * per_layer + d
    f_pad = -f_tot % 1024
    packed = jnp.pad(packed, ((0, 0), (0, f_pad)))
    (small_all,) = _all_gather([packed], "gather_small")
    small_parts = small_all.reshape(2 * N_DEV, 1, f_tot + f_pad)

    def section(arr, l, off, size):
        return lax.slice_in_dim(arr, l * per_layer + off, l * per_layer + off + size, axis=-1)

    offs = {}
    o_ = 0
    for nm_, sz in (("mod", N_MOD * d), ("gn1", d), ("gn2", d), ("cb", dc), ("sink", n_heads), ("goc", dc), ("goa", dc), ("cw", 3 * dc)):
        offs[nm_] = (o_, sz)
        o_ += sz

    def packw(b_ada_, gn1_, gn2_, cb_, sk_, goc_, goa_, gf_):
        rows = []
        for l in range(depth):
            rows += [b_ada_[l], gn1_[l], gn2_[l], cb_[l], sk_[l], goc_[l], goa_[l], jnp.zeros((3 * dc,), F32)]
        return jnp.pad(jnp.concatenate(rows + [gf_]), (0, f_pad))[None]
    pw = packw(b_ada, g_norm1, g_norm2, conv_b, sink, g_out_conv, g_out_attn, g_final)
    pm = packw(m_b_ada, m_g_norm1, m_g_norm2, m_conv_b, m_sink, m_g_out_conv, m_g_out_attn, m_g_final)
    pv = packw(v_b_ada, v_g_norm1, v_g_norm2, v_conv_b, v_sink, v_g_out_conv, v_g_out_attn, v_g_final)
    sg, sd, sm, sv = _adamw(pw, pm, pv, small_parts, "adamw_small")

    def unpack(arr):
        arr = arr[0]
        out = {}
        for nm_ in ("mod", "gn1", "gn2", "cb", "sink", "goc", "goa", "cw"):
            off, size = offs[nm_]
            out[nm_] = jnp.stack([section(arr, l, off, size) for l in range(depth)])
        out["gf"] = arr[depth * per_layer:depth * per_layer + d]
        return out
    ug, ud, um, uv = unpack(sg), unpack(sd), unpack(sm), unpack(sv)

    cw_grad_full = ug["cw"].reshape(depth, 3, N_DEV, cw)
    cw_grad = lax.dynamic_index_in_dim(cw_grad_full, dev, axis=2, keepdims=False).reshape(1, depth * 3, cw)
    cwg, cwd, cwm, cwv = _adamw(conv_w.reshape(depth * 3, cw), m_conv_w.reshape(depth * 3, cw),
                                v_conv_w.reshape(depth * 3, cw), cw_grad, "adamw_conv_w")
    cw_shape = conv_w.shape

    mod_rows = small_all[:, :, :depth * per_layer].reshape(N_DEV, 2, depth, per_layer)[:, :, :, :N_MOD * d]
    dm16 = jnp.concatenate([mod_rows[:, 0], mod_rows[:, 1]], axis=0)
    dm16 = jnp.transpose(dm16, (1, 0, 2)).reshape(depth, 16, N_DEV, c_ada)
    dm16_loc = lax.dynamic_index_in_dim(dm16, dev, axis=2, keepdims=False)
    gb_ada = lax.empty((depth, d, c_ada), F32)
    dsc_parts = []
    tm_g = _pick(d, FEAT)
    for l in range(depth):
        gb_ada = _matmul(sc16, dm16_loc[l], dims="tn", shape=(d, c_ada, 16), tiles=(tm_g, tn_ada, 16),
                         out_specs=(_g_spec("nat", l, None, tm_g, tn_ada),), out_shapes=(jax.ShapeDtypeStruct(gb_ada.shape, F32),),
                         alias=gb_ada, name=f"ada_dw{l}")
        tn = _pick(d, FEAT)
        tk = _pick(c_ada, KDIM)
        dsc_parts.append(_matmul(dm16_loc[l], w_ada, dims="nt", shape=(16, d, c_ada), tiles=(16, tn, tk),
                                 b_spec=_w_spec("nat", l, None, "nt", tn, tk), name=f"ada_dx{l}"))
    (dsc_all,) = _all_gather([jnp.stack(dsc_parts)[:, 8:16]], "gather_dcond")
    g_cctx = _cctx_grad(dsc_all.reshape(N_DEV * depth * 8, d), c_ctx[None], "c_ctx_grad")
    ccg, ccd, ccm, ccv = _adamw(c_ctx[None], m_c_ctx[None], v_c_ctx[None], g_cctx[None], "adamw_c_ctx")
    adg, add, adm, adv = _adamw(w_ada.reshape(depth * d, c_ada), m_w_ada.reshape(depth * d, c_ada),
                                v_w_ada.reshape(depth * d, c_ada), gb_ada.reshape(1, depth * d, c_ada), "adamw_w_ada")

    parts4 =[jnp.swapaxes(buf_in, 2, 3), buf_out, buf_w1, buf_w2]
    big = []
    for k, (w_, m_, v_) in enumerate(((w_in, m_w_in, v_w_in), (w_out, m_w_out, v_w_out),
                                      (w_mlp1, m_w_mlp1, v_w_mlp1), (w_mlp2, m_w_mlp2, v_w_mlp2))):
        r2, c2 = w_.shape[0] * w_.shape[1], w_.shape[2]
        res = _adamw(w_.reshape(r2, c2), m_.reshape(r2, c2), v_.reshape(r2, c2), parts4[k].reshape(N_CHIP, r2, c2),
                     f"adamw_big{k}")
        big.append([a_.reshape(w_.shape) for a_ in res])

    def leaf(i):
        return (
            (ccg, ccd, ccm, ccv)[i][0], (adg, add, adm, adv)[i].reshape(w_ada.shape),
            (ug, ud, um, uv)[i]["mod"], (ug, ud, um, uv)[i]["gn1"], (ug, ud, um, uv)[i]["gn2"], big[0][i],
            (cwg, cwd, cwm, cwv)[i].reshape(cw_shape), (ug, ud, um, uv)[i]["cb"], (ug, ud, um, uv)[i]["sink"],
            (ug, ud, um, uv)[i]["goc"], (ug, ud, um, uv)[i]["goa"], big[1][i], big[2][i], big[3][i], (ug, ud, um, uv)[i]["gf"])

    return (loss, grad_x) + leaf(0) + leaf(1) + leaf(2) + leaf(3)
```

```python
import functools
import math

import jax
import jax.numpy as jnp
from jax import lax
from jax.experimental import pallas as pl
from jax.experimental.pallas import tpu as pltpu

HEAD_DIM = 64
N_KV_HEADS = 4
BLOCK = 128
GRID_W = 64
ROPE_THETA = 10000.0
EPS = 1e-6
N_MOD = 6
SCALE = HEAD_DIM ** -0.5
NEG_INF = -1e30
ADAM_LR = 0.001
ADAM_B1 = 0.9
ADAM_B2 = 0.999
ADAM_EPS = 1e-08
ADAM_WD = 0.01
ADAM_STEP = 10
N_DEV = 8
N_CHIP = 4
VMEM_LIMIT_BYTES = 48 * 1024 * 1024
MESH = pl.DeviceIdType.MESH
BF16 = jnp.bfloat16
F32 = jnp.float32
ANY = pl.BlockSpec(memory_space=pl.ANY)


def _pick(dim, prefs):
    for p in prefs:
        if p <= dim and dim % p == 0:
            return p
    return dim


def _params(sem):
    return pltpu.CompilerParams(dimension_semantics=sem, vmem_limit_bytes=VMEM_LIMIT_BYTES)


def _row_ids(i, tr):
    return i * tr + lax.broadcasted_iota(jnp.int32, (tr, 1), 0)


def _colsum(v):
    return jnp.sum(v, axis=0, keepdims=True)


def _rowmean(v):
    return jnp.mean(v, axis=1, keepdims=True)


def _all_gather(xs, name):
    na = len(xs)

    def body(*refs):
        x_refs, o_refs = refs[:na], refs[na:2 * na]
        send_sems, recv_sems, local_sems = refs[2 * na:]
        x, y, c = lax.axis_index("x"), lax.axis_index("y"), lax.axis_index("c")
        me, sibling = (x, y, c), (x, y, 1 - c)
        chips = [(1 - x, y), (x, 1 - y), (1 - x, 1 - y)]

        def slot(a, px, py, pc):
            return o_refs[a].at[4 * px + 2 * py + pc]

        def copy(a, k, block, to, src=None):
            return pltpu.make_async_remote_copy(
                src_ref=slot(a, *block) if src is None else src, dst_ref=slot(a, *block),
                send_sem=send_sems.at[a, k], recv_sem=recv_sems.at[a, k], device_id=to, device_id_type=MESH)

        mine = [pltpu.make_async_copy(x_refs[a], slot(a, *me), local_sems.at[a]) for a in range(na)]
        for cp in mine:
            cp.start()
        first = []
        for a in range(na):
            first.append(copy(a, 0, me, sibling, src=x_refs[a]))
            first += [copy(a, 1 + j, me, (*chip, c), src=x_refs[a]) for j, chip in enumerate(chips)]
        for cp in first:
            cp.start()
        passed = []
        for j, chip in enumerate(chips):
            for a in range(na):
                copy(a, 1 + j, (*chip, c), me).wait_recv()
                fwd = copy(a, 4 + j, (*chip, c), sibling)
                fwd.start()
                passed.append(fwd)
        for a in range(na):
            copy(a, 0, sibling, me).wait_recv()
            for j, chip in enumerate(chips):
                copy(a, 4 + j, (*chip, 1 - c), me).wait_recv()
        for cp in first + passed:
            cp.wait_send()
        for cp in mine:
            cp.wait()

    outs = pl.pallas_call(
        body, name=name,
        out_shape=tuple(jax.ShapeDtypeStruct((N_DEV,) + x.shape, x.dtype) for x in xs),
        in_specs=[ANY] * na, out_specs=tuple([ANY] * na),
        scratch_shapes=[pltpu.SemaphoreType.DMA((na, 7)), pltpu.SemaphoreType.DMA((na, 7)),
                        pltpu.SemaphoreType.DMA((na,))],
    )(*xs)
    return list(outs)


class _Job:
    def __init__(self, ins, outs, alias, sems, copies):
        self.ins, self.outs, self.alias, self.sems, self._copies = ins, outs, alias, sems, copies

    def start(self, in_refs, out_refs, sems):
        local, sends, _ = self._copies(in_refs, out_refs, sems)
        for make in local + sends:
            make().start()

    def wait(self, in_refs, out_refs, sems):
        local, sends, arrivals = self._copies(in_refs, out_refs, sems)
        for make in arrivals:
            make().wait_recv()
        for make in sends:
            make().wait_send()
        for make in local:
            make().wait()


def _other_chips():
    x, y = lax.axis_index("x"), lax.axis_index("y")
    return [(1 - x, y), (x, 1 - y), (1 - x, 1 - y)]


def _remote(src, dst, send_sem, recv_sem, to):
    return functools.partial(pltpu.make_async_remote_copy, src_ref=src, dst_ref=dst, send_sem=send_sem, recv_sem=recv_sem,
                             device_id=to, device_id_type=MESH)


def _local(src, dst, sem):
    return functools.partial(pltpu.make_async_copy, src, dst, sem)


def _job_gather_ici(xs, layer):
    na = len(xs)

    def copies(in_refs, out_refs, sems):
        send, recv, loc = sems
        x, y, c = lax.axis_index("x"), lax.axis_index("y"), lax.axis_index("c")
        me = 4 * x + 2 * y + c
        local, sends, arrivals = [], [], []
        for a in range(na):
            src = in_refs[a].at[layer]
            local.append(_local(src, out_refs[a].at[me], loc.at[a]))
            for j, (px, py) in enumerate(_other_chips()):
                sends.append(_remote(src, out_refs[a].at[me], send.at[a, j], recv.at[a, j], (px, py, c)))
                arrivals.append(_remote(src, out_refs[a].at[4 * px + 2 * py + c], send.at[a, j], recv.at[a, j], (px, py, c)))
        return local, sends, arrivals

    outs = [jax.ShapeDtypeStruct((N_DEV,) + x.shape[1:], x.dtype) for x in xs]
    sems = [pltpu.SemaphoreType.DMA((na, 3)), pltpu.SemaphoreType.DMA((na, 3)), pltpu.SemaphoreType.DMA((na,))]
    return _Job(list(xs), outs, {}, sems, copies)


def _job_gather_d2d(gs):
    na = len(gs)

    def copies(in_refs, out_refs, sems):
        send, recv = sems
        x, y, c = lax.axis_index("x"), lax.axis_index("y"), lax.axis_index("c")
        sends, arrivals = [], []
        for a in range(na):
            for k in range(N_CHIP):
                mine, theirs = 2 * k + c, 2 * k + (1 - c)
                sends.append(_remote(in_refs[a].at[mine], out_refs[a].at[mine], send.at[a, k], recv.at[a, k], (x, y, 1 - c)))
                arrivals.append(_remote(in_refs[a].at[theirs], out_refs[a].at[theirs], send.at[a, k], recv.at[a, k], (x, y, 1 - c)))
        return [], sends, arrivals

    outs = [jax.ShapeDtypeStruct(g.shape, g.dtype) for g in gs]
    sems = [pltpu.SemaphoreType.DMA((na, N_CHIP)), pltpu.SemaphoreType.DMA((na, N_CHIP))]
    return _Job(list(gs), outs, {a: a for a in range(na)}, sems, copies)


def _job_scatter_d2d(gs):
    na = len(gs)

    def copies(in_refs, out_refs, sems):
        send, recv = sems
        x, y, c = lax.axis_index("x"), lax.axis_index("y"), lax.axis_index("c")
        sends, arrivals = [], []
        for a in range(na):
            for k in range(N_CHIP):
                cp = _remote(in_refs[a].at[2 * k + (1 - c)], out_refs[a].at[k], send.at[a, k], recv.at[a, k], (x, y, 1 - c))
                sends.append(cp)
                arrivals.append(cp)
        return [], sends, arrivals

    outs = [jax.ShapeDtypeStruct((N_CHIP,) + g.shape[1:], g.dtype) for g in gs]
    sems = [pltpu.SemaphoreType.DMA((na, N_CHIP)), pltpu.SemaphoreType.DMA((na, N_CHIP))]
    return _Job(list(gs), outs, {}, sems, copies)


def _job_scatter_ici(pairs, bufs, layer):
    na = len(pairs)

    def copies(in_refs, out_refs, sems):
        send, recv, loc = sems
        x, y, c = lax.axis_index("x"), lax.axis_index("y"), lax.axis_index("c")
        my_chip = 2 * x + y
        local, sends, arrivals = [], [], []
        for a in range(na):
            local.append(_local(in_refs[a].at[my_chip], out_refs[a].at[my_chip, layer], loc.at[a]))
            for j, (px, py) in enumerate(_other_chips()):
                src = in_refs[a].at[2 * px + py]
                sends.append(_remote(src, out_refs[a].at[my_chip, layer], send.at[a, j], recv.at[a, j], (px, py, c)))
                arrivals.append(_remote(src, out_refs[a].at[2 * px + py, layer], send.at[a, j], recv.at[a, j], (px, py, c)))
        return local, sends, arrivals

    outs = [jax.ShapeDtypeStruct(b.shape, b.dtype) for b in bufs]
    sems = [pltpu.SemaphoreType.DMA((na, 3)), pltpu.SemaphoreType.DMA((na, 3)), pltpu.SemaphoreType.DMA((na,))]
    return _Job(list(pairs) + list(bufs), outs, {na + a: a for a in range(na)}, sems, copies)


def _split_jobs(jobs, in_refs, out_refs, sem_refs):
    out, i0, o0, s0 = [], 0, 0, 0
    for jb in jobs:
        out.append((jb, in_refs[i0:i0 + len(jb.ins)], out_refs[o0:o0 + len(jb.outs)], sem_refs[s0:s0 + len(jb.sems)]))
        i0, o0, s0 = i0 + len(jb.ins), o0 + len(jb.outs), s0 + len(jb.sems)
    return out


def _carry(body, jobs, n_in, n_out, n_steps):
    if not jobs:
        return body
    n_ji = sum(len(jb.ins) for jb in jobs)
    n_jo = sum(len(jb.outs) for jb in jobs)

    def wrapped(*refs):
        o0 = n_in + n_ji
        parts = _split_jobs(jobs, refs[n_in:o0], refs[o0 + n_out:o0 + n_out + n_jo], refs[o0 + n_out + n_jo:])

        @pl.when(pl.program_id(0) == 0)
        def _():
            for jb, i_r, o_r, s_r in parts:
                jb.start(i_r, o_r, s_r)

        body(*refs[:n_in], *refs[o0:o0 + n_out])

        @pl.when(pl.program_id(0) == n_steps - 1)
        def _():
            for jb, i_r, o_r, s_r in parts:
                jb.wait(i_r, o_r, s_r)

    return wrapped


def _carry_args(jobs, n_in, n_out):
    io_alias, i0, o0 = {}, n_in, n_out
    for jb in jobs:
        io_alias.update({i0 + a: o0 + b for a, b in jb.alias.items()})
        i0, o0 = i0 + len(jb.ins), o0 + len(jb.outs)
    ins = [v for jb in jobs for v in jb.ins]
    outs = [s for jb in jobs for s in jb.outs]
    return ins, [ANY] * len(ins), outs, [ANY] * len(outs), [s for jb in jobs for s in jb.sems], io_alias


def _carry_results(jobs, outs, n_out):
    res, o0 = [], n_out
    for jb in jobs:
        res.append(list(outs[o0:o0 + len(jb.outs)]))
        o0 += len(jb.outs)
    return res


def _comm_call(jobs, name):
    n_in = sum(len(jb.ins) for jb in jobs)
    n_out = sum(len(jb.outs) for jb in jobs)

    def body(*refs):
        parts = _split_jobs(jobs, refs[:n_in], refs[n_in:n_in + n_out], refs[n_in + n_out:])
        for jb, i_r, o_r, s_r in parts:
            jb.start(i_r, o_r, s_r)
        for jb, i_r, o_r, s_r in parts:
            jb.wait(i_r, o_r, s_r)

    io_alias, i0, o0 = {}, 0, 0
    for jb in jobs:
        io_alias.update({i0 + a: o0 + b for a, b in jb.alias.items()})
        i0, o0 = i0 + len(jb.ins), o0 + len(jb.outs)
    outs = pl.pallas_call(
        body, name=name, out_shape=tuple(s for jb in jobs for s in jb.outs),
        in_specs=[ANY] * n_in, out_specs=tuple([ANY] * n_out),
        scratch_shapes=[s for jb in jobs for s in jb.sems], input_output_aliases=io_alias,
    )(*[v for jb in jobs for v in jb.ins])
    res, o0 = [], 0
    for jb in jobs:
        res.append(list(outs[o0:o0 + len(jb.outs)]))
        o0 += len(jb.outs)
    return res


def _to_bf16(x2d, name):
    r, c = x2d.shape
    tr = _pick(r, (512, 256, 128, 64, 32, 16))

    def body(x_ref, o_ref):
        o_ref[...] = x_ref[...].astype(BF16)

    return pl.pallas_call(
        body, name=name, out_shape=jax.ShapeDtypeStruct((r, c), BF16), grid=(r // tr,),
        in_specs=[pl.BlockSpec((tr, c), lambda i: (i, 0))], out_specs=pl.BlockSpec((tr, c), lambda i: (i, 0)),
        compiler_params=_params(("parallel",)),
    )(x2d)


def _pair_add(own8, got4, name):
    _, r, c = own8.shape
    tr = _pick(r, (512, 256, 128, 64, 32, 16))
    core = lax.axis_index("c").astype(jnp.int32).reshape(1)

    def body(c_ref, a_ref, b_ref, o_ref):
        o_ref[...] = (a_ref[...] + b_ref[...]).astype(BF16)

    return pl.pallas_call(
        body, name=name, out_shape=jax.ShapeDtypeStruct((N_CHIP, r, c), BF16),
        grid_spec=pltpu.PrefetchScalarGridSpec(
            num_scalar_prefetch=1, grid=(N_CHIP, r // tr),
            in_specs=[pl.BlockSpec((None, tr, c), lambda k, i, cr: (2 * k + cr[0], i, 0)),
                      pl.BlockSpec((None, tr, c), lambda k, i, cr: (k, i, 0))],
            out_specs=pl.BlockSpec((None, tr, c), lambda k, i, cr: (k, i, 0))),
        compiler_params=_params(("parallel", "parallel")),
    )(core, own8, got4)


def _adamw(w, m, v, parts, name):
    r, c = w.shape
    n_parts = parts.shape[0]
    tr = _pick(r, (256, 128, 64, 32, 16, 8))
    tc = _pick(c, (1024, 512, 256, 128))

    def body(w_ref, m_ref, v_ref, p_ref, g_ref, d_ref, nm_ref, nv_ref):
        g = p_ref[0].astype(F32)
        for k in range(1, n_parts):
            g = g + p_ref[k].astype(F32)
        wv = w_ref[...]
        nm = ADAM_B1 * m_ref[...] + (1.0 - ADAM_B1) * g
        nv = ADAM_B2 * v_ref[...] + (1.0 - ADAM_B2) * (g * g)
        m_hat = nm / (1.0 - ADAM_B1 ** ADAM_STEP)
        v_hat = nv / (1.0 - ADAM_B2 ** ADAM_STEP)
        g_ref[...] = g
        d_ref[...] = -ADAM_LR * (m_hat / (jnp.sqrt(v_hat) + ADAM_EPS) + ADAM_WD * wv)
        nm_ref[...] = nm
        nv_ref[...] = nv

    tile = pl.BlockSpec((tr, tc), lambda i, j: (i, j))
    sh = jax.ShapeDtypeStruct((r, c), F32)
    return pl.pallas_call(
        body, name=name, out_shape=(sh, sh, sh, sh), grid=(r // tr, c // tc),
        in_specs=[tile, tile, tile, pl.BlockSpec((n_parts, tr, tc), lambda i, j: (0, i, j))],
        out_specs=(tile, tile, tile, tile),
        compiler_params=_params(("parallel", "parallel")),
    )(w, m, v, parts)


def _matmul(a, b, *, dims, shape, tiles, b_spec=None, out_specs=None, out_shapes=None, out_dtypes=(F32,),
            epilogue=None, extras=(), alias=None, jobs=(), name):
    m_dim, n_dim, k_dim = shape
    tm, tn, tk = tiles
    assert m_dim % tm == 0 and n_dim % tn == 0 and k_dim % tk == 0, (name, shape, tiles)
    nk = k_dim // tk
    n_extra = len(extras)
    n_alias = 0 if alias is None else 1
    n_out = len(out_dtypes)
    if dims == "tn":
        a_spec = pl.BlockSpec((tk, tm), lambda i, j, k: (k, i))
        contract = (((0,), (0,)), ((), ()))
    else:
        a_spec = pl.BlockSpec((tm, tk), lambda i, j, k: (i, k))
        contract = (((1,), (1,)), ((), ())) if dims == "nt" else (((1,), (0,)), ((), ()))
    if b_spec is None:
        b_spec = (pl.BlockSpec((tn, tk), lambda i, j, k: (j, k)) if dims == "nt"
                  else pl.BlockSpec((tk, tn), lambda i, j, k: (k, j)))
    if out_specs is None:
        out_specs = tuple(pl.BlockSpec((tm, tn), lambda i, j, k: (i, j)) for _ in range(n_out))
    if out_shapes is None:
        out_shapes = tuple(jax.ShapeDtypeStruct((m_dim, n_dim), d) for d in out_dtypes)

    n_local_in = 2 + n_extra + n_alias
    n_job_in = sum(len(jb.ins) for jb in jobs)
    n_job_out = sum(len(jb.outs) for jb in jobs)
    n_acc = 1 if nk > 1 else 0
    grid = (m_dim // tm, n_dim // tn, nk)

    def body(*refs):
        a_ref, b_ref = refs[0], refs[1]
        extra_refs = refs[2:2 + n_extra]
        o0 = n_local_in + n_job_in
        out_refs = refs[o0:o0 + n_out]
        s0 = o0 + n_out + n_job_out
        job_parts = _split_jobs(jobs, refs[n_local_in:o0], refs[o0 + n_out:s0], refs[s0 + n_acc:])
        i, j, k = pl.program_id(0), pl.program_id(1), pl.program_id(2)

        if jobs:
            @pl.when((i == 0) & (j == 0) & (k == 0))
            def _():
                for jb, i_r, o_r, s_r in job_parts:
                    jb.start(i_r, o_r, s_r)

        def finish(acc):
            if epilogue is None:
                out_refs[0][...] = acc.astype(out_refs[0].dtype)
            else:
                epilogue(acc, i, j, extra_refs, out_refs)

        def product():
            return lax.dot_general(a_ref[...].astype(BF16), b_ref[...].astype(BF16), contract, preferred_element_type=F32)

        if nk == 1:
            finish(product())
        else:
            acc_ref = refs[s0]

            @pl.when(k == 0)
            def _():
                acc_ref[...] = jnp.zeros_like(acc_ref)

            acc_ref[...] += product()

            @pl.when(k == nk - 1)
            def _():
                finish(acc_ref[...])

        if jobs:
            @pl.when((i == grid[0] - 1) & (j == grid[1] - 1) & (k == nk - 1))
            def _():
                for jb, i_r, o_r, s_r in job_parts:
                    jb.wait(i_r, o_r, s_r)

    ins = [a, b] + [e[0] for e in extras]
    in_specs = [a_spec, b_spec] + [e[1] for e in extras]
    io_alias = {}
    if alias is not None:
        ins.append(alias)
        in_specs.append(ANY)
        io_alias = {len(ins) - 1: 0}
    all_out_shapes, all_out_specs = list(out_shapes), list(out_specs)
    for jb in jobs:
        io_alias.update({len(ins) + a_: len(all_out_shapes) + b_ for a_, b_ in jb.alias.items()})
        ins += jb.ins
        in_specs += [ANY] * len(jb.ins)
        all_out_shapes += jb.outs
        all_out_specs += [ANY] * len(jb.outs)
    scratch = ([pltpu.VMEM((tm, tn), F32)] if nk > 1 else []) + [s for jb in jobs for s in jb.sems]
    outs = pl.pallas_call(
        body, name=name, out_shape=tuple(all_out_shapes), grid=grid,
        in_specs=in_specs, out_specs=tuple(all_out_specs), scratch_shapes=scratch,
        input_output_aliases=io_alias,
        compiler_params=_params(("arbitrary",) * 3 if jobs else ("parallel", "parallel", "arbitrary")),
    )(*ins)
    main = outs[0] if n_out == 1 else tuple(outs[:n_out])
    if not jobs:
        return main
    job_outs, o0 = [], n_out
    for jb in jobs:
        job_outs.append(list(outs[o0:o0 + len(jb.outs)]))
        o0 += len(jb.outs)
    return main, job_outs


def _lead(layer, block, index):
    if layer is None:
        return pl.BlockSpec(block, index)
    return pl.BlockSpec((None,) + block, lambda i, j, k: (layer,) + index(i, j, k))


def _w_spec(kind, layer, per, dims, tn, tk):
    if kind == "nat":
        if dims == "nn":
            return _lead(layer, (tk, tn), lambda i, j, k: (k, j))
        return _lead(layer, (tn, tk), lambda i, j, k: (j, k))
    if dims == "nn":
        q = per // tn
        return _lead(layer, (None, tk, tn), lambda i, j, k: (j // q, k, j % q))
    q = per // tk
    return _lead(layer, (None, tn, tk), lambda i, j, k: (k // q, j, k % q))


def _g_spec(kind, layer, per, tm, tn):
    if kind == "nat":
        return _lead(layer, (tm, tn), lambda i, j, k: (i, j))
    q = per // tn
    return _lead(layer, (None, tm, tn), lambda i, j, k: (j // q, i, j % q))


TOK = (1408, 768, 256, 128)
FEAT = (1024, 512, 256, 128)
KDIM = (2048, 1536, 1024, 512, 256, 128)


def _sel(is_ctx, ref):
    return jnp.where(is_ctx, ref[1:2, :], ref[0:1, :])


def _row_tile(n, n_ctx):
    tr = _pick(n_ctx, (256, 128))
    assert n % tr == 0 and n_ctx % tr == 0
    return tr


def _add_by_segment(acc_ref, cols, ctx_tile, v):
    zero = jnp.zeros_like(v)
    acc_ref[0:1, cols] += jnp.where(ctx_tile, zero, v)
    acc_ref[1:2, cols] += jnp.where(ctx_tile, v, zero)


def _norm_mod(t, gain, mod8, shift_k, n_ctx, name, jobs=()):
    n, d = t.shape
    tr = _row_tile(n, n_ctx)

    def body(t_ref, g_ref, sh_ref, sc_ref, o_ref):
        x = t_ref[...]
        r = lax.rsqrt(_rowmean(x * x) + EPS)
        y = (x * r) * g_ref[...]
        ctx_tile = pl.program_id(0) * tr < n_ctx
        o_ref[...] = (y * (1.0 + _sel(ctx_tile, sc_ref)) + _sel(ctx_tile, sh_ref)).astype(BF16)

    j_ins, j_in_specs, j_outs, j_out_specs, j_sems, io_alias = _carry_args(jobs, 4, 1)
    outs = pl.pallas_call(
        _carry(body, jobs, 4, 1, n // tr), name=name, grid=(n // tr,),
        out_shape=(jax.ShapeDtypeStruct((n, d), BF16),) + tuple(j_outs),
        in_specs=[pl.BlockSpec((tr, d), lambda i: (i, 0)), pl.BlockSpec((1, d), lambda i: (0, 0)),
                  pl.BlockSpec((8, d), lambda i: (0, shift_k)), pl.BlockSpec((8, d), lambda i: (0, shift_k + 1))] + j_in_specs,
        out_specs=(pl.BlockSpec((tr, d), lambda i: (i, 0)),) + tuple(j_out_specs),
        scratch_shapes=j_sems, input_output_aliases=io_alias,
        compiler_params=_params(("arbitrary",) if jobs else ("parallel",)),
    )(t, gain, mod8, mod8, *j_ins)
    return (outs[0], _carry_results(jobs, outs, 1)) if jobs else outs[0]


def _norm_mod_bwd(dh, t, d_res, gain, mod8, shift_k, n_ctx, name, gated=None, latent_only=False, jobs=()):
    n, d = t.shape
    tr = _row_tile(n, n_ctx)

    def body(*refs):
        if gated is None:
            dh_ref, t_ref, dr_ref, g_ref, sc_ref, dt_ref, dss_ref, dg_ref = refs
        else:
            dh_ref, t_ref, dr_ref, g_ref, sc_ref, br_ref, gt_ref, dt_ref, dss_ref, dg_ref, dob_ref, dgate_ref = refs
        i = pl.program_id(0)

        @pl.when(i == 0)
        def _():
            dss_ref[...] = jnp.zeros_like(dss_ref)
            dg_ref[...] = jnp.zeros_like(dg_ref)
            if gated is not None:
                dgate_ref[...] = jnp.zeros_like(dgate_ref)

        x = t_ref[...]
        r = lax.rsqrt(_rowmean(x * x) + EPS)
        xn = x * r
        g = g_ref[...]
        y = xn * g
        dhv = dh_ref[...]
        ctx_tile = i * tr < n_ctx
        _add_by_segment(dss_ref, slice(0, d), ctx_tile, _colsum(dhv))
        _add_by_segment(dss_ref, slice(d, 2 * d), ctx_tile, _colsum(dhv * y))
        dy = dhv * (1.0 + _sel(ctx_tile, sc_ref))
        dg_ref[0:1, :] += _colsum(dy * xn)
        dxn = dy * g
        d_t = dr_ref[...] + r * (dxn - xn * _rowmean(dxn * xn))
        dt_ref[...] = d_t
        if gated is not None:
            dob_ref[...] = (d_t * _sel(ctx_tile, gt_ref)).astype(BF16)
            _add_by_segment(dgate_ref, slice(None), ctx_tile, _colsum(d_t * br_ref[...].astype(F32)))

    row = pl.BlockSpec((tr, d), lambda i: (i, 0))
    acc = pl.BlockSpec((8, d), lambda i: (0, 0))
    ins = [dh, t, d_res, gain, mod8]
    in_specs = [row, row, row, pl.BlockSpec((1, d), lambda i: (0, 0)), pl.BlockSpec((8, d), lambda i: (0, shift_k + 1))]
    if latent_only:
        first = n_ctx // tr
        out_shape = [jax.ShapeDtypeStruct((n - n_ctx, d), F32)]
        out_specs = [pl.BlockSpec((tr, d), lambda i: (jnp.maximum(i - first, 0), 0))]
    else:
        out_shape, out_specs = [jax.ShapeDtypeStruct((n, d), F32)], [row]
    out_shape += [jax.ShapeDtypeStruct((8, 2 * d), F32), jax.ShapeDtypeStruct((8, d), F32)]
    out_specs += [pl.BlockSpec((8, 2 * d), lambda i: (0, 0)), acc]
    if gated is not None:
        branch, gate_mod8, gate_k = gated
        ins += [branch, gate_mod8]
        in_specs += [row, pl.BlockSpec((8, d), lambda i: (0, gate_k))]
        out_shape += [jax.ShapeDtypeStruct((n, d), BF16), jax.ShapeDtypeStruct((8, d), F32)]
        out_specs += [row, acc]
    n_in, n_out = len(ins), len(out_shape)
    j_ins, j_in_specs, j_outs, j_out_specs, j_sems, io_alias = _carry_args(jobs, n_in, n_out)
    outs = pl.pallas_call(
        _carry(body, jobs, n_in, n_out, n // tr), name=name, out_shape=tuple(out_shape + j_outs), grid=(n // tr,),
        in_specs=in_specs + j_in_specs, out_specs=tuple(out_specs + j_out_specs), scratch_shapes=j_sems,
        input_output_aliases=io_alias, compiler_params=_params(("arbitrary",)),
    )(*ins, *j_ins)
    return (tuple(outs[:n_out]), _carry_results(jobs, outs, n_out)) if jobs else outs


def _swap16(v):
    w = v.shape[1]
    lane = lax.broadcasted_iota(jnp.int32, v.shape, 1)
    return jnp.where((lane % 32) < 16, pltpu.roll(v, w - 16, 1), pltpu.roll(v, 16, 1))


def _rope(v, cs, sn, sign):
    reps = v.shape[1] // 128
    c = jnp.tile(cs, (1, reps)) if reps > 1 else cs
    s = jnp.tile(sn, (1, reps)) if reps > 1 else sn
    return v * c + sign * (_swap16(v) * s)


def _attn_specs(nb, n_ctx, hd, kd):
    kci = hd // kd
    specs = [pl.BlockSpec((BLOCK, hd), lambda b: (b, 0)),
             pl.BlockSpec((n_ctx, kd), lambda b: (0, kci)), pl.BlockSpec((n_ctx, kd), lambda b: (0, kci + 1))]
    for col in (kci, kci + 1):
        specs.append(pl.BlockSpec((BLOCK, kd), lambda b, col=col: (jnp.maximum(b - 1, 0), col)))
        specs.append(pl.BlockSpec((BLOCK, kd), lambda b, col=col: (b, col)))
        specs.append(pl.BlockSpec((BLOCK, kd), lambda b, col=col: (jnp.minimum(b + 1, nb - 1), col)))
    return specs


def _band_valid(b, group, n_ctx, n):
    q_pos = b * BLOCK + lax.broadcasted_iota(jnp.int32, (group * BLOCK, 1), 0) % BLOCK
    k_pos = (b - 1) * BLOCK + lax.broadcasted_iota(jnp.int32, (1, 3 * BLOCK), 1)
    return (jnp.abs(k_pos - q_pos) <= BLOCK) & (k_pos >= n_ctx) & (k_pos < n) & (q_pos >= n_ctx)


NT = (((1,), (1,)), ((), ()))
NN = (((1,), (0,)), ((), ()))
TN = (((0,), (0,)), ((), ()))


def _dot(a, b, dn):
    return lax.dot_general(a, b, dn, preferred_element_type=F32)


def _keys_of_block(b, kv_refs, group, n_ctx, n):
    kc_ref, vc_ref, k0, k1, k2, v0, v1, v2 = kv_refs
    bias = jnp.where(_band_valid(b, group, n_ctx, n), 0.0, NEG_INF)
    kcat = jnp.concatenate([kc_ref[...], k0[...], k1[...], k2[...]], axis=0)
    vcat = jnp.concatenate([vc_ref[...], v0[...], v1[...], v2[...]], axis=0)
    return bias, kcat, vcat


def _stack_heads(v, h, group):
    return jnp.concatenate([v[:, (h * group + j) * HEAD_DIM:(h * group + j + 1) * HEAD_DIM] for j in range(group)], axis=0)


def _scores(qg, keys, bias, n_ctx):
    s = _dot(qg, keys, NT)
    return jnp.concatenate([s[:, :n_ctx], s[:, n_ctx:] + bias], axis=1)


def _softmax_terms(s, sink_ref, h, group):
    snk = jnp.concatenate([jnp.full((BLOCK, 1), sink_ref[h * group + j], F32) for j in range(group)], axis=0)
    m = jnp.maximum(jnp.max(s, axis=1, keepdims=True), snk)
    e, e_s = jnp.exp(s - m), jnp.exp(snk - m)
    return e, e_s, 1.0 / (jnp.sum(e, axis=1, keepdims=True) + e_s)


def _softmax_sink(s, sink_ref, h, group):
    e, e_s, inv = _softmax_terms(s, sink_ref, h, group)
    return e * inv, e_s * inv


def _attention(qkv, sink, n_ctx, hd, kd, name, jobs=()):
    n = qkv.shape[0]
    nb = n // BLOCK
    n_kv = kd // HEAD_DIM
    group = hd // kd
    n_job_in = sum(len(jb.ins) for jb in jobs)
    n_job_out = sum(len(jb.outs) for jb in jobs)

    def body(*refs):
        q_ref, kc_ref, vc_ref, k0, k1, k2, v0, v1, v2, sink_ref = refs[:10]
        o_ref = refs[10 + n_job_in]
        job_parts = _split_jobs(jobs, refs[10:10 + n_job_in], refs[11 + n_job_in:11 + n_job_in + n_job_out],
                                refs[11 + n_job_in + n_job_out:])
        b = pl.program_id(0)

        if jobs:
            @pl.when(b == 0)
            def _():
                for jb, i_r, o_r, s_r in job_parts:
                    jb.start(i_r, o_r, s_r)

        bias, kcat, vcat = _keys_of_block(b, (kc_ref, vc_ref, k0, k1, k2, v0, v1, v2), group, n_ctx, n)
        q = q_ref[...]

        def scores(h):
            return _scores(_stack_heads(q, h, group), kcat[:, h * HEAD_DIM:(h + 1) * HEAD_DIM], bias, n_ctx)

        heads, s_next = [], scores(0)
        for h in range(n_kv):
            s = s_next
            if h + 1 < n_kv:
                s_next = scores(h + 1)
            e, _, inv = _softmax_terms(s, sink_ref, h, group)
            o = _dot(e.astype(BF16), vcat[:, h * HEAD_DIM:(h + 1) * HEAD_DIM], NN) * inv
            heads += [o[j * BLOCK:(j + 1) * BLOCK, :] for j in range(group)]
        o_ref[...] = jnp.concatenate(heads, axis=1)

        if jobs:
            @pl.when(b == nb - 1)
            def _():
                for jb, i_r, o_r, s_r in job_parts:
                    jb.wait(i_r, o_r, s_r)

    io_alias, i0, o0 = {}, 10, 1
    for jb in jobs:
        io_alias.update({i0 + a_: o0 + b_ for a_, b_ in jb.alias.items()})
        i0, o0 = i0 + len(jb.ins), o0 + len(jb.outs)
    outs = pl.pallas_call(
        body, name=name, grid=(nb,),
        out_shape=(jax.ShapeDtypeStruct((n, hd), F32),) + tuple(s for jb in jobs for s in jb.outs),
        in_specs=_attn_specs(nb, n_ctx, hd, kd) + [pl.BlockSpec(memory_space=pltpu.SMEM)] + [ANY] * n_job_in,
        out_specs=(pl.BlockSpec((BLOCK, hd), lambda b: (b, 0)),) + tuple([ANY] * n_job_out),
        scratch_shapes=[s for jb in jobs for s in jb.sems], input_output_aliases=io_alias,
        compiler_params=_params(("arbitrary",) if jobs else ("parallel",)),
    )(qkv, qkv, qkv, qkv, qkv, qkv, qkv, qkv, qkv, sink, *[v for jb in jobs for v in jb.ins])
    if not jobs:
        return outs[0]
    job_outs, o0 = [], 1
    for jb in jobs:
        job_outs.append(list(outs[o0:o0 + len(jb.outs)]))
        o0 += len(jb.outs)
    return outs[0], job_outs


def _attention_bwd(qkv, sink, ao, d_mg, cpar, n_ctx, hd, kd, name):
    n = qkv.shape[0]
    nb = n // BLOCK
    n_kv = kd // HEAD_DIM
    group = hd // kd
    n_heads = n_kv * group

    def body(q_ref, kc_ref, vc_ref, k0, k1, k2, v0, v1, v2, sink_ref, ao_ref, dmg_ref, cp_ref,
             dq_ref, part_ref, dctx_ref, dsink_ref, dgain_ref):
        b = pl.program_id(0)

        @pl.when(b == 0)
        def _():
            dctx_ref[...] = jnp.zeros_like(dctx_ref)
            dsink_ref[...] = jnp.zeros_like(dsink_ref)
            dgain_ref[...] = jnp.zeros_like(dgain_ref)

        ao_v = ao_ref[...]
        ra = lax.rsqrt(_rowmean(ao_v * ao_v) + EPS)
        an = ao_v * ra
        dmg = dmg_ref[...]
        dgain_ref[0:1, :] += _colsum(dmg * an)
        d_an = dmg * cp_ref[5:6, :]
        d_ao = (ra * (d_an - an * _rowmean(d_an * an))).astype(BF16)

        bias, kcat, vcat = _keys_of_block(b, (kc_ref, vc_ref, k0, k1, k2, v0, v1, v2), group, n_ctx, n)
        q = q_ref[...]
        lane = lax.broadcasted_iota(jnp.int32, (1, 128), 1)
        dsink_row = jnp.zeros((1, 128), F32)

        def first_half(h):
            hs = slice(h * HEAD_DIM, (h + 1) * HEAD_DIM)
            qg, dog = _stack_heads(q, h, group), _stack_heads(d_ao, h, group)
            return qg, dog, _scores(qg, kcat[:, hs], bias, n_ctx), _dot(dog, vcat[:, hs], NT)

        dq_heads, dk, dv = [], [], []
        nxt = first_half(0)
        for h in range(n_kv):
            qg, dog, s, d_p = nxt
            if h + 1 < n_kv:
                nxt = first_half(h + 1)
            p, p_s = _softmax_sink(s, sink_ref, h, group)
            delta = jnp.sum(p * d_p, axis=1, keepdims=True)
            ds = (p * (d_p - delta)).astype(BF16)
            psd = p_s * delta
            for j in range(group):
                val = -jnp.sum(psd[j * BLOCK:(j + 1) * BLOCK, :], axis=0, keepdims=True)
                dsink_row = dsink_row + jnp.where(lane == h * group + j, val, 0.0)
            dq = _dot(ds, kcat[:, h * HEAD_DIM:(h + 1) * HEAD_DIM], NN) * SCALE
            dq_heads += [dq[j * BLOCK:(j + 1) * BLOCK, :] for j in range(group)]
            dk.append(_dot(ds, qg, TN))
            dv.append(_dot(p.astype(BF16), dog, TN))
        dq_ref[...] = jnp.concatenate(dq_heads, axis=1)
        d_kv = jnp.concatenate(dk + dv, axis=1)
        dctx_ref[...] += d_kv[:n_ctx]
        for j in range(3):
            part_ref[j] = d_kv[n_ctx + j * BLOCK:n_ctx + (j + 1) * BLOCK, :]
        dsink_ref[0:1, :] += dsink_row

    assert n_heads <= 128
    out_shape = (jax.ShapeDtypeStruct((n, hd), F32), jax.ShapeDtypeStruct((nb, 3, BLOCK, 2 * kd), F32),
                 jax.ShapeDtypeStruct((n_ctx, 2 * kd), F32), jax.ShapeDtypeStruct((8, 128), F32),
                 jax.ShapeDtypeStruct((8, hd), F32))
    return pl.pallas_call(
        body, name=name, out_shape=out_shape, grid=(nb,),
        in_specs=_attn_specs(nb, n_ctx, hd, kd) + [
            pl.BlockSpec(memory_space=pltpu.SMEM), pl.BlockSpec((BLOCK, hd), lambda b: (b, 0)),
            pl.BlockSpec((BLOCK, hd), lambda b: (b, 1)), pl.BlockSpec((8, hd), lambda b: (0, 0))],
        out_specs=(pl.BlockSpec((BLOCK, hd), lambda b: (b, 0)),
                   pl.BlockSpec((None, 3, BLOCK, 2 * kd), lambda b: (b, 0, 0, 0)),
                   pl.BlockSpec((n_ctx, 2 * kd), lambda b: (0, 0)), pl.BlockSpec((8, 128), lambda b: (0, 0)),
                   pl.BlockSpec((8, hd), lambda b: (0, 0))),
        compiler_params=_params(("arbitrary",)),
    )(qkv, qkv, qkv, qkv, qkv, qkv, qkv, qkv, qkv, sink, ao, d_mg, cpar)


def _halo_specs(tr, n, width, col=0):
    q = tr // 8
    return [pl.BlockSpec((8, width), lambda i: (jnp.maximum(i * q - 1, 0), col)),
            pl.BlockSpec((8, width), lambda i: (jnp.minimum((i + 1) * q, n // 8 - 1), col))]


def _mix_fwd(p, ao, cpar, n_ctx, dc, name):
    n = p.shape[0]
    tr = _pick(n, (256, 128))

    def body(p_ref, pp_ref, pn_ref, ao_ref, cp_ref, o_ref):
        i = pl.program_id(0)
        bg = p_ref[:, 0:dc]
        u = p_ref[:, dc:2 * dc] * p_ref[:, 2 * dc:3 * dc]
        u_before = pp_ref[7:8, dc:2 * dc] * pp_ref[7:8, 2 * dc:3 * dc]
        u_after = pn_ref[0:1, dc:2 * dc] * pn_ref[0:1, 2 * dc:3 * dc]
        loc = lax.broadcasted_iota(jnp.int32, (tr, 1), 0)
        gid = i * tr + loc
        has_prev = (gid != 0) & (gid != n_ctx)
        has_next = (gid != n_ctx - 1) & (gid != n - 1)
        u_m1 = jnp.where(has_prev, jnp.where(loc == 0, u_before, pltpu.roll(u, 1, 0)), 0.0)
        u_p1 = jnp.where(has_next, jnp.where(loc == tr - 1, u_after, pltpu.roll(u, tr - 1, 0)), 0.0)
        cv = u_m1 * cp_ref[0:1, :] + u * cp_ref[1:2, :] + u_p1 * cp_ref[2:3, :] + cp_ref[3:4, :]
        co = bg * cv
        nc = (co * lax.rsqrt(_rowmean(co * co) + EPS)) * cp_ref[4:5, :]
        ao_v = ao_ref[...]
        na = (ao_v * lax.rsqrt(_rowmean(ao_v * ao_v) + EPS)) * cp_ref[5:6, :]
        o_ref[...] = jnp.concatenate([nc, na], axis=1).astype(BF16)

    return pl.pallas_call(
        body, name=name, out_shape=jax.ShapeDtypeStruct((n, 2 * dc), BF16), grid=(n // tr,),
        in_specs=[pl.BlockSpec((tr, 3 * dc), lambda i: (i, 0))] + _halo_specs(tr, n, 3 * dc)
        + [pl.BlockSpec((tr, dc), lambda i: (i, 0)), pl.BlockSpec((8, dc), lambda i: (0, 0))],
        out_specs=pl.BlockSpec((tr, 2 * dc), lambda i: (i, 0)),
        compiler_params=_params(("parallel",)),
    )(p, p, p, ao, cpar)


def _mix_bwd(d_mg, p, cpar, d_q, parts, d_ctx, cs, sn, n_ctx, dc, hd, kd, name):
    n = p.shape[0]
    d_in = 3 * dc + hd + 2 * kd
    tr = BLOCK
    nb = n // tr
    ext = tr + 16
    n_ctx_blocks = n_ctx // BLOCK

    def body(dm_ref, dmp_ref, dmn_ref, p_ref, pp_ref, pn_ref, cp_ref, dq_ref, pa_ref, pb_ref, pc_ref, dctx_ref,
             cs_ref, sn_ref, dp_ref, acc_ref):
        i = pl.program_id(0)

        @pl.when(i == 0)
        def _():
            acc_ref[...] = jnp.zeros_like(acc_ref)

        def cat(before, here, after):
            return jnp.concatenate([before, here, after], axis=0)

        bg = cat(pp_ref[:, 0:dc], p_ref[:, 0:dc], pn_ref[:, 0:dc])
        cg = cat(pp_ref[:, dc:2 * dc], p_ref[:, dc:2 * dc], pn_ref[:, dc:2 * dc])
        hh = cat(pp_ref[:, 2 * dc:3 * dc], p_ref[:, 2 * dc:3 * dc], pn_ref[:, 2 * dc:3 * dc])
        dme = cat(dmp_ref[...], dm_ref[...], dmn_ref[...])
        gid = i * tr - 8 + lax.broadcasted_iota(jnp.int32, (ext, 1), 0)
        inside = (gid >= 0) & (gid < n)
        has_prev = inside & (gid != 0) & (gid != n_ctx)
        has_next = inside & (gid != n_ctx - 1) & (gid != n - 1)
        w0, w1, w2, bias, gain = cp_ref[0:1, :], cp_ref[1:2, :], cp_ref[2:3, :], cp_ref[3:4, :], cp_ref[4:5, :]
        u = jnp.where(inside, cg * hh, 0.0)
        u_m1 = jnp.where(has_prev, pltpu.roll(u, 1, 0), 0.0)
        u_p1 = jnp.where(has_next, pltpu.roll(u, ext - 1, 0), 0.0)
        cv = u_m1 * w0 + u * w1 + u_p1 * w2 + bias
        co = bg * cv
        rc = lax.rsqrt(_rowmean(co * co) + EPS)
        cn = co * rc
        d_cn = dme * gain
        d_co = rc * (d_cn - cn * _rowmean(d_cn * cn))
        d_cv = jnp.where(inside, d_co * bg, 0.0)
        d_bg = d_co * cv
        d_cv_p1 = jnp.where(has_next, pltpu.roll(d_cv, ext - 1, 0), 0.0)
        d_cv_m1 = jnp.where(has_prev, pltpu.roll(d_cv, 1, 0), 0.0)
        d_u = d_cv_p1 * w0 + d_cv * w1 + d_cv_m1 * w2
        mid = slice(8, 8 + tr)
        acc_ref[0:1, :] += _colsum((d_cv * u_m1)[mid])
        acc_ref[1:2, :] += _colsum((d_cv * u)[mid])
        acc_ref[2:3, :] += _colsum((d_cv * u_p1)[mid])
        acc_ref[3:4, :] += _colsum(d_cv[mid])
        acc_ref[4:5, :] += _colsum((dme * cn)[mid])

        d_kv = (jnp.where(i >= 1, pa_ref[...], 0.0) + pb_ref[...] + jnp.where(i + 1 < nb, pc_ref[...], 0.0))
        ctx_rows = dctx_ref[pl.ds(pl.multiple_of(jnp.minimum(i, n_ctx_blocks - 1) * BLOCK, BLOCK), BLOCK), :]
        d_kv = d_kv + jnp.where(i < n_ctx_blocks, ctx_rows, 0.0)
        cs_v, sn_v = cs_ref[...], sn_ref[...]
        d_qu = _rope(dq_ref[...], cs_v, sn_v, -1.0)
        d_ku = _rope(d_kv[:, 0:kd], cs_v, sn_v, -1.0)
        dp_ref[...] = jnp.concatenate(
            [d_bg[mid], (d_u * hh)[mid], (d_u * cg)[mid], d_qu, d_ku, d_kv[:, kd:2 * kd]], axis=1).astype(BF16)

    part = lambda sel, which: pl.BlockSpec((None, None, BLOCK, 2 * kd), lambda i: (sel(i), which, 0, 0))
    return pl.pallas_call(
        body, name=name, out_shape=(jax.ShapeDtypeStruct((n, d_in), BF16), jax.ShapeDtypeStruct((8, dc), F32)),
        grid=(nb,),
        in_specs=[pl.BlockSpec((tr, dc), lambda i: (i, 0))] + _halo_specs(tr, n, dc)
        + [pl.BlockSpec((tr, 3 * dc), lambda i: (i, 0))] + _halo_specs(tr, n, 3 * dc)
        + [pl.BlockSpec((8, dc), lambda i: (0, 0)), pl.BlockSpec((tr, hd), lambda i: (i, 0)),
           part(lambda i: jnp.maximum(i - 1, 0), 2), part(lambda i: i, 1), part(lambda i: jnp.minimum(i + 1, nb - 1), 0),
           pl.BlockSpec((n_ctx, 2 * kd), lambda i: (0, 0)),
           pl.BlockSpec((tr, 128), lambda i: (i, 0)), pl.BlockSpec((tr, 128), lambda i: (i, 0))],
        out_specs=(pl.BlockSpec((tr, d_in), lambda i: (i, 0)), pl.BlockSpec((8, dc), lambda i: (0, 0))),
        compiler_params=_params(("arbitrary",)),
    )(d_mg, d_mg, d_mg, p, p, p, cpar, d_q, parts, parts, parts, d_ctx, cs, sn)


def _loss_bwd(t, gain, target, branch, mod8, gate_k, n_ctx, name):
    n, d = t.shape
    tr = _pick(n_ctx, (256, 128))
    first = n_ctx // tr

    def body(t_ref, g_ref, y_ref, br_ref, gt_ref, dt_ref, loss_ref, dg_ref, dob_ref, dgate_ref):
        i = pl.program_id(0)

        @pl.when(i == 0)
        def _():
            loss_ref[...] = jnp.zeros_like(loss_ref)
            dg_ref[...] = jnp.zeros_like(dg_ref)
            dgate_ref[...] = jnp.zeros_like(dgate_ref)

        @pl.when(i < first)
        def _():
            dt_ref[...] = jnp.zeros_like(dt_ref)
            dob_ref[...] = jnp.zeros_like(dob_ref)

        @pl.when(i >= first)
        def _():
            x = t_ref[...]
            g = g_ref[...]
            r = lax.rsqrt(_rowmean(x * x) + EPS)
            xn = x * r
            err = xn * g - y_ref[...]
            loss_ref[...] += 0.5 * _colsum(_rowmean(err * err))
            dy = err * (1.0 / d)
            dg_ref[0:1, :] += _colsum(dy * xn)
            dxn = dy * g
            d_t = r * (dxn - xn * _rowmean(dxn * xn))
            dt_ref[...] = d_t
            dob_ref[...] = (d_t * gt_ref[0:1, :]).astype(BF16)
            dgate_ref[0:1, :] += _colsum(d_t * br_ref[...].astype(F32))

    row = pl.BlockSpec((tr, d), lambda i: (i, 0))
    acc = pl.BlockSpec((8, d), lambda i: (0, 0))
    return pl.pallas_call(
        body, name=name,
        out_shape=(jax.ShapeDtypeStruct((n, d), F32), jax.ShapeDtypeStruct((8, 128), F32), jax.ShapeDtypeStruct((8, d), F32),
                   jax.ShapeDtypeStruct((n, d), BF16), jax.ShapeDtypeStruct((8, d), F32)),
        grid=(n // tr,),
        in_specs=[row, pl.BlockSpec((1, d), lambda i: (0, 0)), pl.BlockSpec((tr, d), lambda i: (jnp.maximum(i - first, 0), 0)),
                  row, pl.BlockSpec((8, d), lambda i: (0, gate_k))],
        out_specs=(row, pl.BlockSpec((8, 128), lambda i: (0, 0)), acc, row, acc),
        compiler_params=_params(("arbitrary",)),
    )(t, gain, target, branch, mod8)


def _silu16(c16, name):
    def body(c_ref, o_ref):
        v = c_ref[...]
        o_ref[...] = (v * jax.nn.sigmoid(v)).astype(BF16)

    return pl.pallas_call(body, name=name, out_shape=jax.ShapeDtypeStruct(c16.shape, BF16))(c16)


def _cctx_grad(parts, c_ctx, name):
    def body(p_ref, c_ref, o_ref):
        g = _colsum(p_ref[...])
        v = c_ref[...]
        s = jax.nn.sigmoid(v)
        o_ref[...] = g * (s * (1.0 + v * (1.0 - s)))

    return pl.pallas_call(body, name=name, out_shape=jax.ShapeDtypeStruct(c_ctx.shape, F32))(parts, c_ctx)


def _rope_tables(n_ctx, n_tok):
    half = HEAD_DIM // 4
    inv = ROPE_THETA ** (-jnp.arange(0, HEAD_DIM // 2, 2, dtype=F32) / (HEAD_DIM // 2))
    rows = n_tok // GRID_W
    row_pos = jnp.repeat(jnp.arange(rows, dtype=F32), GRID_W)
    col_pos = jnp.tile(jnp.arange(GRID_W, dtype=F32), rows)
    ang_r, ang_c = row_pos[:, None] * inv[None, :], col_pos[:, None] * inv[None, :]
    cos = jnp.concatenate([jnp.cos(ang_r), jnp.cos(ang_r), jnp.cos(ang_c), jnp.cos(ang_c)], axis=1)
    sin = jnp.concatenate([-jnp.sin(ang_r), jnp.sin(ang_r), -jnp.sin(ang_c), jnp.sin(ang_c)], axis=1)
    assert cos.shape[1] == 4 * half == HEAD_DIM
    cos = jnp.concatenate([jnp.ones((n_ctx, HEAD_DIM), F32), cos], axis=0)
    sin = jnp.concatenate([jnp.zeros((n_ctx, HEAD_DIM), F32), sin], axis=0)
    return jnp.tile(cos, (1, 2)), jnp.tile(sin, (1, 2))


def kernel(x, c, ctx, c_ctx, w_ada, b_ada, g_norm1, g_norm2, w_in, conv_w, conv_b, sink, g_out_conv, g_out_attn, w_out, w_mlp1, w_mlp2, g_final, loss_target, m_c_ctx, m_w_ada, m_b_ada, m_g_norm1, m_g_norm2, m_w_in, m_conv_w, m_conv_b, m_sink, m_g_out_conv, m_g_out_attn, m_w_out, m_w_mlp1, m_w_mlp2, m_g_final, v_c_ctx, v_w_ada, v_b_ada, v_g_norm1, v_g_norm2, v_w_in, v_conv_w, v_conv_b, v_sink, v_g_out_conv, v_g_out_attn, v_w_out, v_w_mlp1, v_w_mlp2, v_g_final):
    n_lat, d = x.shape[1], x.shape[2]
    n_ctx = ctx.shape[1]
    n = n_ctx + n_lat
    depth = w_in.shape[0]
    dc = d // 2
    hd, kd = dc, N_KV_HEADS * HEAD_DIM
    n_heads = hd // HEAD_DIM
    d_in = 3 * dc + hd + 2 * kd
    cin, c_ada, r_out, c_ff, r_ff = w_in.shape[2], w_ada.shape[2], w_out.shape[1], w_mlp1.shape[2], w_mlp2.shape[1]
    d_ff = N_DEV * c_ff
    cw = conv_w.shape[2]
    assert d_in == N_DEV * cin and n_ctx % BLOCK == 0 and n_lat % BLOCK == 0 and hd % kd == 0
    dev = 4 * lax.axis_index("x") + 2 * lax.axis_index("y") + lax.axis_index("c")

    c_all, conv_w_all = _all_gather([c, conv_w], "gather_cond")
    conv_w_full = jnp.transpose(conv_w_all, (1, 2, 0, 3)).reshape(depth, 3, dc)
    c16 = jnp.concatenate([c_all.reshape(N_DEV, d), jnp.broadcast_to(c_ctx[None, :], (8, d))], axis=0)
    sc16 = _silu16(c16, "silu_cond")

    b_ada_loc = lax.dynamic_index_in_dim(b_ada.reshape(depth, N_DEV, c_ada), dev, axis=1, keepdims=False)
    tn_ada = _pick(c_ada, FEAT)

    def add_bias(acc, i, j, extra, outs):
        outs[0][...] = acc + extra[0][...]

    mod_loc = []
    for l in range(depth):
        mod_loc.append(_matmul(
            sc16, w_ada, dims="nn", shape=(16, c_ada, d), tiles=(16, tn_ada, _pick(d, KDIM)),
            b_spec=_w_spec("nat", l, None, "nn", tn_ada, _pick(d, KDIM)), epilogue=add_bias,
            extras=[(b_ada_loc[l][None, :], pl.BlockSpec((1, tn_ada), lambda i, j, k: (0, j)))], name=f"ada_fwd{l}"))
    (mod_all,) = _all_gather([jnp.stack(mod_loc)], "gather_mod")
    mod_full = jnp.transpose(mod_all, (1, 2, 0, 3)).reshape(depth, 16, N_MOD * d)
    mod_mine = lax.dynamic_index_in_dim(mod_full, dev, axis=1, keepdims=True)
    mod8 = jnp.concatenate([mod_mine, mod_full[:, 8:9], jnp.zeros((depth, 6, N_MOD * d), F32)], axis=1)

    w_in_b = _to_bf16(w_in.reshape(depth * d, cin), "cast_w_in").reshape(depth, d, cin)
    w_out_b = _to_bf16(w_out.reshape(depth * r_out, d), "cast_w_out").reshape(depth, r_out, d)
    w1_b = _to_bf16(w_mlp1.reshape(depth * d, c_ff), "cast_w_mlp1").reshape(depth, d, c_ff)
    w2_b = _to_bf16(w_mlp2.reshape(depth * r_ff, d), "cast_w_mlp2").reshape(depth, r_ff, d)
    def full_in(g):
        return jnp.transpose(g, (1, 0, 2)).reshape(d, d_in)

    w_in_full, w_out_full, g_w1, w2_full = [], [], [], []
    pend_w2 = None

    t = jnp.concatenate([ctx[0], x[0]], axis=0)
    cs, sn = _rope_tables(n_ctx, n_lat)
    tm = _pick(n, TOK)
    tk_d = _pick(d, KDIM)
    tm_res = _pick(n, (1056, 768, 256, 128))
    tn_in = _pick(math.gcd(3 * dc, hd + 2 * kd), (512, 256, 128))
    conv_tiles = 3 * dc // tn_in
    bounds = [(0, hd, "q"), (hd, hd + kd, "k"), (hd + kd, hd + 2 * kd, "v")]
    qkv_tiles = []
    for jj in range((hd + 2 * kd) // tn_in):
        lo, hi = jj * tn_in, (jj + 1) * tn_in
        segs = [(max(lo, a) - lo, min(hi, b) - lo, kind) for a, b, kind in bounds if a < hi and b > lo]
        assert all((e - s) % 128 == 0 for s, e, _ in segs)
        qkv_tiles.append(segs)

    def in_proj_epilogue(acc, i, j, extra, outs):
        @pl.when(j < conv_tiles)
        def _():
            outs[0][...] = acc

        for jj, segs in enumerate(qkv_tiles):
            @pl.when(j == conv_tiles + jj)
            def _(segs=segs):
                cs_v, sn_v = extra[0][...], extra[1][...]
                pieces = []
                for s0, s1, kind in segs:
                    v = acc[:, s0:s1]
                    if kind == "q":
                        v = _rope(v, cs_v, sn_v, 1.0) * SCALE
                    elif kind == "k":
                        v = _rope(v, cs_v, sn_v, 1.0)
                    pieces.append(v)
                outs[1][...] = (pieces[0] if len(pieces) == 1 else jnp.concatenate(pieces, axis=1)).astype(BF16)

    def in_proj_specs(rows):
        return dict(
            out_specs=(pl.BlockSpec((rows, tn_in), lambda i, j, k: (i, jnp.minimum(j, conv_tiles - 1))),
                       pl.BlockSpec((rows, tn_in), lambda i, j, k: (i, jnp.maximum(j - conv_tiles, 0)))),
            out_shapes=(jax.ShapeDtypeStruct((n, 3 * dc), F32), jax.ShapeDtypeStruct((n, hd + 2 * kd), BF16)),
            out_dtypes=(F32, BF16), epilogue=in_proj_epilogue,
            extras=[(cs, pl.BlockSpec((rows, 128), lambda i, j, k: (i, 0))), (sn, pl.BlockSpec((rows, 128), lambda i, j, k: (i, 0)))])

    def resid_epilogue(tile_rows):
        def epi(acc, i, j, extra, outs):
            is_ctx = _row_ids(i, tile_rows) < n_ctx
            outs[0][...] = extra[0][...] + _sel(is_ctx, extra[1]) * acc
            outs[1][...] = acc.astype(BF16)
        return epi

    def sq_relu_epilogue(acc, i, j, extra, outs):
        outs[0][...] = acc.astype(BF16)
        rl = jnp.maximum(acc, 0.0)
        outs[1][...] = (rl * rl).astype(BF16)

    def d_sq_relu_epilogue(acc, i, j, extra, outs):
        outs[0][...] = (acc * (2.0 * jnp.maximum(extra[0][...].astype(F32), 0.0))).astype(BF16)

    saved = []
    for l in range(depth):
        cpar = jnp.concatenate([conv_w_full[l], conv_b[l][None], g_out_conv[l][None], g_out_attn[l][None],
                                jnp.zeros((2, dc), F32)], axis=0)
        more = l + 1 < depth
        if l == 0:
            h, (gath0,) = _norm_mod(t, g_norm1[l][None], mod8[l], 0, n_ctx, f"norm1_{l}",
                                    jobs=[_job_gather_ici([w_in_b, w_out_b], 0)])
            (gath0,) = _comm_call([_job_gather_d2d(gath0)], "gather_w0_d2d")
            w_in_full.append(full_in(gath0[0]))
            w_out_full.append(gath0[1].reshape(d, d))
        else:
            h = _norm_mod(t, g_norm1[l][None], mod8[l], 0, n_ctx, f"norm1_{l}")
        jobs = ([_job_gather_d2d(pend_w2)] if pend_w2 is not None else []) + (
            [_job_gather_ici([w_in_b, w_out_b], l + 1)] if more else [])
        res = _matmul(h, w_in_full[l], dims="nn", shape=(n, d_in, d), tiles=(tm, tn_in, tk_d), jobs=jobs, name=f"in_proj{l}",
                      **in_proj_specs(tm))
        if jobs:
            (p, qkv), job_outs = res
            if pend_w2 is not None:
                w2_full.append(job_outs[0][0].reshape(d_ff, d))
            pend_io = job_outs[-1] if more else None
        else:
            p, qkv = res
        if l == 0:
            ao, (pend_mlp0,) = _attention(qkv, sink[l], n_ctx, hd, kd, f"attn{l}", jobs=[_job_gather_ici([w1_b, w2_b], 0)])
        else:
            ao = _attention(qkv, sink[l], n_ctx, hd, kd, f"attn{l}")
        mg = _mix_fwd(p, ao, cpar, n_ctx, dc, f"mix{l}")
        tn = _pick(d, (512, 256, 128))
        jobs = ([_job_gather_d2d(pend_io)] if more else []) + ([_job_gather_d2d(pend_mlp0)] if l == 0 else [])
        res = _matmul(
            mg, w_out_full[l], dims="nn", shape=(n, d, d), tiles=(tm, tn, tk_d),
            out_dtypes=(F32, BF16), epilogue=resid_epilogue(tm),
            extras=[(t, pl.BlockSpec((tm, tn), lambda i, j, k: (i, j))),
                    (mod8[l], pl.BlockSpec((8, tn), lambda i, j, k, tn=tn: (0, 2 * (d // tn) + j)))],
            jobs=jobs, name=f"out_proj{l}")
        if jobs:
            (t2, z), job_outs = res
            if more:
                w_in_full.append(full_in(job_outs[0][0]))
                w_out_full.append(job_outs[0][1].reshape(d, d))
            if l == 0:
                g_w1.append(job_outs[-1][0])
                w2_full.append(job_outs[-1][1].reshape(d_ff, d))
        else:
            t2, z = res
        h2 = _norm_mod(t2, g_norm2[l][None], mod8[l], 3, n_ctx, f"norm2_{l}")
        tn = _pick(c_ff, FEAT)
        res = _matmul(h2, g_w1[l], dims="nn", shape=(n, d_ff, d), tiles=(tm, tn, tk_d),
                      b_spec=_w_spec("cols", None, c_ff, "nn", tn, tk_d), out_dtypes=(BF16, BF16),
                      epilogue=sq_relu_epilogue, jobs=[_job_gather_ici([w1_b], l + 1)] if more else [], name=f"mlp_up{l}")
        if more:
            (a, s), ((pend_w1,),) = res
        else:
            a, s = res
        tn = _pick(d, FEAT)
        tk = _pick(d_ff, (2048, 1024, 512, 256, 128))
        res = _matmul(
            s, w2_full[l], dims="nn", shape=(n, d, d_ff), tiles=(tm_res, tn, tk),
            out_dtypes=(F32, BF16), epilogue=resid_epilogue(tm_res),
            extras=[(t2, pl.BlockSpec((tm_res, tn), lambda i, j, k: (i, j))),
                    (mod8[l], pl.BlockSpec((8, tn), lambda i, j, k, tn=tn: (0, 5 * (d // tn) + j)))],
            jobs=[_job_gather_ici([w2_b], l + 1), _job_gather_d2d([pend_w1])] if more else [], name=f"mlp_down{l}")
        if more:
            (t3, o), (pend_w2, (g_w1_next,)) = res
            g_w1.append(g_w1_next)
        else:
            (t3, o), pend_w2 = res, None
        saved.append((t, h, p, qkv, ao, mg, z, t2, h2, a, s, o, cpar))
        t = t3

    d_t, loss_tile, dg_final, dob, dgate2 = _loss_bwd(t, g_final[None], loss_target[0], saved[depth - 1][11], mod8[depth - 1], 5,
                                                      n_ctx, "loss")
    loss = lax.psum(loss_tile[0, 0], ("x", "y", "c"))

    buf_in = lax.empty((N_CHIP, depth, cin, d), BF16)
    buf_out = lax.empty((N_CHIP, depth, r_out, d), BF16)
    buf_w1 = lax.empty((N_CHIP, depth, d, c_ff), BF16)
    buf_w2 = lax.empty((N_CHIP, depth, r_ff, d), BF16)
    pend_in, pend_layer = None, None
    tkn = _pick(n, (2816,) + TOK)
    small = [None] * depth
    for l in reversed(range(depth)):
        t_in, h, p, qkv, ao, mg, z, t2, h2, a, s, o, cpar = saved[l]
        tm_g = _pick(d_ff, FEAT)
        tn = _pick(d, FEAT)
        gw2 = _matmul(s, dob, dims="tn", shape=(d_ff, d, n), tiles=(tm_g, tn, tkn), name=f"mlp_down_dw{l}")
        gw2 = gw2.reshape(N_DEV, r_ff, d)
        tn = _pick(d_ff, FEAT)
        jobs = [_job_scatter_d2d([gw2])] + ([_job_scatter_ici([pend_in], [buf_in], pend_layer)] if pend_in is not None else [])
        da, job_outs = _matmul(dob, w2_full[l], dims="nt", shape=(n, d_ff, d), tiles=(tm, tn, tk_d),
                               out_dtypes=(BF16,), epilogue=d_sq_relu_epilogue,
                               extras=[(a, pl.BlockSpec((tm, tn), lambda i, j, k: (i, j)))], jobs=jobs, name=f"mlp_down_dx{l}")
        if pend_in is not None:
            (buf_in,) = job_outs[1]
        pair_w2 = _pair_add(gw2, job_outs[0][0], f"pair_w2_{l}")
        tm_g = _pick(d, FEAT)
        tn = _pick(c_ff, FEAT)
        gw1 = _matmul(h2, da, dims="tn", shape=(d, d_ff, n), tiles=(tm_g, tn, tkn),
                      out_specs=(_g_spec("cols", None, c_ff, tm_g, tn),),
                      out_shapes=(jax.ShapeDtypeStruct((N_DEV, d, c_ff), F32),), name=f"mlp_up_dw{l}")
        tn = _pick(d, FEAT)
        tk = _pick(c_ff, (1024, 512, 256, 128))
        dh2, job_outs = _matmul(da, g_w1[l], dims="nt", shape=(n, d, d_ff), tiles=(tm, tn, tk),
                                b_spec=_w_spec("cols", None, c_ff, "nt", tn, tk),
                                jobs=[_job_scatter_d2d([gw1]), _job_scatter_ici([pair_w2], [buf_w2], l)], name=f"mlp_up_dx{l}")
        (buf_w2,) = job_outs[1]
        pair_w1 = _pair_add(gw1, job_outs[0][0], f"pair_w1_{l}")
        d_t2, dss2, dgn2, dzb, dgate1 = _norm_mod_bwd(dh2, t2, d_t, g_norm2[l][None], mod8[l], 3, n_ctx, f"norm2_bwd{l}",
                                                      gated=(z, mod8[l], 2))
        tm_g = _pick(d, FEAT)
        tn = _pick(d, FEAT)
        gout = _matmul(mg, dzb, dims="tn", shape=(d, d, n), tiles=(tm_g, tn, tkn), name=f"out_proj_dw{l}")
        gout = gout.reshape(N_DEV, r_out, d)
        tn = _pick(d, FEAT)
        d_mg, job_outs = _matmul(dzb, w_out_full[l], dims="nt", shape=(n, d, d), tiles=(tm, tn, tk_d),
                                 jobs=[_job_scatter_d2d([gout])], name=f"out_proj_dx{l}")
        pair_out = _pair_add(gout, job_outs[0][0], f"pair_out_{l}")
        d_q, parts, d_kv_ctx, d_sink, d_goa = _attention_bwd(qkv, sink[l], ao, d_mg, cpar, n_ctx, hd, kd, f"attn_bwd{l}")
        d_p, conv_acc = _mix_bwd(d_mg, p, cpar, d_q, parts, d_kv_ctx, cs, sn, n_ctx, dc, hd, kd, f"mix_bwd{l}")
        tm_g = _pick(d_in, (768, 512, 256, 128))
        tn = _pick(d, FEAT)
        gin = _matmul(d_p, h, dims="tn", shape=(d_in, d, n), tiles=(tm_g, tn, tkn), name=f"in_proj_dw{l}")
        gin = gin.reshape(N_DEV, cin, d)
        tn = _pick(d, FEAT)
        tk = _pick(d_in, (1536, 768, 512, 256, 128))
        dh, job_outs = _matmul(d_p, w_in_full[l], dims="nt", shape=(n, d, d_in), tiles=(tm, tn, tk),
                               jobs=[_job_scatter_d2d([gin]), _job_scatter_ici([pair_w1, pair_out], [buf_w1, buf_out], l)],
                               name=f"in_proj_dx{l}")
        buf_w1, buf_out = job_outs[1]
        pend_in, pend_layer = _pair_add(gin, job_outs[0][0], f"pair_in_{l}"), l
        d_mod2_tail = [dgate1[0:2], dss2[0:2], dgate2[0:2]]
        if l > 0:
            d_t, dss1, dgn1, dob, dgate2 = _norm_mod_bwd(dh, t_in, d_t2, g_norm1[l][None], mod8[l], 0, n_ctx, f"norm1_bwd{l}",
                                                         gated=(saved[l - 1][11], mod8[l - 1], 5))
        else:
            (grad_x, dss1, dgn1), ((buf_in,),) = _norm_mod_bwd(
                dh, t_in, d_t2, g_norm1[l][None], mod8[l], 0, n_ctx, f"norm1_bwd{l}", latent_only=True,
                jobs=[_job_scatter_ici([pend_in], [buf_in], pend_layer)])
        d_mod2 = jnp.concatenate([dss1[0:2]] + d_mod2_tail, axis=1)
        small[l] = (d_mod2, dgn1[0], dgn2[0], conv_acc, d_sink[0, 0:n_heads], d_goa[0])
    grad_x = grad_x[None]

    def pack(l):
        d_mod2, dgn1, dgn2, conv_acc, d_sink, d_goa = small[l]
        row0 = [d_mod2[0], dgn1, dgn2, conv_acc[3], d_sink, conv_acc[4], d_goa, conv_acc[0:3].reshape(-1)]
        row1 = [d_mod2[1]] + [jnp.zeros_like(v) for v in row0[1:]]
        return jnp.stack([jnp.concatenate(row0), jnp.concatenate(row1)])
    per_layer = N_MOD * d + 2 * d + dc + n_heads + 2 * dc + 3 * dc
    packed = jnp.concatenate([pack(l) for l in range(depth)] +
                             [jnp.stack([dg_final[0], jnp.zeros((d,), F32)])], axis=1)
    f_tot = depth * per_layer + d
    f_pad = -f_tot % 1024
    packed = jnp.pad(packed, ((0, 0), (0, f_pad)))
    (small_all,) = _all_gather([packed], "gather_small")
    small_parts = small_all.reshape(2 * N_DEV, 1, f_tot + f_pad)

    def section(arr, l, off, size):
        return lax.slice_in_dim(arr, l * per_layer + off, l * per_layer + off + size, axis=-1)

    offs = {}
    o_ = 0
    for nm_, sz in (("mod", N_MOD * d), ("gn1", d), ("gn2", d), ("cb", dc), ("sink", n_heads), ("goc", dc), ("goa", dc), ("cw", 3 * dc)):
        offs[nm_] = (o_, sz)
        o_ += sz

    def packw(b_ada_, gn1_, gn2_, cb_, sk_, goc_, goa_, gf_):
        rows = []
        for l in range(depth):
            rows += [b_ada_[l], gn1_[l], gn2_[l], cb_[l], sk_[l], goc_[l], goa_[l], jnp.zeros((3 * dc,), F32)]
        return jnp.pad(jnp.concatenate(rows + [gf_]), (0, f_pad))[None]
    pw = packw(b_ada, g_norm1, g_norm2, conv_b, sink, g_out_conv, g_out_attn, g_final)
    pm = packw(m_b_ada, m_g_norm1, m_g_norm2, m_conv_b, m_sink, m_g_out_conv, m_g_out_attn, m_g_final)
    pv = packw(v_b_ada, v_g_norm1, v_g_norm2, v_conv_b, v_sink, v_g_out_conv, v_g_out_attn, v_g_final)
    sg, sd, sm, sv = _adamw(pw, pm, pv, small_parts, "adamw_small")

    def unpack(arr):
        arr = arr[0]
        out = {}
        for nm_ in ("mod", "gn1", "gn2", "cb", "sink", "goc", "goa", "cw"):
            off, size = offs[nm_]
            out[nm_] = jnp.stack([section(arr, l, off, size) for l in range(depth)])
        out["gf"] = arr[depth * per_layer:depth * per_layer + d]
        return out
    ug, ud, um, uv = unpack(sg), unpack(sd), unpack(sm), unpack(sv)

    cw_grad_full = ug["cw"].reshape(depth, 3, N_DEV, cw)
    cw_grad = lax.dynamic_index_in_dim(cw_grad_full, dev, axis=2, keepdims=False).reshape(1, depth * 3, cw)
    cwg, cwd, cwm, cwv = _adamw(conv_w.reshape(depth * 3, cw), m_conv_w.reshape(depth * 3, cw),
                                v_conv_w.reshape(depth * 3, cw), cw_grad, "adamw_conv_w")
    cw_shape = conv_w.shape

    mod_rows = small_all[:, :, :depth * per_layer].reshape(N_DEV, 2, depth, per_layer)[:, :, :, :N_MOD * d]
    dm16 = jnp.concatenate([mod_rows[:, 0], mod_rows[:, 1]], axis=0)
    dm16 = jnp.transpose(dm16, (1, 0, 2)).reshape(depth, 16, N_DEV, c_ada)
    dm16_loc = lax.dynamic_index_in_dim(dm16, dev, axis=2, keepdims=False)
    gb_ada = lax.empty((depth, d, c_ada), F32)
    dsc_parts = []
    tm_g = _pick(d, FEAT)
    for l in range(depth):
        gb_ada = _matmul(sc16, dm16_loc[l], dims="tn", shape=(d, c_ada, 16), tiles=(tm_g, tn_ada, 16),
                         out_specs=(_g_spec("nat", l, None, tm_g, tn_ada),), out_shapes=(jax.ShapeDtypeStruct(gb_ada.shape, F32),),
                         alias=gb_ada, name=f"ada_dw{l}")
        tn = _pick(d, FEAT)
        tk = _pick(c_ada, KDIM)
        dsc_parts.append(_matmul(dm16_loc[l], w_ada, dims="nt", shape=(16, d, c_ada), tiles=(16, tn, tk),
                                 b_spec=_w_spec("nat", l, None, "nt", tn, tk), name=f"ada_dx{l}"))
    (dsc_all,) = _all_gather([jnp.stack(dsc_parts)[:, 8:16]], "gather_dcond")
    g_cctx = _cctx_grad(dsc_all.reshape(N_DEV * depth * 8, d), c_ctx[None], "c_ctx_grad")
    ccg, ccd, ccm, ccv = _adamw(c_ctx[None], m_c_ctx[None], v_c_ctx[None], g_cctx[None], "adamw_c_ctx")
    adg, add, adm, adv = _adamw(w_ada.reshape(depth * d, c_ada), m_w_ada.reshape(depth * d, c_ada),
                                v_w_ada.reshape(depth * d, c_ada), gb_ada.reshape(1, depth * d, c_ada), "adamw_w_ada")

    parts4 =[jnp.swapaxes(buf_in, 2, 3), buf_out, buf_w1, buf_w2]
    big = []
    for k, (w_, m_, v_) in enumerate(((w_in, m_w_in, v_w_in), (w_out, m_w_out, v_w_out),
                                      (w_mlp1, m_w_mlp1, v_w_mlp1), (w_mlp2, m_w_mlp2, v_w_mlp2))):
        r2, c2 = w_.shape[0] * w_.shape[1], w_.shape[2]
        res = _adamw(w_.reshape(r2, c2), m_.reshape(r2, c2), v_.reshape(r2, c2), parts4[k].reshape(N_CHIP, r2, c2),
                     f"adamw_big{k}")
        big.append([a_.reshape(w_.shape) for a_ in res])

    def leaf(i):
        return (
            (ccg, ccd, ccm, ccv)[i][0], (adg, add, adm, adv)[i].reshape(w_ada.shape),
            (ug, ud, um, uv)[i]["mod"], (ug, ud, um, uv)[i]["gn1"], (ug, ud, um, uv)[i]["gn2"], big[0][i],
            (cwg, cwd, cwm, cwv)[i].reshape(cw_shape), (ug, ud, um, uv)[i]["cb"], (ug, ud, um, uv)[i]["sink"],
            (ug, ud, um, uv)[i]["goc"], (ug, ud, um, uv)[i]["goa"], big[1][i], big[2][i], big[3][i], (ug, ud, um, uv)[i]["gf"])

    return (loss, grad_x) + leaf(0) + leaf(1) + leaf(2) + leaf(3)
```

```python
import functools
import math

import jax
import jax.numpy as jnp
from jax import lax
from jax.experimental import pallas as pl
from jax.experimental.pallas import tpu as pltpu

HEAD_DIM = 64
N_KV_HEADS = 4
BLOCK = 128
GRID_W = 64
ROPE_THETA = 10000.0
EPS = 1e-6
N_MOD = 6
SCALE = HEAD_DIM ** -0.5
NEG_INF = -1e30
ADAM_LR = 0.001
ADAM_B1 = 0.9
ADAM_B2 = 0.999
ADAM_EPS = 1e-08
ADAM_WD = 0.01
ADAM_STEP = 10
N_DEV = 8
N_CHIP = 4
VMEM_LIMIT_BYTES = 48 * 1024 * 1024
MESH = pl.DeviceIdType.MESH
BF16 = jnp.bfloat16
F32 = jnp.float32
ANY = pl.BlockSpec(memory_space=pl.ANY)


def _pick(dim, prefs):
    for p in prefs:
        if p <= dim and dim % p == 0:
            return p
    return dim


def _params(sem):
    return pltpu.CompilerParams(dimension_semantics=sem, vmem_limit_bytes=VMEM_LIMIT_BYTES)


def _row_ids(i, tr):
    return i * tr + lax.broadcasted_iota(jnp.int32, (tr, 1), 0)


def _colsum(v):
    return jnp.sum(v, axis=0, keepdims=True)


def _rowmean(v):
    return jnp.mean(v, axis=1, keepdims=True)


def _all_gather(xs, name):
    na = len(xs)

    def body(*refs):
        x_refs, o_refs = refs[:na], refs[na:2 * na]
        send_sems, recv_sems, local_sems = refs[2 * na:]
        x, y, c = lax.axis_index("x"), lax.axis_index("y"), lax.axis_index("c")
        me, sibling = (x, y, c), (x, y, 1 - c)
        chips = [(1 - x, y), (x, 1 - y), (1 - x, 1 - y)]

        def slot(a, px, py, pc):
            return o_refs[a].at[4 * px + 2 * py + pc]

        def copy(a, k, block, to, src=None):
            return pltpu.make_async_remote_copy(
                src_ref=slot(a, *block) if src is None else src, dst_ref=slot(a, *block),
                send_sem=send_sems.at[a, k], recv_sem=recv_sems.at[a, k], device_id=to, device_id_type=MESH)

        mine = [pltpu.make_async_copy(x_refs[a], slot(a, *me), local_sems.at[a]) for a in range(na)]
        for cp in mine:
            cp.start()
        first = []
        for a in range(na):
            first.append(copy(a, 0, me, sibling, src=x_refs[a]))
            first += [copy(a, 1 + j, me, (*chip, c), src=x_refs[a]) for j, chip in enumerate(chips)]
        for cp in first:
            cp.start()
        passed = []
        for j, chip in enumerate(chips):
            for a in range(na):
                copy(a, 1 + j, (*chip, c), me).wait_recv()
                fwd = copy(a, 4 + j, (*chip, c), sibling)
                fwd.start()
                passed.append(fwd)
        for a in range(na):
            copy(a, 0, sibling, me).wait_recv()
            for j, chip in enumerate(chips):
                copy(a, 4 + j, (*chip, 1 - c), me).wait_recv()
        for cp in first + passed:
            cp.wait_send()
        for cp in mine:
            cp.wait()

    outs = pl.pallas_call(
        body, name=name,
        out_shape=tuple(jax.ShapeDtypeStruct((N_DEV,) + x.shape, x.dtype) for x in xs),
        in_specs=[ANY] * na, out_specs=tuple([ANY] * na),
        scratch_shapes=[pltpu.SemaphoreType.DMA((na, 7)), pltpu.SemaphoreType.DMA((na, 7)),
                        pltpu.SemaphoreType.DMA((na,))],
    )(*xs)
    return list(outs)


class _Job:
    def __init__(self, ins, outs, alias, sems, copies):
        self.ins, self.outs, self.alias, self.sems, self._copies = ins, outs, alias, sems, copies

    def start(self, in_refs, out_refs, sems):
        local, sends, _ = self._copies(in_refs, out_refs, sems)
        for make in local + sends:
            make().start()

    def wait(self, in_refs, out_refs, sems):
        local, sends, arrivals = self._copies(in_refs, out_refs, sems)
        for make in arrivals:
            make().wait_recv()
        for make in sends:
            make().wait_send()
        for make in local:
            make().wait()


def _other_chips():
    x, y = lax.axis_index("x"), lax.axis_index("y")
    return [(1 - x, y), (x, 1 - y), (1 - x, 1 - y)]


def _remote(src, dst, send_sem, recv_sem, to):
    return functools.partial(pltpu.make_async_remote_copy, src_ref=src, dst_ref=dst, send_sem=send_sem, recv_sem=recv_sem,
                             device_id=to, device_id_type=MESH)


def _local(src, dst, sem):
    return functools.partial(pltpu.make_async_copy, src, dst, sem)


def _job_gather_ici(xs, layer):
    na = len(xs)

    def copies(in_refs, out_refs, sems):
        send, recv, loc = sems
        x, y, c = lax.axis_index("x"), lax.axis_index("y"), lax.axis_index("c")
        me = 4 * x + 2 * y + c
        local, sends, arrivals = [], [], []
        for a in range(na):
            src = in_refs[a].at[layer]
            local.append(_local(src, out_refs[a].at[me], loc.at[a]))
            for j, (px, py) in enumerate(_other_chips()):
                sends.append(_remote(src, out_refs[a].at[me], send.at[a, j], recv.at[a, j], (px, py, c)))
                arrivals.append(_remote(src, out_refs[a].at[4 * px + 2 * py + c], send.at[a, j], recv.at[a, j], (px, py, c)))
        return local, sends, arrivals

    outs = [jax.ShapeDtypeStruct((N_DEV,) + x.shape[1:], x.dtype) for x in xs]
    sems = [pltpu.SemaphoreType.DMA((na, 3)), pltpu.SemaphoreType.DMA((na, 3)), pltpu.SemaphoreType.DMA((na,))]
    return _Job(list(xs), outs, {}, sems, copies)


def _job_gather_d2d(gs):
    na = len(gs)

    def copies(in_refs, out_refs, sems):
        send, recv = sems
        x, y, c = lax.axis_index("x"), lax.axis_index("y"), lax.axis_index("c")
        sends, arrivals = [], []
        for a in range(na):
            for k in range(N_CHIP):
                mine, theirs = 2 * k + c, 2 * k + (1 - c)
                sends.append(_remote(in_refs[a].at[mine], out_refs[a].at[mine], send.at[a, k], recv.at[a, k], (x, y, 1 - c)))
                arrivals.append(_remote(in_refs[a].at[theirs], out_refs[a].at[theirs], send.at[a, k], recv.at[a, k], (x, y, 1 - c)))
        return [], sends, arrivals

    outs = [jax.ShapeDtypeStruct(g.shape, g.dtype) for g in gs]
    sems = [pltpu.SemaphoreType.DMA((na, N_CHIP)), pltpu.SemaphoreType.DMA((na, N_CHIP))]
    return _Job(list(gs), outs, {a: a for a in range(na)}, sems, copies)


def _job_scatter_d2d(gs):
    na = len(gs)

    def copies(in_refs, out_refs, sems):
        send, recv = sems
        x, y, c = lax.axis_index("x"), lax.axis_index("y"), lax.axis_index("c")
        sends, arrivals = [], []
        for a in range(na):
            for k in range(N_CHIP):
                cp = _remote(in_refs[a].at[2 * k + (1 - c)], out_refs[a].at[k], send.at[a, k], recv.at[a, k], (x, y, 1 - c))
                sends.append(cp)
                arrivals.append(cp)
        return [], sends, arrivals

    outs = [jax.ShapeDtypeStruct((N_CHIP,) + g.shape[1:], g.dtype) for g in gs]
    sems = [pltpu.SemaphoreType.DMA((na, N_CHIP)), pltpu.SemaphoreType.DMA((na, N_CHIP))]
    return _Job(list(gs), outs, {}, sems, copies)


def _job_scatter_ici(pairs, bufs, layer):
    na = len(pairs)

    def copies(in_refs, out_refs, sems):
        send, recv, loc = sems
        x, y, c = lax.axis_index("x"), lax.axis_index("y"), lax.axis_index("c")
        my_chip = 2 * x + y
        local, sends, arrivals = [], [], []
        for a in range(na):
            local.append(_local(in_refs[a].at[my_chip], out_refs[a].at[my_chip, layer], loc.at[a]))
            for j, (px, py) in enumerate(_other_chips()):
                src = in_refs[a].at[2 * px + py]
                sends.append(_remote(src, out_refs[a].at[my_chip, layer], send.at[a, j], recv.at[a, j], (px, py, c)))
                arrivals.append(_remote(src, out_refs[a].at[2 * px + py, layer], send.at[a, j], recv.at[a, j], (px, py, c)))
        return local, sends, arrivals

    outs = [jax.ShapeDtypeStruct(b.shape, b.dtype) for b in bufs]
    sems = [pltpu.SemaphoreType.DMA((na, 3)), pltpu.SemaphoreType.DMA((na, 3)), pltpu.SemaphoreType.DMA((na,))]
    return _Job(list(pairs) + list(bufs), outs, {na + a: a for a in range(na)}, sems, copies)


def _split_jobs(jobs, in_refs, out_refs, sem_refs):
    out, i0, o0, s0 = [], 0, 0, 0
    for jb in jobs:
        out.append((jb, in_refs[i0:i0 + len(jb.ins)], out_refs[o0:o0 + len(jb.outs)], sem_refs[s0:s0 + len(jb.sems)]))
        i0, o0, s0 = i0 + len(jb.ins), o0 + len(jb.outs), s0 + len(jb.sems)
    return out


def _carry(body, jobs, n_in, n_out, n_steps):
    if not jobs:
        return body
    n_ji = sum(len(jb.ins) for jb in jobs)
    n_jo = sum(len(jb.outs) for jb in jobs)

    def wrapped(*refs):
        o0 = n_in + n_ji
        parts = _split_jobs(jobs, refs[n_in:o0], refs[o0 + n_out:o0 + n_out + n_jo], refs[o0 + n_out + n_jo:])

        @pl.when(pl.program_id(0) == 0)
        def _():
            for jb, i_r, o_r, s_r in parts:
                jb.start(i_r, o_r, s_r)

        body(*refs[:n_in], *refs[o0:o0 + n_out])

        @pl.when(pl.program_id(0) == n_steps - 1)
        def _():
            for jb, i_r, o_r, s_r in parts:
                jb.wait(i_r, o_r, s_r)

    return wrapped


def _carry_args(jobs, n_in, n_out):
    io_alias, i0, o0 = {}, n_in, n_out
    for jb in jobs:
        io_alias.update({i0 + a: o0 + b for a, b in jb.alias.items()})
        i0, o0 = i0 + len(jb.ins), o0 + len(jb.outs)
    ins = [v for jb in jobs for v in jb.ins]
    outs = [s for jb in jobs for s in jb.outs]
    return ins, [ANY] * len(ins), outs, [ANY] * len(outs), [s for jb in jobs for s in jb.sems], io_alias


def _carry_results(jobs, outs, n_out):
    res, o0 = [], n_out
    for jb in jobs:
        res.append(list(outs[o0:o0 + len(jb.outs)]))
        o0 += len(jb.outs)
    return res


def _comm_call(jobs, name):
    n_in = sum(len(jb.ins) for jb in jobs)
    n_out = sum(len(jb.outs) for jb in jobs)

    def body(*refs):
        parts = _split_jobs(jobs, refs[:n_in], refs[n_in:n_in + n_out], refs[n_in + n_out:])
        for jb, i_r, o_r, s_r in parts:
            jb.start(i_r, o_r, s_r)
        for jb, i_r, o_r, s_r in parts:
            jb.wait(i_r, o_r, s_r)

    io_alias, i0, o0 = {}, 0, 0
    for jb in jobs:
        io_alias.update({i0 + a: o0 + b for a, b in jb.alias.items()})
        i0, o0 = i0 + len(jb.ins), o0 + len(jb.outs)
    outs = pl.pallas_call(
        body, name=name, out_shape=tuple(s for jb in jobs for s in jb.outs),
        in_specs=[ANY] * n_in, out_specs=tuple([ANY] * n_out),
        scratch_shapes=[s for jb in jobs for s in jb.sems], input_output_aliases=io_alias,
    )(*[v for jb in jobs for v in jb.ins])
    res, o0 = [], 0
    for jb in jobs:
        res.append(list(outs[o0:o0 + len(jb.outs)]))
        o0 += len(jb.outs)
    return res


def _to_bf16(x2d, name):
    r, c = x2d.shape
    tr = _pick(r, (512, 256, 128, 64, 32, 16))

    def body(x_ref, o_ref):
        o_ref[...] = x_ref[...].astype(BF16)

    return pl.pallas_call(
        body, name=name, out_shape=jax.ShapeDtypeStruct((r, c), BF16), grid=(r // tr,),
        in_specs=[pl.BlockSpec((tr, c), lambda i: (i, 0))], out_specs=pl.BlockSpec((tr, c), lambda i: (i, 0)),
        compiler_params=_params(("parallel",)),
    )(x2d)


def _pair_add(own8, got4, name):
    _, r, c = own8.shape
    tr = _pick(r, (512, 256, 128, 64, 32, 16))
    core = lax.axis_index("c").astype(jnp.int32).reshape(1)

    def body(c_ref, a_ref, b_ref, o_ref):
        o_ref[...] = (a_ref[...] + b_ref[...]).astype(BF16)

    return pl.pallas_call(
        body, name=name, out_shape=jax.ShapeDtypeStruct((N_CHIP, r, c), BF16),
        grid_spec=pltpu.PrefetchScalarGridSpec(
            num_scalar_prefetch=1, grid=(N_CHIP, r // tr),
            in_specs=[pl.BlockSpec((None, tr, c), lambda k, i, cr: (2 * k + cr[0], i, 0)),
                      pl.BlockSpec((None, tr, c), lambda k, i, cr: (k, i, 0))],
            out_specs=pl.BlockSpec((None, tr, c), lambda k, i, cr: (k, i, 0))),
        compiler_params=_params(("parallel", "parallel")),
    )(core, own8, got4)


def _adamw(w, m, v, parts, name):
    r, c = w.shape
    n_parts = parts.shape[0]
    tr = _pick(r, (256, 128, 64, 32, 16, 8))
    tc = _pick(c, (1024, 512, 256, 128))

    def body(w_ref, m_ref, v_ref, p_ref, g_ref, d_ref, nm_ref, nv_ref):
        g = p_ref[0].astype(F32)
        for k in range(1, n_parts):
            g = g + p_ref[k].astype(F32)
        wv = w_ref[...]
        nm = ADAM_B1 * m_ref[...] + (1.0 - ADAM_B1) * g
        nv = ADAM_B2 * v_ref[...] + (1.0 - ADAM_B2) * (g * g)
        m_hat = nm / (1.0 - ADAM_B1 ** ADAM_STEP)
        v_hat = nv / (1.0 - ADAM_B2 ** ADAM_STEP)
        g_ref[...] = g
        d_ref[...] = -ADAM_LR * (m_hat / (jnp.sqrt(v_hat) + ADAM_EPS) + ADAM_WD * wv)
        nm_ref[...] = nm
        nv_ref[...] = nv

    tile = pl.BlockSpec((tr, tc), lambda i, j: (i, j))
    sh = jax.ShapeDtypeStruct((r, c), F32)
    return pl.pallas_call(
        body, name=name, out_shape=(sh, sh, sh, sh), grid=(r // tr, c // tc),
        in_specs=[tile, tile, tile, pl.BlockSpec((n_parts, tr, tc), lambda i, j: (0, i, j))],
        out_specs=(tile, tile, tile, tile),
        compiler_params=_params(("parallel", "parallel")),
    )(w, m, v, parts)


def _matmul(a, b, *, dims, shape, tiles, b_spec=None, out_specs=None, out_shapes=None, out_dtypes=(F32,),
            epilogue=None, extras=(), alias=None, jobs=(), name):
    m_dim, n_dim, k_dim = shape
    tm, tn, tk = tiles
    assert m_dim % tm == 0 and n_dim % tn == 0 and k_dim % tk == 0, (name, shape, tiles)
    nk = k_dim // tk
    n_extra = len(extras)
    n_alias = 0 if alias is None else 1
    n_out = len(out_dtypes)
    if dims == "tn":
        a_spec = pl.BlockSpec((tk, tm), lambda i, j, k: (k, i))
        contract = (((0,), (0,)), ((), ()))
    else:
        a_spec = pl.BlockSpec((tm, tk), lambda i, j, k: (i, k))
        contract = (((1,), (1,)), ((), ())) if dims == "nt" else (((1,), (0,)), ((), ()))
    if b_spec is None:
        b_spec = (pl.BlockSpec((tn, tk), lambda i, j, k: (j, k)) if dims == "nt"
                  else pl.BlockSpec((tk, tn), lambda i, j, k: (k, j)))
    if out_specs is None:
        out_specs = tuple(pl.BlockSpec((tm, tn), lambda i, j, k: (i, j)) for _ in range(n_out))
    if out_shapes is None:
        out_shapes = tuple(jax.ShapeDtypeStruct((m_dim, n_dim), d) for d in out_dtypes)

    n_local_in = 2 + n_extra + n_alias
    n_job_in = sum(len(jb.ins) for jb in jobs)
    n_job_out = sum(len(jb.outs) for jb in jobs)
    n_acc = 1 if nk > 1 else 0
    grid = (m_dim // tm, n_dim // tn, nk)

    def body(*refs):
        a_ref, b_ref = refs[0], refs[1]
        extra_refs = refs[2:2 + n_extra]
        o0 = n_local_in + n_job_in
        out_refs = refs[o0:o0 + n_out]
        s0 = o0 + n_out + n_job_out
        job_parts = _split_jobs(jobs, refs[n_local_in:o0], refs[o0 + n_out:s0], refs[s0 + n_acc:])
        i, j, k = pl.program_id(0), pl.program_id(1), pl.program_id(2)

        if jobs:
            @pl.when((i == 0) & (j == 0) & (k == 0))
            def _():
                for jb, i_r, o_r, s_r in job_parts:
                    jb.start(i_r, o_r, s_r)

        def finish(acc):
            if epilogue is None:
                out_refs[0][...] = acc.astype(out_refs[0].dtype)
            else:
                epilogue(acc, i, j, extra_refs, out_refs)

        def product():
            return lax.dot_general(a_ref[...].astype(BF16), b_ref[...].astype(BF16), contract, preferred_element_type=F32)

        if nk == 1:
            finish(product())
        else:
            acc_ref = refs[s0]

            @pl.when(k == 0)
            def _():
                acc_ref[...] = product()

            @pl.when(k > 0)
            def _():
                acc_ref[...] += product()

            @pl.when(k == nk - 1)
            def _():
                finish(acc_ref[...])

        if jobs:
            @pl.when((i == grid[0] - 1) & (j == grid[1] - 1) & (k == nk - 1))
            def _():
                for jb, i_r, o_r, s_r in job_parts:
                    jb.wait(i_r, o_r, s_r)

    ins = [a, b] + [e[0] for e in extras]
    in_specs = [a_spec, b_spec] + [e[1] for e in extras]
    io_alias = {}
    if alias is not None:
        ins.append(alias)
        in_specs.append(ANY)
        io_alias = {len(ins) - 1: 0}
    all_out_shapes, all_out_specs = list(out_shapes), list(out_specs)
    for jb in jobs:
        io_alias.update({len(ins) + a_: len(all_out_shapes) + b_ for a_, b_ in jb.alias.items()})
        ins += jb.ins
        in_specs += [ANY] * len(jb.ins)
        all_out_shapes += jb.outs
        all_out_specs += [ANY] * len(jb.outs)
    scratch = ([pltpu.VMEM((tm, tn), F32)] if nk > 1 else []) + [s for jb in jobs for s in jb.sems]
    outs = pl.pallas_call(
        body, name=name, out_shape=tuple(all_out_shapes), grid=grid,
        in_specs=in_specs, out_specs=tuple(all_out_specs), scratch_shapes=scratch,
        input_output_aliases=io_alias,
        compiler_params=_params(("arbitrary",) * 3 if jobs else ("parallel", "parallel", "arbitrary")),
    )(*ins)
    main = outs[0] if n_out == 1 else tuple(outs[:n_out])
    if not jobs:
        return main
    job_outs, o0 = [], n_out
    for jb in jobs:
        job_outs.append(list(outs[o0:o0 + len(jb.outs)]))
        o0 += len(jb.outs)
    return main, job_outs


def _lead(layer, block, index):
    if layer is None:
        return pl.BlockSpec(block, index)
    return pl.BlockSpec((None,) + block, lambda i, j, k: (layer,) + index(i, j, k))


def _w_spec(kind, layer, per, dims, tn, tk):
    if kind == "nat":
        if dims == "nn":
            return _lead(layer, (tk, tn), lambda i, j, k: (k, j))
        return _lead(layer, (tn, tk), lambda i, j, k: (j, k))
    if dims == "nn":
        q = per // tn
        return _lead(layer, (None, tk, tn), lambda i, j, k: (j // q, k, j % q))
    q = per // tk
    return _lead(layer, (None, tn, tk), lambda i, j, k: (k // q, j, k % q))


def _g_spec(kind, layer, per, tm, tn):
    if kind == "nat":
        return _lead(layer, (tm, tn), lambda i, j, k: (i, j))
    q = per // tn
    return _lead(layer, (None, tm, tn), lambda i, j, k: (j // q, i, j % q))


TOK = (1408, 768, 256, 128)
FEAT = (1024, 512, 256, 128)
KDIM = (2048, 1536, 1024, 512, 256, 128)


def _sel(is_ctx, ref):
    return jnp.where(is_ctx, ref[1:2, :], ref[0:1, :])


def _row_tile(n, n_ctx):
    tr = _pick(n_ctx, (256, 128))
    assert n % tr == 0 and n_ctx % tr == 0
    return tr


def _add_by_segment(acc_ref, cols, ctx_tile, v):
    zero = jnp.zeros_like(v)
    acc_ref[0:1, cols] += jnp.where(ctx_tile, zero, v)
    acc_ref[1:2, cols] += jnp.where(ctx_tile, v, zero)


def _norm_mod(t, gain, mod8, shift_k, n_ctx, name, jobs=()):
    n, d = t.shape
    tr = _row_tile(n, n_ctx)

    def body(t_ref, g_ref, sh_ref, sc_ref, o_ref):
        x = t_ref[...]
        r = lax.rsqrt(_rowmean(x * x) + EPS)
        y = (x * r) * g_ref[...]
        ctx_tile = pl.program_id(0) * tr < n_ctx
        o_ref[...] = (y * (1.0 + _sel(ctx_tile, sc_ref)) + _sel(ctx_tile, sh_ref)).astype(BF16)

    j_ins, j_in_specs, j_outs, j_out_specs, j_sems, io_alias = _carry_args(jobs, 4, 1)
    outs = pl.pallas_call(
        _carry(body, jobs, 4, 1, n // tr), name=name, grid=(n // tr,),
        out_shape=(jax.ShapeDtypeStruct((n, d), BF16),) + tuple(j_outs),
        in_specs=[pl.BlockSpec((tr, d), lambda i: (i, 0)), pl.BlockSpec((1, d), lambda i: (0, 0)),
                  pl.BlockSpec((8, d), lambda i: (0, shift_k)), pl.BlockSpec((8, d), lambda i: (0, shift_k + 1))] + j_in_specs,
        out_specs=(pl.BlockSpec((tr, d), lambda i: (i, 0)),) + tuple(j_out_specs),
        scratch_shapes=j_sems, input_output_aliases=io_alias,
        compiler_params=_params(("arbitrary",) if jobs else ("parallel",)),
    )(t, gain, mod8, mod8, *j_ins)
    return (outs[0], _carry_results(jobs, outs, 1)) if jobs else outs[0]


def _norm_mod_bwd(dh, t, d_res, gain, mod8, shift_k, n_ctx, name, gated=None, latent_only=False, jobs=()):
    n, d = t.shape
    tr = _row_tile(n, n_ctx)

    def body(*refs):
        if gated is None:
            dh_ref, t_ref, dr_ref, g_ref, sc_ref, dt_ref, dss_ref, dg_ref = refs
        else:
            dh_ref, t_ref, dr_ref, g_ref, sc_ref, br_ref, gt_ref, dt_ref, dss_ref, dg_ref, dob_ref, dgate_ref = refs
        i = pl.program_id(0)

        @pl.when(i == 0)
        def _():
            dss_ref[...] = jnp.zeros_like(dss_ref)
            dg_ref[...] = jnp.zeros_like(dg_ref)
            if gated is not None:
                dgate_ref[...] = jnp.zeros_like(dgate_ref)

        x = t_ref[...]
        r = lax.rsqrt(_rowmean(x * x) + EPS)
        xn = x * r
        g = g_ref[...]
        y = xn * g
        dhv = dh_ref[...]
        ctx_tile = i * tr < n_ctx
        _add_by_segment(dss_ref, slice(0, d), ctx_tile, _colsum(dhv))
        _add_by_segment(dss_ref, slice(d, 2 * d), ctx_tile, _colsum(dhv * y))
        dy = dhv * (1.0 + _sel(ctx_tile, sc_ref))
        dg_ref[0:1, :] += _colsum(dy * xn)
        dxn = dy * g
        d_t = dr_ref[...] + r * (dxn - xn * _rowmean(dxn * xn))
        dt_ref[...] = d_t
        if gated is not None:
            dob_ref[...] = (d_t * _sel(ctx_tile, gt_ref)).astype(BF16)
            _add_by_segment(dgate_ref, slice(None), ctx_tile, _colsum(d_t * br_ref[...].astype(F32)))

    row = pl.BlockSpec((tr, d), lambda i: (i, 0))
    acc = pl.BlockSpec((8, d), lambda i: (0, 0))
    ins = [dh, t, d_res, gain, mod8]
    in_specs = [row, row, row, pl.BlockSpec((1, d), lambda i: (0, 0)), pl.BlockSpec((8, d), lambda i: (0, shift_k + 1))]
    if latent_only:
        first = n_ctx // tr
        out_shape = [jax.ShapeDtypeStruct((n - n_ctx, d), F32)]
        out_specs = [pl.BlockSpec((tr, d), lambda i: (jnp.maximum(i - first, 0), 0))]
    else:
        out_shape, out_specs = [jax.ShapeDtypeStruct((n, d), F32)], [row]
    out_shape += [jax.ShapeDtypeStruct((8, 2 * d), F32), jax.ShapeDtypeStruct((8, d), F32)]
    out_specs += [pl.BlockSpec((8, 2 * d), lambda i: (0, 0)), acc]
    if gated is not None:
        branch, gate_mod8, gate_k = gated
        ins += [branch, gate_mod8]
        in_specs += [row, pl.BlockSpec((8, d), lambda i: (0, gate_k))]
        out_shape += [jax.ShapeDtypeStruct((n, d), BF16), jax.ShapeDtypeStruct((8, d), F32)]
        out_specs += [row, acc]
    n_in, n_out = len(ins), len(out_shape)
    j_ins, j_in_specs, j_outs, j_out_specs, j_sems, io_alias = _carry_args(jobs, n_in, n_out)
    outs = pl.pallas_call(
        _carry(body, jobs, n_in, n_out, n // tr), name=name, out_shape=tuple(out_shape + j_outs), grid=(n // tr,),
        in_specs=in_specs + j_in_specs, out_specs=tuple(out_specs + j_out_specs), scratch_shapes=j_sems,
        input_output_aliases=io_alias, compiler_params=_params(("arbitrary",)),
    )(*ins, *j_ins)
    return (tuple(outs[:n_out]), _carry_results(jobs, outs, n_out)) if jobs else outs


def _swap16(v):
    w = v.shape[1]
    lane = lax.broadcasted_iota(jnp.int32, v.shape, 1)
    return jnp.where((lane % 32) < 16, pltpu.roll(v, w - 16, 1), pltpu.roll(v, 16, 1))


def _rope(v, cs, sn, sign):
    reps = v.shape[1] // 128
    c = jnp.tile(cs, (1, reps)) if reps > 1 else cs
    s = jnp.tile(sn, (1, reps)) if reps > 1 else sn
    return v * c + sign * (_swap16(v) * s)


def _attn_specs(nb, n_ctx, hd, kd):
    kci = hd // kd
    specs = [pl.BlockSpec((BLOCK, hd), lambda b: (b, 0)),
             pl.BlockSpec((n_ctx, kd), lambda b: (0, kci)), pl.BlockSpec((n_ctx, kd), lambda b: (0, kci + 1))]
    for col in (kci, kci + 1):
        specs.append(pl.BlockSpec((BLOCK, kd), lambda b, col=col: (jnp.maximum(b - 1, 0), col)))
        specs.append(pl.BlockSpec((BLOCK, kd), lambda b, col=col: (b, col)))
        specs.append(pl.BlockSpec((BLOCK, kd), lambda b, col=col: (jnp.minimum(b + 1, nb - 1), col)))
    return specs


def _band_valid(b, group, n_ctx, n):
    q_pos = b * BLOCK + lax.broadcasted_iota(jnp.int32, (group * BLOCK, 1), 0) % BLOCK
    k_pos = (b - 1) * BLOCK + lax.broadcasted_iota(jnp.int32, (1, 3 * BLOCK), 1)
    return (jnp.abs(k_pos - q_pos) <= BLOCK) & (k_pos >= n_ctx) & (k_pos < n) & (q_pos >= n_ctx)


NT = (((1,), (1,)), ((), ()))
NN = (((1,), (0,)), ((), ()))
TN = (((0,), (0,)), ((), ()))


def _dot(a, b, dn):
    return lax.dot_general(a, b, dn, preferred_element_type=F32)


def _keys_of_block(b, kv_refs, group, n_ctx, n):
    kc_ref, vc_ref, k0, k1, k2, v0, v1, v2 = kv_refs
    bias = jnp.where(_band_valid(b, group, n_ctx, n), 0.0, NEG_INF)
    kcat = jnp.concatenate([kc_ref[...], k0[...], k1[...], k2[...]], axis=0)
    vcat = jnp.concatenate([vc_ref[...], v0[...], v1[...], v2[...]], axis=0)
    return bias, kcat, vcat


def _stack_heads(v, h, group):
    return jnp.concatenate([v[:, (h * group + j) * HEAD_DIM:(h * group + j + 1) * HEAD_DIM] for j in range(group)], axis=0)


def _scores(qg, keys, bias, n_ctx):
    s = _dot(qg, keys, NT)
    return jnp.concatenate([s[:, :n_ctx], s[:, n_ctx:] + bias], axis=1)


def _softmax_terms(s, sink_ref, h, group):
    snk = jnp.concatenate([jnp.full((BLOCK, 1), sink_ref[h * group + j], F32) for j in range(group)], axis=0)
    m = jnp.maximum(jnp.max(s, axis=1, keepdims=True), snk)
    e, e_s = jnp.exp(s - m), jnp.exp(snk - m)
    return e, e_s, 1.0 / (jnp.sum(e, axis=1, keepdims=True) + e_s)


def _softmax_sink(s, sink_ref, h, group):
    e, e_s, inv = _softmax_terms(s, sink_ref, h, group)
    return e * inv, e_s * inv


def _attention(qkv, sink, n_ctx, hd, kd, name, jobs=()):
    n = qkv.shape[0]
    nb = n // BLOCK
    n_kv = kd // HEAD_DIM
    group = hd // kd
    n_job_in = sum(len(jb.ins) for jb in jobs)
    n_job_out = sum(len(jb.outs) for jb in jobs)

    def body(*refs):
        q_ref, kc_ref, vc_ref, k0, k1, k2, v0, v1, v2, sink_ref = refs[:10]
        o_ref = refs[10 + n_job_in]
        job_parts = _split_jobs(jobs, refs[10:10 + n_job_in], refs[11 + n_job_in:11 + n_job_in + n_job_out],
                                refs[11 + n_job_in + n_job_out:])
        b = pl.program_id(0)

        if jobs:
            @pl.when(b == 0)
            def _():
                for jb, i_r, o_r, s_r in job_parts:
                    jb.start(i_r, o_r, s_r)

        bias, kcat, vcat = _keys_of_block(b, (kc_ref, vc_ref, k0, k1, k2, v0, v1, v2), group, n_ctx, n)
        q = q_ref[...]

        def scores(h):
            return _scores(_stack_heads(q, h, group), kcat[:, h * HEAD_DIM:(h + 1) * HEAD_DIM], bias, n_ctx)

        heads, s_next = [], scores(0)
        for h in range(n_kv):
            s = s_next
            if h + 1 < n_kv:
                s_next = scores(h + 1)
            e, _, inv = _softmax_terms(s, sink_ref, h, group)
            o = _dot(e.astype(BF16), vcat[:, h * HEAD_DIM:(h + 1) * HEAD_DIM], NN) * inv
            heads += [o[j * BLOCK:(j + 1) * BLOCK, :] for j in range(group)]
        o_ref[...] = jnp.concatenate(heads, axis=1)

        if jobs:
            @pl.when(b == nb - 1)
            def _():
                for jb, i_r, o_r, s_r in job_parts:
                    jb.wait(i_r, o_r, s_r)

    io_alias, i0, o0 = {}, 10, 1
    for jb in jobs:
        io_alias.update({i0 + a_: o0 + b_ for a_, b_ in jb.alias.items()})
        i0, o0 = i0 + len(jb.ins), o0 + len(jb.outs)
    outs = pl.pallas_call(
        body, name=name, grid=(nb,),
        out_shape=(jax.ShapeDtypeStruct((n, hd), F32),) + tuple(s for jb in jobs for s in jb.outs),
        in_specs=_attn_specs(nb, n_ctx, hd, kd) + [pl.BlockSpec(memory_space=pltpu.SMEM)] + [ANY] * n_job_in,
        out_specs=(pl.BlockSpec((BLOCK, hd), lambda b: (b, 0)),) + tuple([ANY] * n_job_out),
        scratch_shapes=[s for jb in jobs for s in jb.sems], input_output_aliases=io_alias,
        compiler_params=_params(("arbitrary",) if jobs else ("parallel",)),
    )(qkv, qkv, qkv, qkv, qkv, qkv, qkv, qkv, qkv, sink, *[v for jb in jobs for v in jb.ins])
    if not jobs:
        return outs[0]
    job_outs, o0 = [], 1
    for jb in jobs:
        job_outs.append(list(outs[o0:o0 + len(jb.outs)]))
        o0 += len(jb.outs)
    return outs[0], job_outs


def _attention_bwd(qkv, sink, ao, d_mg, cpar, n_ctx, hd, kd, name):
    n = qkv.shape[0]
    nb = n // BLOCK
    n_kv = kd // HEAD_DIM
    group = hd // kd
    n_heads = n_kv * group

    def body(q_ref, kc_ref, vc_ref, k0, k1, k2, v0, v1, v2, sink_ref, ao_ref, dmg_ref, cp_ref,
             dq_ref, part_ref, dctx_ref, dsink_ref, dgain_ref):
        b = pl.program_id(0)

        @pl.when(b == 0)
        def _():
            dctx_ref[...] = jnp.zeros_like(dctx_ref)
            dsink_ref[...] = jnp.zeros_like(dsink_ref)
            dgain_ref[...] = jnp.zeros_like(dgain_ref)

        ao_v = ao_ref[...]
        ra = lax.rsqrt(_rowmean(ao_v * ao_v) + EPS)
        an = ao_v * ra
        dmg = dmg_ref[...]
        dgain_ref[0:1, :] += _colsum(dmg * an)
        d_an = dmg * cp_ref[5:6, :]
        d_ao = (ra * (d_an - an * _rowmean(d_an * an))).astype(BF16)

        bias, kcat, vcat = _keys_of_block(b, (kc_ref, vc_ref, k0, k1, k2, v0, v1, v2), group, n_ctx, n)
        q = q_ref[...]
        lane = lax.broadcasted_iota(jnp.int32, (1, 128), 1)
        dsink_row = jnp.zeros((1, 128), F32)

        def first_half(h):
            hs = slice(h * HEAD_DIM, (h + 1) * HEAD_DIM)
            qg, dog = _stack_heads(q, h, group), _stack_heads(d_ao, h, group)
            return qg, dog, _scores(qg, kcat[:, hs], bias, n_ctx), _dot(dog, vcat[:, hs], NT)

        dq_heads, dk, dv = [], [], []
        nxt = first_half(0)
        for h in range(n_kv):
            qg, dog, s, d_p = nxt
            if h + 1 < n_kv:
                nxt = first_half(h + 1)
            p, p_s = _softmax_sink(s, sink_ref, h, group)
            delta = jnp.sum(p * d_p, axis=1, keepdims=True)
            ds = (p * (d_p - delta)).astype(BF16)
            psd = p_s * delta
            for j in range(group):
                val = -jnp.sum(psd[j * BLOCK:(j + 1) * BLOCK, :], axis=0, keepdims=True)
                dsink_row = dsink_row + jnp.where(lane == h * group + j, val, 0.0)
            dq = _dot(ds, kcat[:, h * HEAD_DIM:(h + 1) * HEAD_DIM], NN) * SCALE
            dq_heads += [dq[j * BLOCK:(j + 1) * BLOCK, :] for j in range(group)]
            dk.append(_dot(ds, qg, TN))
            dv.append(_dot(p.astype(BF16), dog, TN))
        dq_ref[...] = jnp.concatenate(dq_heads, axis=1)
        d_kv = jnp.concatenate(dk + dv, axis=1)
        dctx_ref[...] += d_kv[:n_ctx]
        for j in range(3):
            part_ref[j] = d_kv[n_ctx + j * BLOCK:n_ctx + (j + 1) * BLOCK, :]
        dsink_ref[0:1, :] += dsink_row

    assert n_heads <= 128
    out_shape = (jax.ShapeDtypeStruct((n, hd), F32), jax.ShapeDtypeStruct((nb, 3, BLOCK, 2 * kd), F32),
                 jax.ShapeDtypeStruct((n_ctx, 2 * kd), F32), jax.ShapeDtypeStruct((8, 128), F32),
                 jax.ShapeDtypeStruct((8, hd), F32))
    return pl.pallas_call(
        body, name=name, out_shape=out_shape, grid=(nb,),
        in_specs=_attn_specs(nb, n_ctx, hd, kd) + [
            pl.BlockSpec(memory_space=pltpu.SMEM), pl.BlockSpec((BLOCK, hd), lambda b: (b, 0)),
            pl.BlockSpec((BLOCK, hd), lambda b: (b, 1)), pl.BlockSpec((8, hd), lambda b: (0, 0))],
        out_specs=(pl.BlockSpec((BLOCK, hd), lambda b: (b, 0)),
                   pl.BlockSpec((None, 3, BLOCK, 2 * kd), lambda b: (b, 0, 0, 0)),
                   pl.BlockSpec((n_ctx, 2 * kd), lambda b: (0, 0)), pl.BlockSpec((8, 128), lambda b: (0, 0)),
                   pl.BlockSpec((8, hd), lambda b: (0, 0))),
        compiler_params=_params(("arbitrary",)),
    )(qkv, qkv, qkv, qkv, qkv, qkv, qkv, qkv, qkv, sink, ao, d_mg, cpar)


def _halo_specs(tr, n, width, col=0):
    q = tr // 8
    return [pl.BlockSpec((8, width), lambda i: (jnp.maximum(i * q - 1, 0), col)),
            pl.BlockSpec((8, width), lambda i: (jnp.minimum((i + 1) * q, n // 8 - 1), col))]


def _mix_fwd(p, ao, cpar, n_ctx, dc, name):
    n = p.shape[0]
    tr = _pick(n, (256, 128))

    def body(p_ref, pp_ref, pn_ref, ao_ref, cp_ref, o_ref):
        i = pl.program_id(0)
        bg = p_ref[:, 0:dc]
        u = p_ref[:, dc:2 * dc] * p_ref[:, 2 * dc:3 * dc]
        u_before = pp_ref[7:8, dc:2 * dc] * pp_ref[7:8, 2 * dc:3 * dc]
        u_after = pn_ref[0:1, dc:2 * dc] * pn_ref[0:1, 2 * dc:3 * dc]
        loc = lax.broadcasted_iota(jnp.int32, (tr, 1), 0)
        gid = i * tr + loc
        has_prev = (gid != 0) & (gid != n_ctx)
        has_next = (gid != n_ctx - 1) & (gid != n - 1)
        u_m1 = jnp.where(has_prev, jnp.where(loc == 0, u_before, pltpu.roll(u, 1, 0)), 0.0)
        u_p1 = jnp.where(has_next, jnp.where(loc == tr - 1, u_after, pltpu.roll(u, tr - 1, 0)), 0.0)
        cv = u_m1 * cp_ref[0:1, :] + u * cp_ref[1:2, :] + u_p1 * cp_ref[2:3, :] + cp_ref[3:4, :]
        co = bg * cv
        nc = (co * lax.rsqrt(_rowmean(co * co) + EPS)) * cp_ref[4:5, :]
        ao_v = ao_ref[...]
        na = (ao_v * lax.rsqrt(_rowmean(ao_v * ao_v) + EPS)) * cp_ref[5:6, :]
        o_ref[...] = jnp.concatenate([nc, na], axis=1).astype(BF16)

    return pl.pallas_call(
        body, name=name, out_shape=jax.ShapeDtypeStruct((n, 2 * dc), BF16), grid=(n // tr,),
        in_specs=[pl.BlockSpec((tr, 3 * dc), lambda i: (i, 0))] + _halo_specs(tr, n, 3 * dc)
        + [pl.BlockSpec((tr, dc), lambda i: (i, 0)), pl.BlockSpec((8, dc), lambda i: (0, 0))],
        out_specs=pl.BlockSpec((tr, 2 * dc), lambda i: (i, 0)),
        compiler_params=_params(("parallel",)),
    )(p, p, p, ao, cpar)


def _mix_bwd(d_mg, p, cpar, d_q, parts, d_ctx, cs, sn, n_ctx, dc, hd, kd, name):
    n = p.shape[0]
    d_in = 3 * dc + hd + 2 * kd
    tr = BLOCK
    nb = n // tr
    ext = tr + 16
    n_ctx_blocks = n_ctx // BLOCK

    def body(dm_ref, dmp_ref, dmn_ref, p_ref, pp_ref, pn_ref, cp_ref, dq_ref, pa_ref, pb_ref, pc_ref, dctx_ref,
             cs_ref, sn_ref, dp_ref, acc_ref):
        i = pl.program_id(0)

        @pl.when(i == 0)
        def _():
            acc_ref[...] = jnp.zeros_like(acc_ref)

        def cat(before, here, after):
            return jnp.concatenate([before, here, after], axis=0)

        bg = cat(pp_ref[:, 0:dc], p_ref[:, 0:dc], pn_ref[:, 0:dc])
        cg = cat(pp_ref[:, dc:2 * dc], p_ref[:, dc:2 * dc], pn_ref[:, dc:2 * dc])
        hh = cat(pp_ref[:, 2 * dc:3 * dc], p_ref[:, 2 * dc:3 * dc], pn_ref[:, 2 * dc:3 * dc])
        dme = cat(dmp_ref[...], dm_ref[...], dmn_ref[...])
        gid = i * tr - 8 + lax.broadcasted_iota(jnp.int32, (ext, 1), 0)
        inside = (gid >= 0) & (gid < n)
        has_prev = inside & (gid != 0) & (gid != n_ctx)
        has_next = inside & (gid != n_ctx - 1) & (gid != n - 1)
        w0, w1, w2, bias, gain = cp_ref[0:1, :], cp_ref[1:2, :], cp_ref[2:3, :], cp_ref[3:4, :], cp_ref[4:5, :]
        u = jnp.where(inside, cg * hh, 0.0)
        u_m1 = jnp.where(has_prev, pltpu.roll(u, 1, 0), 0.0)
        u_p1 = jnp.where(has_next, pltpu.roll(u, ext - 1, 0), 0.0)
        cv = u_m1 * w0 + u * w1 + u_p1 * w2 + bias
        co = bg * cv
        rc = lax.rsqrt(_rowmean(co * co) + EPS)
        cn = co * rc
        d_cn = dme * gain
        d_co = rc * (d_cn - cn * _rowmean(d_cn * cn))
        d_cv = jnp.where(inside, d_co * bg, 0.0)
        d_bg = d_co * cv
        d_cv_p1 = jnp.where(has_next, pltpu.roll(d_cv, ext - 1, 0), 0.0)
        d_cv_m1 = jnp.where(has_prev, pltpu.roll(d_cv, 1, 0), 0.0)
        d_u = d_cv_p1 * w0 + d_cv * w1 + d_cv_m1 * w2
        mid = slice(8, 8 + tr)
        acc_ref[0:1, :] += _colsum((d_cv * u_m1)[mid])
        acc_ref[1:2, :] += _colsum((d_cv * u)[mid])
        acc_ref[2:3, :] += _colsum((d_cv * u_p1)[mid])
        acc_ref[3:4, :] += _colsum(d_cv[mid])
        acc_ref[4:5, :] += _colsum((dme * cn)[mid])

        d_kv = (jnp.where(i >= 1, pa_ref[...], 0.0) + pb_ref[...] + jnp.where(i + 1 < nb, pc_ref[...], 0.0))
        ctx_rows = dctx_ref[pl.ds(pl.multiple_of(jnp.minimum(i, n_ctx_blocks - 1) * BLOCK, BLOCK), BLOCK), :]
        d_kv = d_kv + jnp.where(i < n_ctx_blocks, ctx_rows, 0.0)
        cs_v, sn_v = cs_ref[...], sn_ref[...]
        d_qu = _rope(dq_ref[...], cs_v, sn_v, -1.0)
        d_ku = _rope(d_kv[:, 0:kd], cs_v, sn_v, -1.0)
        dp_ref[...] = jnp.concatenate(
            [d_bg[mid], (d_u * hh)[mid], (d_u * cg)[mid], d_qu, d_ku, d_kv[:, kd:2 * kd]], axis=1).astype(BF16)

    part = lambda sel, which: pl.BlockSpec((None, None, BLOCK, 2 * kd), lambda i: (sel(i), which, 0, 0))
    return pl.pallas_call(
        body, name=name, out_shape=(jax.ShapeDtypeStruct((n, d_in), BF16), jax.ShapeDtypeStruct((8, dc), F32)),
        grid=(nb,),
        in_specs=[pl.BlockSpec((tr, dc), lambda i: (i, 0))] + _halo_specs(tr, n, dc)
        + [pl.BlockSpec((tr, 3 * dc), lambda i: (i, 0))] + _halo_specs(tr, n, 3 * dc)
        + [pl.BlockSpec((8, dc), lambda i: (0, 0)), pl.BlockSpec((tr, hd), lambda i: (i, 0)),
           part(lambda i: jnp.maximum(i - 1, 0), 2), part(lambda i: i, 1), part(lambda i: jnp.minimum(i + 1, nb - 1), 0),
           pl.BlockSpec((n_ctx, 2 * kd), lambda i: (0, 0)),
           pl.BlockSpec((tr, 128), lambda i: (i, 0)), pl.BlockSpec((tr, 128), lambda i: (i, 0))],
        out_specs=(pl.BlockSpec((tr, d_in), lambda i: (i, 0)), pl.BlockSpec((8, dc), lambda i: (0, 0))),
        compiler_params=_params(("arbitrary",)),
    )(d_mg, d_mg, d_mg, p, p, p, cpar, d_q, parts, parts, parts, d_ctx, cs, sn)


def _loss_bwd(t, gain, target, branch, mod8, gate_k, n_ctx, name):
    n, d = t.shape
    tr = _pick(n_ctx, (256, 128))
    first = n_ctx // tr

    def body(t_ref, g_ref, y_ref, br_ref, gt_ref, dt_ref, loss_ref, dg_ref, dob_ref, dgate_ref):
        i = pl.program_id(0)

        @pl.when(i == 0)
        def _():
            loss_ref[...] = jnp.zeros_like(loss_ref)
            dg_ref[...] = jnp.zeros_like(dg_ref)
            dgate_ref[...] = jnp.zeros_like(dgate_ref)

        @pl.when(i < first)
        def _():
            dt_ref[...] = jnp.zeros_like(dt_ref)
            dob_ref[...] = jnp.zeros_like(dob_ref)

        @pl.when(i >= first)
        def _():
            x = t_ref[...]
            g = g_ref[...]
            r = lax.rsqrt(_rowmean(x * x) + EPS)
            xn = x * r
            err = xn * g - y_ref[...]
            loss_ref[...] += 0.5 * _colsum(_rowmean(err * err))
            dy = err * (1.0 / d)
            dg_ref[0:1, :] += _colsum(dy * xn)
            dxn = dy * g
            d_t = r * (dxn - xn * _rowmean(dxn * xn))
            dt_ref[...] = d_t
            dob_ref[...] = (d_t * gt_ref[0:1, :]).astype(BF16)
            dgate_ref[0:1, :] += _colsum(d_t * br_ref[...].astype(F32))

    row = pl.BlockSpec((tr, d), lambda i: (i, 0))
    acc = pl.BlockSpec((8, d), lambda i: (0, 0))
    return pl.pallas_call(
        body, name=name,
        out_shape=(jax.ShapeDtypeStruct((n, d), F32), jax.ShapeDtypeStruct((8, 128), F32), jax.ShapeDtypeStruct((8, d), F32),
                   jax.ShapeDtypeStruct((n, d), BF16), jax.ShapeDtypeStruct((8, d), F32)),
        grid=(n // tr,),
        in_specs=[row, pl.BlockSpec((1, d), lambda i: (0, 0)), pl.BlockSpec((tr, d), lambda i: (jnp.maximum(i - first, 0), 0)),
                  row, pl.BlockSpec((8, d), lambda i: (0, gate_k))],
        out_specs=(row, pl.BlockSpec((8, 128), lambda i: (0, 0)), acc, row, acc),
        compiler_params=_params(("arbitrary",)),
    )(t, gain, target, branch, mod8)


def _silu16(c16, name):
    def body(c_ref, o_ref):
        v = c_ref[...]
        o_ref[...] = (v * jax.nn.sigmoid(v)).astype(BF16)

    return pl.pallas_call(body, name=name, out_shape=jax.ShapeDtypeStruct(c16.shape, BF16))(c16)


def _cctx_grad(parts, c_ctx, name):
    def body(p_ref, c_ref, o_ref):
        g = _colsum(p_ref[...])
        v = c_ref[...]
        s = jax.nn.sigmoid(v)
        o_ref[...] = g * (s * (1.0 + v * (1.0 - s)))

    return pl.pallas_call(body, name=name, out_shape=jax.ShapeDtypeStruct(c_ctx.shape, F32))(parts, c_ctx)


def _rope_tables(n_ctx, n_tok):
    half = HEAD_DIM // 4
    inv = ROPE_THETA ** (-jnp.arange(0, HEAD_DIM // 2, 2, dtype=F32) / (HEAD_DIM // 2))
    rows = n_tok // GRID_W
    row_pos = jnp.repeat(jnp.arange(rows, dtype=F32), GRID_W)
    col_pos = jnp.tile(jnp.arange(GRID_W, dtype=F32), rows)
    ang_r, ang_c = row_pos[:, None] * inv[None, :], col_pos[:, None] * inv[None, :]
    cos = jnp.concatenate([jnp.cos(ang_r), jnp.cos(ang_r), jnp.cos(ang_c), jnp.cos(ang_c)], axis=1)
    sin = jnp.concatenate([-jnp.sin(ang_r), jnp.sin(ang_r), -jnp.sin(ang_c), jnp.sin(ang_c)], axis=1)
    assert cos.shape[1] == 4 * half == HEAD_DIM
    cos = jnp.concatenate([jnp.ones((n_ctx, HEAD_DIM), F32), cos], axis=0)
    sin = jnp.concatenate([jnp.zeros((n_ctx, HEAD_DIM), F32), sin], axis=0)
    return jnp.tile(cos, (1, 2)), jnp.tile(sin, (1, 2))


def kernel(x, c, ctx, c_ctx, w_ada, b_ada, g_norm1, g_norm2, w_in, conv_w, conv_b, sink, g_out_conv, g_out_attn, w_out, w_mlp1, w_mlp2, g_final, loss_target, m_c_ctx, m_w_ada, m_b_ada, m_g_norm1, m_g_norm2, m_w_in, m_conv_w, m_conv_b, m_sink, m_g_out_conv, m_g_out_attn, m_w_out, m_w_mlp1, m_w_mlp2, m_g_final, v_c_ctx, v_w_ada, v_b_ada, v_g_norm1, v_g_norm2, v_w_in, v_conv_w, v_conv_b, v_sink, v_g_out_conv, v_g_out_attn, v_w_out, v_w_mlp1, v_w_mlp2, v_g_final):
    n_lat, d = x.shape[1], x.shape[2]
    n_ctx = ctx.shape[1]
    n = n_ctx + n_lat
    depth = w_in.shape[0]
    dc = d // 2
    hd, kd = dc, N_KV_HEADS * HEAD_DIM
    n_heads = hd // HEAD_DIM
    d_in = 3 * dc + hd + 2 * kd
    cin, c_ada, r_out, c_ff, r_ff = w_in.shape[2], w_ada.shape[2], w_out.shape[1], w_mlp1.shape[2], w_mlp2.shape[1]
    d_ff = N_DEV * c_ff
    cw = conv_w.shape[2]
    assert d_in == N_DEV * cin and n_ctx % BLOCK == 0 and n_lat % BLOCK == 0 and hd % kd == 0
    dev = 4 * lax.axis_index("x") + 2 * lax.axis_index("y") + lax.axis_index("c")

    c_all, conv_w_all = _all_gather([c, conv_w], "gather_cond")
    conv_w_full = jnp.transpose(conv_w_all, (1, 2, 0, 3)).reshape(depth, 3, dc)
    c16 = jnp.concatenate([c_all.reshape(N_DEV, d), jnp.broadcast_to(c_ctx[None, :], (8, d))], axis=0)
    sc16 = _silu16(c16, "silu_cond")

    b_ada_loc = lax.dynamic_index_in_dim(b_ada.reshape(depth, N_DEV, c_ada), dev, axis=1, keepdims=False)
    tn_ada = _pick(c_ada, FEAT)

    def add_bias(acc, i, j, extra, outs):
        outs[0][...] = acc + extra[0][...]

    mod_loc = []
    for l in range(depth):
        mod_loc.append(_matmul(
            sc16, w_ada, dims="nn", shape=(16, c_ada, d), tiles=(16, tn_ada, _pick(d, KDIM)),
            b_spec=_w_spec("nat", l, None, "nn", tn_ada, _pick(d, KDIM)), epilogue=add_bias,
            extras=[(b_ada_loc[l][None, :], pl.BlockSpec((1, tn_ada), lambda i, j, k: (0, j)))], name=f"ada_fwd{l}"))
    (mod_all,) = _all_gather([jnp.stack(mod_loc)], "gather_mod")
    mod_full = jnp.transpose(mod_all, (1, 2, 0, 3)).reshape(depth, 16, N_MOD * d)
    mod_mine = lax.dynamic_index_in_dim(mod_full, dev, axis=1, keepdims=True)
    mod8 = jnp.concatenate([mod_mine, mod_full[:, 8:9], jnp.zeros((depth, 6, N_MOD * d), F32)], axis=1)

    w_in_b = _to_bf16(w_in.reshape(depth * d, cin), "cast_w_in").reshape(depth, d, cin)
    w_out_b = _to_bf16(w_out.reshape(depth * r_out, d), "cast_w_out").reshape(depth, r_out, d)
    w1_b = _to_bf16(w_mlp1.reshape(depth * d, c_ff), "cast_w_mlp1").reshape(depth, d, c_ff)
    w2_b = _to_bf16(w_mlp2.reshape(depth * r_ff, d), "cast_w_mlp2").reshape(depth, r_ff, d)
    def full_in(g):
        return jnp.transpose(g, (1, 0, 2)).reshape(d, d_in)

    w_in_full, w_out_full, g_w1, w2_full = [], [], [], []
    pend_w2 = None

    t = jnp.concatenate([ctx[0], x[0]], axis=0)
    cs, sn = _rope_tables(n_ctx, n_lat)
    tm = _pick(n, TOK)
    tk_d = _pick(d, KDIM)
    tm_res = _pick(n, (1056, 768, 256, 128))
    tn_in = _pick(math.gcd(3 * dc, hd + 2 * kd), (512, 256, 128))
    conv_tiles = 3 * dc // tn_in
    bounds = [(0, hd, "q"), (hd, hd + kd, "k"), (hd + kd, hd + 2 * kd, "v")]
    qkv_tiles = []
    for jj in range((hd + 2 * kd) // tn_in):
        lo, hi = jj * tn_in, (jj + 1) * tn_in
        segs = [(max(lo, a) - lo, min(hi, b) - lo, kind) for a, b, kind in bounds if a < hi and b > lo]
        assert all((e - s) % 128 == 0 for s, e, _ in segs)
        qkv_tiles.append(segs)

    def in_proj_epilogue(acc, i, j, extra, outs):
        @pl.when(j < conv_tiles)
        def _():
            outs[0][...] = acc

        for jj, segs in enumerate(qkv_tiles):
            @pl.when(j == conv_tiles + jj)
            def _(segs=segs):
                cs_v, sn_v = extra[0][...], extra[1][...]
                pieces = []
                for s0, s1, kind in segs:
                    v = acc[:, s0:s1]
                    if kind == "q":
                        v = _rope(v, cs_v, sn_v, 1.0) * SCALE
                    elif kind == "k":
                        v = _rope(v, cs_v, sn_v, 1.0)
                    pieces.append(v)
                outs[1][...] = (pieces[0] if len(pieces) == 1 else jnp.concatenate(pieces, axis=1)).astype(BF16)

    def in_proj_specs(rows):
        return dict(
            out_specs=(pl.BlockSpec((rows, tn_in), lambda i, j, k: (i, jnp.minimum(j, conv_tiles - 1))),
                       pl.BlockSpec((rows, tn_in), lambda i, j, k: (i, jnp.maximum(j - conv_tiles, 0)))),
            out_shapes=(jax.ShapeDtypeStruct((n, 3 * dc), F32), jax.ShapeDtypeStruct((n, hd + 2 * kd), BF16)),
            out_dtypes=(F32, BF16), epilogue=in_proj_epilogue,
            extras=[(cs, pl.BlockSpec((rows, 128), lambda i, j, k: (i, 0))), (sn, pl.BlockSpec((rows, 128), lambda i, j, k: (i, 0)))])

    def resid_epilogue(tile_rows):
        def epi(acc, i, j, extra, outs):
            is_ctx = _row_ids(i, tile_rows) < n_ctx
            outs[0][...] = extra[0][...] + _sel(is_ctx, extra[1]) * acc
            outs[1][...] = acc.astype(BF16)
        return epi

    def sq_relu_epilogue(acc, i, j, extra, outs):
        outs[0][...] = acc.astype(BF16)
        rl = jnp.maximum(acc, 0.0)
        outs[1][...] = (rl * rl).astype(BF16)

    def d_sq_relu_epilogue(acc, i, j, extra, outs):
        outs[0][...] = (acc * (2.0 * jnp.maximum(extra[0][...].astype(F32), 0.0))).astype(BF16)

    saved = []
    for l in range(depth):
        cpar = jnp.concatenate([conv_w_full[l], conv_b[l][None], g_out_conv[l][None], g_out_attn[l][None],
                                jnp.zeros((2, dc), F32)], axis=0)
        more = l + 1 < depth
        if l == 0:
            h, (gath0,) = _norm_mod(t, g_norm1[l][None], mod8[l], 0, n_ctx, f"norm1_{l}",
                                    jobs=[_job_gather_ici([w_in_b, w_out_b], 0)])
            (gath0,) = _comm_call([_job_gather_d2d(gath0)], "gather_w0_d2d")
            w_in_full.append(full_in(gath0[0]))
            w_out_full.append(gath0[1].reshape(d, d))
        else:
            h = _norm_mod(t, g_norm1[l][None], mod8[l], 0, n_ctx, f"norm1_{l}")
        jobs = ([_job_gather_d2d(pend_w2)] if pend_w2 is not None else []) + (
            [_job_gather_ici([w_in_b, w_out_b], l + 1)] if more else [])
        res = _matmul(h, w_in_full[l], dims="nn", shape=(n, d_in, d), tiles=(tm, tn_in, tk_d), jobs=jobs, name=f"in_proj{l}",
                      **in_proj_specs(tm))
        if jobs:
            (p, qkv), job_outs = res
            if pend_w2 is not None:
                w2_full.append(job_outs[0][0].reshape(d_ff, d))
            pend_io = job_outs[-1] if more else None
        else:
            p, qkv = res
        if l == 0:
            ao, (pend_mlp0,) = _attention(qkv, sink[l], n_ctx, hd, kd, f"attn{l}", jobs=[_job_gather_ici([w1_b, w2_b], 0)])
        else:
            ao = _attention(qkv, sink[l], n_ctx, hd, kd, f"attn{l}")
        mg = _mix_fwd(p, ao, cpar, n_ctx, dc, f"mix{l}")
        tn = _pick(d, (512, 256, 128))
        jobs = ([_job_gather_d2d(pend_io)] if more else []) + ([_job_gather_d2d(pend_mlp0)] if l == 0 else [])
        res = _matmul(
            mg, w_out_full[l], dims="nn", shape=(n, d, d), tiles=(tm, tn, tk_d),
            out_dtypes=(F32, BF16), epilogue=resid_epilogue(tm),
            extras=[(t, pl.BlockSpec((tm, tn), lambda i, j, k: (i, j))),
                    (mod8[l], pl.BlockSpec((8, tn), lambda i, j, k, tn=tn: (0, 2 * (d // tn) + j)))],
            jobs=jobs, name=f"out_proj{l}")
        if jobs:
            (t2, z), job_outs = res
            if more:
                w_in_full.append(full_in(job_outs[0][0]))
                w_out_full.append(job_outs[0][1].reshape(d, d))
            if l == 0:
                g_w1.append(job_outs[-1][0])
                w2_full.append(job_outs[-1][1].reshape(d_ff, d))
        else:
            t2, z = res
        h2 = _norm_mod(t2, g_norm2[l][None], mod8[l], 3, n_ctx, f"norm2_{l}")
        tn = _pick(c_ff, FEAT)
        res = _matmul(h2, g_w1[l], dims="nn", shape=(n, d_ff, d), tiles=(tm, tn, tk_d),
                      b_spec=_w_spec("cols", None, c_ff, "nn", tn, tk_d), out_dtypes=(BF16, BF16),
                      epilogue=sq_relu_epilogue, jobs=[_job_gather_ici([w1_b], l + 1)] if more else [], name=f"mlp_up{l}")
        if more:
            (a, s), ((pend_w1,),) = res
        else:
            a, s = res
        tn = _pick(d, FEAT)
        tk = _pick(d_ff, (2048, 1024, 512, 256, 128))
        res = _matmul(
            s, w2_full[l], dims="nn", shape=(n, d, d_ff), tiles=(tm_res, tn, tk),
            out_dtypes=(F32, BF16), epilogue=resid_epilogue(tm_res),
            extras=[(t2, pl.BlockSpec((tm_res, tn), lambda i, j, k: (i, j))),
                    (mod8[l], pl.BlockSpec((8, tn), lambda i, j, k, tn=tn: (0, 5 * (d // tn) + j)))],
            jobs=[_job_gather_ici([w2_b], l + 1), _job_gather_d2d([pend_w1])] if more else [], name=f"mlp_down{l}")
        if more:
            (t3, o), (pend_w2, (g_w1_next,)) = res
            g_w1.append(g_w1_next)
        else:
            (t3, o), pend_w2 = res, None
        saved.append((t, h, p, qkv, ao, mg, z, t2, h2, a, s, o, cpar))
        t = t3

    d_t, loss_tile, dg_final, dob, dgate2 = _loss_bwd(t, g_final[None], loss_target[0], saved[depth - 1][11], mod8[depth - 1], 5,
                                                      n_ctx, "loss")
    loss = lax.psum(loss_tile[0, 0], ("x", "y", "c"))

    buf_in = lax.empty((N_CHIP, depth, cin, d), BF16)
    buf_out = lax.empty((N_CHIP, depth, r_out, d), BF16)
    buf_w1 = lax.empty((N_CHIP, depth, d, c_ff), BF16)
    buf_w2 = lax.empty((N_CHIP, depth, r_ff, d), BF16)
    pend_in, pend_layer = None, None
    tkn = _pick(n, (2816,) + TOK)
    small = [None] * depth
    for l in reversed(range(depth)):
        t_in, h, p, qkv, ao, mg, z, t2, h2, a, s, o, cpar = saved[l]
        tm_g = _pick(d_ff, FEAT)
        tn = _pick(d, FEAT)
        gw2 = _matmul(s, dob, dims="tn", shape=(d_ff, d, n), tiles=(tm_g, tn, tkn), name=f"mlp_down_dw{l}")
        gw2 = gw2.reshape(N_DEV, r_ff, d)
        tn = _pick(d_ff, FEAT)
        jobs = [_job_scatter_d2d([gw2])] + ([_job_scatter_ici([pend_in], [buf_in], pend_layer)] if pend_in is not None else [])
        da, job_outs = _matmul(dob, w2_full[l], dims="nt", shape=(n, d_ff, d), tiles=(tm, tn, tk_d),
                               out_dtypes=(BF16,), epilogue=d_sq_relu_epilogue,
                               extras=[(a, pl.BlockSpec((tm, tn), lambda i, j, k: (i, j)))], jobs=jobs, name=f"mlp_down_dx{l}")
        if pend_in is not None:
            (buf_in,) = job_outs[1]
        pair_w2 = _pair_add(gw2, job_outs[0][0], f"pair_w2_{l}")
        tm_g = _pick(d, FEAT)
        tn = _pick(c_ff, FEAT)
        gw1 = _matmul(h2, da, dims="tn", shape=(d, d_ff, n), tiles=(tm_g, tn, tkn),
                      out_specs=(_g_spec("cols", None, c_ff, tm_g, tn),),
                      out_shapes=(jax.ShapeDtypeStruct((N_DEV, d, c_ff), F32),), name=f"mlp_up_dw{l}")
        tn = _pick(d, FEAT)
        tk = _pick(c_ff, (1024, 512, 256, 128))
        dh2, job_outs = _matmul(da, g_w1[l], dims="nt", shape=(n, d, d_ff), tiles=(tm, tn, tk),
                                b_spec=_w_spec("cols", None, c_ff, "nt", tn, tk),
                                jobs=[_job_scatter_d2d([gw1]), _job_scatter_ici([pair_w2], [buf_w2], l)], name=f"mlp_up_dx{l}")
        (buf_w2,) = job_outs[1]
        pair_w1 = _pair_add(gw1, job_outs[0][0], f"pair_w1_{l}")
        d_t2, dss2, dgn2, dzb, dgate1 = _norm_mod_bwd(dh2, t2, d_t, g_norm2[l][None], mod8[l], 3, n_ctx, f"norm2_bwd{l}",
                                                      gated=(z, mod8[l], 2))
        tm_g = _pick(d, FEAT)
        tn = _pick(d, FEAT)
        gout = _matmul(mg, dzb, dims="tn", shape=(d, d, n), tiles=(tm_g, tn, tkn), name=f"out_proj_dw{l}")
        gout = gout.reshape(N_DEV, r_out, d)
        tn = _pick(d, FEAT)
        d_mg, job_outs = _matmul(dzb, w_out_full[l], dims="nt", shape=(n, d, d), tiles=(tm, tn, tk_d),
                                 jobs=[_job_scatter_d2d([gout])], name=f"out_proj_dx{l}")
        pair_out = _pair_add(gout, job_outs[0][0], f"pair_out_{l}")
        d_q, parts, d_kv_ctx, d_sink, d_goa = _attention_bwd(qkv, sink[l], ao, d_mg, cpar, n_ctx, hd, kd, f"attn_bwd{l}")
        d_p, conv_acc = _mix_bwd(d_mg, p, cpar, d_q, parts, d_kv_ctx, cs, sn, n_ctx, dc, hd, kd, f"mix_bwd{l}")
        tm_g = _pick(d_in, (768, 512, 256, 128))
        tn = _pick(d, FEAT)
        gin = _matmul(d_p, h, dims="tn", shape=(d_in, d, n), tiles=(tm_g, tn, tkn), name=f"in_proj_dw{l}")
        gin = gin.reshape(N_DEV, cin, d)
        tn = _pick(d, FEAT)
        tk = _pick(d_in, (1536, 768, 512, 256, 128))
        dh, job_outs = _matmul(d_p, w_in_full[l], dims="nt", shape=(n, d, d_in), tiles=(tm, tn, tk),
                               jobs=[_job_scatter_d2d([gin]), _job_scatter_ici([pair_w1, pair_out], [buf_w1, buf_out], l)],
                               name=f"in_proj_dx{l}")
        buf_w1, buf_out = job_outs[1]
        pend_in, pend_layer = _pair_add(gin, job_outs[0][0], f"pair_in_{l}"), l
        d_mod2_tail = [dgate1[0:2], dss2[0:2], dgate2[0:2]]
        if l > 0:
            d_t, dss1, dgn1, dob, dgate2 = _norm_mod_bwd(dh, t_in, d_t2, g_norm1[l][None], mod8[l], 0, n_ctx, f"norm1_bwd{l}",
                                                         gated=(saved[l - 1][11], mod8[l - 1], 5))
        else:
            (grad_x, dss1, dgn1), ((buf_in,),) = _norm_mod_bwd(
                dh, t_in, d_t2, g_norm1[l][None], mod8[l], 0, n_ctx, f"norm1_bwd{l}", latent_only=True,
                jobs=[_job_scatter_ici([pend_in], [buf_in], pend_layer)])
        d_mod2 = jnp.concatenate([dss1[0:2]] + d_mod2_tail, axis=1)
        small[l] = (d_mod2, dgn1[0], dgn2[0], conv_acc, d_sink[0, 0:n_heads], d_goa[0])
    grad_x = grad_x[None]

    def pack(l):
        d_mod2, dgn1, dgn2, conv_acc, d_sink, d_goa = small[l]
        row0 = [d_mod2[0], dgn1, dgn2, conv_acc[3], d_sink, conv_acc[4], d_goa, conv_acc[0:3].reshape(-1)]
        row1 = [d_mod2[1]] + [jnp.zeros_like(v) for v in row0[1:]]
        return jnp.stack([jnp.concatenate(row0), jnp.concatenate(row1)])
    per_layer = N_MOD * d + 2 * d + dc + n_heads + 2 * dc + 3 * dc
    packed = jnp.concatenate([pack(l) for l in range(depth)] +
                             [jnp.stack([dg_final[0], jnp.zeros((d,), F32)])], axis=1)
    f_tot = depth * per_layer + d
    f_pad = -f_tot % 1024
    packed = jnp.pad(packed, ((0, 0), (0, f_pad)))
    (small_all,) = _all_gather([packed], "gather_small")
    small_parts = small_all.reshape(2 * N_DEV, 1, f_tot + f_pad)

    def section(arr, l, off, size):
        return lax.slice_in_dim(arr, l * per_layer + off, l * per_layer + off + size, axis=-1)

    offs = {}
    o_ = 0
    for nm_, sz in (("mod", N_MOD * d), ("gn1", d), ("gn2", d), ("cb", dc), ("sink", n_heads), ("goc", dc), ("goa", dc), ("cw", 3 * dc)):
        offs[nm_] = (o_, sz)
        o_ += sz

    def packw(b_ada_, gn1_, gn2_, cb_, sk_, goc_, goa_, gf_):
        rows = []
        for l in range(depth):
            rows += [b_ada_[l], gn1_[l], gn2_[l], cb_[l], sk_[l], goc_[l], goa_[l], jnp.zeros((3 * dc,), F32)]
        return jnp.pad(jnp.concatenate(rows + [gf_]), (0, f_pad))[None]
    pw = packw(b_ada, g_norm1, g_norm2, conv_b, sink, g_out_conv, g_out_attn, g_final)
    pm = packw(m_b_ada, m_g_norm1, m_g_norm2, m_conv_b, m_sink, m_g_out_conv, m_g_out_attn, m_g_final)
    pv = packw(v_b_ada, v_g_norm1, v_g_norm2, v_conv_b, v_sink, v_g_out_conv, v_g_out_attn, v_g_final)
    sg, sd, sm, sv = _adamw(pw, pm, pv, small_parts, "adamw_small")

    def unpack(arr):
        arr = arr[0]
        out = {}
        for nm_ in ("mod", "gn1", "gn2", "cb", "sink", "goc", "goa", "cw"):
            off, size = offs[nm_]
            out[nm_] = jnp.stack([section(arr, l, off, size) for l in range(depth)])
        out["gf"] = arr[depth * per_layer:depth * per_layer + d]
        return out
    ug, ud, um, uv = unpack(sg), unpack(sd), unpack(sm), unpack(sv)

    cw_grad_full = ug["cw"].reshape(depth, 3, N_DEV, cw)
    cw_grad = lax.dynamic_index_in_dim(cw_grad_full, dev, axis=2, keepdims=False).reshape(1, depth * 3, cw)
    cwg, cwd, cwm, cwv = _adamw(conv_w.reshape(depth * 3, cw), m_conv_w.reshape(depth * 3, cw),
                                v_conv_w.reshape(depth * 3, cw), cw_grad, "adamw_conv_w")
    cw_shape = conv_w.shape

    mod_rows = small_all[:, :, :depth * per_layer].reshape(N_DEV, 2, depth, per_layer)[:, :, :, :N_MOD * d]
    dm16 = jnp.concatenate([mod_rows[:, 0], mod_rows[:, 1]], axis=0)
    dm16 = jnp.transpose(dm16, (1, 0, 2)).reshape(depth, 16, N_DEV, c_ada)
    dm16_loc = lax.dynamic_index_in_dim(dm16, dev, axis=2, keepdims=False)
    gb_ada = lax.empty((depth, d, c_ada), F32)
    dsc_parts = []
    tm_g = _pick(d, FEAT)
    for l in range(depth):
        gb_ada = _matmul(sc16, dm16_loc[l], dims="tn", shape=(d, c_ada, 16), tiles=(tm_g, tn_ada, 16),
                         out_specs=(_g_spec("nat", l, None, tm_g, tn_ada),), out_shapes=(jax.ShapeDtypeStruct(gb_ada.shape, F32),),
                         alias=gb_ada, name=f"ada_dw{l}")
        tn = _pick(d, FEAT)
        tk = _pick(c_ada, KDIM)
        dsc_parts.append(_matmul(dm16_loc[l], w_ada, dims="nt", shape=(16, d, c_ada), tiles=(16, tn, tk),
                                 b_spec=_w_spec("nat", l, None, "nt", tn, tk), name=f"ada_dx{l}"))
    (dsc_all,) = _all_gather([jnp.stack(dsc_parts)[:, 8:16]], "gather_dcond")
    g_cctx = _cctx_grad(dsc_all.reshape(N_DEV * depth * 8, d), c_ctx[None], "c_ctx_grad")
    ccg, ccd, ccm, ccv = _adamw(c_ctx[None], m_c_ctx[None], v_c_ctx[None], g_cctx[None], "adamw_c_ctx")
    adg, add, adm, adv = _adamw(w_ada.reshape(depth * d, c_ada), m_w_ada.reshape(depth * d, c_ada),
                                v_w_ada.reshape(depth * d, c_ada), gb_ada.reshape(1, depth * d, c_ada), "adamw_w_ada")

    parts4 =[jnp.swapaxes(buf_in, 2, 3), buf_out, buf_w1, buf_w2]
    big = []
    for k, (w_, m_, v_) in enumerate(((w_in, m_w_in, v_w_in), (w_out, m_w_out, v_w_out),
                                      (w_mlp1, m_w_mlp1, v_w_mlp1), (w_mlp2, m_w_mlp2, v_w_mlp2))):
        r2, c2 = w_.shape[0] * w_.shape[1], w_.shape[2]
        res = _adamw(w_.reshape(r2, c2), m_.reshape(r2, c2), v_.reshape(r2, c2), parts4[k].reshape(N_CHIP, r2, c2),
                     f"adamw_big{k}")
        big.append([a_.reshape(w_.shape) for a_ in res])

    def leaf(i):
        return (
            (ccg, ccd, ccm, ccv)[i][0], (adg, add, adm, adv)[i].reshape(w_ada.shape),
            (ug, ud, um, uv)[i]["mod"], (ug, ud, um, uv)[i]["gn1"], (ug, ud, um, uv)[i]["gn2"], big[0][i],
            (cwg, cwd, cwm, cwv)[i].reshape(cw_shape), (ug, ud, um, uv)[i]["cb"], (ug, ud, um, uv)[i]["sink"],
            (ug, ud, um, uv)[i]["goc"], (ug, ud, um, uv)[i]["goa"], big[1][i], big[2][i], big[3][i], (ug, ud, um, uv)[i]["gf"])

    return (loss, grad_x) + leaf(0) + leaf(1) + leaf(2) + leaf(3)
```
